```python
import jax, jax.numpy as jnp
from jax import lax
import numpy as np

D_MODEL = 1024
BATCH = 8
SEQ = 2048
DEPTH = 1
DEC_BATCH = 128
DEC_SEQ = 4
PAST_LEN = 16384
PAGE_SIZE = 128

D_CONV = D_MODEL // 2
CONV_WIDTH = 31
GLA_HEADS = 4
GLA_DK = D_MODEL // 8
GLA_DV = D_MODEL // 4
GLA_RANK = 16
GLA_GATE_NORM = 16.0
GLA_CHUNK = 16
QK_W = GLA_HEADS * GLA_DK
V_W = GLA_HEADS * GLA_DV
N_GROUPS = 4
EXPERTS_PER_GROUP = 8
N_EXPERTS = N_GROUPS * EXPERTS_PER_GROUP
TOP_K = 2
D_EXPERT = D_MODEL // 2
ROUTE_BLOCK = 128
EPS = 1e-6
IN_SPLITS = (2 * D_CONV, QK_W, QK_W, V_W, V_W, GLA_RANK, D_MODEL, D_MODEL)
IN_WIDTH = sum(IN_SPLITS)

kernel_name = "hybrid_conv_gla_hiermoe_step"


def rms_norm(x, g):
    xf = x.astype(jnp.float32)
    y = xf * lax.rsqrt(jnp.mean(xf * xf, axis=-1, keepdims=True) + EPS)
    return (y * g.astype(jnp.float32)).astype(x.dtype)


def conformer_conv(u_in, buf, conv_dw, conv_db, ln_g, ln_b, w_pw):
    a, b = jnp.split(u_in, 2, axis=-1)
    u = a * jax.nn.sigmoid(b)
    full = jnp.concatenate([buf.astype(u.dtype), u], axis=1)
    y = lax.conv_general_dilated(full, conv_dw[:, None, :].astype(u.dtype), window_strides=(1,),
                                 padding='VALID', dimension_numbers=('NWC', 'WIO', 'NWC'),
                                 feature_group_count=D_CONV) + conv_db
    yf = y.astype(jnp.float32)
    mu = jnp.mean(yf, axis=-1, keepdims=True)
    var = jnp.mean(jnp.square(yf - mu), axis=-1, keepdims=True)
    yn = ((yf - mu) * lax.rsqrt(var + EPS) * ln_g + ln_b).astype(u.dtype)
    out = jax.nn.silu(yn) @ w_pw
    return out, full[:, -(CONV_WIDTH - 1):]


def gla_recurrence(q, k, v, log_a, s0, chunk):
    B, L = q.shape[:2]
    n = L // chunk

    def to_chunks(t):
        return t.astype(jnp.float32).reshape(B, n, chunk, GLA_HEADS, t.shape[-1]).transpose(1, 0, 3, 2, 4)

    qc, kc, vc, gc = to_chunks(q), to_chunks(k), to_chunks(v), to_chunks(log_a)
    causal = jnp.tril(jnp.ones((chunk, chunk), dtype=bool))[:, :, None]

    def step(s, inp):
        q_, k_, v_, g_ = inp
        b = jnp.cumsum(g_, axis=2)
        o_inter = jnp.einsum('bhtk,bhkv->bhtv', q_ * jnp.exp(b), s)
        diff = b[:, :, :, None, :] - b[:, :, None, :, :]
        decay = jnp.exp(jnp.where(causal, diff, -jnp.inf))
        att = jnp.einsum('bhtk,bhsk,bhtsk->bhts', q_, k_, decay)
        o = o_inter + jnp.einsum('bhts,bhsv->bhtv', att, v_)
        b_last = b[:, :, -1:, :]
        s_new = jnp.exp(b_last[:, :, 0, :])[..., None] * s + jnp.einsum(
            'bhsk,bhsv->bhkv', k_ * jnp.exp(b_last - b), v_)
        return s_new, o

    s_fin, o = lax.scan(step, s0.astype(jnp.float32), (qc, kc, vc, gc))
    o = o.transpose(1, 0, 3, 2, 4).reshape(B, L, GLA_HEADS, GLA_DV)
    return o, s_fin


def hier_moe(x, w_rg, b_rg, w_re, b_re, w_g, w_u, w_d):
    shp = x.shape
    xt = x.reshape(-1, D_MODEL)
    T = xt.shape[0]
    xf = xt.astype(jnp.float32)
    p_group = jax.nn.softmax(xf @ w_rg.astype(jnp.float32) + b_rg.astype(jnp.float32), axis=-1)
    g_idx = jnp.argmax(p_group, axis=-1)
    p_sel = jnp.take_along_axis(p_group, g_idx[:, None], axis=1)
    e_logits = (xf @ w_re.astype(jnp.float32) + b_re.astype(jnp.float32)).reshape(T, N_GROUPS, EXPERTS_PER_GROUP)
    e_logits = jnp.take_along_axis(e_logits, g_idx[:, None, None], axis=1)[:, 0]
    top_v, top_i = lax.top_k(e_logits, TOP_K)
    w_pair = p_sel * jax.nn.softmax(top_v, axis=-1)
    eid = (g_idx[:, None] * EXPERTS_PER_GROUP + top_i).reshape(-1).astype(jnp.int32)
    tok = jnp.arange(T * TOP_K, dtype=jnp.int32) // TOP_K
    wt = w_pair.reshape(-1)
    order = jnp.argsort(eid)
    eid_s, tok_s, wt_s = eid[order], tok[order], wt[order]
    counts = jnp.bincount(eid, length=N_EXPERTS)
    padded = (counts + ROUTE_BLOCK - 1) // ROUTE_BLOCK * ROUTE_BLOCK
    raw_start = jnp.cumsum(counts) - counts
    pad_end = jnp.cumsum(padded)
    pad_start = pad_end - padded
    dest = pad_start[eid_s] + (jnp.arange(T * TOP_K, dtype=jnp.int32) - raw_start[eid_s])
    n_blocks = -(-(T * TOP_K) // ROUTE_BLOCK) + N_EXPERTS
    n_rows = n_blocks * ROUTE_BLOCK
    row_tok = jnp.zeros((n_rows,), jnp.int32).at[dest].set(tok_s)
    row_wt = jnp.zeros((n_rows,), jnp.float32).at[dest].set(wt_s)
    block_e = jnp.minimum(jnp.searchsorted(pad_end, jnp.arange(n_blocks) * ROUTE_BLOCK, side='right'),
                          N_EXPERTS - 1)
    xb = xt[row_tok].reshape(n_blocks, ROUTE_BLOCK, D_MODEL)

    def expert_block(args):
        xblk, e = args
        hidden = jax.nn.silu(xblk @ w_g[e]) * (xblk @ w_u[e])
        return hidden @ w_d[e]

    yb = lax.map(expert_block, (xb, block_e))
    y = jax.ops.segment_sum(yb.reshape(n_rows, D_MODEL).astype(jnp.float32) * row_wt[:, None],
                            row_tok, num_segments=T)
    return y.astype(x.dtype).reshape(shp)


def decoder_layer(x, conv_buf, gla_s, norm_mix, w_in, b_gates, w_gla_gate_up, b_gla_gate_up,
                  conv_dw, conv_db, conv_ln_g, conv_ln_b, w_conv_out, gla_norm_g, w_gla_out, w_out,
                  norm_ffn, w_router_group, b_router_group, w_router_expert, b_router_expert,
                  w_expert_gate, w_expert_up, w_expert_down):
    B, L, _ = x.shape
    hn = rms_norm(x, norm_mix)
    proj = hn @ w_in
    split_points = np.cumsum(np.array(IN_SPLITS))[:-1].tolist()
    u_glu, q, k, v, r, a_low, gate_c, gate_g = jnp.split(proj, split_points, axis=-1)
    conv_out, new_buf = conformer_conv(u_glu, conv_buf, conv_dw, conv_db, conv_ln_g, conv_ln_b, w_conv_out)
    log_a = jax.nn.log_sigmoid((a_low @ w_gla_gate_up + b_gla_gate_up).astype(jnp.float32)) / GLA_GATE_NORM
    chunk = GLA_CHUNK if L % GLA_CHUNK == 0 else L
    o, new_s = gla_recurrence(q.reshape(B, L, GLA_HEADS, GLA_DK) * (GLA_DK ** -0.5),
                              k.reshape(B, L, GLA_HEADS, GLA_DK),
                              v.reshape(B, L, GLA_HEADS, GLA_DV),
                              log_a.reshape(B, L, GLA_HEADS, GLA_DK), gla_s, chunk)
    o = rms_norm(o.astype(x.dtype), gla_norm_g.reshape(GLA_HEADS, GLA_DV)).reshape(B, L, V_W)
    gla_out = (o * jax.nn.silu(r)) @ w_gla_out
    gc = jax.nn.sigmoid(gate_c + b_gates[:D_MODEL])
    gg = jax.nn.sigmoid(gate_g + b_gates[D_MODEL:])
    h = x + (gc * conv_out + gg * gla_out) @ w_out
    h = h + hier_moe(rms_norm(h, norm_ffn), w_router_group, b_router_group, w_router_expert,
                     b_router_expert, w_expert_gate, w_expert_up, w_expert_down)
    return h, new_buf.astype(x.dtype), new_s.astype(x.dtype)


def setup_inputs(seed: int = 0) -> dict:
    key = jax.random.key(seed)
    ks = jax.random.split(key, 32)
    f32 = jnp.float32
    nrm = lambda k, shape, s: jax.random.normal(k, shape, f32) * s
    L_ = DEPTH
    return {
        "x_prompt": nrm(ks[0], (BATCH, SEQ, D_MODEL), 1.0),
        "x_sample": nrm(ks[1], (DEC_BATCH, DEC_SEQ, D_MODEL), 1.0),
        "state_conv": nrm(ks[2], (L_, DEC_BATCH, CONV_WIDTH - 1, D_CONV), 0.5),
        "state_gla": nrm(ks[3], (L_, DEC_BATCH, GLA_HEADS, GLA_DK, GLA_DV), 0.3),
        "norm_mix": 1.0 + nrm(ks[4], (L_, D_MODEL), 0.02),
        "w_in": nrm(ks[5], (L_, D_MODEL, IN_WIDTH), D_MODEL ** -0.5),
        "b_gates": nrm(ks[6], (L_, 2 * D_MODEL), 0.02),
        "w_gla_gate_up": nrm(ks[7], (L_, GLA_RANK, QK_W), GLA_RANK ** -0.5),
        "b_gla_gate_up": nrm(ks[8], (L_, QK_W), 0.1),
        "conv_dw": nrm(ks[9], (L_, CONV_WIDTH, D_CONV), CONV_WIDTH ** -0.5),
        "conv_db": nrm(ks[10], (L_, D_CONV), 0.02),
        "conv_ln_g": 1.0 + nrm(ks[11], (L_, D_CONV), 0.02),
        "conv_ln_b": nrm(ks[12], (L_, D_CONV), 0.02),
        "w_conv_out": nrm(ks[13], (L_, D_CONV, D_MODEL), D_CONV ** -0.5),
        "gla_norm_g": 1.0 + nrm(ks[14], (L_, V_W), 0.02),
        "w_gla_out": nrm(ks[15], (L_, V_W, D_MODEL), V_W ** -0.5),
        "w_out": nrm(ks[16], (L_, D_MODEL, D_MODEL), D_MODEL ** -0.5),
        "norm_ffn": 1.0 + nrm(ks[17], (L_, D_MODEL), 0.02),
        "w_router_group": nrm(ks[18], (L_, D_MODEL, N_GROUPS), D_MODEL ** -0.5),
        "b_router_group": nrm(ks[19], (L_, N_GROUPS), 0.01),
        "w_router_expert": nrm(ks[20], (L_, D_MODEL, N_EXPERTS), D_MODEL ** -0.5),
        "b_router_expert": nrm(ks[21], (L_, N_EXPERTS), 0.01),
        "w_expert_gate": nrm(ks[22], (L_, N_EXPERTS, D_MODEL, D_EXPERT), D_MODEL ** -0.5),
        "w_expert_up": nrm(ks[23], (L_, N_EXPERTS, D_MODEL, D_EXPERT), D_MODEL ** -0.5),
        "w_expert_down": nrm(ks[24], (L_, N_EXPERTS, D_EXPERT, D_MODEL), D_EXPERT ** -0.5),
        "norm_final": 1.0 + nrm(ks[25], (D_MODEL,), 0.02),
    }


def reference(x_prompt, x_sample, state_conv, state_gla, norm_mix, w_in, b_gates, w_gla_gate_up,
              b_gla_gate_up, conv_dw, conv_db, conv_ln_g, conv_ln_b, w_conv_out, gla_norm_g, w_gla_out,
              w_out, norm_ffn, w_router_group, b_router_group, w_router_expert, b_router_expert,
              w_expert_gate, w_expert_up, w_expert_down, norm_final):
    hp, hs = x_prompt, x_sample
    bp = x_prompt.shape[0]
    conv_p, gla_p, conv_s, gla_s = [], [], [], []
    for l in range(DEPTH):
        params = (norm_mix[l], w_in[l], b_gates[l], w_gla_gate_up[l], b_gla_gate_up[l], conv_dw[l],
                  conv_db[l], conv_ln_g[l], conv_ln_b[l], w_conv_out[l], gla_norm_g[l], w_gla_out[l],
                  w_out[l], norm_ffn[l], w_router_group[l], b_router_group[l], w_router_expert[l],
                  b_router_expert[l], w_expert_gate[l], w_expert_up[l], w_expert_down[l])
        buf0 = jnp.zeros((bp, CONV_WIDTH - 1, D_CONV), x_prompt.dtype)
        s0 = jnp.zeros((bp, GLA_HEADS, GLA_DK, GLA_DV), jnp.float32)
        hp, cb, sb = decoder_layer(hp, buf0, s0, *params)
        conv_p.append(cb)
        gla_p.append(sb)
        hs, cb, sb = decoder_layer(hs, state_conv[l], state_gla[l], *params)
        conv_s.append(cb)
        gla_s.append(sb)
    y_prompt = rms_norm(hp, norm_final)
    y_sample = rms_norm(hs, norm_final)
    new_conv_prompt = jnp.stack(conv_p, axis=0)
    new_gla_prompt = jnp.stack(gla_p, axis=0)
    new_conv_sample = jnp.stack(conv_s, axis=0)
    new_gla_sample = jnp.stack(gla_s, axis=0)
    return (y_prompt, y_sample, new_conv_prompt, new_gla_prompt, new_conv_sample, new_gla_sample)
```

```python
import functools

import jax
import jax.numpy as jnp
from jax import lax
from jax.experimental import pallas as pl
from jax.experimental.pallas import tpu as pltpu

F32 = jnp.float32
BF16 = jnp.bfloat16

D_MODEL = 1024
D_CONV = 512
CONV_WIDTH = 31
CONV_HIST = CONV_WIDTH - 1
HEADS = 4
DK = 128
DV = 256
QK_W = HEADS * DK
V_W = HEADS * DV
RANK = 16
GATE_NORM = 16.0
N_GROUPS = 4
EXPERTS_PER_GROUP = 8
N_EXPERTS = 32
D_EXPERT = 512
EPS = 1e-6

LANES = 128
SUBLANES = 8
VMEM_LIMIT = 56 * 1024 * 1024

TOK_TILE = 256
GLA_CHUNK = 128
SAFE_LOG_DECAY = -80.0
CONV_ROWS = 64
CONV_HALO = 40
ROW_BLOCK = 128

_C_GLU_A, _C_GLU_B = 0, 512
_C_Q, _C_K, _C_V, _C_R = 1024, 1536, 2048, 3072
_C_GC, _C_GG, _C_ALOW, _C_END = 4096, 5120, 6144, 6272


def _sigmoid(x):
    return jax.nn.sigmoid(x)


def _dot(a, b):
    return jnp.dot(a, b, preferred_element_type=F32)


def _proj_kernel(x_ref, g_ref, w_ref, wup_ref, bup_ref, bg_ref,
                 u_ref, q_ref, k_ref, v_ref, la_ref, sr_ref, gc_ref, gg_ref):
    x = x_ref[...]
    hn = x * lax.rsqrt(jnp.mean(x * x, axis=-1, keepdims=True) + EPS) * g_ref[...]
    hb = hn.astype(BF16)

    def mm(lo, hi):
        return _dot(hb, w_ref[:, lo:hi])

    u_ref[...] = mm(_C_GLU_A, _C_GLU_B) * _sigmoid(mm(_C_GLU_B, _C_Q))
    q_ref[...] = mm(_C_Q, _C_K) * (DK ** -0.5)
    k_ref[...] = mm(_C_K, _C_V)
    v_ref[...] = mm(_C_V, _C_R)
    r = mm(_C_R, _C_GC)
    sr_ref[...] = r * _sigmoid(r)
    gc_ref[...] = _sigmoid(mm(_C_GC, _C_GG) + bg_ref[:, :D_MODEL])
    gg_ref[...] = _sigmoid(mm(_C_GG, _C_ALOW) + bg_ref[:, D_MODEL:])
    a_low = mm(_C_ALOW, _C_END)
    z = _dot(a_low.astype(BF16), wup_ref[...]) + bup_ref[...]
    la_ref[...] = (jnp.minimum(z, 0.0) - jnp.log1p(jnp.exp(-jnp.abs(z)))) * (1.0 / GATE_NORM)


def _proj(x2d, norm_g, w_pack, w_up, b_up, b_gates):
    n = x2d.shape[0]
    tm = TOK_TILE
    row = lambda w: pl.BlockSpec((tm, w), lambda i: (i, 0))
    full = lambda a: pl.BlockSpec(a.shape, lambda i: (0,) * a.ndim)
    widths = (D_CONV, QK_W, QK_W, V_W, QK_W, V_W, D_MODEL, D_MODEL)
    return pl.pallas_call(
        _proj_kernel,
        grid=(n // tm,),
        in_specs=[row(D_MODEL), full(norm_g), full(w_pack), full(w_up), full(b_up), full(b_gates)],
        out_specs=[row(w) for w in widths],
        out_shape=[jax.ShapeDtypeStruct((n, w), F32) for w in widths],
        compiler_params=pltpu.CompilerParams(dimension_semantics=("parallel",),
                                             vmem_limit_bytes=VMEM_LIMIT),
        name="proj",
    )(x2d, norm_g, w_pack, w_up, b_up, b_gates)


def _conv_kernel(*refs, seq, has_state):
    if has_state:
        u_ref, st_ref, dw_ref, db_ref, lg_ref, lb_ref, wpw_ref, out_ref, ns_ref, full_ref, y_ref = refs
    else:
        u_ref, dw_ref, db_ref, lg_ref, lb_ref, wpw_ref, out_ref, ns_ref, full_ref, y_ref = refs
    pad = 32
    hist0 = pad - CONV_HIST
    n_rows = y_ref.shape[0]
    rt = min(CONV_ROWS, n_rows)
    win = rt + CONV_HALO

    full_ref[0:pad, :] = jnp.zeros((pad, D_CONV), F32)
    tail = full_ref.shape[0] - (pad + seq)
    full_ref[pad + seq:, :] = jnp.zeros((tail, D_CONV), F32)
    if has_state:
        full_ref[hist0:pad, :] = st_ref[0]
    full_ref[pad:pad + seq, :] = u_ref[0]
    ns_ref[0] = full_ref[seq + hist0:seq + pad, :]

    def row_tile(i, carry):
        base = pl.multiple_of(i * rt, SUBLANES)
        for c in range(D_CONV // LANES):
            cs = slice(c * LANES, (c + 1) * LANES)
            w = full_ref[pl.ds(base, win), cs]
            acc = jnp.zeros((rt, LANES), F32)
            for s in range(SUBLANES):
                ws = w if s == 0 else pltpu.roll(w, win - s, axis=0)
                for a in range((CONV_HALO // SUBLANES)):
                    j = a * SUBLANES + s - hist0
                    if 0 <= j < CONV_WIDTH:
                        acc = acc + ws[a * SUBLANES:a * SUBLANES + rt, :] * dw_ref[j:j + 1, cs]
            y_ref[pl.ds(base, rt), cs] = acc + db_ref[:, cs]
        return carry

    lax.fori_loop(0, n_rows // rt, row_tile, 0)

    mt = min(TOK_TILE, n_rows)

    def norm_tile(i, carry):
        base = pl.multiple_of(i * mt, SUBLANES)
        y = y_ref[pl.ds(base, mt), :]
        mu = jnp.mean(y, axis=-1, keepdims=True)
        var = jnp.mean(jnp.square(y - mu), axis=-1, keepdims=True)
        yn = (y - mu) * lax.rsqrt(var + EPS) * lg_ref[...] + lb_ref[...]
        act = (yn * _sigmoid(yn)).astype(BF16)
        res = _dot(act, wpw_ref[...])
        if mt <= seq:
            out_ref[0, pl.ds(base, mt), :] = res
        else:
            out_ref[0] = res[:seq]
        return carry

    lax.fori_loop(0, n_rows // mt, norm_tile, 0)


def _conv(u3, state, dw, db, lg, lb, w_pw):
    nb, seq, _ = u3.shape
    n_rows = -(-seq // SUBLANES) * SUBLANES
    has_state = state is not None
    full = lambda a: pl.BlockSpec(a.shape, lambda b: (0,) * a.ndim)
    per_b = lambda r, w: pl.BlockSpec((1, r, w), lambda b: (b, 0, 0))
    in_specs = [per_b(seq, D_CONV)]
    args = [u3]
    if has_state:
        in_specs.append(per_b(CONV_HIST, D_CONV))
        args.append(state)
    in_specs += [full(dw), full(db), full(lg), full(lb), full(w_pw)]
    args += [dw, db, lg, lb, w_pw]
    return pl.pallas_call(
        functools.partial(_conv_kernel, seq=seq, has_state=has_state),
        grid=(nb,),
        in_specs=in_specs,
        out_specs=[per_b(seq, D_MODEL), per_b(CONV_HIST, D_CONV)],
        out_shape=[jax.ShapeDtypeStruct((nb, seq, D_MODEL), F32),
                   jax.ShapeDtypeStruct((nb, CONV_HIST, D_CONV), F32)],
        scratch_shapes=[pltpu.VMEM((n_rows + CONV_HALO, D_CONV), F32),
                        pltpu.VMEM((n_rows, D_CONV), F32)],
        compiler_params=pltpu.CompilerParams(dimension_semantics=("parallel",),
                                             vmem_limit_bytes=VMEM_LIMIT),
        name="conv_state" if has_state else "conv_fresh",
    )(*args)


def _token_step(s, a_col, k_col, q_col, v_row):
    s = s * a_col + k_col * v_row
    return s, jnp.sum(q_col * s, axis=0, keepdims=True)


def _gla_prompt_kernel(q_ref, k_ref, la_ref, v_ref, o_ref, sout_ref, s_ref):
    c = pl.program_id(1)
    n = GLA_CHUNK

    @pl.when(c == 0)
    def _():
        s_ref[...] = jnp.zeros_like(s_ref)

    g = la_ref[...]
    row = lax.broadcasted_iota(jnp.int32, (n, 1), 0)
    b = g
    sh = 1
    while sh < n:
        b = b + jnp.where(row >= sh, pltpu.roll(b, sh, axis=0), 0.0)
        sh *= 2
    b_last = b[n - 1:n, :]
    fast = jnp.min(b_last) > SAFE_LOG_DECAY

    @pl.when(fast)
    def _():
        r_i = lax.broadcasted_iota(jnp.int32, (n, n), 0)
        c_i = lax.broadcasted_iota(jnp.int32, (n, n), 1)
        for h in range(HEADS):
            ks = slice(h * DK, (h + 1) * DK)
            vs = slice(h * DV, (h + 1) * DV)
            bh = b[:, ks]
            bl = b_last[:, ks]
            kh = k_ref[:, ks]
            vh = v_ref[:, vs].astype(BF16)
            qp = (q_ref[:, ks] * jnp.exp(bh)).astype(BF16)
            kp = (kh * jnp.exp(-bh)).astype(BF16)
            kpp = kh * jnp.exp(bl - bh)
            att = lax.dot_general(qp, kp, (((1,), (1,)), ((), ())), preferred_element_type=F32)
            att = jnp.where(r_i >= c_i, att, 0.0).astype(BF16)
            s = s_ref[h]
            o_ref[:, vs] = _dot(qp, s.astype(BF16)) + _dot(att, vh)
            d_col = jnp.sum(jnp.where(r_i == c_i, jnp.exp(bl), 0.0), axis=1, keepdims=True)
            s_ref[h] = s * d_col + _dot(kpp.T.astype(BF16), vh)

    @pl.when(jnp.logical_not(fast))
    def _():
        lane = lax.broadcasted_iota(jnp.int32, (1, n), 1)
        for h in range(HEADS):
            ks = slice(h * DK, (h + 1) * DK)
            vs = slice(h * DV, (h + 1) * DV)
            q_t = q_ref[:, ks].T
            k_t = k_ref[:, ks].T
            a_t = jnp.exp(g[:, ks]).T

            def body(t, s):
                m = lane == t
                col = lambda x: jnp.sum(jnp.where(m, x, 0.0), axis=1, keepdims=True)
                s, o_row = _token_step(s, col(a_t), col(k_t), col(q_t), v_ref[pl.ds(t, 1), vs])
                o_ref[pl.ds(t, 1), vs] = o_row
                return s

            s_ref[h] = lax.fori_loop(0, n, body, s_ref[h])

    @pl.when(c == pl.num_programs(1) - 1)
    def _():
        sout_ref[0] = s_ref[...]


def _gla_prompt(q, k, la, v, nb, seq):
    nc = seq // GLA_CHUNK
    tok = lambda w: pl.BlockSpec((GLA_CHUNK, w), lambda b, c: (b * nc + c, 0))
    return pl.pallas_call(
        _gla_prompt_kernel,
        grid=(nb, nc),
        in_specs=[tok(QK_W), tok(QK_W), tok(QK_W), tok(V_W)],
        out_specs=[tok(V_W), pl.BlockSpec((1, HEADS, DK, DV), lambda b, c: (b, 0, 0, 0))],
        out_shape=[jax.ShapeDtypeStruct((nb * seq, V_W), F32),
                   jax.ShapeDtypeStruct((nb, HEADS, DK, DV), F32)],
        scratch_shapes=[pltpu.VMEM((HEADS, DK, DV), F32)],
        compiler_params=pltpu.CompilerParams(dimension_semantics=("parallel", "arbitrary"),
                                             vmem_limit_bytes=VMEM_LIMIT),
        name="gla_prompt",
    )(q, k, la, v)


def _gla_sample_kernel(qt_ref, kt_ref, at_ref, v_ref, s_ref, o_ref, sout_ref, *, seq):
    for h in range(HEADS):
        ks = slice(h * DK, (h + 1) * DK)
        vs = slice(h * DV, (h + 1) * DV)
        s = s_ref[0, h]
        for t in range(seq):
            ts = slice(t, t + 1)
            s, o_row = _token_step(s, jnp.exp(at_ref[0, ks, ts]), kt_ref[0, ks, ts], qt_ref[0, ks, ts],
                                   v_ref[0, ts, vs])
            o_ref[0, ts, vs] = o_row
        sout_ref[0, h] = s


def _gla_sample(q_t, k_t, a_t, v3, state):
    nb, _, seq = q_t.shape
    col = pl.BlockSpec((1, QK_W, seq), lambda b: (b, 0, 0))
    tok = pl.BlockSpec((1, seq, V_W), lambda b: (b, 0, 0))
    st = pl.BlockSpec((1, HEADS, DK, DV), lambda b: (b, 0, 0, 0))
    return pl.pallas_call(
        functools.partial(_gla_sample_kernel, seq=seq),
        grid=(nb,),
        in_specs=[col, col, col, tok, st],
        out_specs=[tok, st],
        out_shape=[jax.ShapeDtypeStruct((nb, seq, V_W), F32),
                   jax.ShapeDtypeStruct((nb, HEADS, DK, DV), F32)],
        compiler_params=pltpu.CompilerParams(dimension_semantics=("parallel",),
                                             vmem_limit_bytes=VMEM_LIMIT),
        name="gla_sample",
    )(q_t, k_t, a_t, v3, state)


def _merge_kernel(x_ref, co_ref, o_ref, sr_ref, gc_ref, gg_ref, gn_ref, wgo_ref, wo_ref, h_ref):
    gla_out = None
    for h in range(HEADS):
        vs = slice(h * DV, (h + 1) * DV)
        oh = o_ref[:, vs]
        on = oh * lax.rsqrt(jnp.mean(oh * oh, axis=-1, keepdims=True) + EPS) * gn_ref[:, vs]
        part = _dot((on * sr_ref[:, vs]).astype(BF16), wgo_ref[vs, :])
        gla_out = part if gla_out is None else gla_out + part
    mix = (gc_ref[...] * co_ref[...] + gg_ref[...] * gla_out).astype(BF16)
    h_ref[...] = x_ref[...] + _dot(mix, wo_ref[...])


def _merge(x2d, conv_out, o, sr, gc, gg, gn, w_gla_out, w_out):
    n = x2d.shape[0]
    tm = TOK_TILE
    row = pl.BlockSpec((tm, D_MODEL), lambda i: (i, 0))
    full = lambda a: pl.BlockSpec(a.shape, lambda i: (0,) * a.ndim)
    return pl.pallas_call(
        _merge_kernel,
        grid=(n // tm,),
        in_specs=[row] * 6 + [full(gn), full(w_gla_out), full(w_out)],
        out_specs=row,
        out_shape=jax.ShapeDtypeStruct((n, D_MODEL), F32),
        compiler_params=pltpu.CompilerParams(dimension_semantics=("parallel",),
                                             vmem_limit_bytes=VMEM_LIMIT),
        name="merge",
    )(x2d, conv_out, o, sr, gc, gg, gn, w_gla_out, w_out)


_L_E0, _L_E1, _L_R0, _L_R1 = 0, 1, 2, 3
_GROUP_LANE0 = N_EXPERTS


def _router_kernel(hp_ref, hs_ref, g_ref, wr_ref, br_ref, hn_ref, mi_ref, mf_ref, cnt_ref, carry_ref, *,
                   prompt_tiles):
    i = pl.program_id(0)
    tm = hp_ref.shape[0]

    @pl.when(i == 0)
    def _():
        carry_ref[...] = jnp.zeros_like(carry_ref)

    x = jnp.where(i < prompt_tiles, hp_ref[...], hs_ref[...])
    hn = x * lax.rsqrt(jnp.mean(x * x, axis=-1, keepdims=True) + EPS) * g_ref[...]
    hn_ref[...] = hn
    logits = jnp.dot(hn, wr_ref[...], precision=lax.Precision.HIGHEST,
                     preferred_element_type=F32) + br_ref[...]
    lane = lax.broadcasted_iota(jnp.int32, (tm, LANES), 1)
    big = jnp.int32(LANES)
    neg = jnp.float32(-jnp.inf)

    is_g = (lane >= _GROUP_LANE0) & (lane < _GROUP_LANE0 + N_GROUPS)
    lg = jnp.where(is_g, logits, neg)
    mg = jnp.max(lg, axis=1, keepdims=True)
    p_sel = 1.0 / jnp.sum(jnp.exp(lg - mg), axis=1, keepdims=True)
    g_idx = jnp.min(jnp.where(lg == mg, lane, big), axis=1, keepdims=True) - _GROUP_LANE0

    is_e = (lane < N_EXPERTS) & (jnp.right_shift(lane, 3) == g_idx)
    le = jnp.where(is_e, logits, neg)
    v0 = jnp.max(le, axis=1, keepdims=True)
    i0 = jnp.min(jnp.where(le == v0, lane, big), axis=1, keepdims=True)
    le1 = jnp.where(lane == i0, neg, le)
    v1 = jnp.max(le1, axis=1, keepdims=True)
    i1 = jnp.min(jnp.where(le1 == v1, lane, big), axis=1, keepdims=True)
    e1 = jnp.exp(v1 - v0)
    den = 1.0 + e1
    w0 = p_sel * (1.0 / den)
    w1 = p_sel * (e1 / den)

    hit0 = lane == i0
    hit1 = lane == i1
    cnt = jnp.where(hit0 | hit1, 1.0, 0.0)
    r_i = lax.broadcasted_iota(jnp.int32, (tm, tm), 0)
    c_i = lax.broadcasted_iota(jnp.int32, (tm, tm), 1)
    before = jnp.where(r_i > c_i, 1.0, 0.0).astype(BF16)
    seen = _dot(before, cnt.astype(BF16)) + carry_ref[...]
    rank0 = jnp.sum(jnp.where(hit0, seen, 0.0), axis=1, keepdims=True).astype(jnp.int32)
    rank1 = jnp.sum(jnp.where(hit1, seen, 0.0), axis=1, keepdims=True).astype(jnp.int32)
    carry_ref[...] = carry_ref[...] + jnp.sum(cnt, axis=0, keepdims=True)
    cnt_ref[...] = jnp.broadcast_to(carry_ref[...], cnt_ref.shape)

    zero_i = jnp.zeros((tm, LANES), jnp.int32)
    mi = jnp.where(lane == _L_E0, i0, zero_i)
    mi = jnp.where(lane == _L_E1, i1, mi)
    mi = jnp.where(lane == _L_R0, rank0, mi)
    mi = jnp.where(lane == _L_R1, rank1, mi)
    mi_ref[...] = mi
    mf_ref[...] = jnp.where(lane == 0, w0, jnp.where(lane == 1, w1, 0.0))


def _router(h_p, h_s, norm_g, w_route, b_route):
    tm = TOK_TILE
    tiles_p, tiles_s = h_p.shape[0] // tm, h_s.shape[0] // tm
    n = h_p.shape[0] + h_s.shape[0]
    row = lambda w: pl.BlockSpec((tm, w), lambda i: (i, 0))
    full = lambda a: pl.BlockSpec(a.shape, lambda i: (0,) * a.ndim)
    return pl.pallas_call(
        functools.partial(_router_kernel, prompt_tiles=tiles_p),
        grid=(tiles_p + tiles_s,),
        in_specs=[pl.BlockSpec((tm, D_MODEL), lambda i: (jnp.minimum(i, tiles_p - 1), 0)),
                  pl.BlockSpec((tm, D_MODEL), lambda i: (jnp.maximum(i - tiles_p, 0), 0)),
                  full(norm_g), full(w_route), full(b_route)],
        out_specs=[row(D_MODEL), row(LANES), row(LANES), pl.BlockSpec((SUBLANES, LANES), lambda i: (0, 0))],
        out_shape=[jax.ShapeDtypeStruct((n, D_MODEL), F32),
                   jax.ShapeDtypeStruct((n, LANES), jnp.int32),
                   jax.ShapeDtypeStruct((n, LANES), F32),
                   jax.ShapeDtypeStruct((SUBLANES, LANES), F32)],
        scratch_shapes=[pltpu.VMEM((1, LANES), F32)],
        compiler_params=pltpu.CompilerParams(dimension_semantics=("arbitrary",),
                                             vmem_limit_bytes=VMEM_LIMIT),
        name="router",
    )(h_p, h_s, norm_g, w_route, b_route)


def _row_copy(src_ref, src_row, dst_ref, dst_row, sem):
    return pltpu.make_async_copy(src_ref.at[pl.ds(src_row, 1), :], dst_ref.at[pl.ds(dst_row, 1), :], sem)


def _dispatch_kernel(p0_ref, p1_ref, hn_ref, xs_ref, sem):
    base = pl.program_id(0) * TOK_TILE

    def issue(t, carry):
        _row_copy(hn_ref, base + t, xs_ref, p0_ref[base + t], sem).start()
        _row_copy(hn_ref, base + t, xs_ref, p1_ref[base + t], sem).start()
        return carry

    lax.fori_loop(0, TOK_TILE, issue, 0)

    def drain(t, carry):
        _row_copy(hn_ref, base + t, xs_ref, p0_ref[base + t], sem).wait()
        _row_copy(hn_ref, base + t, xs_ref, p1_ref[base + t], sem).wait()
        return carry

    lax.fori_loop(0, TOK_TILE, drain, 0)


def _dispatch(pos0, pos1, hn):
    n = hn.shape[0]
    return pl.pallas_call(
        _dispatch_kernel,
        grid_spec=pltpu.PrefetchScalarGridSpec(
            num_scalar_prefetch=2,
            grid=(n // TOK_TILE,),
            in_specs=[pl.BlockSpec(memory_space=pl.ANY)],
            out_specs=pl.BlockSpec(memory_space=pl.ANY),
            scratch_shapes=[pltpu.SemaphoreType.DMA(())],
        ),
        out_shape=jax.ShapeDtypeStruct((2 * n, D_MODEL), F32),
        compiler_params=pltpu.CompilerParams(dimension_semantics=("arbitrary",)),
        name="dispatch",
    )(pos0, pos1, hn)


def _combine_kernel(p0_ref, p1_ref, h_ref, wf_ref, g_ref, ys_ref, out_ref, b0_ref, b1_ref, sem, *, tok_offset):
    tm = h_ref.shape[0]
    base = tok_offset + pl.program_id(0) * tm

    def issue(t, carry):
        _row_copy(ys_ref, p0_ref[base + t], b0_ref, t, sem).start()
        _row_copy(ys_ref, p1_ref[base + t], b1_ref, t, sem).start()
        return carry

    lax.fori_loop(0, tm, issue, 0)

    def drain(t, carry):
        _row_copy(ys_ref, p0_ref[base + t], b0_ref, t, sem).wait()
        _row_copy(ys_ref, p1_ref[base + t], b1_ref, t, sem).wait()
        return carry

    lax.fori_loop(0, tm, drain, 0)

    y = h_ref[...] + (b0_ref[...] * wf_ref[:, 0:1] + b1_ref[...] * wf_ref[:, 1:2])
    out_ref[...] = y * lax.rsqrt(jnp.mean(y * y, axis=-1, keepdims=True) + EPS) * g_ref[...]


def _combine(pos0, pos1, h, wf, norm_g, ys, tok_offset):
    n_rows = h.shape[0]
    tm = TOK_TILE
    off = tok_offset // tm
    return pl.pallas_call(
        functools.partial(_combine_kernel, tok_offset=tok_offset),
        grid_spec=pltpu.PrefetchScalarGridSpec(
            num_scalar_prefetch=2,
            grid=(n_rows // tm,),
            in_specs=[pl.BlockSpec((tm, D_MODEL), lambda i, p0, p1: (i, 0)),
                      pl.BlockSpec((tm, LANES), lambda i, p0, p1: (i + off, 0)),
                      pl.BlockSpec((1, D_MODEL), lambda i, p0, p1: (0, 0)),
                      pl.BlockSpec(memory_space=pl.ANY)],
            out_specs=pl.BlockSpec((tm, D_MODEL), lambda i, p0, p1: (i, 0)),
            scratch_shapes=[pltpu.VMEM((tm, D_MODEL), F32), pltpu.VMEM((tm, D_MODEL), F32),
                            pltpu.SemaphoreType.DMA(())],
        ),
        out_shape=jax.ShapeDtypeStruct((n_rows, D_MODEL), F32),
        compiler_params=pltpu.CompilerParams(dimension_semantics=("arbitrary",),
                                             vmem_limit_bytes=VMEM_LIMIT),
        name="combine",
    )(pos0, pos1, h, wf, norm_g, ys)


def _expert_kernel(blk_ref, exp_ref, lo_ref, hi_ref, first_ref,
                   xs_ref, wg_ref, wu_ref, wd_ref, ys_ref, wgb_ref, wub_ref, wdb_ref):
    w = pl.program_id(0)
    e = exp_ref[w]
    changed = jnp.logical_or(w == 0, e != exp_ref[jnp.maximum(w - 1, 0)])

    @pl.when(changed)
    def _():
        wgb_ref[...] = wg_ref[0].astype(BF16)
        wub_ref[...] = wu_ref[0].astype(BF16)
        wdb_ref[...] = wd_ref[0].astype(BF16)

    lo = lo_ref[w]
    hi = hi_ref[w]

    @pl.when(hi > lo)
    def _():
        x = xs_ref[...].astype(BF16)
        gate = _dot(x, wgb_ref[...])
        hid = (gate * _sigmoid(gate)) * _dot(x, wub_ref[...])
        y = _dot(hid.astype(BF16), wdb_ref[...])
        rows = blk_ref[w] * ROW_BLOCK + lax.broadcasted_iota(jnp.int32, (ROW_BLOCK, 1), 0)
        y = jnp.where((rows >= lo) & (rows < hi), y, 0.0)

        @pl.when(first_ref[w] == 1)
        def _():
            ys_ref[...] = y

        @pl.when(first_ref[w] == 0)
        def _():
            ys_ref[...] = ys_ref[...] + y


def _experts(items, xs, w_gate, w_up, w_down):
    n_rows = xs.shape[0]
    n_items = items[0].shape[0]
    return pl.pallas_call(
        _expert_kernel,
        grid_spec=pltpu.PrefetchScalarGridSpec(
            num_scalar_prefetch=5,
            grid=(n_items,),
            in_specs=[pl.BlockSpec((ROW_BLOCK, D_MODEL), lambda w, blk, ex, lo, hi, fi: (blk[w], 0)),
                      pl.BlockSpec((1, D_MODEL, D_EXPERT), lambda w, blk, ex, lo, hi, fi: (ex[w], 0, 0)),
                      pl.BlockSpec((1, D_MODEL, D_EXPERT), lambda w, blk, ex, lo, hi, fi: (ex[w], 0, 0)),
                      pl.BlockSpec((1, D_EXPERT, D_MODEL), lambda w, blk, ex, lo, hi, fi: (ex[w], 0, 0))],
            out_specs=pl.BlockSpec((ROW_BLOCK, D_MODEL), lambda w, blk, ex, lo, hi, fi: (blk[w], 0)),
            scratch_shapes=[pltpu.VMEM((D_MODEL, D_EXPERT), BF16), pltpu.VMEM((D_MODEL, D_EXPERT), BF16),
                            pltpu.VMEM((D_EXPERT, D_MODEL), BF16)],
        ),
        out_shape=jax.ShapeDtypeStruct((n_rows, D_MODEL), F32),
        compiler_params=pltpu.CompilerParams(dimension_semantics=("arbitrary",),
                                             vmem_limit_bytes=VMEM_LIMIT),
        name="experts",
    )(*items, xs, w_gate, w_up, w_down)


def _work_items(counts, n_rows):
    n_blocks = n_rows // ROW_BLOCK
    n_items = n_blocks + N_EXPERTS - 1
    ends = jnp.cumsum(counts)
    starts = ends - counts
    blk_lo = starts // ROW_BLOCK
    blk_hi = jnp.maximum(ends - 1, 0) // ROW_BLOCK
    per_e = jnp.where(counts > 0, blk_hi - blk_lo + 1, 0)
    item_end = jnp.cumsum(per_e)
    item_start = item_end - per_e
    total = item_end[-1]
    w = jnp.arange(n_items, dtype=jnp.int32)
    live = w < total
    e = jnp.minimum(jnp.searchsorted(item_end, w, side="right"), N_EXPERTS - 1).astype(jnp.int32)
    e_last = e[jnp.maximum(total - 1, 0)]
    e = jnp.where(live, e, e_last)
    blk = jnp.where(live, blk_lo[e] + (w - item_start[e]), n_blocks - 1).astype(jnp.int32)
    lo = jnp.where(live, jnp.maximum(starts[e], blk * ROW_BLOCK), 0).astype(jnp.int32)
    hi = jnp.where(live, jnp.minimum(ends[e], (blk + 1) * ROW_BLOCK), 0).astype(jnp.int32)
    prev_blk = jnp.concatenate([jnp.full((1,), -1, jnp.int32), blk[:-1]])
    first = (live & (blk != prev_blk)).astype(jnp.int32)
    return (blk, e, lo, hi, first), starts


def kernel(x_prompt, x_sample, state_conv, state_gla, norm_mix, w_in, b_gates, w_gla_gate_up, b_gla_gate_up, conv_dw, conv_db, conv_ln_g, conv_ln_b, w_conv_out, gla_norm_g, w_gla_out, w_out, norm_ffn, w_router_group, b_router_group, w_router_expert, b_router_expert, w_expert_gate, w_expert_up, w_expert_down, norm_final):
    depth = norm_mix.shape[0]
    assert depth == 1
    l = 0
    bp, seq_p, _ = x_prompt.shape
    bs, seq_s, _ = x_sample.shape
    n_p, n_s = bp * seq_p, bs * seq_s
    n_tok = n_p + n_s
    row2 = lambda a: a.reshape(1, -1)

    wi = w_in[l]
    a0 = _C_GC
    w_pack = jnp.concatenate(
        [wi[:, :a0], wi[:, a0 + RANK:], jnp.pad(wi[:, a0:a0 + RANK], ((0, 0), (0, LANES - RANK)))],
        axis=1).astype(BF16)
    w_up_pad = jnp.pad(w_gla_gate_up[l], ((0, LANES - RANK), (0, 0))).astype(BF16)
    w_route = jnp.pad(jnp.concatenate([w_router_expert[l], w_router_group[l]], axis=1),
                      ((0, 0), (0, LANES - N_EXPERTS - N_GROUPS)))
    b_route = jnp.pad(jnp.concatenate([b_router_expert[l], b_router_group[l]]),
                      (0, LANES - N_EXPERTS - N_GROUPS)).reshape(1, LANES)
    w_pw = w_conv_out[l].astype(BF16)
    w_go = w_gla_out[l].astype(BF16)
    w_o = w_out[l].astype(BF16)
    conv_args = (conv_dw[l], row2(conv_db[l]), row2(conv_ln_g[l]), row2(conv_ln_b[l]), w_pw)
    proj_args = (row2(norm_mix[l]), w_pack, w_up_pad, row2(b_gla_gate_up[l]), row2(b_gates[l]))

    xp = x_prompt.reshape(n_p, D_MODEL)
    u, q, k, v, la, sr, gc, gg = _proj(xp, *proj_args)
    conv_out, conv_p = _conv(u.reshape(bp, seq_p, D_CONV), None, *conv_args)
    o, gla_p = _gla_prompt(q, k, la, v, bp, seq_p)
    h_p = _merge(xp, conv_out.reshape(n_p, D_MODEL), o, sr, gc, gg, row2(gla_norm_g[l]), w_go, w_o)

    xs_ = x_sample.reshape(n_s, D_MODEL)
    u, q, k, v, la, sr, gc, gg = _proj(xs_, *proj_args)
    conv_out, conv_s = _conv(u.reshape(bs, seq_s, D_CONV), state_conv[l], *conv_args)
    cols = lambda a: a.reshape(bs, seq_s, QK_W).transpose(0, 2, 1)
    o, gla_s = _gla_sample(cols(q), cols(k), cols(la), v.reshape(bs, seq_s, V_W), state_gla[l])
    h_s = _merge(xs_, conv_out.reshape(n_s, D_MODEL), o.reshape(n_s, V_W), sr, gc, gg,
                 row2(gla_norm_g[l]), w_go, w_o)

    hn, meta_i, meta_f, cnt = _router(h_p, h_s, row2(norm_ffn[l]), w_route, b_route)
    counts = cnt[0, :N_EXPERTS].astype(jnp.int32)
    items, starts = _work_items(counts, 2 * n_tok)
    pos0 = (starts[meta_i[:, _L_E0]] + meta_i[:, _L_R0]).astype(jnp.int32)
    pos1 = (starts[meta_i[:, _L_E1]] + meta_i[:, _L_R1]).astype(jnp.int32)
    xs_sorted = _dispatch(pos0, pos1, hn)
    ys = _experts(items, xs_sorted, w_expert_gate[l], w_expert_up[l], w_expert_down[l])
    y_p = _combine(pos0, pos1, h_p, meta_f, row2(norm_final), ys, 0)
    y_s = _combine(pos0, pos1, h_s, meta_f, row2(norm_final), ys, n_p)

    return (y_p.reshape(bp, seq_p, D_MODEL), y_s.reshape(bs, seq_s, D_MODEL),
            conv_p[None], gla_p[None], conv_s[None], gla_s[None])
```

```python
import functools

import jax
import jax.numpy as jnp
from jax import lax
from jax.experimental import pallas as pl
from jax.experimental.pallas import tpu as pltpu

F32 = jnp.float32
BF16 = jnp.bfloat16

D_MODEL = 1024
D_CONV = 512
CONV_WIDTH = 31
CONV_HIST = CONV_WIDTH - 1
HEADS = 4
DK = 128
DV = 256
QK_W = HEADS * DK
V_W = HEADS * DV
RANK = 16
GATE_NORM = 16.0
N_GROUPS = 4
EXPERTS_PER_GROUP = 8
N_EXPERTS = 32
D_EXPERT = 512
EPS = 1e-6

LANES = 128
SUBLANES = 8
VMEM_LIMIT = 56 * 1024 * 1024

TOK_TILE = 256
GLA_CHUNK = 128
SAFE_LOG_DECAY = -80.0
CONV_ROWS = 64
CONV_HALO = 40
ROW_BLOCK = 256
DMA_UNROLL = 8

_C_GLU_A, _C_GLU_B = 0, 512
_C_Q, _C_K, _C_V, _C_R = 1024, 1536, 2048, 3072
_C_GC, _C_GG, _C_ALOW, _C_END = 4096, 5120, 6144, 6272


def _sigmoid(x):
    return jax.nn.sigmoid(x)


def _dot(a, b):
    return jnp.dot(a, b, preferred_element_type=F32)


def _proj_kernel(x_ref, g_ref, w_ref, wup_ref, bup_ref, bg_ref,
                 u_ref, q_ref, k_ref, v_ref, la_ref, sr_ref, gc_ref, gg_ref):
    x = x_ref[...]
    hn = x * lax.rsqrt(jnp.mean(x * x, axis=-1, keepdims=True) + EPS) * g_ref[...]
    hb = hn.astype(BF16)

    def mm(lo, hi):
        return _dot(hb, w_ref[:, lo:hi])

    u_ref[...] = mm(_C_GLU_A, _C_GLU_B) * _sigmoid(mm(_C_GLU_B, _C_Q))
    q_ref[...] = mm(_C_Q, _C_K) * (DK ** -0.5)
    k_ref[...] = mm(_C_K, _C_V)
    v_ref[...] = mm(_C_V, _C_R)
    r = mm(_C_R, _C_GC)
    sr_ref[...] = r * _sigmoid(r)
    gc_ref[...] = _sigmoid(mm(_C_GC, _C_GG) + bg_ref[:, :D_MODEL])
    gg_ref[...] = _sigmoid(mm(_C_GG, _C_ALOW) + bg_ref[:, D_MODEL:])
    a_low = mm(_C_ALOW, _C_END)
    z = _dot(a_low.astype(BF16), wup_ref[...]) + bup_ref[...]
    la_ref[...] = (jnp.minimum(z, 0.0) - jnp.log1p(jnp.exp(-jnp.abs(z)))) * (1.0 / GATE_NORM)


def _proj(x2d, norm_g, w_pack, w_up, b_up, b_gates):
    n = x2d.shape[0]
    tm = TOK_TILE
    row = lambda w: pl.BlockSpec((tm, w), lambda i: (i, 0))
    full = lambda a: pl.BlockSpec(a.shape, lambda i: (0,) * a.ndim)
    widths = (D_CONV, QK_W, QK_W, V_W, QK_W, V_W, D_MODEL, D_MODEL)
    return pl.pallas_call(
        _proj_kernel,
        grid=(n // tm,),
        in_specs=[row(D_MODEL), full(norm_g), full(w_pack), full(w_up), full(b_up), full(b_gates)],
        out_specs=[row(w) for w in widths],
        out_shape=[jax.ShapeDtypeStruct((n, w), F32) for w in widths],
        compiler_params=pltpu.CompilerParams(dimension_semantics=("parallel",),
                                             vmem_limit_bytes=VMEM_LIMIT),
        name="proj",
    )(x2d, norm_g, w_pack, w_up, b_up, b_gates)


def _conv_kernel(*refs, seq, has_state):
    if has_state:
        u_ref, st_ref, dw_ref, db_ref, lg_ref, lb_ref, wpw_ref, out_ref, ns_ref, full_ref, y_ref = refs
    else:
        u_ref, dw_ref, db_ref, lg_ref, lb_ref, wpw_ref, out_ref, ns_ref, full_ref, y_ref = refs
    pad = 32
    hist0 = pad - CONV_HIST
    n_rows = y_ref.shape[0]
    rt = min(CONV_ROWS, n_rows)
    win = rt + CONV_HALO

    full_ref[0:pad, :] = jnp.zeros((pad, D_CONV), F32)
    tail = full_ref.shape[0] - (pad + seq)
    full_ref[pad + seq:, :] = jnp.zeros((tail, D_CONV), F32)
    if has_state:
        full_ref[hist0:pad, :] = st_ref[0]
    full_ref[pad:pad + seq, :] = u_ref[0]
    ns_ref[0] = full_ref[seq + hist0:seq + pad, :]

    def row_tile(i, carry):
        base = pl.multiple_of(i * rt, SUBLANES)
        for c in range(D_CONV // LANES):
            cs = slice(c * LANES, (c + 1) * LANES)
            w = full_ref[pl.ds(base, win), cs]
            acc = jnp.zeros((rt, LANES), F32)
            for s in range(SUBLANES):
                ws = w if s == 0 else pltpu.roll(w, win - s, axis=0)
                for a in range((CONV_HALO // SUBLANES)):
                    j = a * SUBLANES + s - hist0
                    if 0 <= j < CONV_WIDTH:
                        acc = acc + ws[a * SUBLANES:a * SUBLANES + rt, :] * dw_ref[j:j + 1, cs]
            y_ref[pl.ds(base, rt), cs] = acc + db_ref[:, cs]
        return carry

    lax.fori_loop(0, n_rows // rt, row_tile, 0)

    mt = min(TOK_TILE, n_rows)

    def norm_tile(i, carry):
        base = pl.multiple_of(i * mt, SUBLANES)
        y = y_ref[pl.ds(base, mt), :]
        mu = jnp.mean(y, axis=-1, keepdims=True)
        var = jnp.mean(jnp.square(y - mu), axis=-1, keepdims=True)
        yn = (y - mu) * lax.rsqrt(var + EPS) * lg_ref[...] + lb_ref[...]
        act = (yn * _sigmoid(yn)).astype(BF16)
        res = _dot(act, wpw_ref[...])
        if mt <= seq:
            out_ref[0, pl.ds(base, mt), :] = res
        else:
            out_ref[0] = res[:seq]
        return carry

    lax.fori_loop(0, n_rows // mt, norm_tile, 0)


def _conv(u3, state, dw, db, lg, lb, w_pw):
    nb, seq, _ = u3.shape
    n_rows = -(-seq // SUBLANES) * SUBLANES
    has_state = state is not None
    full = lambda a: pl.BlockSpec(a.shape, lambda b: (0,) * a.ndim)
    per_b = lambda r, w: pl.BlockSpec((1, r, w), lambda b: (b, 0, 0))
    in_specs = [per_b(seq, D_CONV)]
    args = [u3]
    if has_state:
        in_specs.append(per_b(CONV_HIST, D_CONV))
        args.append(state)
    in_specs += [full(dw), full(db), full(lg), full(lb), full(w_pw)]
    args += [dw, db, lg, lb, w_pw]
    return pl.pallas_call(
        functools.partial(_conv_kernel, seq=seq, has_state=has_state),
        grid=(nb,),
        in_specs=in_specs,
        out_specs=[per_b(seq, D_MODEL), per_b(CONV_HIST, D_CONV)],
        out_shape=[jax.ShapeDtypeStruct((nb, seq, D_MODEL), F32),
                   jax.ShapeDtypeStruct((nb, CONV_HIST, D_CONV), F32)],
        scratch_shapes=[pltpu.VMEM((n_rows + CONV_HALO, D_CONV), F32),
                        pltpu.VMEM((n_rows, D_CONV), F32)],
        compiler_params=pltpu.CompilerParams(dimension_semantics=("parallel",),
                                             vmem_limit_bytes=VMEM_LIMIT),
        name="conv_state" if has_state else "conv_fresh",
    )(*args)


def _token_step(s, a_col, k_col, q_col, v_row):
    s = s * a_col + k_col * v_row
    return s, jnp.sum(q_col * s, axis=0, keepdims=True)


def _gla_prompt_kernel(q_ref, k_ref, la_ref, v_ref, o_ref, sout_ref, s_ref):
    c = pl.program_id(1)
    n = GLA_CHUNK

    @pl.when(c == 0)
    def _():
        s_ref[...] = jnp.zeros_like(s_ref)

    g = la_ref[...]
    row = lax.broadcasted_iota(jnp.int32, (n, 1), 0)
    b = g
    sh = 1
    while sh < n:
        b = b + jnp.where(row >= sh, pltpu.roll(b, sh, axis=0), 0.0)
        sh *= 2
    b_last = b[n - 1:n, :]
    fast = jnp.min(b_last) > SAFE_LOG_DECAY

    @pl.when(fast)
    def _():
        r_i = lax.broadcasted_iota(jnp.int32, (n, n), 0)
        c_i = lax.broadcasted_iota(jnp.int32, (n, n), 1)
        for h in range(HEADS):
            ks = slice(h * DK, (h + 1) * DK)
            vs = slice(h * DV, (h + 1) * DV)
            bh = b[:, ks]
            bl = b_last[:, ks]
            kh = k_ref[:, ks]
            vh = v_ref[:, vs].astype(BF16)
            qp = (q_ref[:, ks] * jnp.exp(bh)).astype(BF16)
            kp = (kh * jnp.exp(-bh)).astype(BF16)
            kpp = kh * jnp.exp(bl - bh)
            att = lax.dot_general(qp, kp, (((1,), (1,)), ((), ())), preferred_element_type=F32)
            att = jnp.where(r_i >= c_i, att, 0.0).astype(BF16)
            s = s_ref[h]
            o_ref[:, vs] = _dot(qp, s.astype(BF16)) + _dot(att, vh)
            d_col = jnp.sum(jnp.where(r_i == c_i, jnp.exp(bl), 0.0), axis=1, keepdims=True)
            s_ref[h] = s * d_col + _dot(kpp.T.astype(BF16), vh)

    @pl.when(jnp.logical_not(fast))
    def _():
        lane = lax.broadcasted_iota(jnp.int32, (1, n), 1)
        for h in range(HEADS):
            ks = slice(h * DK, (h + 1) * DK)
            vs = slice(h * DV, (h + 1) * DV)
            q_t = q_ref[:, ks].T
            k_t = k_ref[:, ks].T
            a_t = jnp.exp(g[:, ks]).T

            def body(t, s):
                m = lane == t
                col = lambda x: jnp.sum(jnp.where(m, x, 0.0), axis=1, keepdims=True)
                s, o_row = _token_step(s, col(a_t), col(k_t), col(q_t), v_ref[pl.ds(t, 1), vs])
                o_ref[pl.ds(t, 1), vs] = o_row
                return s

            s_ref[h] = lax.fori_loop(0, n, body, s_ref[h])

    @pl.when(c == pl.num_programs(1) - 1)
    def _():
        sout_ref[0] = s_ref[...]


def _gla_prompt(q, k, la, v, nb, seq):
    nc = seq // GLA_CHUNK
    tok = lambda w: pl.BlockSpec((GLA_CHUNK, w), lambda b, c: (b * nc + c, 0))
    return pl.pallas_call(
        _gla_prompt_kernel,
        grid=(nb, nc),
        in_specs=[tok(QK_W), tok(QK_W), tok(QK_W), tok(V_W)],
        out_specs=[tok(V_W), pl.BlockSpec((1, HEADS, DK, DV), lambda b, c: (b, 0, 0, 0))],
        out_shape=[jax.ShapeDtypeStruct((nb * seq, V_W), F32),
                   jax.ShapeDtypeStruct((nb, HEADS, DK, DV), F32)],
        scratch_shapes=[pltpu.VMEM((HEADS, DK, DV), F32)],
        compiler_params=pltpu.CompilerParams(dimension_semantics=("parallel", "arbitrary"),
                                             vmem_limit_bytes=VMEM_LIMIT),
        name="gla_prompt",
    )(q, k, la, v)


def _gla_sample_kernel(q_ref, k_ref, la_ref, v_ref, s_ref, o_ref, sout_ref, *, seq):
    def columns(x):
        pad = jnp.zeros((SUBLANES - seq, QK_W), F32)
        return jnp.concatenate([x, pad], axis=0).T

    q_t = columns(q_ref[0])
    k_t = columns(k_ref[0])
    a_t = columns(jnp.exp(la_ref[0]))
    for h in range(HEADS):
        ks = slice(h * DK, (h + 1) * DK)
        vs = slice(h * DV, (h + 1) * DV)
        s = s_ref[0, h]
        for t in range(seq):
            ts = slice(t, t + 1)
            s, o_row = _token_step(s, a_t[ks, ts], k_t[ks, ts], q_t[ks, ts], v_ref[0, ts, vs])
            o_ref[0, ts, vs] = o_row
        sout_ref[0, h] = s


def _gla_sample(q3, k3, la3, v3, state):
    nb, seq, _ = q3.shape
    assert seq <= SUBLANES
    col = pl.BlockSpec((1, seq, QK_W), lambda b: (b, 0, 0))
    tok = pl.BlockSpec((1, seq, V_W), lambda b: (b, 0, 0))
    st = pl.BlockSpec((1, HEADS, DK, DV), lambda b: (b, 0, 0, 0))
    return pl.pallas_call(
        functools.partial(_gla_sample_kernel, seq=seq),
        grid=(nb,),
        in_specs=[col, col, col, tok, st],
        out_specs=[tok, st],
        out_shape=[jax.ShapeDtypeStruct((nb, seq, V_W), F32),
                   jax.ShapeDtypeStruct((nb, HEADS, DK, DV), F32)],
        compiler_params=pltpu.CompilerParams(dimension_semantics=("parallel",),
                                             vmem_limit_bytes=VMEM_LIMIT),
        name="gla_sample",
    )(q3, k3, la3, v3, state)


def _merge_kernel(x_ref, co_ref, o_ref, sr_ref, gc_ref, gg_ref, gn_ref, wgo_ref, wo_ref, h_ref):
    gla_out = None
    for h in range(HEADS):
        vs = slice(h * DV, (h + 1) * DV)
        oh = o_ref[:, vs]
        on = oh * lax.rsqrt(jnp.mean(oh * oh, axis=-1, keepdims=True) + EPS) * gn_ref[:, vs]
        part = _dot((on * sr_ref[:, vs]).astype(BF16), wgo_ref[vs, :])
        gla_out = part if gla_out is None else gla_out + part
    mix = (gc_ref[...] * co_ref[...] + gg_ref[...] * gla_out).astype(BF16)
    h_ref[...] = x_ref[...] + _dot(mix, wo_ref[...])


def _merge(x2d, conv_out, o, sr, gc, gg, gn, w_gla_out, w_out):
    n = x2d.shape[0]
    tm = TOK_TILE
    row = pl.BlockSpec((tm, D_MODEL), lambda i: (i, 0))
    full = lambda a: pl.BlockSpec(a.shape, lambda i: (0,) * a.ndim)
    return pl.pallas_call(
        _merge_kernel,
        grid=(n // tm,),
        in_specs=[row] * 6 + [full(gn), full(w_gla_out), full(w_out)],
        out_specs=row,
        out_shape=jax.ShapeDtypeStruct((n, D_MODEL), F32),
        compiler_params=pltpu.CompilerParams(dimension_semantics=("parallel",),
                                             vmem_limit_bytes=VMEM_LIMIT),
        name="merge",
    )(x2d, conv_out, o, sr, gc, gg, gn, w_gla_out, w_out)


_L_E0, _L_E1, _L_R0, _L_R1 = 0, 1, 2, 3
_GROUP_LANE0 = N_EXPERTS


def _router_kernel(hp_ref, hs_ref, g_ref, wr_ref, br_ref, hn_ref, mi_ref, mf_ref, cnt_ref, carry_ref, *,
                   prompt_tiles):
    i = pl.program_id(0)
    tm = hp_ref.shape[0]

    @pl.when(i == 0)
    def _():
        carry_ref[...] = jnp.zeros_like(carry_ref)

    x = jnp.where(i < prompt_tiles, hp_ref[...], hs_ref[...])
    hn = x * lax.rsqrt(jnp.mean(x * x, axis=-1, keepdims=True) + EPS) * g_ref[...]
    hn_ref[...] = hn
    logits = jnp.dot(hn, wr_ref[...], precision=lax.Precision.HIGHEST,
                     preferred_element_type=F32) + br_ref[...]
    lane = lax.broadcasted_iota(jnp.int32, (tm, LANES), 1)
    big = jnp.int32(LANES)
    neg = jnp.float32(-jnp.inf)

    is_g = (lane >= _GROUP_LANE0) & (lane < _GROUP_LANE0 + N_GROUPS)
    lg = jnp.where(is_g, logits, neg)
    mg = jnp.max(lg, axis=1, keepdims=True)
    p_sel = 1.0 / jnp.sum(jnp.exp(lg - mg), axis=1, keepdims=True)
    g_idx = jnp.min(jnp.where(lg == mg, lane, big), axis=1, keepdims=True) - _GROUP_LANE0

    is_e = (lane < N_EXPERTS) & (jnp.right_shift(lane, 3) == g_idx)
    le = jnp.where(is_e, logits, neg)
    v0 = jnp.max(le, axis=1, keepdims=True)
    i0 = jnp.min(jnp.where(le == v0, lane, big), axis=1, keepdims=True)
    le1 = jnp.where(lane == i0, neg, le)
    v1 = jnp.max(le1, axis=1, keepdims=True)
    i1 = jnp.min(jnp.where(le1 == v1, lane, big), axis=1, keepdims=True)
    e1 = jnp.exp(v1 - v0)
    den = 1.0 + e1
    w0 = p_sel * (1.0 / den)
    w1 = p_sel * (e1 / den)

    hit0 = lane == i0
    hit1 = lane == i1
    cnt = jnp.where(hit0 | hit1, 1.0, 0.0)
    r_i = lax.broadcasted_iota(jnp.int32, (tm, tm), 0)
    c_i = lax.broadcasted_iota(jnp.int32, (tm, tm), 1)
    before = jnp.where(r_i > c_i, 1.0, 0.0).astype(BF16)
    seen = _dot(before, cnt.astype(BF16)) + carry_ref[...]
    rank0 = jnp.sum(jnp.where(hit0, seen, 0.0), axis=1, keepdims=True)
    rank1 = jnp.sum(jnp.where(hit1, seen, 0.0), axis=1, keepdims=True)
    carry_ref[...] = carry_ref[...] + jnp.sum(cnt, axis=0, keepdims=True)
    cnt_ref[...] = jnp.broadcast_to(carry_ref[...], cnt_ref.shape)

    rec = jnp.where(lane == _L_E0, i0.astype(F32), 0.0)
    rec = jnp.where(lane == _L_E1, i1.astype(F32), rec)
    rec = jnp.where(lane == _L_R0, rank0, rec)
    rec = jnp.where(lane == _L_R1, rank1, rec)
    mi_ref[...] = rec.T[:SUBLANES, :]
    mf_ref[...] = jnp.where(lane == 0, w0, jnp.where(lane == 1, w1, 0.0))


def _router(h_p, h_s, norm_g, w_route, b_route):
    tm = TOK_TILE
    tiles_p, tiles_s = h_p.shape[0] // tm, h_s.shape[0] // tm
    n = h_p.shape[0] + h_s.shape[0]
    row = lambda w: pl.BlockSpec((tm, w), lambda i: (i, 0))
    full = lambda a: pl.BlockSpec(a.shape, lambda i: (0,) * a.ndim)
    return pl.pallas_call(
        functools.partial(_router_kernel, prompt_tiles=tiles_p),
        grid=(tiles_p + tiles_s,),
        in_specs=[pl.BlockSpec((tm, D_MODEL), lambda i: (jnp.minimum(i, tiles_p - 1), 0)),
                  pl.BlockSpec((tm, D_MODEL), lambda i: (jnp.maximum(i - tiles_p, 0), 0)),
                  full(norm_g), full(w_route), full(b_route)],
        out_specs=[row(D_MODEL), pl.BlockSpec((SUBLANES, tm), lambda i: (0, i)), row(LANES),
                   pl.BlockSpec((SUBLANES, LANES), lambda i: (0, 0))],
        out_shape=[jax.ShapeDtypeStruct((n, D_MODEL), F32),
                   jax.ShapeDtypeStruct((SUBLANES, n), F32),
                   jax.ShapeDtypeStruct((n, LANES), F32),
                   jax.ShapeDtypeStruct((SUBLANES, LANES), F32)],
        scratch_shapes=[pltpu.VMEM((1, LANES), F32)],
        compiler_params=pltpu.CompilerParams(dimension_semantics=("arbitrary",),
                                             vmem_limit_bytes=VMEM_LIMIT),
        name="router",
    )(h_p, h_s, norm_g, w_route, b_route)


def _row_copy(src_ref, src_row, dst_ref, dst_row, sem):
    return pltpu.make_async_copy(src_ref.at[pl.ds(src_row, 1), :], dst_ref.at[pl.ds(dst_row, 1), :], sem)


def _dispatch_kernel(p0_ref, p1_ref, hn_ref, xs_ref, sem):
    tm = hn_ref.shape[0]
    base = pl.program_id(0) * tm

    def issue(t, carry):
        _row_copy(hn_ref, t, xs_ref, p0_ref[base + t], sem).start()
        _row_copy(hn_ref, t, xs_ref, p1_ref[base + t], sem).start()
        return carry

    lax.fori_loop(0, tm, issue, 0, unroll=DMA_UNROLL)

    def drain(t, carry):
        _row_copy(hn_ref, t, xs_ref, p0_ref[base + t], sem).wait()
        _row_copy(hn_ref, t, xs_ref, p1_ref[base + t], sem).wait()
        return carry

    lax.fori_loop(0, tm, drain, 0, unroll=DMA_UNROLL)


def _dispatch(pos0, pos1, hn):
    n = hn.shape[0]
    tm = TOK_TILE
    return pl.pallas_call(
        _dispatch_kernel,
        grid_spec=pltpu.PrefetchScalarGridSpec(
            num_scalar_prefetch=2,
            grid=(n // tm,),
            in_specs=[pl.BlockSpec((tm, D_MODEL), lambda i, p0, p1: (i, 0))],
            out_specs=pl.BlockSpec(memory_space=pl.ANY),
            scratch_shapes=[pltpu.SemaphoreType.DMA(())],
        ),
        out_shape=jax.ShapeDtypeStruct((2 * n, D_MODEL), F32),
        compiler_params=pltpu.CompilerParams(dimension_semantics=("arbitrary",)),
        name="dispatch",
    )(pos0, pos1, hn)


def _combine_kernel(p0_ref, p1_ref, h_ref, wf_ref, g_ref, ys_ref, out_ref, b0_ref, b1_ref, sem, *, tok_offset):
    tm = h_ref.shape[0]
    base = tok_offset + pl.program_id(0) * tm

    def issue(t, carry):
        _row_copy(ys_ref, p0_ref[base + t], b0_ref, t, sem).start()
        _row_copy(ys_ref, p1_ref[base + t], b1_ref, t, sem).start()
        return carry

    lax.fori_loop(0, tm, issue, 0, unroll=DMA_UNROLL)

    def drain(t, carry):
        _row_copy(ys_ref, p0_ref[base + t], b0_ref, t, sem).wait()
        _row_copy(ys_ref, p1_ref[base + t], b1_ref, t, sem).wait()
        return carry

    lax.fori_loop(0, tm, drain, 0, unroll=DMA_UNROLL)

    y = h_ref[...] + (b0_ref[...] * wf_ref[:, 0:1] + b1_ref[...] * wf_ref[:, 1:2])
    out_ref[...] = y * lax.rsqrt(jnp.mean(y * y, axis=-1, keepdims=True) + EPS) * g_ref[...]


def _combine(pos0, pos1, h, wf, norm_g, ys, tok_offset):
    n_rows = h.shape[0]
    tm = TOK_TILE
    off = tok_offset // tm
    return pl.pallas_call(
        functools.partial(_combine_kernel, tok_offset=tok_offset),
        grid_spec=pltpu.PrefetchScalarGridSpec(
            num_scalar_prefetch=2,
            grid=(n_rows // tm,),
            in_specs=[pl.BlockSpec((tm, D_MODEL), lambda i, p0, p1: (i, 0)),
                      pl.BlockSpec((tm, LANES), lambda i, p0, p1: (i + off, 0)),
                      pl.BlockSpec((1, D_MODEL), lambda i, p0, p1: (0, 0)),
                      pl.BlockSpec(memory_space=pl.ANY)],
            out_specs=pl.BlockSpec((tm, D_MODEL), lambda i, p0, p1: (i, 0)),
            scratch_shapes=[pltpu.VMEM((tm, D_MODEL), F32), pltpu.VMEM((tm, D_MODEL), F32),
                            pltpu.SemaphoreType.DMA(())],
        ),
        out_shape=jax.ShapeDtypeStruct((n_rows, D_MODEL), F32),
        compiler_params=pltpu.CompilerParams(dimension_semantics=("arbitrary",),
                                             vmem_limit_bytes=VMEM_LIMIT),
        name="combine",
    )(pos0, pos1, h, wf, norm_g, ys)


def _expert_kernel(blk_ref, exp_ref, lo_ref, hi_ref, first_ref,
                   xs_ref, wg_ref, wu_ref, wd_ref, ys_ref, wgb_ref, wub_ref, wdb_ref):
    w = pl.program_id(0)
    e = exp_ref[w]
    changed = jnp.logical_or(w == 0, e != exp_ref[jnp.maximum(w - 1, 0)])

    @pl.when(changed)
    def _():
        wgb_ref[...] = wg_ref[0].astype(BF16)
        wub_ref[...] = wu_ref[0].astype(BF16)
        wdb_ref[...] = wd_ref[0].astype(BF16)

    lo = lo_ref[w]
    hi = hi_ref[w]

    @pl.when(hi > lo)
    def _():
        x = xs_ref[...].astype(BF16)
        gate = _dot(x, wgb_ref[...])
        hid = (gate * _sigmoid(gate)) * _dot(x, wub_ref[...])
        y = _dot(hid.astype(BF16), wdb_ref[...])
        rows = blk_ref[w] * ROW_BLOCK + lax.broadcasted_iota(jnp.int32, (ROW_BLOCK, 1), 0)
        y = jnp.where((rows >= lo) & (rows < hi), y, 0.0)

        @pl.when(first_ref[w] == 1)
        def _():
            ys_ref[...] = y

        @pl.when(first_ref[w] == 0)
        def _():
            ys_ref[...] = ys_ref[...] + y


def _experts(items, xs, w_gate, w_up, w_down):
    n_rows = xs.shape[0]
    n_items = items[0].shape[0]
    return pl.pallas_call(
        _expert_kernel,
        grid_spec=pltpu.PrefetchScalarGridSpec(
            num_scalar_prefetch=5,
            grid=(n_items,),
            in_specs=[pl.BlockSpec((ROW_BLOCK, D_MODEL), lambda w, blk, ex, lo, hi, fi: (blk[w], 0)),
                      pl.BlockSpec((1, D_MODEL, D_EXPERT), lambda w, blk, ex, lo, hi, fi: (ex[w], 0, 0)),
                      pl.BlockSpec((1, D_MODEL, D_EXPERT), lambda w, blk, ex, lo, hi, fi: (ex[w], 0, 0)),
                      pl.BlockSpec((1, D_EXPERT, D_MODEL), lambda w, blk, ex, lo, hi, fi: (ex[w], 0, 0))],
            out_specs=pl.BlockSpec((ROW_BLOCK, D_MODEL), lambda w, blk, ex, lo, hi, fi: (blk[w], 0)),
            scratch_shapes=[pltpu.VMEM((D_MODEL, D_EXPERT), BF16), pltpu.VMEM((D_MODEL, D_EXPERT), BF16),
                            pltpu.VMEM((D_EXPERT, D_MODEL), BF16)],
        ),
        out_shape=jax.ShapeDtypeStruct((n_rows, D_MODEL), F32),
        compiler_params=pltpu.CompilerParams(dimension_semantics=("arbitrary",),
                                             vmem_limit_bytes=VMEM_LIMIT),
        name="experts",
    )(*items, xs, w_gate, w_up, w_down)


def _work_items(counts, n_rows):
    n_blocks = n_rows // ROW_BLOCK
    n_items = n_blocks + N_EXPERTS - 1
    ends = jnp.cumsum(counts)
    starts = ends - counts
    blk_lo = starts // ROW_BLOCK
    blk_hi = jnp.maximum(ends - 1, 0) // ROW_BLOCK
    per_e = jnp.where(counts > 0, blk_hi - blk_lo + 1, 0)
    item_end = jnp.cumsum(per_e)
    item_start = item_end - per_e
    total = item_end[-1]
    w = jnp.arange(n_items, dtype=jnp.int32)
    live = w < total
    e = jnp.sum((jnp.minimum(w, total - 1)[:, None] >= item_end[None, :]).astype(jnp.int32), axis=1)
    e = jnp.minimum(e, N_EXPERTS - 1)
    sel = (e[:, None] == jnp.arange(N_EXPERTS, dtype=jnp.int32)[None, :]).astype(jnp.int32)
    pick = lambda table: jnp.sum(sel * table[None, :], axis=1)
    blk = jnp.where(live, pick(blk_lo) + (w - pick(item_start)), n_blocks - 1).astype(jnp.int32)
    lo = jnp.where(live, jnp.maximum(pick(starts), blk * ROW_BLOCK), 0).astype(jnp.int32)
    hi = jnp.where(live, jnp.minimum(pick(ends), (blk + 1) * ROW_BLOCK), 0).astype(jnp.int32)
    prev_blk = jnp.concatenate([jnp.full((1,), -1, jnp.int32), blk[:-1]])
    first = (live & (blk != prev_blk)).astype(jnp.int32)
    return (blk, e.astype(jnp.int32), lo, hi, first), starts


def kernel(x_prompt, x_sample, state_conv, state_gla, norm_mix, w_in, b_gates, w_gla_gate_up, b_gla_gate_up, conv_dw, conv_db, conv_ln_g, conv_ln_b, w_conv_out, gla_norm_g, w_gla_out, w_out, norm_ffn, w_router_group, b_router_group, w_router_expert, b_router_expert, w_expert_gate, w_expert_up, w_expert_down, norm_final):
    depth = norm_mix.shape[0]
    assert depth == 1
    l = 0
    bp, seq_p, _ = x_prompt.shape
    bs, seq_s, _ = x_sample.shape
    n_p, n_s = bp * seq_p, bs * seq_s
    n_tok = n_p + n_s
    row2 = lambda a: a.reshape(1, -1)

    wi = w_in[l]
    a0 = _C_GC
    w_pack = jnp.concatenate(
        [wi[:, :a0], wi[:, a0 + RANK:], jnp.pad(wi[:, a0:a0 + RANK], ((0, 0), (0, LANES - RANK)))],
        axis=1).astype(BF16)
    w_up_pad = jnp.pad(w_gla_gate_up[l], ((0, LANES - RANK), (0, 0))).astype(BF16)
    w_route = jnp.pad(jnp.concatenate([w_router_expert[l], w_router_group[l]], axis=1),
                      ((0, 0), (0, LANES - N_EXPERTS - N_GROUPS)))
    b_route = jnp.pad(jnp.concatenate([b_router_expert[l], b_router_group[l]]),
                      (0, LANES - N_EXPERTS - N_GROUPS)).reshape(1, LANES)
    w_pw = w_conv_out[l].astype(BF16)
    w_go = w_gla_out[l].astype(BF16)
    w_o = w_out[l].astype(BF16)
    conv_args = (conv_dw[l], row2(conv_db[l]), row2(conv_ln_g[l]), row2(conv_ln_b[l]), w_pw)
    proj_args = (row2(norm_mix[l]), w_pack, w_up_pad, row2(b_gla_gate_up[l]), row2(b_gates[l]))

    xp = x_prompt.reshape(n_p, D_MODEL)
    u, q, k, v, la, sr, gc, gg = _proj(xp, *proj_args)
    conv_out, conv_p = _conv(u.reshape(bp, seq_p, D_CONV), None, *conv_args)
    o, gla_p = _gla_prompt(q, k, la, v, bp, seq_p)
    h_p = _merge(xp, conv_out.reshape(n_p, D_MODEL), o, sr, gc, gg, row2(gla_norm_g[l]), w_go, w_o)

    xs_ = x_sample.reshape(n_s, D_MODEL)
    u, q, k, v, la, sr, gc, gg = _proj(xs_, *proj_args)
    conv_out, conv_s = _conv(u.reshape(bs, seq_s, D_CONV), state_conv[l], *conv_args)
    seqs = lambda a: a.reshape(bs, seq_s, a.shape[-1])
    o, gla_s = _gla_sample(seqs(q), seqs(k), seqs(la), seqs(v), state_gla[l])
    h_s = _merge(xs_, conv_out.reshape(n_s, D_MODEL), o.reshape(n_s, V_W), sr, gc, gg,
                 row2(gla_norm_g[l]), w_go, w_o)

    hn, meta_i, meta_f, cnt = _router(h_p, h_s, row2(norm_ffn[l]), w_route, b_route)
    counts = cnt[0, :N_EXPERTS].astype(jnp.int32)
    items, starts = _work_items(counts, 2 * n_tok)
    rec = meta_i.astype(jnp.int32)
    e_ids = jnp.arange(N_EXPERTS, dtype=jnp.int32)[:, None]
    start_of = lambda e: jnp.sum(jnp.where(e[None, :] == e_ids, starts[:, None], 0), axis=0)
    pos0 = start_of(rec[_L_E0]) + rec[_L_R0]
    pos1 = start_of(rec[_L_E1]) + rec[_L_R1]
    xs_sorted = _dispatch(pos0, pos1, hn)
    ys = _experts(items, xs_sorted, w_expert_gate[l], w_expert_up[l], w_expert_down[l])
    y_p = _combine(pos0, pos1, h_p, meta_f, row2(norm_final), ys, 0)
    y_s = _combine(pos0, pos1, h_s, meta_f, row2(norm_final), ys, n_p)

    return (y_p.reshape(bp, seq_p, D_MODEL), y_s.reshape(bs, seq_s, D_MODEL),
            conv_p[None], gla_p[None], conv_s[None], gla_s[None])
```

```python
import functools

import jax
import jax.numpy as jnp
from jax import lax
from jax.experimental import pallas as pl
from jax.experimental.pallas import tpu as pltpu

F32 = jnp.float32
BF16 = jnp.bfloat16

D_MODEL = 1024
D_CONV = 512
CONV_WIDTH = 31
CONV_HIST = CONV_WIDTH - 1
HEADS = 4
DK = 128
DV = 256
QK_W = HEADS * DK
V_W = HEADS * DV
RANK = 16
GATE_NORM = 16.0
N_GROUPS = 4
EXPERTS_PER_GROUP = 8
N_EXPERTS = 32
D_EXPERT = 512
EPS = 1e-6

LANES = 128
SUBLANES = 8
VMEM_LIMIT = 56 * 1024 * 1024

TOK_TILE = 256
GLA_CHUNK = 128
SAFE_LOG_DECAY = -80.0
CONV_ROWS = 64
CONV_HALO = 40
ROW_BLOCK = 256
DMA_UNROLL = 8
SAMPLE_SEQS = 32
SAMPLE_GLA_SEQS = 8
SAMPLE_PAD = 16

_C_GLU_A, _C_GLU_B = 0, 512
_C_Q, _C_K, _C_V, _C_R = 1024, 1536, 2048, 3072
_C_ALOW = 4096
_C_GATES = _C_ALOW + RANK


def _sigmoid(x):
    return jax.nn.sigmoid(x)


def _dot(a, b):
    return jnp.dot(a, b, preferred_element_type=F32)


def _proj_kernel(x_ref, g_ref, wm_ref, wa_ref, wg_ref, wup_ref, bup_ref, bg_ref,
                 u_ref, q_ref, k_ref, v_ref, la_ref, sr_ref, gc_ref, gg_ref):
    x = x_ref[...]
    hn = x * lax.rsqrt(jnp.mean(x * x, axis=-1, keepdims=True) + EPS) * g_ref[...]
    hb = hn.astype(BF16)

    def mm(lo, hi):
        return _dot(hb, wm_ref[:, lo:hi])

    u_ref[...] = mm(_C_GLU_A, _C_GLU_B) * _sigmoid(mm(_C_GLU_B, _C_Q))
    q_ref[...] = mm(_C_Q, _C_K) * (DK ** -0.5)
    k_ref[...] = mm(_C_K, _C_V)
    v_ref[...] = mm(_C_V, _C_R)
    r = mm(_C_R, _C_ALOW)
    sr_ref[...] = r * _sigmoid(r)
    gc_ref[...] = _sigmoid(_dot(hb, wg_ref[:, :D_MODEL]) + bg_ref[:, :D_MODEL])
    gg_ref[...] = _sigmoid(_dot(hb, wg_ref[:, D_MODEL:]) + bg_ref[:, D_MODEL:])
    a_low = _dot(hb, wa_ref[...])
    z = _dot(a_low.astype(BF16), wup_ref[...]) + bup_ref[...]
    la_ref[...] = (jnp.minimum(z, 0.0) - jnp.log1p(jnp.exp(-jnp.abs(z)))) * (1.0 / GATE_NORM)


def _proj(x2d, norm_g, w_main, w_alow, w_gates, w_up, b_up, b_gates):
    n = x2d.shape[0]
    tm = TOK_TILE
    row = lambda w: pl.BlockSpec((tm, w), lambda i: (i, 0))
    full = lambda a: pl.BlockSpec(a.shape, lambda i: (0,) * a.ndim)
    widths = (D_CONV, QK_W, QK_W, V_W, QK_W, V_W, D_MODEL, D_MODEL)
    consts = (norm_g, w_main, w_alow, w_gates, w_up, b_up, b_gates)
    return pl.pallas_call(
        _proj_kernel,
        grid=(n // tm,),
        in_specs=[row(D_MODEL)] + [full(a) for a in consts],
        out_specs=[row(w) for w in widths],
        out_shape=[jax.ShapeDtypeStruct((n, w), F32) for w in widths],
        compiler_params=pltpu.CompilerParams(dimension_semantics=("parallel",),
                                             vmem_limit_bytes=VMEM_LIMIT),
        name="proj",
    )(x2d, *consts)


def _conv_kernel(*refs, seq, nseq, has_state):
    if has_state:
        u_ref, st_ref, dw_ref, db_ref, lg_ref, lb_ref, wpw_ref, out_ref, ns_ref, full_ref, y_ref = refs
    else:
        u_ref, dw_ref, db_ref, lg_ref, lb_ref, wpw_ref, out_ref, ns_ref, full_ref, y_ref = refs
    pad = 32
    hist0 = pad - CONV_HIST
    n_rows = y_ref.shape[0] // nseq
    rt = min(CONV_ROWS, n_rows)
    win = rt + CONV_HALO
    tail = full_ref.shape[0] - (pad + seq)

    def one_seq(b, carry):
        full_ref[0:pad, :] = jnp.zeros((pad, D_CONV), F32)
        full_ref[pad + seq:, :] = jnp.zeros((tail, D_CONV), F32)
        if has_state:
            full_ref[hist0:pad, :] = st_ref[b]
        full_ref[pad:pad + seq, :] = u_ref[b]
        ns_ref[b] = full_ref[seq + hist0:seq + pad, :]
        y0 = b * n_rows

        def row_tile(i, carry):
            base = pl.multiple_of(i * rt, SUBLANES)
            for c in range(D_CONV // LANES):
                cs = slice(c * LANES, (c + 1) * LANES)
                w = full_ref[pl.ds(base, win), cs]
                acc = jnp.zeros((rt, LANES), F32)
                for s in range(SUBLANES):
                    ws = w if s == 0 else pltpu.roll(w, win - s, axis=0)
                    for a in range((CONV_HALO // SUBLANES)):
                        j = a * SUBLANES + s - hist0
                        if 0 <= j < CONV_WIDTH:
                            acc = acc + ws[a * SUBLANES:a * SUBLANES + rt, :] * dw_ref[j:j + 1, cs]
                y_ref[pl.ds(pl.multiple_of(y0 + base, SUBLANES), rt), cs] = acc + db_ref[:, cs]
            return carry

        return _loop(n_rows // rt, row_tile, carry)

    _loop(nseq, one_seq, 0)

    mt = min(TOK_TILE, nseq * n_rows)
    per_tile = mt // n_rows

    def norm_tile(i, carry):
        base = pl.multiple_of(i * mt, SUBLANES)
        y = y_ref[pl.ds(base, mt), :]
        mu = jnp.mean(y, axis=-1, keepdims=True)
        var = jnp.mean(jnp.square(y - mu), axis=-1, keepdims=True)
        yn = (y - mu) * lax.rsqrt(var + EPS) * lg_ref[...] + lb_ref[...]
        act = (yn * _sigmoid(yn)).astype(BF16)
        res = _dot(act, wpw_ref[...])
        if per_tile == 0:
            out_ref[0, pl.ds(base, mt), :] = res
        else:
            for j in range(per_tile):
                out_ref[i * per_tile + j] = res[j * n_rows:j * n_rows + seq]
        return carry

    _loop(nseq * n_rows // mt, norm_tile, 0)


def _loop(trips, body, carry):
    if trips == 1:
        return body(0, carry)
    return lax.fori_loop(0, trips, body, carry)


def _conv(u3, state, dw, db, lg, lb, w_pw, nseq):
    nb, seq, _ = u3.shape
    n_rows = -(-seq // SUBLANES) * SUBLANES
    assert nseq == 1 or (TOK_TILE % n_rows == 0 and (nseq * n_rows) % TOK_TILE == 0)
    has_state = state is not None
    full = lambda a: pl.BlockSpec(a.shape, lambda b: (0,) * a.ndim)
    per_b = lambda r, w: pl.BlockSpec((nseq, r, w), lambda b: (b, 0, 0))
    in_specs = [per_b(seq, D_CONV)]
    args = [u3]
    if has_state:
        in_specs.append(pl.BlockSpec((None, nseq, CONV_HIST, D_CONV), lambda b: (0, b, 0, 0)))
        args.append(state)
    in_specs += [full(dw), full(db), full(lg), full(lb), full(w_pw)]
    args += [dw, db, lg, lb, w_pw]
    return pl.pallas_call(
        functools.partial(_conv_kernel, seq=seq, nseq=nseq, has_state=has_state),
        grid=(nb // nseq,),
        in_specs=in_specs,
        out_specs=[per_b(seq, D_MODEL), per_b(CONV_HIST, D_CONV)],
        out_shape=[jax.ShapeDtypeStruct((nb, seq, D_MODEL), F32),
                   jax.ShapeDtypeStruct((nb, CONV_HIST, D_CONV), F32)],
        scratch_shapes=[pltpu.VMEM((n_rows + CONV_HALO, D_CONV), F32),
                        pltpu.VMEM((nseq * n_rows, D_CONV), F32)],
        compiler_params=pltpu.CompilerParams(dimension_semantics=("parallel",),
                                             vmem_limit_bytes=VMEM_LIMIT),
        name="conv_state" if has_state else "conv_fresh",
    )(*args)


def _token_step(s, a_col, k_col, q_col, v_row):
    s = s * a_col + k_col * v_row
    return s, jnp.sum(q_col * s, axis=0, keepdims=True)


def _gla_prompt_kernel(q_ref, k_ref, la_ref, v_ref, o_ref, sout_ref, s_ref):
    c = pl.program_id(1)
    n = GLA_CHUNK

    @pl.when(c == 0)
    def _():
        s_ref[...] = jnp.zeros_like(s_ref)

    g = la_ref[...]
    row = lax.broadcasted_iota(jnp.int32, (n, 1), 0)
    b = g
    sh = 1
    while sh < n:
        b = b + jnp.where(row >= sh, pltpu.roll(b, sh, axis=0), 0.0)
        sh *= 2
    b_last = b[n - 1:n, :]
    fast = jnp.min(b_last) > SAFE_LOG_DECAY

    @pl.when(fast)
    def _():
        r_i = lax.broadcasted_iota(jnp.int32, (n, n), 0)
        c_i = lax.broadcasted_iota(jnp.int32, (n, n), 1)
        for h in range(HEADS):
            ks = slice(h * DK, (h + 1) * DK)
            vs = slice(h * DV, (h + 1) * DV)
            bh = b[:, ks]
            bl = b_last[:, ks]
            kh = k_ref[:, ks]
            vh = v_ref[:, vs].astype(BF16)
            qp = (q_ref[:, ks] * jnp.exp(bh)).astype(BF16)
            kp = (kh * jnp.exp(-bh)).astype(BF16)
            kpp = kh * jnp.exp(bl - bh)
            att = lax.dot_general(qp, kp, (((1,), (1,)), ((), ())), preferred_element_type=F32)
            att = jnp.where(r_i >= c_i, att, 0.0).astype(BF16)
            s = s_ref[h]
            o_ref[:, vs] = _dot(qp, s.astype(BF16)) + _dot(att, vh)
            d_col = jnp.sum(jnp.where(r_i == c_i, jnp.exp(bl), 0.0), axis=1, keepdims=True)
            s_ref[h] = s * d_col + _dot(kpp.T.astype(BF16), vh)

    @pl.when(jnp.logical_not(fast))
    def _():
        lane = lax.broadcasted_iota(jnp.int32, (1, n), 1)
        for h in range(HEADS):
            ks = slice(h * DK, (h + 1) * DK)
            vs = slice(h * DV, (h + 1) * DV)
            q_t = q_ref[:, ks].T
            k_t = k_ref[:, ks].T
            a_t = jnp.exp(g[:, ks]).T

            def body(t, s):
                m = lane == t
                col = lambda x: jnp.sum(jnp.where(m, x, 0.0), axis=1, keepdims=True)
                s, o_row = _token_step(s, col(a_t), col(k_t), col(q_t), v_ref[pl.ds(t, 1), vs])
                o_ref[pl.ds(t, 1), vs] = o_row
                return s

            s_ref[h] = lax.fori_loop(0, n, body, s_ref[h])

    @pl.when(c == pl.num_programs(1) - 1)
    def _():
        sout_ref[0] = s_ref[...]


def _gla_prompt(q, k, la, v, nb, seq):
    nc = seq // GLA_CHUNK
    tok = lambda w: pl.BlockSpec((GLA_CHUNK, w), lambda b, c: (b * nc + c, 0))
    return pl.pallas_call(
        _gla_prompt_kernel,
        grid=(nb, nc),
        in_specs=[tok(QK_W), tok(QK_W), tok(QK_W), tok(V_W)],
        out_specs=[tok(V_W), pl.BlockSpec((1, HEADS, DK, DV), lambda b, c: (b, 0, 0, 0))],
        out_shape=[jax.ShapeDtypeStruct((nb * seq, V_W), F32),
                   jax.ShapeDtypeStruct((nb, HEADS, DK, DV), F32)],
        scratch_shapes=[pltpu.VMEM((HEADS, DK, DV), F32)],
        compiler_params=pltpu.CompilerParams(dimension_semantics=("parallel", "arbitrary"),
                                             vmem_limit_bytes=VMEM_LIMIT),
        name="gla_prompt",
    )(q, k, la, v)


def _gla_sample_kernel(q_ref, k_ref, la_ref, v_ref, s_ref, o_ref, sout_ref, *, seq):
    p = SAMPLE_PAD
    row = lax.broadcasted_iota(jnp.int32, (p, 1), 0)

    def padded(x):
        return jnp.concatenate([x, jnp.zeros((p - seq, x.shape[1]), F32)], axis=0)

    for b in range(q_ref.shape[0]):
        q, k, v, g = padded(q_ref[b]), padded(k_ref[b]), padded(v_ref[b]), padded(la_ref[b])
        bc = g
        sh = 1
        while sh < seq:
            bc = bc + jnp.where(row >= sh, pltpu.roll(bc, sh, axis=0), 0.0)
            sh *= 2
        b_last = bc[seq - 1:seq, :]
        bc = jnp.where(row < seq, bc, b_last)
        qp = (q * jnp.exp(bc)).astype(BF16)
        kpp_t = (k * jnp.exp(b_last - bc)).T.astype(BF16)
        d_cols = jnp.broadcast_to(jnp.exp(b_last), (SUBLANES, QK_W)).T
        vb = v.astype(BF16)

        o_in = [jnp.zeros((p, DV), F32) for _ in range(HEADS)]
        for d in range(seq):
            k_d = k if d == 0 else pltpu.roll(k, d, axis=0)
            v_d = v if d == 0 else pltpu.roll(v, d, axis=0)
            b_d = bc if d == 0 else pltpu.roll(bc, d, axis=0)
            pair = q * k_d * jnp.exp(jnp.where(row >= d, bc - b_d, -jnp.inf))
            for h in range(HEADS):
                att = jnp.sum(pair[:, h * DK:(h + 1) * DK], axis=1, keepdims=True)
                o_in[h] = o_in[h] + att * v_d[:, h * DV:(h + 1) * DV]

        for h in range(HEADS):
            ks = slice(h * DK, (h + 1) * DK)
            vs = slice(h * DV, (h + 1) * DV)
            s = s_ref[b, h]
            o = _dot(qp[:, ks], s.astype(BF16)) + o_in[h]
            o_ref[b, :, vs] = o[:seq]
            sout_ref[b, h] = s * d_cols[ks, 0:1] + _dot(kpp_t[ks, :], vb[:, vs])


def _gla_sample(q3, k3, la3, v3, state):
    nb, seq, _ = q3.shape
    assert seq <= SUBLANES
    ns = SAMPLE_GLA_SEQS
    col = pl.BlockSpec((ns, seq, QK_W), lambda b: (b, 0, 0))
    tok = pl.BlockSpec((ns, seq, V_W), lambda b: (b, 0, 0))
    st = pl.BlockSpec((ns, HEADS, DK, DV), lambda b: (b, 0, 0, 0))
    st_in = pl.BlockSpec((None, ns, HEADS, DK, DV), lambda b: (0, b, 0, 0, 0))
    return pl.pallas_call(
        functools.partial(_gla_sample_kernel, seq=seq),
        grid=(nb // ns,),
        in_specs=[col, col, col, tok, st_in],
        out_specs=[tok, st],
        out_shape=[jax.ShapeDtypeStruct((nb, seq, V_W), F32),
                   jax.ShapeDtypeStruct((nb, HEADS, DK, DV), F32)],
        compiler_params=pltpu.CompilerParams(dimension_semantics=("parallel",),
                                             vmem_limit_bytes=VMEM_LIMIT),
        name="gla_sample",
    )(q3, k3, la3, v3, state)


def _merge_kernel(x_ref, co_ref, o_ref, sr_ref, gc_ref, gg_ref, gn_ref, wgo_ref, wo_ref, h_ref):
    gla_out = None
    for h in range(HEADS):
        vs = slice(h * DV, (h + 1) * DV)
        oh = o_ref[:, vs]
        on = oh * lax.rsqrt(jnp.mean(oh * oh, axis=-1, keepdims=True) + EPS) * gn_ref[:, vs]
        part = _dot((on * sr_ref[:, vs]).astype(BF16), wgo_ref[vs, :])
        gla_out = part if gla_out is None else gla_out + part
    mix = (gc_ref[...] * co_ref[...] + gg_ref[...] * gla_out).astype(BF16)
    h_ref[...] = x_ref[...] + _dot(mix, wo_ref[...])


def _merge(x2d, conv_out, o, sr, gc, gg, gn, w_gla_out, w_out):
    n = x2d.shape[0]
    tm = TOK_TILE
    row = pl.BlockSpec((tm, D_MODEL), lambda i: (i, 0))
    full = lambda a: pl.BlockSpec(a.shape, lambda i: (0,) * a.ndim)
    return pl.pallas_call(
        _merge_kernel,
        grid=(n // tm,),
        in_specs=[row] * 6 + [full(gn), full(w_gla_out), full(w_out)],
        out_specs=row,
        out_shape=jax.ShapeDtypeStruct((n, D_MODEL), F32),
        compiler_params=pltpu.CompilerParams(dimension_semantics=("parallel",),
                                             vmem_limit_bytes=VMEM_LIMIT),
        name="merge",
    )(x2d, conv_out, o, sr, gc, gg, gn, w_gla_out, w_out)


_L_E0, _L_E1, _L_R0, _L_R1 = 0, 1, 2, 3
_GROUP_LANE0 = N_EXPERTS


def _router_kernel(hp_ref, hs_ref, g_ref, wr_ref, br_ref, hn_ref, mi_ref, mf_ref, cnt_ref, carry_ref, *,
                   prompt_tiles):
    i = pl.program_id(0)
    tm = hp_ref.shape[0]

    @pl.when(i == 0)
    def _():
        carry_ref[...] = jnp.zeros_like(carry_ref)

    x = jnp.where(i < prompt_tiles, hp_ref[...], hs_ref[...])
    hn = x * lax.rsqrt(jnp.mean(x * x, axis=-1, keepdims=True) + EPS) * g_ref[...]
    hn_ref[...] = hn
    hi = hn.astype(BF16)
    lo = (hn - hi.astype(F32)).astype(BF16)
    logits = _dot(hi, wr_ref[0]) + (_dot(lo, wr_ref[0]) + _dot(hi, wr_ref[1])) + br_ref[...]
    lane = lax.broadcasted_iota(jnp.int32, (tm, LANES), 1)
    lane_f = lane.astype(F32)
    group_f = jnp.right_shift(lane, 3).astype(F32)
    big = jnp.float32(LANES)
    neg = jnp.float32(-jnp.inf)

    is_g = (lane >= _GROUP_LANE0) & (lane < _GROUP_LANE0 + N_GROUPS)
    lg = jnp.where(is_g, logits, neg)
    mg = jnp.max(lg, axis=1, keepdims=True)
    p_sel = 1.0 / jnp.sum(jnp.exp(lg - mg), axis=1, keepdims=True)
    g_idx = jnp.min(jnp.where(lg == mg, lane_f, big), axis=1, keepdims=True) - _GROUP_LANE0

    is_e = (lane < N_EXPERTS) & (group_f == g_idx)
    le = jnp.where(is_e, logits, neg)
    v0 = jnp.max(le, axis=1, keepdims=True)
    i0 = jnp.min(jnp.where(le == v0, lane_f, big), axis=1, keepdims=True)
    le1 = jnp.where(lane_f == i0, neg, le)
    v1 = jnp.max(le1, axis=1, keepdims=True)
    i1 = jnp.min(jnp.where(le1 == v1, lane_f, big), axis=1, keepdims=True)
    e1 = jnp.exp(v1 - v0)
    den = 1.0 + e1
    w0 = p_sel * (1.0 / den)
    w1 = p_sel * (e1 / den)

    hit0 = lane_f == i0
    hit1 = lane_f == i1
    cnt = jnp.where(hit0 | hit1, 1.0, 0.0)
    r_i = lax.broadcasted_iota(jnp.int32, (tm, tm), 0)
    c_i = lax.broadcasted_iota(jnp.int32, (tm, tm), 1)
    before = jnp.where(r_i > c_i, 1.0, 0.0).astype(BF16)
    seen = _dot(before, cnt.astype(BF16)) + carry_ref[...]
    rank0 = jnp.sum(jnp.where(hit0, seen, 0.0), axis=1, keepdims=True)
    rank1 = jnp.sum(jnp.where(hit1, seen, 0.0), axis=1, keepdims=True)
    carry_ref[...] = carry_ref[...] + jnp.sum(cnt, axis=0, keepdims=True)
    cnt_ref[...] = jnp.broadcast_to(carry_ref[...], cnt_ref.shape)

    rec = jnp.where(lane == _L_E0, i0, 0.0)
    rec = jnp.where(lane == _L_E1, i1, rec)
    rec = jnp.where(lane == _L_R0, rank0, rec)
    rec = jnp.where(lane == _L_R1, rank1, rec)
    mi_ref[...] = rec.T[:SUBLANES, :]
    mf_ref[...] = jnp.where(lane == 0, w0, jnp.where(lane == 1, w1, 0.0))


def _router(h_p, h_s, norm_g, w_route, b_route):
    tm = TOK_TILE
    tiles_p, tiles_s = h_p.shape[0] // tm, h_s.shape[0] // tm
    n = h_p.shape[0] + h_s.shape[0]
    row = lambda w: pl.BlockSpec((tm, w), lambda i: (i, 0))
    full = lambda a: pl.BlockSpec(a.shape, lambda i: (0,) * a.ndim)
    return pl.pallas_call(
        functools.partial(_router_kernel, prompt_tiles=tiles_p),
        grid=(tiles_p + tiles_s,),
        in_specs=[pl.BlockSpec((tm, D_MODEL), lambda i: (jnp.minimum(i, tiles_p - 1), 0)),
                  pl.BlockSpec((tm, D_MODEL), lambda i: (jnp.maximum(i - tiles_p, 0), 0)),
                  full(norm_g), full(w_route), full(b_route)],
        out_specs=[row(D_MODEL), pl.BlockSpec((SUBLANES, tm), lambda i: (0, i)), row(LANES),
                   pl.BlockSpec((SUBLANES, LANES), lambda i: (0, 0))],
        out_shape=[jax.ShapeDtypeStruct((n, D_MODEL), F32),
                   jax.ShapeDtypeStruct((SUBLANES, n), F32),
                   jax.ShapeDtypeStruct((n, LANES), F32),
                   jax.ShapeDtypeStruct((SUBLANES, LANES), F32)],
        scratch_shapes=[pltpu.VMEM((1, LANES), F32)],
        compiler_params=pltpu.CompilerParams(dimension_semantics=("arbitrary",),
                                             vmem_limit_bytes=VMEM_LIMIT),
        name="router",
    )(h_p, h_s, norm_g, w_route, b_route)


def _row_copy(src_ref, src_row, dst_ref, dst_row, sem):
    return pltpu.make_async_copy(src_ref.at[pl.ds(src_row, 1), :], dst_ref.at[pl.ds(dst_row, 1), :], sem)


def _dispatch_kernel(p0_ref, p1_ref, hn_ref, xs_ref, sem):
    tm = hn_ref.shape[0]
    base = pl.program_id(0) * tm

    def issue(t, carry):
        _row_copy(hn_ref, t, xs_ref, p0_ref[base + t], sem).start()
        _row_copy(hn_ref, t, xs_ref, p1_ref[base + t], sem).start()
        return carry

    lax.fori_loop(0, tm, issue, 0, unroll=DMA_UNROLL)

    def drain(t, carry):
        _row_copy(hn_ref, t, xs_ref, p0_ref[base + t], sem).wait()
        _row_copy(hn_ref, t, xs_ref, p1_ref[base + t], sem).wait()
        return carry

    lax.fori_loop(0, tm, drain, 0, unroll=DMA_UNROLL)


def _dispatch(pos0, pos1, hn):
    n = hn.shape[0]
    tm = TOK_TILE
    return pl.pallas_call(
        _dispatch_kernel,
        grid_spec=pltpu.PrefetchScalarGridSpec(
            num_scalar_prefetch=2,
            grid=(n // tm,),
            in_specs=[pl.BlockSpec((tm, D_MODEL), lambda i, p0, p1: (i, 0))],
            out_specs=pl.BlockSpec(memory_space=pl.ANY),
            scratch_shapes=[pltpu.SemaphoreType.DMA(())],
        ),
        out_shape=jax.ShapeDtypeStruct((2 * n, D_MODEL), F32),
        compiler_params=pltpu.CompilerParams(dimension_semantics=("arbitrary",)),
        name="dispatch",
    )(pos0, pos1, hn)


def _combine_kernel(p0_ref, p1_ref, h_ref, wf_ref, g_ref, ys_ref, out_ref, b0_ref, b1_ref, sem, *, tok_offset):
    tm = h_ref.shape[0]
    base = tok_offset + pl.program_id(0) * tm

    def issue(t, carry):
        _row_copy(ys_ref, p0_ref[base + t], b0_ref, t, sem).start()
        _row_copy(ys_ref, p1_ref[base + t], b1_ref, t, sem).start()
        return carry

    lax.fori_loop(0, tm, issue, 0, unroll=DMA_UNROLL)

    def drain(t, carry):
        _row_copy(ys_ref, p0_ref[base + t], b0_ref, t, sem).wait()
        _row_copy(ys_ref, p1_ref[base + t], b1_ref, t, sem).wait()
        return carry

    lax.fori_loop(0, tm, drain, 0, unroll=DMA_UNROLL)

    y = h_ref[...] + (b0_ref[...] * wf_ref[:, 0:1] + b1_ref[...] * wf_ref[:, 1:2])
    out_ref[...] = y * lax.rsqrt(jnp.mean(y * y, axis=-1, keepdims=True) + EPS) * g_ref[...]


def _combine(pos0, pos1, h, wf, norm_g, ys, tok_offset):
    n_rows = h.shape[0]
    tm = TOK_TILE
    off = tok_offset // tm
    return pl.pallas_call(
        functools.partial(_combine_kernel, tok_offset=tok_offset),
        grid_spec=pltpu.PrefetchScalarGridSpec(
            num_scalar_prefetch=2,
            grid=(n_rows // tm,),
            in_specs=[pl.BlockSpec((tm, D_MODEL), lambda i, p0, p1: (i, 0)),
                      pl.BlockSpec((tm, LANES), lambda i, p0, p1: (i + off, 0)),
                      pl.BlockSpec((1, D_MODEL), lambda i, p0, p1: (0, 0)),
                      pl.BlockSpec(memory_space=pl.ANY)],
            out_specs=pl.BlockSpec((tm, D_MODEL), lambda i, p0, p1: (i, 0)),
            scratch_shapes=[pltpu.VMEM((tm, D_MODEL), F32), pltpu.VMEM((tm, D_MODEL), F32),
                            pltpu.SemaphoreType.DMA(())],
        ),
        out_shape=jax.ShapeDtypeStruct((n_rows, D_MODEL), F32),
        compiler_params=pltpu.CompilerParams(dimension_semantics=("arbitrary",),
                                             vmem_limit_bytes=VMEM_LIMIT),
        name="combine",
    )(pos0, pos1, h, wf, norm_g, ys)


def _expert_kernel(blk_ref, exp_ref, lo_ref, hi_ref, first_ref,
                   xs_ref, wg_ref, wu_ref, wd_ref, ys_ref, wgb_ref, wub_ref, wdb_ref):
    w = pl.program_id(0)
    e = exp_ref[w]
    changed = jnp.logical_or(w == 0, e != exp_ref[jnp.maximum(w - 1, 0)])

    @pl.when(changed)
    def _():
        wgb_ref[...] = wg_ref[0].astype(BF16)
        wub_ref[...] = wu_ref[0].astype(BF16)
        wdb_ref[...] = wd_ref[0].astype(BF16)

    lo = lo_ref[w]
    hi = hi_ref[w]

    @pl.when(hi > lo)
    def _():
        x = xs_ref[...].astype(BF16)
        gate = _dot(x, wgb_ref[...])
        hid = (gate * _sigmoid(gate)) * _dot(x, wub_ref[...])
        y = _dot(hid.astype(BF16), wdb_ref[...])
        rows = blk_ref[w] * ROW_BLOCK + lax.broadcasted_iota(jnp.int32, (ROW_BLOCK, 1), 0)
        y = jnp.where((rows >= lo) & (rows < hi), y, 0.0)

        @pl.when(first_ref[w] == 1)
        def _():
            ys_ref[...] = y

        @pl.when(first_ref[w] == 0)
        def _():
            ys_ref[...] = ys_ref[...] + y


def _experts(items, xs, w_gate, w_up, w_down):
    n_rows = xs.shape[0]
    n_items = items[0].shape[0]
    return pl.pallas_call(
        _expert_kernel,
        grid_spec=pltpu.PrefetchScalarGridSpec(
            num_scalar_prefetch=5,
            grid=(n_items,),
            in_specs=[pl.BlockSpec((ROW_BLOCK, D_MODEL), lambda w, blk, ex, lo, hi, fi: (blk[w], 0)),
                      pl.BlockSpec((1, D_MODEL, D_EXPERT), lambda w, blk, ex, lo, hi, fi: (ex[w], 0, 0)),
                      pl.BlockSpec((1, D_MODEL, D_EXPERT), lambda w, blk, ex, lo, hi, fi: (ex[w], 0, 0)),
                      pl.BlockSpec((1, D_EXPERT, D_MODEL), lambda w, blk, ex, lo, hi, fi: (ex[w], 0, 0))],
            out_specs=pl.BlockSpec((ROW_BLOCK, D_MODEL), lambda w, blk, ex, lo, hi, fi: (blk[w], 0)),
            scratch_shapes=[pltpu.VMEM((D_MODEL, D_EXPERT), BF16), pltpu.VMEM((D_MODEL, D_EXPERT), BF16),
                            pltpu.VMEM((D_EXPERT, D_MODEL), BF16)],
        ),
        out_shape=jax.ShapeDtypeStruct((n_rows, D_MODEL), F32),
        compiler_params=pltpu.CompilerParams(dimension_semantics=("arbitrary",),
                                             vmem_limit_bytes=VMEM_LIMIT),
        name="experts",
    )(*items, xs, w_gate, w_up, w_down)


def _work_items(counts, n_rows):
    n_blocks = n_rows // ROW_BLOCK
    n_items = n_blocks + N_EXPERTS - 1
    ends = jnp.cumsum(counts)
    starts = ends - counts
    blk_lo = starts // ROW_BLOCK
    blk_hi = jnp.maximum(ends - 1, 0) // ROW_BLOCK
    per_e = jnp.where(counts > 0, blk_hi - blk_lo + 1, 0)
    item_end = jnp.cumsum(per_e)
    item_start = item_end - per_e
    total = item_end[-1]
    w = jnp.arange(n_items, dtype=jnp.int32)
    live = w < total
    e = jnp.sum((jnp.minimum(w, total - 1)[:, None] >= item_end[None, :]).astype(jnp.int32), axis=1)
    e = jnp.minimum(e, N_EXPERTS - 1)
    sel = (e[:, None] == jnp.arange(N_EXPERTS, dtype=jnp.int32)[None, :]).astype(jnp.int32)
    pick = lambda table: jnp.sum(sel * table[None, :], axis=1)
    blk = jnp.where(live, pick(blk_lo) + (w - pick(item_start)), n_blocks - 1).astype(jnp.int32)
    lo = jnp.where(live, jnp.maximum(pick(starts), blk * ROW_BLOCK), 0).astype(jnp.int32)
    hi = jnp.where(live, jnp.minimum(pick(ends), (blk + 1) * ROW_BLOCK), 0).astype(jnp.int32)
    prev_blk = jnp.concatenate([jnp.full((1,), -1, jnp.int32), blk[:-1]])
    first = (live & (blk != prev_blk)).astype(jnp.int32)
    return (blk, e.astype(jnp.int32), lo, hi, first), starts


def kernel(x_prompt, x_sample, state_conv, state_gla, norm_mix, w_in, b_gates, w_gla_gate_up, b_gla_gate_up, conv_dw, conv_db, conv_ln_g, conv_ln_b, w_conv_out, gla_norm_g, w_gla_out, w_out, norm_ffn, w_router_group, b_router_group, w_router_expert, b_router_expert, w_expert_gate, w_expert_up, w_expert_down, norm_final):
    depth = norm_mix.shape[0]
    assert depth == 1
    l = 0
    bp, seq_p, _ = x_prompt.shape
    bs, seq_s, _ = x_sample.shape
    n_p, n_s = bp * seq_p, bs * seq_s
    n_tok = n_p + n_s
    row2 = lambda a: a.reshape(1, -1)

    wi = w_in[l]
    w_main = wi[:, :_C_ALOW].astype(BF16)
    w_alow = jnp.pad(wi[:, _C_ALOW:_C_GATES], ((0, 0), (0, LANES - RANK))).astype(BF16)
    w_gates = wi[:, _C_GATES:].astype(BF16)
    w_up_pad = jnp.pad(w_gla_gate_up[l], ((0, LANES - RANK), (0, 0))).astype(BF16)
    w_route = jnp.pad(jnp.concatenate([w_router_expert[l], w_router_group[l]], axis=1),
                      ((0, 0), (0, LANES - N_EXPERTS - N_GROUPS)))
    w_route_hi = w_route.astype(BF16)
    w_route = jnp.stack([w_route_hi, (w_route - w_route_hi.astype(F32)).astype(BF16)])
    b_route = jnp.pad(jnp.concatenate([b_router_expert[l], b_router_group[l]]),
                      (0, LANES - N_EXPERTS - N_GROUPS)).reshape(1, LANES)
    w_pw = w_conv_out[l].astype(BF16)
    w_go = w_gla_out[l].astype(BF16)
    w_o = w_out[l].astype(BF16)
    conv_args = (conv_dw[l], row2(conv_db[l]), row2(conv_ln_g[l]), row2(conv_ln_b[l]), w_pw)
    proj_args = (row2(norm_mix[l]), w_main, w_alow, w_gates, w_up_pad, row2(b_gla_gate_up[l]),
                 row2(b_gates[l]))

    xp = x_prompt.reshape(n_p, D_MODEL)
    u, q, k, v, la, sr, gc, gg = _proj(xp, *proj_args)
    conv_out, conv_p = _conv(u.reshape(bp, seq_p, D_CONV), None, *conv_args, nseq=1)
    o, gla_p = _gla_prompt(q, k, la, v, bp, seq_p)
    h_p = _merge(xp, conv_out.reshape(n_p, D_MODEL), o, sr, gc, gg, row2(gla_norm_g[l]), w_go, w_o)

    xs_ = x_sample.reshape(n_s, D_MODEL)
    u, q, k, v, la, sr, gc, gg = _proj(xs_, *proj_args)
    conv_out, conv_s = _conv(u.reshape(bs, seq_s, D_CONV), state_conv, *conv_args, nseq=SAMPLE_SEQS)
    seqs = lambda a: a.reshape(bs, seq_s, a.shape[-1])
    o, gla_s = _gla_sample(seqs(q), seqs(k), seqs(la), seqs(v), state_gla)
    h_s = _merge(xs_, conv_out.reshape(n_s, D_MODEL), o.reshape(n_s, V_W), sr, gc, gg,
                 row2(gla_norm_g[l]), w_go, w_o)

    hn, meta_i, meta_f, cnt = _router(h_p, h_s, row2(norm_ffn[l]), w_route, b_route)
    counts = cnt[0, :N_EXPERTS].astype(jnp.int32)
    items, starts = _work_items(counts, 2 * n_tok)
    rec = meta_i.astype(jnp.int32)
    e_ids = jnp.arange(N_EXPERTS, dtype=jnp.int32)[:, None]
    start_of = lambda e: jnp.sum(jnp.where(e[None, :] == e_ids, starts[:, None], 0), axis=0)
    pos0 = start_of(rec[_L_E0]) + rec[_L_R0]
    pos1 = start_of(rec[_L_E1]) + rec[_L_R1]
    xs_sorted = _dispatch(pos0, pos1, hn)
    ys = _experts(items, xs_sorted, w_expert_gate[l], w_expert_up[l], w_expert_down[l])
    y_p = _combine(pos0, pos1, h_p, meta_f, row2(norm_final), ys, 0)
    y_s = _combine(pos0, pos1, h_s, meta_f, row2(norm_final), ys, n_p)

    return (y_p.reshape(bp, seq_p, D_MODEL), y_s.reshape(bs, seq_s, D_MODEL),
            conv_p[None], gla_p[None], conv_s[None], gla_s[None])
```

```python
import functools

import jax
import jax.numpy as jnp
from jax import lax
from jax.experimental import pallas as pl
from jax.experimental.pallas import tpu as pltpu

F32 = jnp.float32
BF16 = jnp.bfloat16

D_MODEL = 1024
D_CONV = 512
CONV_WIDTH = 31
CONV_HIST = CONV_WIDTH - 1
HEADS = 4
DK = 128
DV = 256
QK_W = HEADS * DK
V_W = HEADS * DV
RANK = 16
GATE_NORM = 16.0
N_GROUPS = 4
EXPERTS_PER_GROUP = 8
N_EXPERTS = 32
D_EXPERT = 512
EPS = 1e-6

LANES = 128
SUBLANES = 8
VMEM_LIMIT = 56 * 1024 * 1024

TOK_TILE = 256
GLA_CHUNK = 128
GLA_STEP = 256
SAFE_LOG_DECAY = -80.0
CONV_ROWS = 64
CONV_HALO = 40
ROW_BLOCK = 256
DMA_UNROLL = 8
SAMPLE_SEQS = 32
SAMPLE_GLA_SEQS = 8
SAMPLE_PAD = 16

_C_GLU_A, _C_GLU_B = 0, 512
_C_Q, _C_K, _C_V, _C_R = 1024, 1536, 2048, 3072
_C_ALOW = 4096
_C_GATES = _C_ALOW + RANK


def _sigmoid(x):
    return jax.nn.sigmoid(x)


def _dot(a, b):
    return jnp.dot(a, b, preferred_element_type=F32)


def _proj_kernel(x_ref, g_ref, wm_ref, wa_ref, wg_ref, wup_ref, bup_ref, bg_ref,
                 u_ref, q_ref, k_ref, v_ref, la_ref, sr_ref, gc_ref, gg_ref):
    x = x_ref[...]
    hn = x * lax.rsqrt(jnp.mean(x * x, axis=-1, keepdims=True) + EPS) * g_ref[...]
    hb = hn.astype(BF16)

    def mm(lo, hi):
        return _dot(hb, wm_ref[:, lo:hi])

    u_ref[...] = mm(_C_GLU_A, _C_GLU_B) * _sigmoid(mm(_C_GLU_B, _C_Q))
    q_ref[...] = mm(_C_Q, _C_K) * (DK ** -0.5)
    k_ref[...] = mm(_C_K, _C_V)
    v_ref[...] = mm(_C_V, _C_R).astype(BF16)
    r = mm(_C_R, _C_ALOW)
    sr_ref[...] = (r * _sigmoid(r)).astype(BF16)
    gc_ref[...] = _sigmoid(_dot(hb, wg_ref[:, :D_MODEL]) + bg_ref[:, :D_MODEL]).astype(BF16)
    gg_ref[...] = _sigmoid(_dot(hb, wg_ref[:, D_MODEL:]) + bg_ref[:, D_MODEL:]).astype(BF16)
    a_low = _dot(hb, wa_ref[...])
    z = _dot(a_low.astype(BF16), wup_ref[...]) + bup_ref[...]
    la_ref[...] = (jnp.minimum(z, 0.0) - jnp.log1p(jnp.exp(-jnp.abs(z)))) * (1.0 / GATE_NORM)


def _proj(x2d, norm_g, w_main, w_alow, w_gates, w_up, b_up, b_gates):
    n = x2d.shape[0]
    tm = TOK_TILE
    row = lambda w: pl.BlockSpec((tm, w), lambda i: (i, 0))
    full = lambda a: pl.BlockSpec(a.shape, lambda i: (0,) * a.ndim)
    widths = (D_CONV, QK_W, QK_W, V_W, QK_W, V_W, D_MODEL, D_MODEL)
    dtypes = (F32, F32, F32, BF16, F32, BF16, BF16, BF16)
    consts = (norm_g, w_main, w_alow, w_gates, w_up, b_up, b_gates)
    return pl.pallas_call(
        _proj_kernel,
        grid=(n // tm,),
        in_specs=[row(D_MODEL)] + [full(a) for a in consts],
        out_specs=[row(w) for w in widths],
        out_shape=[jax.ShapeDtypeStruct((n, w), dt) for w, dt in zip(widths, dtypes)],
        compiler_params=pltpu.CompilerParams(dimension_semantics=("parallel",),
                                             vmem_limit_bytes=VMEM_LIMIT),
        name="proj",
    )(x2d, *consts)


def _conv_kernel(*refs, seq, nseq, has_state):
    if has_state:
        u_ref, st_ref, dw_ref, db_ref, lg_ref, lb_ref, wpw_ref, out_ref, ns_ref, full_ref, y_ref = refs
    else:
        u_ref, dw_ref, db_ref, lg_ref, lb_ref, wpw_ref, out_ref, ns_ref, full_ref, y_ref = refs
    pad = 32
    hist0 = pad - CONV_HIST
    n_rows = y_ref.shape[0] // nseq
    rt = min(CONV_ROWS, n_rows)
    win = rt + CONV_HALO
    tail = full_ref.shape[0] - (pad + seq)

    def one_seq(b, carry):
        full_ref[0:pad, :] = jnp.zeros((pad, D_CONV), F32)
        full_ref[pad + seq:, :] = jnp.zeros((tail, D_CONV), F32)
        if has_state:
            full_ref[hist0:pad, :] = st_ref[b]
        full_ref[pad:pad + seq, :] = u_ref[b]
        ns_ref[b] = full_ref[seq + hist0:seq + pad, :]
        y0 = b * n_rows

        def row_tile(i, carry):
            base = pl.multiple_of(i * rt, SUBLANES)
            for c in range(D_CONV // LANES):
                cs = slice(c * LANES, (c + 1) * LANES)
                w = full_ref[pl.ds(base, win), cs]
                acc = jnp.zeros((rt, LANES), F32)
                for s in range(SUBLANES):
                    ws = w if s == 0 else pltpu.roll(w, win - s, axis=0)
                    for a in range((CONV_HALO // SUBLANES)):
                        j = a * SUBLANES + s - hist0
                        if 0 <= j < CONV_WIDTH:
                            acc = acc + ws[a * SUBLANES:a * SUBLANES + rt, :] * dw_ref[j:j + 1, cs]
                y_ref[pl.ds(pl.multiple_of(y0 + base, SUBLANES), rt), cs] = acc + db_ref[:, cs]
            return carry

        return _loop(n_rows // rt, row_tile, carry)

    _loop(nseq, one_seq, 0)

    mt = min(TOK_TILE, nseq * n_rows)
    per_tile = mt // n_rows

    def norm_tile(i, carry):
        base = pl.multiple_of(i * mt, SUBLANES)
        y = y_ref[pl.ds(base, mt), :]
        mu = jnp.mean(y, axis=-1, keepdims=True)
        var = jnp.mean(jnp.square(y - mu), axis=-1, keepdims=True)
        yn = (y - mu) * lax.rsqrt(var + EPS) * lg_ref[...] + lb_ref[...]
        act = (yn * _sigmoid(yn)).astype(BF16)
        res = _dot(act, wpw_ref[...])
        if per_tile == 0:
            out_ref[0, pl.ds(base, mt), :] = res
        else:
            for j in range(per_tile):
                out_ref[i * per_tile + j] = res[j * n_rows:j * n_rows + seq]
        return carry

    _loop(nseq * n_rows // mt, norm_tile, 0)


def _loop(trips, body, carry):
    if trips == 1:
        return body(0, carry)
    return lax.fori_loop(0, trips, body, carry)


def _conv(u3, state, dw, db, lg, lb, w_pw, nseq):
    nb, seq, _ = u3.shape
    n_rows = -(-seq // SUBLANES) * SUBLANES
    assert nseq == 1 or (TOK_TILE % n_rows == 0 and (nseq * n_rows) % TOK_TILE == 0)
    has_state = state is not None
    full = lambda a: pl.BlockSpec(a.shape, lambda b: (0,) * a.ndim)
    per_b = lambda r, w: pl.BlockSpec((nseq, r, w), lambda b: (b, 0, 0))
    in_specs = [per_b(seq, D_CONV)]
    args = [u3]
    if has_state:
        in_specs.append(pl.BlockSpec((None, nseq, CONV_HIST, D_CONV), lambda b: (0, b, 0, 0)))
        args.append(state)
    in_specs += [full(dw), full(db), full(lg), full(lb), full(w_pw)]
    args += [dw, db, lg, lb, w_pw]
    return pl.pallas_call(
        functools.partial(_conv_kernel, seq=seq, nseq=nseq, has_state=has_state),
        grid=(nb // nseq,),
        in_specs=in_specs,
        out_specs=[per_b(seq, D_MODEL), per_b(CONV_HIST, D_CONV)],
        out_shape=[jax.ShapeDtypeStruct((nb, seq, D_MODEL), F32),
                   jax.ShapeDtypeStruct((nb, CONV_HIST, D_CONV), F32)],
        scratch_shapes=[pltpu.VMEM((n_rows + CONV_HALO, D_CONV), F32),
                        pltpu.VMEM((nseq * n_rows, D_CONV), F32)],
        compiler_params=pltpu.CompilerParams(dimension_semantics=("parallel",),
                                             vmem_limit_bytes=VMEM_LIMIT),
        name="conv_state" if has_state else "conv_fresh",
    )(*args)


def _token_step(s, a_col, k_col, q_col, v_row):
    s = s * a_col + k_col * v_row
    return s, jnp.sum(q_col * s, axis=0, keepdims=True)


def _gla_prompt_kernel(q_ref, k_ref, la_ref, v_ref, o_ref, sout_ref, s_ref, vf_ref):
    c = pl.program_id(1)
    n = GLA_CHUNK
    chunks = [slice(i * n, (i + 1) * n) for i in range(q_ref.shape[0] // n)]

    @pl.when(c == 0)
    def _():
        s_ref[...] = jnp.zeros_like(s_ref)

    r_i = lax.broadcasted_iota(jnp.int32, (n, n), 0)
    c_i = lax.broadcasted_iota(jnp.int32, (n, n), 1)
    tri = jnp.where(r_i >= c_i, 1.0, 0.0).astype(BF16)

    def prefix_sum(g):
        hi = g.astype(BF16)
        lo = (g - hi.astype(F32)).astype(BF16)
        return _dot(tri, hi) + _dot(tri, lo)

    bs = [prefix_sum(la_ref[rows, :]) for rows in chunks]
    total = bs[0][n - 1:n, :]
    for b in bs[1:]:
        total = jnp.minimum(total, b[n - 1:n, :])
    fast = jnp.min(total) > SAFE_LOG_DECAY

    @pl.when(fast)
    def _():
        for h in range(HEADS):
            ks = slice(h * DK, (h + 1) * DK)
            vs = slice(h * DV, (h + 1) * DV)
            s = s_ref[h]
            for rows, b in zip(chunks, bs):
                bh = b[:, ks]
                bl = bh[n - 1:n, :]
                kh = k_ref[rows, ks]
                vh = v_ref[rows, vs]
                qp = (q_ref[rows, ks] * jnp.exp(bh)).astype(BF16)
                kp = (kh * jnp.exp(-bh)).astype(BF16)
                kpp = kh * jnp.exp(bl - bh)
                att = lax.dot_general(qp, kp, (((1,), (1,)), ((), ())), preferred_element_type=F32)
                att = jnp.where(r_i >= c_i, att, 0.0).astype(BF16)
                o_ref[rows, vs] = _dot(qp, s.astype(BF16)) + _dot(att, vh)
                d_col = jnp.sum(jnp.where(r_i == c_i, jnp.exp(bl), 0.0), axis=1, keepdims=True)
                s = s * d_col + _dot(kpp.T.astype(BF16), vh)
            s_ref[h] = s

    @pl.when(jnp.logical_not(fast))
    def _():
        lane = lax.broadcasted_iota(jnp.int32, (1, n), 1)
        for rows in chunks:
            vf_ref[...] = v_ref[rows, :].astype(F32)
            for h in range(HEADS):
                ks = slice(h * DK, (h + 1) * DK)
                vs = slice(h * DV, (h + 1) * DV)
                q_t = q_ref[rows, ks].T
                k_t = k_ref[rows, ks].T
                a_t = jnp.exp(la_ref[rows, ks]).T

                def body(t, s):
                    m = lane == t
                    col = lambda x: jnp.sum(jnp.where(m, x, 0.0), axis=1, keepdims=True)
                    s, o_row = _token_step(s, col(a_t), col(k_t), col(q_t), vf_ref[pl.ds(t, 1), vs])
                    o_ref[pl.ds(rows.start + t, 1), vs] = o_row
                    return s

                s_ref[h] = lax.fori_loop(0, n, body, s_ref[h])

    @pl.when(c == pl.num_programs(1) - 1)
    def _():
        sout_ref[0] = s_ref[...]


def _gla_prompt(q, k, la, v, nb, seq):
    nc = seq // GLA_STEP
    tok = lambda w: pl.BlockSpec((GLA_STEP, w), lambda b, c: (b * nc + c, 0))
    return pl.pallas_call(
        _gla_prompt_kernel,
        grid=(nb, nc),
        in_specs=[tok(QK_W), tok(QK_W), tok(QK_W), tok(V_W)],
        out_specs=[tok(V_W), pl.BlockSpec((1, HEADS, DK, DV), lambda b, c: (b, 0, 0, 0))],
        out_shape=[jax.ShapeDtypeStruct((nb * seq, V_W), F32),
                   jax.ShapeDtypeStruct((nb, HEADS, DK, DV), F32)],
        scratch_shapes=[pltpu.VMEM((HEADS, DK, DV), F32), pltpu.VMEM((GLA_CHUNK, V_W), F32)],
        compiler_params=pltpu.CompilerParams(dimension_semantics=("parallel", "arbitrary"),
                                             vmem_limit_bytes=VMEM_LIMIT),
        name="gla_prompt",
    )(q, k, la, v)


def _gla_sample_kernel(q_ref, k_ref, la_ref, v_ref, s_ref, o_ref, sout_ref, *, seq):
    p = SAMPLE_PAD
    row = lax.broadcasted_iota(jnp.int32, (p, 1), 0)

    def padded(x):
        return jnp.concatenate([x, jnp.zeros((p - seq, x.shape[1]), F32)], axis=0)

    for b in range(q_ref.shape[0]):
        q, k, v, g = padded(q_ref[b]), padded(k_ref[b]), padded(v_ref[b].astype(F32)), padded(la_ref[b])
        bc = g
        sh = 1
        while sh < seq:
            bc = bc + jnp.where(row >= sh, pltpu.roll(bc, sh, axis=0), 0.0)
            sh *= 2
        b_last = bc[seq - 1:seq, :]
        bc = jnp.where(row < seq, bc, b_last)
        qp = (q * jnp.exp(bc)).astype(BF16)
        kpp_t = (k * jnp.exp(b_last - bc)).T.astype(BF16)
        d_cols = jnp.broadcast_to(jnp.exp(b_last), (SUBLANES, QK_W)).T
        vb = v.astype(BF16)

        o_in = [jnp.zeros((p, DV), F32) for _ in range(HEADS)]
        for d in range(seq):
            k_d = k if d == 0 else pltpu.roll(k, d, axis=0)
            v_d = v if d == 0 else pltpu.roll(v, d, axis=0)
            b_d = bc if d == 0 else pltpu.roll(bc, d, axis=0)
            pair = q * k_d * jnp.exp(jnp.where(row >= d, bc - b_d, -jnp.inf))
            for h in range(HEADS):
                att = jnp.sum(pair[:, h * DK:(h + 1) * DK], axis=1, keepdims=True)
                o_in[h] = o_in[h] + att * v_d[:, h * DV:(h + 1) * DV]

        for h in range(HEADS):
            ks = slice(h * DK, (h + 1) * DK)
            vs = slice(h * DV, (h + 1) * DV)
            s = s_ref[b, h]
            o = _dot(qp[:, ks], s.astype(BF16)) + o_in[h]
            o_ref[b, :, vs] = o[:seq]
            sout_ref[b, h] = s * d_cols[ks, 0:1] + _dot(kpp_t[ks, :], vb[:, vs])


def _gla_sample(q3, k3, la3, v3, state):
    nb, seq, _ = q3.shape
    assert seq <= SUBLANES
    ns = SAMPLE_GLA_SEQS
    col = pl.BlockSpec((ns, seq, QK_W), lambda b: (b, 0, 0))
    tok = pl.BlockSpec((ns, seq, V_W), lambda b: (b, 0, 0))
    st = pl.BlockSpec((ns, HEADS, DK, DV), lambda b: (b, 0, 0, 0))
    st_in = pl.BlockSpec((None, ns, HEADS, DK, DV), lambda b: (0, b, 0, 0, 0))
    return pl.pallas_call(
        functools.partial(_gla_sample_kernel, seq=seq),
        grid=(nb // ns,),
        in_specs=[col, col, col, tok, st_in],
        out_specs=[tok, st],
        out_shape=[jax.ShapeDtypeStruct((nb, seq, V_W), F32),
                   jax.ShapeDtypeStruct((nb, HEADS, DK, DV), F32)],
        compiler_params=pltpu.CompilerParams(dimension_semantics=("parallel",),
                                             vmem_limit_bytes=VMEM_LIMIT),
        name="gla_sample",
    )(q3, k3, la3, v3, state)


def _merge_kernel(x_ref, co_ref, o_ref, sr_ref, gc_ref, gg_ref, gn_ref, wgo_ref, wo_ref, h_ref):
    gla_out = None
    for h in range(HEADS):
        vs = slice(h * DV, (h + 1) * DV)
        oh = o_ref[:, vs]
        on = oh * lax.rsqrt(jnp.mean(oh * oh, axis=-1, keepdims=True) + EPS) * gn_ref[:, vs]
        part = _dot((on * sr_ref[:, vs]).astype(BF16), wgo_ref[vs, :])
        gla_out = part if gla_out is None else gla_out + part
    mix = (gc_ref[...] * co_ref[...] + gg_ref[...] * gla_out).astype(BF16)
    h_ref[...] = x_ref[...] + _dot(mix, wo_ref[...])


def _merge(x2d, conv_out, o, sr, gc, gg, gn, w_gla_out, w_out):
    n = x2d.shape[0]
    tm = TOK_TILE
    row = pl.BlockSpec((tm, D_MODEL), lambda i: (i, 0))
    full = lambda a: pl.BlockSpec(a.shape, lambda i: (0,) * a.ndim)
    return pl.pallas_call(
        _merge_kernel,
        grid=(n // tm,),
        in_specs=[row] * 6 + [full(gn), full(w_gla_out), full(w_out)],
        out_specs=row,
        out_shape=jax.ShapeDtypeStruct((n, D_MODEL), F32),
        compiler_params=pltpu.CompilerParams(dimension_semantics=("parallel",),
                                             vmem_limit_bytes=VMEM_LIMIT),
        name="merge",
    )(x2d, conv_out, o, sr, gc, gg, gn, w_gla_out, w_out)


_L_E0, _L_E1, _L_R0, _L_R1 = 0, 1, 2, 3
_GROUP_LANE0 = N_EXPERTS


def _router_kernel(hp_ref, hs_ref, g_ref, wr_ref, br_ref, hn_ref, mi_ref, mf_ref, cnt_ref, carry_ref, *,
                   prompt_tiles):
    i = pl.program_id(0)
    tm = hp_ref.shape[0]

    @pl.when(i == 0)
    def _():
        carry_ref[...] = jnp.zeros_like(carry_ref)

    x = jnp.where(i < prompt_tiles, hp_ref[...], hs_ref[...])
    hn = x * lax.rsqrt(jnp.mean(x * x, axis=-1, keepdims=True) + EPS) * g_ref[...]
    hn_ref[...] = hn
    hi = hn.astype(BF16)
    lo = (hn - hi.astype(F32)).astype(BF16)
    logits = _dot(hi, wr_ref[0]) + (_dot(lo, wr_ref[0]) + _dot(hi, wr_ref[1])) + br_ref[...]
    lane = lax.broadcasted_iota(jnp.int32, (tm, LANES), 1)
    lane_f = lane.astype(F32)
    group_f = jnp.right_shift(lane, 3).astype(F32)
    big = jnp.float32(LANES)
    neg = jnp.float32(-jnp.inf)

    is_g = (lane >= _GROUP_LANE0) & (lane < _GROUP_LANE0 + N_GROUPS)
    lg = jnp.where(is_g, logits, neg)
    mg = jnp.max(lg, axis=1, keepdims=True)
    p_sel = 1.0 / jnp.sum(jnp.exp(lg - mg), axis=1, keepdims=True)
    g_idx = jnp.min(jnp.where(lg == mg, lane_f, big), axis=1, keepdims=True) - _GROUP_LANE0

    is_e = (lane < N_EXPERTS) & (group_f == g_idx)
    le = jnp.where(is_e, logits, neg)
    v0 = jnp.max(le, axis=1, keepdims=True)
    i0 = jnp.min(jnp.where(le == v0, lane_f, big), axis=1, keepdims=True)
    le1 = jnp.where(lane_f == i0, neg, le)
    v1 = jnp.max(le1, axis=1, keepdims=True)
    i1 = jnp.min(jnp.where(le1 == v1, lane_f, big), axis=1, keepdims=True)
    e1 = jnp.exp(v1 - v0)
    den = 1.0 + e1
    w0 = p_sel * (1.0 / den)
    w1 = p_sel * (e1 / den)

    hit0 = lane_f == i0
    hit1 = lane_f == i1
    cnt = jnp.where(hit0 | hit1, 1.0, 0.0)
    r_i = lax.broadcasted_iota(jnp.int32, (tm, tm), 0)
    c_i = lax.broadcasted_iota(jnp.int32, (tm, tm), 1)
    before = jnp.where(r_i > c_i, 1.0, 0.0).astype(BF16)
    seen = _dot(before, cnt.astype(BF16)) + carry_ref[...]
    rank0 = jnp.sum(jnp.where(hit0, seen, 0.0), axis=1, keepdims=True)
    rank1 = jnp.sum(jnp.where(hit1, seen, 0.0), axis=1, keepdims=True)
    carry_ref[...] = carry_ref[...] + jnp.sum(cnt, axis=0, keepdims=True)
    cnt_ref[...] = jnp.broadcast_to(carry_ref[...], cnt_ref.shape)

    rec = jnp.where(lane == _L_E0, i0, 0.0)
    rec = jnp.where(lane == _L_E1, i1, rec)
    rec = jnp.where(lane == _L_R0, rank0, rec)
    rec = jnp.where(lane == _L_R1, rank1, rec)
    mi_ref[...] = rec.T[:SUBLANES, :]
    mf_ref[...] = jnp.where(lane == 0, w0, jnp.where(lane == 1, w1, 0.0))


def _router(h_p, h_s, norm_g, w_route, b_route):
    tm = TOK_TILE
    tiles_p, tiles_s = h_p.shape[0] // tm, h_s.shape[0] // tm
    n = h_p.shape[0] + h_s.shape[0]
    row = lambda w: pl.BlockSpec((tm, w), lambda i: (i, 0))
    full = lambda a: pl.BlockSpec(a.shape, lambda i: (0,) * a.ndim)
    return pl.pallas_call(
        functools.partial(_router_kernel, prompt_tiles=tiles_p),
        grid=(tiles_p + tiles_s,),
        in_specs=[pl.BlockSpec((tm, D_MODEL), lambda i: (jnp.minimum(i, tiles_p - 1), 0)),
                  pl.BlockSpec((tm, D_MODEL), lambda i: (jnp.maximum(i - tiles_p, 0), 0)),
                  full(norm_g), full(w_route), full(b_route)],
        out_specs=[row(D_MODEL), pl.BlockSpec((SUBLANES, tm), lambda i: (0, i)), row(LANES),
                   pl.BlockSpec((SUBLANES, LANES), lambda i: (0, 0))],
        out_shape=[jax.ShapeDtypeStruct((n, D_MODEL), F32),
                   jax.ShapeDtypeStruct((SUBLANES, n), F32),
                   jax.ShapeDtypeStruct((n, LANES), F32),
                   jax.ShapeDtypeStruct((SUBLANES, LANES), F32)],
        scratch_shapes=[pltpu.VMEM((1, LANES), F32)],
        compiler_params=pltpu.CompilerParams(dimension_semantics=("arbitrary",),
                                             vmem_limit_bytes=VMEM_LIMIT),
        name="router",
    )(h_p, h_s, norm_g, w_route, b_route)


def _row_copy(src_ref, src_row, dst_ref, dst_row, sem):
    return pltpu.make_async_copy(src_ref.at[pl.ds(src_row, 1), :], dst_ref.at[pl.ds(dst_row, 1), :], sem)


def _dispatch_kernel(p0_ref, p1_ref, hn_ref, xs_ref, sem):
    tm = hn_ref.shape[0]
    base = pl.program_id(0) * tm

    def issue(t, carry):
        _row_copy(hn_ref, t, xs_ref, p0_ref[base + t], sem).start()
        _row_copy(hn_ref, t, xs_ref, p1_ref[base + t], sem).start()
        return carry

    lax.fori_loop(0, tm, issue, 0, unroll=DMA_UNROLL)

    def drain(t, carry):
        _row_copy(hn_ref, t, xs_ref, p0_ref[base + t], sem).wait()
        _row_copy(hn_ref, t, xs_ref, p1_ref[base + t], sem).wait()
        return carry

    lax.fori_loop(0, tm, drain, 0, unroll=DMA_UNROLL)


def _dispatch(pos0, pos1, hn):
    n = hn.shape[0]
    tm = TOK_TILE
    return pl.pallas_call(
        _dispatch_kernel,
        grid_spec=pltpu.PrefetchScalarGridSpec(
            num_scalar_prefetch=2,
            grid=(n // tm,),
            in_specs=[pl.BlockSpec((tm, D_MODEL), lambda i, p0, p1: (i, 0))],
            out_specs=pl.BlockSpec(memory_space=pl.ANY),
            scratch_shapes=[pltpu.SemaphoreType.DMA(())],
        ),
        out_shape=jax.ShapeDtypeStruct((2 * n, D_MODEL), F32),
        compiler_params=pltpu.CompilerParams(dimension_semantics=("arbitrary",)),
        name="dispatch",
    )(pos0, pos1, hn)


def _combine_kernel(p0_ref, p1_ref, h_ref, wf_ref, g_ref, ys_ref, out_ref, b0_ref, b1_ref, sems, *, tok_offset):
    tm = h_ref.shape[0]
    i = pl.program_id(0)

    def gather(tile, slot, wait):
        base = tok_offset + tile * tm

        def rows(t, carry):
            c0 = _row_copy(ys_ref, p0_ref[base + t], b0_ref.at[slot], t, sems.at[slot])
            c1 = _row_copy(ys_ref, p1_ref[base + t], b1_ref.at[slot], t, sems.at[slot])
            if wait:
                c0.wait()
                c1.wait()
            else:
                c0.start()
                c1.start()
            return carry

        lax.fori_loop(0, tm, rows, 0, unroll=DMA_UNROLL)

    slot = lax.rem(i, 2)

    @pl.when(i == 0)
    def _():
        gather(i, slot, wait=False)

    @pl.when(i + 1 < pl.num_programs(0))
    def _():
        gather(i + 1, 1 - slot, wait=False)

    gather(i, slot, wait=True)
    y = h_ref[...] + (b0_ref[slot] * wf_ref[:, 0:1] + b1_ref[slot] * wf_ref[:, 1:2])
    out_ref[...] = y * lax.rsqrt(jnp.mean(y * y, axis=-1, keepdims=True) + EPS) * g_ref[...]


def _combine(pos0, pos1, h, wf, norm_g, ys, tok_offset):
    n_rows = h.shape[0]
    tm = TOK_TILE
    off = tok_offset // tm
    return pl.pallas_call(
        functools.partial(_combine_kernel, tok_offset=tok_offset),
        grid_spec=pltpu.PrefetchScalarGridSpec(
            num_scalar_prefetch=2,
            grid=(n_rows // tm,),
            in_specs=[pl.BlockSpec((tm, D_MODEL), lambda i, p0, p1: (i, 0)),
                      pl.BlockSpec((tm, LANES), lambda i, p0, p1: (i + off, 0)),
                      pl.BlockSpec((1, D_MODEL), lambda i, p0, p1: (0, 0)),
                      pl.BlockSpec(memory_space=pl.ANY)],
            out_specs=pl.BlockSpec((tm, D_MODEL), lambda i, p0, p1: (i, 0)),
            scratch_shapes=[pltpu.VMEM((2, tm, D_MODEL), F32), pltpu.VMEM((2, tm, D_MODEL), F32),
                            pltpu.SemaphoreType.DMA((2,))],
        ),
        out_shape=jax.ShapeDtypeStruct((n_rows, D_MODEL), F32),
        compiler_params=pltpu.CompilerParams(dimension_semantics=("arbitrary",),
                                             vmem_limit_bytes=VMEM_LIMIT),
        name="combine",
    )(pos0, pos1, h, wf, norm_g, ys)


def _expert_kernel(blk_ref, exp_ref, lo_ref, hi_ref, first_ref,
                   xs_ref, wg_ref, wu_ref, wd_ref, ys_ref, wgb_ref, wub_ref, wdb_ref):
    w = pl.program_id(0)
    e = exp_ref[w]
    changed = jnp.logical_or(w == 0, e != exp_ref[jnp.maximum(w - 1, 0)])

    @pl.when(changed)
    def _():
        wgb_ref[...] = wg_ref[0].astype(BF16)
        wub_ref[...] = wu_ref[0].astype(BF16)
        wdb_ref[...] = wd_ref[0].astype(BF16)

    lo = lo_ref[w]
    hi = hi_ref[w]

    @pl.when(hi > lo)
    def _():
        x = xs_ref[...].astype(BF16)
        gate = _dot(x, wgb_ref[...])
        hid = (gate * _sigmoid(gate)) * _dot(x, wub_ref[...])
        y = _dot(hid.astype(BF16), wdb_ref[...])
        rows = blk_ref[w] * ROW_BLOCK + lax.broadcasted_iota(jnp.int32, (ROW_BLOCK, 1), 0)
        y = jnp.where((rows >= lo) & (rows < hi), y, 0.0)

        @pl.when(first_ref[w] == 1)
        def _():
            ys_ref[...] = y

        @pl.when(first_ref[w] == 0)
        def _():
            ys_ref[...] = ys_ref[...] + y


def _experts(items, xs, w_gate, w_up, w_down):
    n_rows = xs.shape[0]
    n_items = items[0].shape[0]
    return pl.pallas_call(
        _expert_kernel,
        grid_spec=pltpu.PrefetchScalarGridSpec(
            num_scalar_prefetch=5,
            grid=(n_items,),
            in_specs=[pl.BlockSpec((ROW_BLOCK, D_MODEL), lambda w, blk, ex, lo, hi, fi: (blk[w], 0)),
                      pl.BlockSpec((1, D_MODEL, D_EXPERT), lambda w, blk, ex, lo, hi, fi: (ex[w], 0, 0)),
                      pl.BlockSpec((1, D_MODEL, D_EXPERT), lambda w, blk, ex, lo, hi, fi: (ex[w], 0, 0)),
                      pl.BlockSpec((1, D_EXPERT, D_MODEL), lambda w, blk, ex, lo, hi, fi: (ex[w], 0, 0))],
            out_specs=pl.BlockSpec((ROW_BLOCK, D_MODEL), lambda w, blk, ex, lo, hi, fi: (blk[w], 0)),
            scratch_shapes=[pltpu.VMEM((D_MODEL, D_EXPERT), BF16), pltpu.VMEM((D_MODEL, D_EXPERT), BF16),
                            pltpu.VMEM((D_EXPERT, D_MODEL), BF16)],
        ),
        out_shape=jax.ShapeDtypeStruct((n_rows, D_MODEL), F32),
        compiler_params=pltpu.CompilerParams(dimension_semantics=("arbitrary",),
                                             vmem_limit_bytes=VMEM_LIMIT),
        name="experts",
    )(*items, xs, w_gate, w_up, w_down)


def _work_items(counts, n_rows):
    n_blocks = n_rows // ROW_BLOCK
    n_items = n_blocks + N_EXPERTS - 1
    ends = jnp.cumsum(counts)
    starts = ends - counts
    blk_lo = starts // ROW_BLOCK
    blk_hi = jnp.maximum(ends - 1, 0) // ROW_BLOCK
    per_e = jnp.where(counts > 0, blk_hi - blk_lo + 1, 0)
    item_end = jnp.cumsum(per_e)
    item_start = item_end - per_e
    total = item_end[-1]
    w = jnp.arange(n_items, dtype=jnp.int32)
    live = w < total
    e = jnp.sum((jnp.minimum(w, total - 1)[:, None] >= item_end[None, :]).astype(jnp.int32), axis=1)
    e = jnp.minimum(e, N_EXPERTS - 1)
    sel = (e[:, None] == jnp.arange(N_EXPERTS, dtype=jnp.int32)[None, :]).astype(jnp.int32)
    pick = lambda table: jnp.sum(sel * table[None, :], axis=1)
    blk = jnp.where(live, pick(blk_lo) + (w - pick(item_start)), n_blocks - 1).astype(jnp.int32)
    lo = jnp.where(live, jnp.maximum(pick(starts), blk * ROW_BLOCK), 0).astype(jnp.int32)
    hi = jnp.where(live, jnp.minimum(pick(ends), (blk + 1) * ROW_BLOCK), 0).astype(jnp.int32)
    prev_blk = jnp.concatenate([jnp.full((1,), -1, jnp.int32), blk[:-1]])
    first = (live & (blk != prev_blk)).astype(jnp.int32)
    return (blk, e.astype(jnp.int32), lo, hi, first), starts


def kernel(x_prompt, x_sample, state_conv, state_gla, norm_mix, w_in, b_gates, w_gla_gate_up, b_gla_gate_up, conv_dw, conv_db, conv_ln_g, conv_ln_b, w_conv_out, gla_norm_g, w_gla_out, w_out, norm_ffn, w_router_group, b_router_group, w_router_expert, b_router_expert, w_expert_gate, w_expert_up, w_expert_down, norm_final):
    depth = norm_mix.shape[0]
    assert depth == 1
    l = 0
    bp, seq_p, _ = x_prompt.shape
    bs, seq_s, _ = x_sample.shape
    n_p, n_s = bp * seq_p, bs * seq_s
    n_tok = n_p + n_s
    row2 = lambda a: a.reshape(1, -1)

    wi = w_in[l]
    w_main = wi[:, :_C_ALOW].astype(BF16)
    w_alow = jnp.pad(wi[:, _C_ALOW:_C_GATES], ((0, 0), (0, LANES - RANK))).astype(BF16)
    w_gates = wi[:, _C_GATES:].astype(BF16)
    w_up_pad = jnp.pad(w_gla_gate_up[l], ((0, LANES - RANK), (0, 0))).astype(BF16)
    w_route = jnp.pad(jnp.concatenate([w_router_expert[l], w_router_group[l]], axis=1),
                      ((0, 0), (0, LANES - N_EXPERTS - N_GROUPS)))
    w_route_hi = w_route.astype(BF16)
    w_route = jnp.stack([w_route_hi, (w_route - w_route_hi.astype(F32)).astype(BF16)])
    b_route = jnp.pad(jnp.concatenate([b_router_expert[l], b_router_group[l]]),
                      (0, LANES - N_EXPERTS - N_GROUPS)).reshape(1, LANES)
    w_pw = w_conv_out[l].astype(BF16)
    w_go = w_gla_out[l].astype(BF16)
    w_o = w_out[l].astype(BF16)
    conv_args = (conv_dw[l], row2(conv_db[l]), row2(conv_ln_g[l]), row2(conv_ln_b[l]), w_pw)
    proj_args = (row2(norm_mix[l]), w_main, w_alow, w_gates, w_up_pad, row2(b_gla_gate_up[l]),
                 row2(b_gates[l]))

    xp = x_prompt.reshape(n_p, D_MODEL)
    u, q, k, v, la, sr, gc, gg = _proj(xp, *proj_args)
    conv_out, conv_p = _conv(u.reshape(bp, seq_p, D_CONV), None, *conv_args, nseq=1)
    o, gla_p = _gla_prompt(q, k, la, v, bp, seq_p)
    h_p = _merge(xp, conv_out.reshape(n_p, D_MODEL), o, sr, gc, gg, row2(gla_norm_g[l]), w_go, w_o)

    xs_ = x_sample.reshape(n_s, D_MODEL)
    u, q, k, v, la, sr, gc, gg = _proj(xs_, *proj_args)
    conv_out, conv_s = _conv(u.reshape(bs, seq_s, D_CONV), state_conv, *conv_args, nseq=SAMPLE_SEQS)
    seqs = lambda a: a.reshape(bs, seq_s, a.shape[-1])
    o, gla_s = _gla_sample(seqs(q), seqs(k), seqs(la), seqs(v), state_gla)
    h_s = _merge(xs_, conv_out.reshape(n_s, D_MODEL), o.reshape(n_s, V_W), sr, gc, gg,
                 row2(gla_norm_g[l]), w_go, w_o)

    hn, meta_i, meta_f, cnt = _router(h_p, h_s, row2(norm_ffn[l]), w_route, b_route)
    counts = cnt[0, :N_EXPERTS].astype(jnp.int32)
    items, starts = _work_items(counts, 2 * n_tok)
    rec = meta_i.astype(jnp.int32)
    e_ids = jnp.arange(N_EXPERTS, dtype=jnp.int32)[:, None]
    start_of = lambda e: jnp.sum(jnp.where(e[None, :] == e_ids, starts[:, None], 0), axis=0)
    pos0 = start_of(rec[_L_E0]) + rec[_L_R0]
    pos1 = start_of(rec[_L_E1]) + rec[_L_R1]
    xs_sorted = _dispatch(pos0, pos1, hn)
    ys = _experts(items, xs_sorted, w_expert_gate[l], w_expert_up[l], w_expert_down[l])
    y_p = _combine(pos0, pos1, h_p, meta_f, row2(norm_final), ys, 0)
    y_s = _combine(pos0, pos1, h_s, meta_f, row2(norm_final), ys, n_p)

    return (y_p.reshape(bp, seq_p, D_MODEL), y_s.reshape(bs, seq_s, D_MODEL),
            conv_p[None], gla_p[None], conv_s[None], gla_s[None])
```

```python
import functools

import jax
import jax.numpy as jnp
from jax import lax
from jax.experimental import pallas as pl
from jax.experimental.pallas import tpu as pltpu

F32 = jnp.float32
BF16 = jnp.bfloat16

D_MODEL = 1024
D_CONV = 512
CONV_WIDTH = 31
CONV_HIST = CONV_WIDTH - 1
HEADS = 4
DK = 128
DV = 256
QK_W = HEADS * DK
V_W = HEADS * DV
RANK = 16
GATE_NORM = 16.0
N_GROUPS = 4
EXPERTS_PER_GROUP = 8
N_EXPERTS = 32
D_EXPERT = 512
EPS = 1e-6

LANES = 128
SUBLANES = 8
VMEM_LIMIT = 56 * 1024 * 1024

TOK_TILE = 256
GLA_CHUNK = 128
GLA_STEP = 256
SAFE_LOG_DECAY = -80.0
CONV_ROWS = 64
CONV_HALO = 40
ROW_BLOCK = 256
DMA_UNROLL = 8
SAMPLE_SEQS = 32
SAMPLE_GLA_SEQS = 8
SAMPLE_PAD = 16

_C_GLU_A, _C_GLU_B = 0, 512
_C_Q, _C_K, _C_V, _C_R = 1024, 1536, 2048, 3072
_C_ALOW = 4096
_C_GATES = _C_ALOW + RANK


def _sigmoid(x):
    return jax.nn.sigmoid(x)


def _dot(a, b):
    return jnp.dot(a, b, preferred_element_type=F32)


def _proj_kernel(x_ref, g_ref, wm_ref, wa_ref, wg_ref, wup_ref, bup_ref, bg_ref,
                 u_ref, q_ref, k_ref, v_ref, la_ref, sr_ref, gc_ref, gg_ref):
    x = x_ref[...]
    hn = x * lax.rsqrt(jnp.mean(x * x, axis=-1, keepdims=True) + EPS) * g_ref[...]
    hb = hn.astype(BF16)

    def mm(lo, hi):
        return _dot(hb, wm_ref[:, lo:hi])

    u_ref[...] = mm(_C_GLU_A, _C_GLU_B) * _sigmoid(mm(_C_GLU_B, _C_Q))
    q_ref[...] = mm(_C_Q, _C_K) * (DK ** -0.5)
    k_ref[...] = mm(_C_K, _C_V)
    v_ref[...] = mm(_C_V, _C_R).astype(BF16)
    r = mm(_C_R, _C_ALOW)
    sr_ref[...] = (r * _sigmoid(r)).astype(BF16)
    gc_ref[...] = _sigmoid(_dot(hb, wg_ref[:, :D_MODEL]) + bg_ref[:, :D_MODEL]).astype(BF16)
    gg_ref[...] = _sigmoid(_dot(hb, wg_ref[:, D_MODEL:]) + bg_ref[:, D_MODEL:]).astype(BF16)
    a_low = _dot(hb, wa_ref[...])
    z = _dot(a_low.astype(BF16), wup_ref[...]) + bup_ref[...]
    la_ref[...] = (jnp.minimum(z, 0.0) - jnp.log1p(jnp.exp(-jnp.abs(z)))) * (1.0 / GATE_NORM)


def _proj(x2d, norm_g, w_main, w_alow, w_gates, w_up, b_up, b_gates):
    n = x2d.shape[0]
    tm = TOK_TILE
    row = lambda w: pl.BlockSpec((tm, w), lambda i: (i, 0))
    full = lambda a: pl.BlockSpec(a.shape, lambda i: (0,) * a.ndim)
    widths = (D_CONV, QK_W, QK_W, V_W, QK_W, V_W, D_MODEL, D_MODEL)
    dtypes = (F32, F32, F32, BF16, F32, BF16, BF16, BF16)
    consts = (norm_g, w_main, w_alow, w_gates, w_up, b_up, b_gates)
    return pl.pallas_call(
        _proj_kernel,
        grid=(n // tm,),
        in_specs=[row(D_MODEL)] + [full(a) for a in consts],
        out_specs=[row(w) for w in widths],
        out_shape=[jax.ShapeDtypeStruct((n, w), dt) for w, dt in zip(widths, dtypes)],
        compiler_params=pltpu.CompilerParams(dimension_semantics=("parallel",),
                                             vmem_limit_bytes=VMEM_LIMIT),
        name="proj",
    )(x2d, *consts)


def _conv_kernel(*refs, seq, nseq, has_state):
    if has_state:
        u_ref, st_ref, dw_ref, db_ref, lg_ref, lb_ref, wpw_ref, out_ref, ns_ref, full_ref, y_ref = refs
    else:
        u_ref, dw_ref, db_ref, lg_ref, lb_ref, wpw_ref, out_ref, ns_ref, full_ref, y_ref = refs
    pad = 32
    hist0 = pad - CONV_HIST
    n_rows = y_ref.shape[0] // nseq
    rt = min(CONV_ROWS, n_rows)
    win = rt + CONV_HALO
    tail = full_ref.shape[0] - (pad + seq)

    def one_seq(b, carry):
        full_ref[0:pad, :] = jnp.zeros((pad, D_CONV), F32)
        full_ref[pad + seq:, :] = jnp.zeros((tail, D_CONV), F32)
        if has_state:
            full_ref[hist0:pad, :] = st_ref[b]
        full_ref[pad:pad + seq, :] = u_ref[b]
        ns_ref[b] = full_ref[seq + hist0:seq + pad, :]
        y0 = b * n_rows

        def row_tile(i, carry):
            base = pl.multiple_of(i * rt, SUBLANES)
            for c in range(D_CONV // LANES):
                cs = slice(c * LANES, (c + 1) * LANES)
                w = full_ref[pl.ds(base, win), cs]
                acc = jnp.zeros((rt, LANES), F32)
                for s in range(SUBLANES):
                    ws = w if s == 0 else pltpu.roll(w, win - s, axis=0)
                    for a in range((CONV_HALO // SUBLANES)):
                        j = a * SUBLANES + s - hist0
                        if 0 <= j < CONV_WIDTH:
                            acc = acc + ws[a * SUBLANES:a * SUBLANES + rt, :] * dw_ref[j:j + 1, cs]
                y_ref[pl.ds(pl.multiple_of(y0 + base, SUBLANES), rt), cs] = acc + db_ref[:, cs]
            return carry

        return _loop(n_rows // rt, row_tile, carry)

    _loop(nseq, one_seq, 0)

    mt = min(TOK_TILE, nseq * n_rows)
    per_tile = mt // n_rows

    def norm_tile(i, carry):
        base = pl.multiple_of(i * mt, SUBLANES)
        y = y_ref[pl.ds(base, mt), :]
        mu = jnp.mean(y, axis=-1, keepdims=True)
        var = jnp.mean(jnp.square(y - mu), axis=-1, keepdims=True)
        yn = (y - mu) * lax.rsqrt(var + EPS) * lg_ref[...] + lb_ref[...]
        act = (yn * _sigmoid(yn)).astype(BF16)
        res = _dot(act, wpw_ref[...])
        if per_tile == 0:
            out_ref[0, pl.ds(base, mt), :] = res
        else:
            for j in range(per_tile):
                out_ref[i * per_tile + j] = res[j * n_rows:j * n_rows + seq]
        return carry

    _loop(nseq * n_rows // mt, norm_tile, 0)


def _loop(trips, body, carry):
    if trips == 1:
        return body(0, carry)
    return lax.fori_loop(0, trips, body, carry)


def _conv(u3, state, dw, db, lg, lb, w_pw, nseq):
    nb, seq, _ = u3.shape
    n_rows = -(-seq // SUBLANES) * SUBLANES
    assert nseq == 1 or (TOK_TILE % n_rows == 0 and (nseq * n_rows) % TOK_TILE == 0)
    has_state = state is not None
    full = lambda a: pl.BlockSpec(a.shape, lambda b: (0,) * a.ndim)
    per_b = lambda r, w: pl.BlockSpec((nseq, r, w), lambda b: (b, 0, 0))
    in_specs = [per_b(seq, D_CONV)]
    args = [u3]
    if has_state:
        in_specs.append(pl.BlockSpec((None, nseq, CONV_HIST, D_CONV), lambda b: (0, b, 0, 0)))
        args.append(state)
    in_specs += [full(dw), full(db), full(lg), full(lb), full(w_pw)]
    args += [dw, db, lg, lb, w_pw]
    return pl.pallas_call(
        functools.partial(_conv_kernel, seq=seq, nseq=nseq, has_state=has_state),
        grid=(nb // nseq,),
        in_specs=in_specs,
        out_specs=[per_b(seq, D_MODEL), per_b(CONV_HIST, D_CONV)],
        out_shape=[jax.ShapeDtypeStruct((nb, seq, D_MODEL), F32),
                   jax.ShapeDtypeStruct((nb, CONV_HIST, D_CONV), F32)],
        scratch_shapes=[pltpu.VMEM((n_rows + CONV_HALO, D_CONV), F32),
                        pltpu.VMEM((nseq * n_rows, D_CONV), F32)],
        compiler_params=pltpu.CompilerParams(dimension_semantics=("parallel",),
                                             vmem_limit_bytes=VMEM_LIMIT),
        name="conv_state" if has_state else "conv_fresh",
    )(*args)


def _token_step(s, a_col, k_col, q_col, v_row):
    s = s * a_col + k_col * v_row
    return s, jnp.sum(q_col * s, axis=0, keepdims=True)


def _gla_prompt_kernel(q_ref, k_ref, la_ref, v_ref, o_ref, sout_ref, s_ref, vf_ref):
    c = pl.program_id(1)
    n = GLA_CHUNK
    chunks = [slice(i * n, (i + 1) * n) for i in range(q_ref.shape[0] // n)]

    @pl.when(c == 0)
    def _():
        s_ref[...] = jnp.zeros_like(s_ref)

    r_i = lax.broadcasted_iota(jnp.int32, (n, n), 0)
    c_i = lax.broadcasted_iota(jnp.int32, (n, n), 1)
    tri = jnp.where(r_i >= c_i, 1.0, 0.0).astype(BF16)

    def prefix_sum(g):
        hi = g.astype(BF16)
        lo = (g - hi.astype(F32)).astype(BF16)
        return _dot(tri, hi) + _dot(tri, lo)

    bs = [prefix_sum(la_ref[rows, :]) for rows in chunks]
    total = bs[0][n - 1:n, :]
    for b in bs[1:]:
        total = jnp.minimum(total, b[n - 1:n, :])
    fast = jnp.min(total) > SAFE_LOG_DECAY

    @pl.when(fast)
    def _():
        for h in range(HEADS):
            ks = slice(h * DK, (h + 1) * DK)
            vs = slice(h * DV, (h + 1) * DV)
            s = s_ref[h]
            for rows, b in zip(chunks, bs):
                bh = b[:, ks]
                bl = bh[n - 1:n, :]
                kh = k_ref[rows, ks]
                vh = v_ref[rows, vs]
                qp = (q_ref[rows, ks] * jnp.exp(bh)).astype(BF16)
                kp = (kh * jnp.exp(-bh)).astype(BF16)
                kpp = kh * jnp.exp(bl - bh)
                att = lax.dot_general(qp, kp, (((1,), (1,)), ((), ())), preferred_element_type=F32)
                att = jnp.where(r_i >= c_i, att, 0.0).astype(BF16)
                o_ref[rows, vs] = _dot(qp, s.astype(BF16)) + _dot(att, vh)
                d_col = jnp.sum(jnp.where(r_i == c_i, jnp.exp(bl), 0.0), axis=1, keepdims=True)
                s = s * d_col + _dot(kpp.T.astype(BF16), vh)
            s_ref[h] = s

    @pl.when(jnp.logical_not(fast))
    def _():
        lane = lax.broadcasted_iota(jnp.int32, (1, n), 1)
        for rows in chunks:
            vf_ref[...] = v_ref[rows, :].astype(F32)
            for h in range(HEADS):
                ks = slice(h * DK, (h + 1) * DK)
                vs = slice(h * DV, (h + 1) * DV)
                q_t = q_ref[rows, ks].T
                k_t = k_ref[rows, ks].T
                a_t = jnp.exp(la_ref[rows, ks]).T

                def body(t, s):
                    m = lane == t
                    col = lambda x: jnp.sum(jnp.where(m, x, 0.0), axis=1, keepdims=True)
                    s, o_row = _token_step(s, col(a_t), col(k_t), col(q_t), vf_ref[pl.ds(t, 1), vs])
                    o_ref[pl.ds(rows.start + t, 1), vs] = o_row
                    return s

                s_ref[h] = lax.fori_loop(0, n, body, s_ref[h])

    @pl.when(c == pl.num_programs(1) - 1)
    def _():
        sout_ref[0] = s_ref[...]


def _gla_prompt(q, k, la, v, nb, seq):
    nc = seq // GLA_STEP
    tok = lambda w: pl.BlockSpec((GLA_STEP, w), lambda b, c: (b * nc + c, 0))
    return pl.pallas_call(
        _gla_prompt_kernel,
        grid=(nb, nc),
        in_specs=[tok(QK_W), tok(QK_W), tok(QK_W), tok(V_W)],
        out_specs=[tok(V_W), pl.BlockSpec((1, HEADS, DK, DV), lambda b, c: (b, 0, 0, 0))],
        out_shape=[jax.ShapeDtypeStruct((nb * seq, V_W), F32),
                   jax.ShapeDtypeStruct((nb, HEADS, DK, DV), F32)],
        scratch_shapes=[pltpu.VMEM((HEADS, DK, DV), F32), pltpu.VMEM((GLA_CHUNK, V_W), F32)],
        compiler_params=pltpu.CompilerParams(dimension_semantics=("parallel", "arbitrary"),
                                             vmem_limit_bytes=VMEM_LIMIT),
        name="gla_prompt",
    )(q, k, la, v)


def _gla_sample_kernel(q_ref, k_ref, la_ref, v_ref, s_ref, o_ref, sout_ref, *, seq):
    p = SAMPLE_PAD
    row = lax.broadcasted_iota(jnp.int32, (p, 1), 0)

    def padded(x):
        return jnp.concatenate([x, jnp.zeros((p - seq, x.shape[1]), F32)], axis=0)

    for b in range(q_ref.shape[0]):
        q, k, v, g = padded(q_ref[b]), padded(k_ref[b]), padded(v_ref[b].astype(F32)), padded(la_ref[b])
        bc = g
        sh = 1
        while sh < seq:
            bc = bc + jnp.where(row >= sh, pltpu.roll(bc, sh, axis=0), 0.0)
            sh *= 2
        b_last = bc[seq - 1:seq, :]
        bc = jnp.where(row < seq, bc, b_last)
        qp = (q * jnp.exp(bc)).astype(BF16)
        kpp_t = (k * jnp.exp(b_last - bc)).T.astype(BF16)
        d_cols = jnp.broadcast_to(jnp.exp(b_last), (SUBLANES, QK_W)).T
        vb = v.astype(BF16)

        o_in = [jnp.zeros((p, DV), F32) for _ in range(HEADS)]
        for d in range(seq):
            k_d = k if d == 0 else pltpu.roll(k, d, axis=0)
            v_d = v if d == 0 else pltpu.roll(v, d, axis=0)
            b_d = bc if d == 0 else pltpu.roll(bc, d, axis=0)
            pair = q * k_d * jnp.exp(jnp.where(row >= d, bc - b_d, -jnp.inf))
            for h in range(HEADS):
                att = jnp.sum(pair[:, h * DK:(h + 1) * DK], axis=1, keepdims=True)
                o_in[h] = o_in[h] + att * v_d[:, h * DV:(h + 1) * DV]

        for h in range(HEADS):
            ks = slice(h * DK, (h + 1) * DK)
            vs = slice(h * DV, (h + 1) * DV)
            s = s_ref[b, h]
            o = _dot(qp[:, ks], s.astype(BF16)) + o_in[h]
            o_ref[b, :, vs] = o[:seq]
            sout_ref[b, h] = s * d_cols[ks, 0:1] + _dot(kpp_t[ks, :], vb[:, vs])


def _gla_sample(q3, k3, la3, v3, state):
    nb, seq, _ = q3.shape
    assert seq <= SUBLANES
    ns = SAMPLE_GLA_SEQS
    col = pl.BlockSpec((ns, seq, QK_W), lambda b: (b, 0, 0))
    tok = pl.BlockSpec((ns, seq, V_W), lambda b: (b, 0, 0))
    st = pl.BlockSpec((ns, HEADS, DK, DV), lambda b: (b, 0, 0, 0))
    st_in = pl.BlockSpec((None, ns, HEADS, DK, DV), lambda b: (0, b, 0, 0, 0))
    return pl.pallas_call(
        functools.partial(_gla_sample_kernel, seq=seq),
        grid=(nb // ns,),
        in_specs=[col, col, col, tok, st_in],
        out_specs=[tok, st],
        out_shape=[jax.ShapeDtypeStruct((nb, seq, V_W), F32),
                   jax.ShapeDtypeStruct((nb, HEADS, DK, DV), F32)],
        compiler_params=pltpu.CompilerParams(dimension_semantics=("parallel",),
                                             vmem_limit_bytes=VMEM_LIMIT),
        name="gla_sample",
    )(q3, k3, la3, v3, state)


_L_E0, _L_E1, _L_R0, _L_R1 = 0, 1, 2, 3
_GROUP_LANE0 = N_EXPERTS
HALF = D_MODEL // 2


def _mix_route_kernel(x_ref, co_ref, o_ref, sr_ref, gc_ref, gg_ref, gn_ref, wgo_ref, wo_ref,
                      g_ref, wr_ref, br_ref, cin_ref,
                      h_ref, xp_ref, mi_ref, mf_ref, cnt_ref, carry_ref):
    i = pl.program_id(0)
    tm = x_ref.shape[0]

    @pl.when(i == 0)
    def _():
        carry_ref[...] = cin_ref[0:1, :]

    gla_out = None
    for h in range(HEADS):
        vs = slice(h * DV, (h + 1) * DV)
        oh = o_ref[:, vs]
        on = oh * lax.rsqrt(jnp.mean(oh * oh, axis=-1, keepdims=True) + EPS) * gn_ref[:, vs]
        part = _dot((on * sr_ref[:, vs]).astype(BF16), wgo_ref[vs, :])
        gla_out = part if gla_out is None else gla_out + part
    mix = (gc_ref[...] * co_ref[...] + gg_ref[...] * gla_out).astype(BF16)
    x = x_ref[...] + _dot(mix, wo_ref[...])
    h_ref[...] = x

    hn = x * lax.rsqrt(jnp.mean(x * x, axis=-1, keepdims=True) + EPS) * g_ref[...]
    hi = hn.astype(BF16)
    hi_f = hi.astype(F32)
    top = lax.bitcast_convert_type(hi_f[:, :HALF], jnp.uint32)
    bot = lax.bitcast_convert_type(hi_f[:, HALF:], jnp.uint32)
    xp_ref[...] = top | jnp.right_shift(bot, jnp.uint32(16))
    lo = (hn - hi_f).astype(BF16)
    logits = _dot(hi, wr_ref[0]) + (_dot(lo, wr_ref[0]) + _dot(hi, wr_ref[1])) + br_ref[...]
    lane = lax.broadcasted_iota(jnp.int32, (tm, LANES), 1)
    lane_f = lane.astype(F32)
    group_f = jnp.right_shift(lane, 3).astype(F32)
    big = jnp.float32(LANES)
    neg = jnp.float32(-jnp.inf)

    is_g = (lane >= _GROUP_LANE0) & (lane < _GROUP_LANE0 + N_GROUPS)
    lg = jnp.where(is_g, logits, neg)
    mg = jnp.max(lg, axis=1, keepdims=True)
    p_sel = 1.0 / jnp.sum(jnp.exp(lg - mg), axis=1, keepdims=True)
    g_idx = jnp.min(jnp.where(lg == mg, lane_f, big), axis=1, keepdims=True) - _GROUP_LANE0

    is_e = (lane < N_EXPERTS) & (group_f == g_idx)
    le = jnp.where(is_e, logits, neg)
    v0 = jnp.max(le, axis=1, keepdims=True)
    i0 = jnp.min(jnp.where(le == v0, lane_f, big), axis=1, keepdims=True)
    le1 = jnp.where(lane_f == i0, neg, le)
    v1 = jnp.max(le1, axis=1, keepdims=True)
    i1 = jnp.min(jnp.where(le1 == v1, lane_f, big), axis=1, keepdims=True)
    e1 = jnp.exp(v1 - v0)
    den = 1.0 + e1
    w0 = p_sel * (1.0 / den)
    w1 = p_sel * (e1 / den)

    hit0 = lane_f == i0
    hit1 = lane_f == i1
    cnt = jnp.where(hit0 | hit1, 1.0, 0.0)
    r_i = lax.broadcasted_iota(jnp.int32, (tm, tm), 0)
    c_i = lax.broadcasted_iota(jnp.int32, (tm, tm), 1)
    before = jnp.where(r_i > c_i, 1.0, 0.0).astype(BF16)
    seen = _dot(before, cnt.astype(BF16)) + carry_ref[...]
    rank0 = jnp.sum(jnp.where(hit0, seen, 0.0), axis=1, keepdims=True)
    rank1 = jnp.sum(jnp.where(hit1, seen, 0.0), axis=1, keepdims=True)
    carry_ref[...] = carry_ref[...] + jnp.sum(cnt, axis=0, keepdims=True)
    cnt_ref[...] = jnp.broadcast_to(carry_ref[...], cnt_ref.shape)

    rec = jnp.where(lane == _L_E0, i0, 0.0)
    rec = jnp.where(lane == _L_E1, i1, rec)
    rec = jnp.where(lane == _L_R0, rank0, rec)
    rec = jnp.where(lane == _L_R1, rank1, rec)
    mi_ref[...] = rec.T[:SUBLANES, :]
    mf_ref[...] = jnp.where(lane == 0, w0, jnp.where(lane == 1, w1, 0.0))


def _mix_route(x2d, conv_out, o, sr, gc, gg, gn, w_gla_out, w_out, norm_g, w_route, b_route, counts_in):
    n = x2d.shape[0]
    tm = TOK_TILE
    row = lambda w: pl.BlockSpec((tm, w), lambda i: (i, 0))
    full = lambda a: pl.BlockSpec(a.shape, lambda i: (0,) * a.ndim)
    consts = (gn, w_gla_out, w_out, norm_g, w_route, b_route, counts_in)
    return pl.pallas_call(
        _mix_route_kernel,
        grid=(n // tm,),
        in_specs=[row(D_MODEL)] * 6 + [full(a) for a in consts],
        out_specs=[row(D_MODEL), row(HALF), pl.BlockSpec((SUBLANES, tm), lambda i: (0, i)), row(LANES),
                   pl.BlockSpec((SUBLANES, LANES), lambda i: (0, 0))],
        out_shape=[jax.ShapeDtypeStruct((n, D_MODEL), F32),
                   jax.ShapeDtypeStruct((n, HALF), jnp.uint32),
                   jax.ShapeDtypeStruct((SUBLANES, n), F32),
                   jax.ShapeDtypeStruct((n, LANES), F32),
                   jax.ShapeDtypeStruct((SUBLANES, LANES), F32)],
        scratch_shapes=[pltpu.VMEM((1, LANES), F32)],
        compiler_params=pltpu.CompilerParams(dimension_semantics=("arbitrary",),
                                             vmem_limit_bytes=VMEM_LIMIT),
        name="mix_route",
    )(x2d, conv_out, o, sr, gc, gg, *consts)


def _row_copy(src_ref, src_row, dst_ref, dst_row, sem):
    return pltpu.make_async_copy(src_ref.at[pl.ds(src_row, 1), :], dst_ref.at[pl.ds(dst_row, 1), :], sem)


def _dispatch_kernel(p0_ref, p1_ref, xa_ref, xb_ref, xs_ref, sem, *, tiles_a):
    tm = xa_ref.shape[0]
    i = pl.program_id(0)
    base = i * tm

    def scatter(src_ref):
        def rows(wait):
            def body(t, carry):
                c0 = _row_copy(src_ref, t, xs_ref, p0_ref[base + t], sem)
                c1 = _row_copy(src_ref, t, xs_ref, p1_ref[base + t], sem)
                if wait:
                    c0.wait()
                    c1.wait()
                else:
                    c0.start()
                    c1.start()
                return carry
            lax.fori_loop(0, tm, body, 0, unroll=DMA_UNROLL)

        rows(wait=False)
        rows(wait=True)

    @pl.when(i < tiles_a)
    def _():
        scatter(xa_ref)

    @pl.when(i >= tiles_a)
    def _():
        scatter(xb_ref)


def _dispatch(pos0, pos1, x_a, x_b):
    tm = TOK_TILE
    tiles_a, tiles_b = x_a.shape[0] // tm, x_b.shape[0] // tm
    n = x_a.shape[0] + x_b.shape[0]
    width = x_a.shape[1]
    return pl.pallas_call(
        functools.partial(_dispatch_kernel, tiles_a=tiles_a),
        grid_spec=pltpu.PrefetchScalarGridSpec(
            num_scalar_prefetch=2,
            grid=(tiles_a + tiles_b,),
            in_specs=[pl.BlockSpec((tm, width), lambda i, p0, p1: (jnp.minimum(i, tiles_a - 1), 0)),
                      pl.BlockSpec((tm, width), lambda i, p0, p1: (jnp.maximum(i - tiles_a, 0), 0))],
            out_specs=pl.BlockSpec(memory_space=pl.ANY),
            scratch_shapes=[pltpu.SemaphoreType.DMA(())],
        ),
        out_shape=jax.ShapeDtypeStruct((2 * n, width), x_a.dtype),
        compiler_params=pltpu.CompilerParams(dimension_semantics=("arbitrary",)),
        name="dispatch",
    )(pos0, pos1, x_a, x_b)


def _combine_kernel(p0_ref, p1_ref, h_ref, wf_ref, g_ref, ys_ref, out_ref, b0_ref, b1_ref, sems, *, tok_offset):
    tm = h_ref.shape[0]
    i = pl.program_id(0)

    def gather(tile, slot, wait):
        base = tok_offset + tile * tm

        def rows(t, carry):
            c0 = _row_copy(ys_ref, p0_ref[base + t], b0_ref.at[slot], t, sems.at[slot])
            c1 = _row_copy(ys_ref, p1_ref[base + t], b1_ref.at[slot], t, sems.at[slot])
            if wait:
                c0.wait()
                c1.wait()
            else:
                c0.start()
                c1.start()
            return carry

        lax.fori_loop(0, tm, rows, 0, unroll=DMA_UNROLL)

    slot = lax.rem(i, 2)

    @pl.when(i == 0)
    def _():
        gather(i, slot, wait=False)

    @pl.when(i + 1 < pl.num_programs(0))
    def _():
        gather(i + 1, 1 - slot, wait=False)

    gather(i, slot, wait=True)
    y = h_ref[...] + (b0_ref[slot] * wf_ref[:, 0:1] + b1_ref[slot] * wf_ref[:, 1:2])
    out_ref[...] = y * lax.rsqrt(jnp.mean(y * y, axis=-1, keepdims=True) + EPS) * g_ref[...]


def _combine(pos0, pos1, h, wf, norm_g, ys, tok_offset):
    n_rows = h.shape[0]
    tm = TOK_TILE
    return pl.pallas_call(
        functools.partial(_combine_kernel, tok_offset=tok_offset),
        grid_spec=pltpu.PrefetchScalarGridSpec(
            num_scalar_prefetch=2,
            grid=(n_rows // tm,),
            in_specs=[pl.BlockSpec((tm, D_MODEL), lambda i, p0, p1: (i, 0)),
                      pl.BlockSpec((tm, LANES), lambda i, p0, p1: (i, 0)),
                      pl.BlockSpec((1, D_MODEL), lambda i, p0, p1: (0, 0)),
                      pl.BlockSpec(memory_space=pl.ANY)],
            out_specs=pl.BlockSpec((tm, D_MODEL), lambda i, p0, p1: (i, 0)),
            scratch_shapes=[pltpu.VMEM((2, tm, D_MODEL), F32), pltpu.VMEM((2, tm, D_MODEL), F32),
                            pltpu.SemaphoreType.DMA((2,))],
        ),
        out_shape=jax.ShapeDtypeStruct((n_rows, D_MODEL), F32),
        compiler_params=pltpu.CompilerParams(dimension_semantics=("arbitrary",),
                                             vmem_limit_bytes=VMEM_LIMIT),
        name="combine",
    )(pos0, pos1, h, wf, norm_g, ys)


def _expert_kernel(blk_ref, exp_ref, lo_ref, hi_ref, first_ref,
                   xs_ref, wg_ref, wu_ref, wd_ref, ys_ref, wgb_ref, wub_ref, wdb_ref):
    w = pl.program_id(0)
    e = exp_ref[w]
    changed = jnp.logical_or(w == 0, e != exp_ref[jnp.maximum(w - 1, 0)])

    @pl.when(changed)
    def _():
        wgb_ref[...] = wg_ref[0].astype(BF16)
        wub_ref[...] = wu_ref[0].astype(BF16)
        wdb_ref[...] = wd_ref[0].astype(BF16)

    lo = lo_ref[w]
    hi = hi_ref[w]

    @pl.when(hi > lo)
    def _():
        packed = xs_ref[...]
        x_top = lax.bitcast_convert_type(packed & jnp.uint32(0xFFFF0000), F32)
        x_bot = lax.bitcast_convert_type(jnp.left_shift(packed, jnp.uint32(16)), F32)
        x = jnp.concatenate([x_top, x_bot], axis=1).astype(BF16)
        gate = _dot(x, wgb_ref[...])
        hid = (gate * _sigmoid(gate)) * _dot(x, wub_ref[...])
        y = _dot(hid.astype(BF16), wdb_ref[...])
        rows = blk_ref[w] * ROW_BLOCK + lax.broadcasted_iota(jnp.int32, (ROW_BLOCK, 1), 0)
        y = jnp.where((rows >= lo) & (rows < hi), y, 0.0)

        @pl.when(first_ref[w] == 1)
        def _():
            ys_ref[...] = y

        @pl.when(first_ref[w] == 0)
        def _():
            ys_ref[...] = ys_ref[...] + y


def _experts(items, xs, w_gate, w_up, w_down):
    n_rows = xs.shape[0]
    n_items = items[0].shape[0]
    return pl.pallas_call(
        _expert_kernel,
        grid_spec=pltpu.PrefetchScalarGridSpec(
            num_scalar_prefetch=5,
            grid=(n_items,),
            in_specs=[pl.BlockSpec((ROW_BLOCK, HALF), lambda w, blk, ex, lo, hi, fi: (blk[w], 0)),
                      pl.BlockSpec((1, D_MODEL, D_EXPERT), lambda w, blk, ex, lo, hi, fi: (ex[w], 0, 0)),
                      pl.BlockSpec((1, D_MODEL, D_EXPERT), lambda w, blk, ex, lo, hi, fi: (ex[w], 0, 0)),
                      pl.BlockSpec((1, D_EXPERT, D_MODEL), lambda w, blk, ex, lo, hi, fi: (ex[w], 0, 0))],
            out_specs=pl.BlockSpec((ROW_BLOCK, D_MODEL), lambda w, blk, ex, lo, hi, fi: (blk[w], 0)),
            scratch_shapes=[pltpu.VMEM((D_MODEL, D_EXPERT), BF16), pltpu.VMEM((D_MODEL, D_EXPERT), BF16),
                            pltpu.VMEM((D_EXPERT, D_MODEL), BF16)],
        ),
        out_shape=jax.ShapeDtypeStruct((n_rows, D_MODEL), F32),
        compiler_params=pltpu.CompilerParams(dimension_semantics=("arbitrary",),
                                             vmem_limit_bytes=VMEM_LIMIT),
        name="experts",
    )(*items, xs, w_gate, w_up, w_down)


def _work_items(counts, n_rows):
    n_blocks = n_rows // ROW_BLOCK
    n_items = n_blocks + N_EXPERTS - 1
    ends = jnp.cumsum(counts)
    starts = ends - counts
    blk_lo = starts // ROW_BLOCK
    blk_hi = jnp.maximum(ends - 1, 0) // ROW_BLOCK
    per_e = jnp.where(counts > 0, blk_hi - blk_lo + 1, 0)
    item_end = jnp.cumsum(per_e)
    item_start = item_end - per_e
    total = item_end[-1]
    w = jnp.arange(n_items, dtype=jnp.int32)
    live = w < total
    e = jnp.sum((jnp.minimum(w, total - 1)[:, None] >= item_end[None, :]).astype(jnp.int32), axis=1)
    e = jnp.minimum(e, N_EXPERTS - 1)
    sel = (e[:, None] == jnp.arange(N_EXPERTS, dtype=jnp.int32)[None, :]).astype(jnp.int32)
    pick = lambda table: jnp.sum(sel * table[None, :], axis=1)
    blk = jnp.where(live, pick(blk_lo) + (w - pick(item_start)), n_blocks - 1).astype(jnp.int32)
    lo = jnp.where(live, jnp.maximum(pick(starts), blk * ROW_BLOCK), 0).astype(jnp.int32)
    hi = jnp.where(live, jnp.minimum(pick(ends), (blk + 1) * ROW_BLOCK), 0).astype(jnp.int32)
    prev_blk = jnp.concatenate([jnp.full((1,), -1, jnp.int32), blk[:-1]])
    first = (live & (blk != prev_blk)).astype(jnp.int32)
    return (blk, e.astype(jnp.int32), lo, hi, first), starts


def kernel(x_prompt, x_sample, state_conv, state_gla, norm_mix, w_in, b_gates, w_gla_gate_up, b_gla_gate_up, conv_dw, conv_db, conv_ln_g, conv_ln_b, w_conv_out, gla_norm_g, w_gla_out, w_out, norm_ffn, w_router_group, b_router_group, w_router_expert, b_router_expert, w_expert_gate, w_expert_up, w_expert_down, norm_final):
    depth = norm_mix.shape[0]
    assert depth == 1
    l = 0
    bp, seq_p, _ = x_prompt.shape
    bs, seq_s, _ = x_sample.shape
    n_p, n_s = bp * seq_p, bs * seq_s
    n_tok = n_p + n_s
    row2 = lambda a: a.reshape(1, -1)

    wi = w_in[l]
    w_main = wi[:, :_C_ALOW].astype(BF16)
    w_alow = jnp.pad(wi[:, _C_ALOW:_C_GATES], ((0, 0), (0, LANES - RANK))).astype(BF16)
    w_gates = wi[:, _C_GATES:].astype(BF16)
    w_up_pad = jnp.pad(w_gla_gate_up[l], ((0, LANES - RANK), (0, 0))).astype(BF16)
    w_route = jnp.pad(jnp.concatenate([w_router_expert[l], w_router_group[l]], axis=1),
                      ((0, 0), (0, LANES - N_EXPERTS - N_GROUPS)))
    w_route_hi = w_route.astype(BF16)
    w_route = jnp.stack([w_route_hi, (w_route - w_route_hi.astype(F32)).astype(BF16)])
    b_route = jnp.pad(jnp.concatenate([b_router_expert[l], b_router_group[l]]),
                      (0, LANES - N_EXPERTS - N_GROUPS)).reshape(1, LANES)
    w_pw = w_conv_out[l].astype(BF16)
    w_go = w_gla_out[l].astype(BF16)
    w_o = w_out[l].astype(BF16)
    conv_args = (conv_dw[l], row2(conv_db[l]), row2(conv_ln_g[l]), row2(conv_ln_b[l]), w_pw)
    proj_args = (row2(norm_mix[l]), w_main, w_alow, w_gates, w_up_pad, row2(b_gla_gate_up[l]),
                 row2(b_gates[l]))
    mix_args = (row2(gla_norm_g[l]), w_go, w_o, row2(norm_ffn[l]), w_route, b_route)

    xp = x_prompt.reshape(n_p, D_MODEL)
    u, q, k, v, la, sr, gc, gg = _proj(xp, *proj_args)
    conv_out, conv_p = _conv(u.reshape(bp, seq_p, D_CONV), None, *conv_args, nseq=1)
    o, gla_p = _gla_prompt(q, k, la, v, bp, seq_p)
    h_p, xr_p, rec_p, wf_p, cnt = _mix_route(xp, conv_out.reshape(n_p, D_MODEL), o, sr, gc, gg, *mix_args,
                                             jnp.zeros((SUBLANES, LANES), F32))

    xs_ = x_sample.reshape(n_s, D_MODEL)
    u, q, k, v, la, sr, gc, gg = _proj(xs_, *proj_args)
    conv_out, conv_s = _conv(u.reshape(bs, seq_s, D_CONV), state_conv, *conv_args, nseq=SAMPLE_SEQS)
    seqs = lambda a: a.reshape(bs, seq_s, a.shape[-1])
    o, gla_s = _gla_sample(seqs(q), seqs(k), seqs(la), seqs(v), state_gla)
    h_s, xr_s, rec_s, wf_s, cnt = _mix_route(xs_, conv_out.reshape(n_s, D_MODEL), o.reshape(n_s, V_W), sr, gc, gg,
                                             *mix_args, cnt)

    counts = cnt[0, :N_EXPERTS].astype(jnp.int32)
    items, starts = _work_items(counts, 2 * n_tok)
    rec = jnp.concatenate([rec_p, rec_s], axis=1).astype(jnp.int32)
    e_ids = jnp.arange(N_EXPERTS, dtype=jnp.int32)[:, None]
    start_of = lambda e: jnp.sum(jnp.where(e[None, :] == e_ids, starts[:, None], 0), axis=0)
    pos0 = start_of(rec[_L_E0]) + rec[_L_R0]
    pos1 = start_of(rec[_L_E1]) + rec[_L_R1]
    xs_sorted = _dispatch(pos0, pos1, xr_p, xr_s)
    ys = _experts(items, xs_sorted, w_expert_gate[l], w_expert_up[l], w_expert_down[l])
    y_p = _combine(pos0, pos1, h_p, wf_p, row2(norm_final), ys, 0)
    y_s = _combine(pos0, pos1, h_s, wf_s, row2(norm_final), ys, n_p)

    return (y_p.reshape(bp, seq_p, D_MODEL), y_s.reshape(bs, seq_s, D_MODEL),
            conv_p[None], gla_p[None], conv_s[None], gla_s[None])
```

```python
import functools

import jax
import jax.numpy as jnp
from jax import lax
from jax.experimental import pallas as pl
from jax.experimental.pallas import tpu as pltpu

F32 = jnp.float32
BF16 = jnp.bfloat16

D_MODEL = 1024
D_CONV = 512
CONV_WIDTH = 31
CONV_HIST = CONV_WIDTH - 1
HEADS = 4
DK = 128
DV = 256
QK_W = HEADS * DK
V_W = HEADS * DV
RANK = 16
GATE_NORM = 16.0
N_GROUPS = 4
EXPERTS_PER_GROUP = 8
N_EXPERTS = 32
D_EXPERT = 512
EPS = 1e-6

LANES = 128
SUBLANES = 8
VMEM_LIMIT = 56 * 1024 * 1024

TOK_TILE = 256
GLA_CHUNK = 128
GLA_STEP = 256
SAFE_LOG_DECAY = -80.0
CONV_ROWS = 64
CONV_HALO = 40
ROW_BLOCK = 256
DMA_UNROLL = 8
SAMPLE_SEQS = 32
SAMPLE_GLA_SEQS = 8
SAMPLE_PAD = 16

_C_GLU_A, _C_GLU_B = 0, 512
_C_Q, _C_K, _C_V, _C_R = 1024, 1536, 2048, 3072
_C_ALOW = 4096
_C_GATES = _C_ALOW + RANK


def _sigmoid(x):
    return jax.nn.sigmoid(x)


def _dot(a, b):
    return jnp.dot(a, b, preferred_element_type=F32)


def _proj_kernel(x_ref, g_ref, wm_ref, wa_ref, wg_ref, wup_ref, bup_ref, bg_ref,
                 u_ref, q_ref, k_ref, v_ref, la_ref, sr_ref, gc_ref, gg_ref):
    x = x_ref[...]
    hn = x * lax.rsqrt(jnp.mean(x * x, axis=-1, keepdims=True) + EPS) * g_ref[...]
    hb = hn.astype(BF16)

    def mm(lo, hi):
        return _dot(hb, wm_ref[:, lo:hi])

    u_ref[...] = mm(_C_GLU_A, _C_GLU_B) * _sigmoid(mm(_C_GLU_B, _C_Q))
    q_ref[...] = mm(_C_Q, _C_K) * (DK ** -0.5)
    k_ref[...] = mm(_C_K, _C_V)
    v_ref[...] = mm(_C_V, _C_R).astype(BF16)
    r = mm(_C_R, _C_ALOW)
    sr_ref[...] = (r * _sigmoid(r)).astype(BF16)
    gc_ref[...] = _sigmoid(_dot(hb, wg_ref[:, :D_MODEL]) + bg_ref[:, :D_MODEL]).astype(BF16)
    gg_ref[...] = _sigmoid(_dot(hb, wg_ref[:, D_MODEL:]) + bg_ref[:, D_MODEL:]).astype(BF16)
    a_low = _dot(hb, wa_ref[...])
    z = _dot(a_low.astype(BF16), wup_ref[...]) + bup_ref[...]
    la_ref[...] = (jnp.minimum(z, 0.0) - jnp.log1p(jnp.exp(-jnp.abs(z)))) * (1.0 / GATE_NORM)


def _proj(x2d, norm_g, w_main, w_alow, w_gates, w_up, b_up, b_gates):
    n = x2d.shape[0]
    tm = TOK_TILE
    row = lambda w: pl.BlockSpec((tm, w), lambda i: (i, 0))
    full = lambda a: pl.BlockSpec(a.shape, lambda i: (0,) * a.ndim)
    widths = (D_CONV, QK_W, QK_W, V_W, QK_W, V_W, D_MODEL, D_MODEL)
    dtypes = (F32, F32, F32, BF16, F32, BF16, BF16, BF16)
    consts = (norm_g, w_main, w_alow, w_gates, w_up, b_up, b_gates)
    return pl.pallas_call(
        _proj_kernel,
        grid=(n // tm,),
        in_specs=[row(D_MODEL)] + [full(a) for a in consts],
        out_specs=[row(w) for w in widths],
        out_shape=[jax.ShapeDtypeStruct((n, w), dt) for w, dt in zip(widths, dtypes)],
        compiler_params=pltpu.CompilerParams(dimension_semantics=("parallel",),
                                             vmem_limit_bytes=VMEM_LIMIT),
        name="proj",
    )(x2d, *consts)


def _conv_kernel(*refs, seq, nseq, has_state):
    if has_state:
        u_ref, st_ref, dw_ref, db_ref, lg_ref, lb_ref, wpw_ref, out_ref, ns_ref, full_ref, y_ref = refs
    else:
        u_ref, dw_ref, db_ref, lg_ref, lb_ref, wpw_ref, out_ref, ns_ref, full_ref, y_ref = refs
    pad = 32
    hist0 = pad - CONV_HIST
    n_rows = y_ref.shape[0] // nseq
    rt = min(CONV_ROWS, n_rows)
    win = rt + CONV_HALO
    tail = full_ref.shape[0] - (pad + seq)

    def one_seq(b, carry):
        full_ref[0:pad, :] = jnp.zeros((pad, D_CONV), F32)
        full_ref[pad + seq:, :] = jnp.zeros((tail, D_CONV), F32)
        if has_state:
            full_ref[hist0:pad, :] = st_ref[b]
        full_ref[pad:pad + seq, :] = u_ref[b]
        ns_ref[b] = full_ref[seq + hist0:seq + pad, :]
        y0 = b * n_rows

        def row_tile(i, carry):
            base = pl.multiple_of(i * rt, SUBLANES)
            for c in range(D_CONV // LANES):
                cs = slice(c * LANES, (c + 1) * LANES)
                w = full_ref[pl.ds(base, win), cs]
                acc = jnp.zeros((rt, LANES), F32)
                for s in range(SUBLANES):
                    ws = w if s == 0 else pltpu.roll(w, win - s, axis=0)
                    for a in range((CONV_HALO // SUBLANES)):
                        j = a * SUBLANES + s - hist0
                        if 0 <= j < CONV_WIDTH:
                            acc = acc + ws[a * SUBLANES:a * SUBLANES + rt, :] * dw_ref[j:j + 1, cs]
                y_ref[pl.ds(pl.multiple_of(y0 + base, SUBLANES), rt), cs] = acc + db_ref[:, cs]
            return carry

        return _loop(n_rows // rt, row_tile, carry)

    _loop(nseq, one_seq, 0)

    mt = min(TOK_TILE, nseq * n_rows)
    per_tile = mt // n_rows

    def norm_tile(i, carry):
        base = pl.multiple_of(i * mt, SUBLANES)
        y = y_ref[pl.ds(base, mt), :]
        mu = jnp.mean(y, axis=-1, keepdims=True)
        var = jnp.mean(jnp.square(y - mu), axis=-1, keepdims=True)
        yn = (y - mu) * lax.rsqrt(var + EPS) * lg_ref[...] + lb_ref[...]
        act = (yn * _sigmoid(yn)).astype(BF16)
        res = _dot(act, wpw_ref[...])
        if per_tile == 0:
            out_ref[0, pl.ds(base, mt), :] = res
        else:
            for j in range(per_tile):
                out_ref[i * per_tile + j] = res[j * n_rows:j * n_rows + seq]
        return carry

    _loop(nseq * n_rows // mt, norm_tile, 0)


def _loop(trips, body, carry):
    if trips == 1:
        return body(0, carry)
    return lax.fori_loop(0, trips, body, carry)


def _conv(u3, state, dw, db, lg, lb, w_pw, nseq):
    nb, seq, _ = u3.shape
    n_rows = -(-seq // SUBLANES) * SUBLANES
    assert nseq == 1 or (TOK_TILE % n_rows == 0 and (nseq * n_rows) % TOK_TILE == 0)
    has_state = state is not None
    full = lambda a: pl.BlockSpec(a.shape, lambda b: (0,) * a.ndim)
    per_b = lambda r, w: pl.BlockSpec((nseq, r, w), lambda b: (b, 0, 0))
    in_specs = [per_b(seq, D_CONV)]
    args = [u3]
    if has_state:
        in_specs.append(pl.BlockSpec((None, nseq, CONV_HIST, D_CONV), lambda b: (0, b, 0, 0)))
        args.append(state)
    in_specs += [full(dw), full(db), full(lg), full(lb), full(w_pw)]
    args += [dw, db, lg, lb, w_pw]
    return pl.pallas_call(
        functools.partial(_conv_kernel, seq=seq, nseq=nseq, has_state=has_state),
        grid=(nb // nseq,),
        in_specs=in_specs,
        out_specs=[per_b(seq, D_MODEL), per_b(CONV_HIST, D_CONV)],
        out_shape=[jax.ShapeDtypeStruct((nb, seq, D_MODEL), F32),
                   jax.ShapeDtypeStruct((nb, CONV_HIST, D_CONV), F32)],
        scratch_shapes=[pltpu.VMEM((n_rows + CONV_HALO, D_CONV), F32),
                        pltpu.VMEM((nseq * n_rows, D_CONV), F32)],
        compiler_params=pltpu.CompilerParams(dimension_semantics=("parallel",),
                                             vmem_limit_bytes=VMEM_LIMIT),
        name="conv_state" if has_state else "conv_fresh",
    )(*args)


def _token_step(s, a_col, k_col, q_col, v_row):
    s = s * a_col + k_col * v_row
    return s, jnp.sum(q_col * s, axis=0, keepdims=True)


def _gla_prompt_kernel(q_ref, k_ref, la_ref, v_ref, o_ref, sout_ref, s_ref, vf_ref):
    c = pl.program_id(1)
    n = GLA_CHUNK
    chunks = [slice(i * n, (i + 1) * n) for i in range(q_ref.shape[0] // n)]

    @pl.when(c == 0)
    def _():
        s_ref[...] = jnp.zeros_like(s_ref)

    r_i = lax.broadcasted_iota(jnp.int32, (n, n), 0)
    c_i = lax.broadcasted_iota(jnp.int32, (n, n), 1)
    tri = jnp.where(r_i >= c_i, 1.0, 0.0).astype(BF16)

    def prefix_sum(g):
        hi = g.astype(BF16)
        lo = (g - hi.astype(F32)).astype(BF16)
        return _dot(tri, hi) + _dot(tri, lo)

    bs = [prefix_sum(la_ref[rows, :]) for rows in chunks]
    total = bs[0][n - 1:n, :]
    for b in bs[1:]:
        total = jnp.minimum(total, b[n - 1:n, :])
    fast = jnp.min(total) > SAFE_LOG_DECAY

    @pl.when(fast)
    def _():
        for h in range(HEADS):
            ks = slice(h * DK, (h + 1) * DK)
            vs = slice(h * DV, (h + 1) * DV)
            s = s_ref[h]
            for rows, b in zip(chunks, bs):
                bh = b[:, ks]
                bl = bh[n - 1:n, :]
                kh = k_ref[rows, ks]
                vh = v_ref[rows, vs]
                qp = (q_ref[rows, ks] * jnp.exp(bh)).astype(BF16)
                kp = (kh * jnp.exp(-bh)).astype(BF16)
                kpp = kh * jnp.exp(bl - bh)
                att = lax.dot_general(qp, kp, (((1,), (1,)), ((), ())), preferred_element_type=F32)
                att = jnp.where(r_i >= c_i, att, 0.0).astype(BF16)
                o_ref[rows, vs] = _dot(qp, s.astype(BF16)) + _dot(att, vh)
                d_col = jnp.sum(jnp.where(r_i == c_i, jnp.exp(bl), 0.0), axis=1, keepdims=True)
                s = s * d_col + _dot(kpp.T.astype(BF16), vh)
            s_ref[h] = s

    @pl.when(jnp.logical_not(fast))
    def _():
        lane = lax.broadcasted_iota(jnp.int32, (1, n), 1)
        for rows in chunks:
            vf_ref[...] = v_ref[rows, :].astype(F32)
            for h in range(HEADS):
                ks = slice(h * DK, (h + 1) * DK)
                vs = slice(h * DV, (h + 1) * DV)
                q_t = q_ref[rows, ks].T
                k_t = k_ref[rows, ks].T
                a_t = jnp.exp(la_ref[rows, ks]).T

                def body(t, s):
                    m = lane == t
                    col = lambda x: jnp.sum(jnp.where(m, x, 0.0), axis=1, keepdims=True)
                    s, o_row = _token_step(s, col(a_t), col(k_t), col(q_t), vf_ref[pl.ds(t, 1), vs])
                    o_ref[pl.ds(rows.start + t, 1), vs] = o_row
                    return s

                s_ref[h] = lax.fori_loop(0, n, body, s_ref[h])

    @pl.when(c == pl.num_programs(1) - 1)
    def _():
        sout_ref[0] = s_ref[...]


def _gla_prompt(q, k, la, v, nb, seq):
    nc = seq // GLA_STEP
    tok = lambda w: pl.BlockSpec((GLA_STEP, w), lambda b, c: (b * nc + c, 0))
    return pl.pallas_call(
        _gla_prompt_kernel,
        grid=(nb, nc),
        in_specs=[tok(QK_W), tok(QK_W), tok(QK_W), tok(V_W)],
        out_specs=[tok(V_W), pl.BlockSpec((1, HEADS, DK, DV), lambda b, c: (b, 0, 0, 0))],
        out_shape=[jax.ShapeDtypeStruct((nb * seq, V_W), F32),
                   jax.ShapeDtypeStruct((nb, HEADS, DK, DV), F32)],
        scratch_shapes=[pltpu.VMEM((HEADS, DK, DV), F32), pltpu.VMEM((GLA_CHUNK, V_W), F32)],
        compiler_params=pltpu.CompilerParams(dimension_semantics=("parallel", "arbitrary"),
                                             vmem_limit_bytes=VMEM_LIMIT),
        name="gla_prompt",
    )(q, k, la, v)


def _gla_sample_kernel(q_ref, k_ref, la_ref, v_ref, s_ref, o_ref, sout_ref, *, seq):
    p = SAMPLE_PAD
    row = lax.broadcasted_iota(jnp.int32, (p, 1), 0)

    def padded(x):
        return jnp.concatenate([x, jnp.zeros((p - seq, x.shape[1]), F32)], axis=0)

    for b in range(q_ref.shape[0]):
        q, k, v, g = padded(q_ref[b]), padded(k_ref[b]), padded(v_ref[b].astype(F32)), padded(la_ref[b])
        bc = g
        sh = 1
        while sh < seq:
            bc = bc + jnp.where(row >= sh, pltpu.roll(bc, sh, axis=0), 0.0)
            sh *= 2
        b_last = bc[seq - 1:seq, :]
        bc = jnp.where(row < seq, bc, b_last)
        qp = (q * jnp.exp(bc)).astype(BF16)
        kpp_t = (k * jnp.exp(b_last - bc)).T.astype(BF16)
        d_cols = jnp.broadcast_to(jnp.exp(b_last), (SUBLANES, QK_W)).T
        vb = v.astype(BF16)

        o_in = [jnp.zeros((p, DV), F32) for _ in range(HEADS)]
        for d in range(seq):
            k_d = k if d == 0 else pltpu.roll(k, d, axis=0)
            v_d = v if d == 0 else pltpu.roll(v, d, axis=0)
            b_d = bc if d == 0 else pltpu.roll(bc, d, axis=0)
            pair = q * k_d * jnp.exp(jnp.where(row >= d, bc - b_d, -jnp.inf))
            for h in range(HEADS):
                att = jnp.sum(pair[:, h * DK:(h + 1) * DK], axis=1, keepdims=True)
                o_in[h] = o_in[h] + att * v_d[:, h * DV:(h + 1) * DV]

        for h in range(HEADS):
            ks = slice(h * DK, (h + 1) * DK)
            vs = slice(h * DV, (h + 1) * DV)
            s = s_ref[b, h]
            o = _dot(qp[:, ks], s.astype(BF16)) + o_in[h]
            o_ref[b, :, vs] = o[:seq]
            sout_ref[b, h] = s * d_cols[ks, 0:1] + _dot(kpp_t[ks, :], vb[:, vs])


def _gla_sample(q3, k3, la3, v3, state):
    nb, seq, _ = q3.shape
    assert seq <= SUBLANES
    ns = SAMPLE_GLA_SEQS
    col = pl.BlockSpec((ns, seq, QK_W), lambda b: (b, 0, 0))
    tok = pl.BlockSpec((ns, seq, V_W), lambda b: (b, 0, 0))
    st = pl.BlockSpec((ns, HEADS, DK, DV), lambda b: (b, 0, 0, 0))
    st_in = pl.BlockSpec((None, ns, HEADS, DK, DV), lambda b: (0, b, 0, 0, 0))
    return pl.pallas_call(
        functools.partial(_gla_sample_kernel, seq=seq),
        grid=(nb // ns,),
        in_specs=[col, col, col, tok, st_in],
        out_specs=[tok, st],
        out_shape=[jax.ShapeDtypeStruct((nb, seq, V_W), F32),
                   jax.ShapeDtypeStruct((nb, HEADS, DK, DV), F32)],
        compiler_params=pltpu.CompilerParams(dimension_semantics=("parallel",),
                                             vmem_limit_bytes=VMEM_LIMIT),
        name="gla_sample",
    )(q3, k3, la3, v3, state)


_L_E0, _L_E1, _L_R0, _L_R1 = 0, 1, 2, 3
_GROUP_LANE0 = N_EXPERTS
HALF = D_MODEL // 2


def _mix_route_kernel(x_ref, co_ref, o_ref, sr_ref, gc_ref, gg_ref, gn_ref, wgo_ref, wo_ref,
                      g_ref, wr_ref, br_ref, cin_ref,
                      h_ref, xp_ref, mi_ref, mf_ref, cnt_ref, carry_ref):
    i = pl.program_id(0)
    tm = x_ref.shape[0]

    @pl.when(i == 0)
    def _():
        carry_ref[...] = cin_ref[0:1, :]

    gla_out = None
    for h in range(HEADS):
        vs = slice(h * DV, (h + 1) * DV)
        oh = o_ref[:, vs]
        on = oh * lax.rsqrt(jnp.mean(oh * oh, axis=-1, keepdims=True) + EPS) * gn_ref[:, vs]
        part = _dot((on * sr_ref[:, vs]).astype(BF16), wgo_ref[vs, :])
        gla_out = part if gla_out is None else gla_out + part
    mix = (gc_ref[...] * co_ref[...] + gg_ref[...] * gla_out).astype(BF16)
    x = x_ref[...] + _dot(mix, wo_ref[...])
    h_ref[...] = x

    hn = x * lax.rsqrt(jnp.mean(x * x, axis=-1, keepdims=True) + EPS) * g_ref[...]
    hi = hn.astype(BF16)
    hi_f = hi.astype(F32)
    top = lax.bitcast_convert_type(hi_f[:, :HALF], jnp.uint32)
    bot = lax.bitcast_convert_type(hi_f[:, HALF:], jnp.uint32)
    xp_ref[...] = top | jnp.right_shift(bot, jnp.uint32(16))
    lo = (hn - hi_f).astype(BF16)
    logits = _dot(hi, wr_ref[0]) + (_dot(lo, wr_ref[0]) + _dot(hi, wr_ref[1])) + br_ref[...]
    lane = lax.broadcasted_iota(jnp.int32, (tm, LANES), 1)
    lane_f = lane.astype(F32)
    group_f = jnp.right_shift(lane, 3).astype(F32)
    big = jnp.float32(LANES)
    neg = jnp.float32(-jnp.inf)

    is_g = (lane >= _GROUP_LANE0) & (lane < _GROUP_LANE0 + N_GROUPS)
    lg = jnp.where(is_g, logits, neg)
    mg = jnp.max(lg, axis=1, keepdims=True)
    p_sel = 1.0 / jnp.sum(jnp.exp(lg - mg), axis=1, keepdims=True)
    g_idx = jnp.min(jnp.where(lg == mg, lane_f, big), axis=1, keepdims=True) - _GROUP_LANE0

    is_e = (lane < N_EXPERTS) & (group_f == g_idx)
    le = jnp.where(is_e, logits, neg)
    v0 = jnp.max(le, axis=1, keepdims=True)
    i0 = jnp.min(jnp.where(le == v0, lane_f, big), axis=1, keepdims=True)
    le1 = jnp.where(lane_f == i0, neg, le)
    v1 = jnp.max(le1, axis=1, keepdims=True)
    i1 = jnp.min(jnp.where(le1 == v1, lane_f, big), axis=1, keepdims=True)
    e1 = jnp.exp(v1 - v0)
    den = 1.0 + e1
    w0 = p_sel * (1.0 / den)
    w1 = p_sel * (e1 / den)

    hit0 = lane_f == i0
    hit1 = lane_f == i1
    cnt = jnp.where(hit0 | hit1, 1.0, 0.0)
    r_i = lax.broadcasted_iota(jnp.int32, (tm, tm), 0)
    c_i = lax.broadcasted_iota(jnp.int32, (tm, tm), 1)
    before = jnp.where(r_i > c_i, 1.0, 0.0).astype(BF16)
    seen = _dot(before, cnt.astype(BF16)) + carry_ref[...]
    rank0 = jnp.sum(jnp.where(hit0, seen, 0.0), axis=1, keepdims=True)
    rank1 = jnp.sum(jnp.where(hit1, seen, 0.0), axis=1, keepdims=True)
    carry_ref[...] = carry_ref[...] + jnp.sum(cnt, axis=0, keepdims=True)
    cnt_ref[...] = jnp.broadcast_to(carry_ref[...], cnt_ref.shape)

    rec = jnp.where(lane == _L_E0, i0, 0.0)
    rec = jnp.where(lane == _L_E1, i1, rec)
    rec = jnp.where(lane == _L_R0, rank0, rec)
    rec = jnp.where(lane == _L_R1, rank1, rec)
    mi_ref[...] = rec.T[:SUBLANES, :]
    mf_ref[...] = jnp.where(lane == 0, w0, jnp.where(lane == 1, w1, 0.0))


def _mix_route(x2d, conv_out, o, sr, gc, gg, gn, w_gla_out, w_out, norm_g, w_route, b_route, counts_in):
    n = x2d.shape[0]
    tm = TOK_TILE
    row = lambda w: pl.BlockSpec((tm, w), lambda i: (i, 0))
    full = lambda a: pl.BlockSpec(a.shape, lambda i: (0,) * a.ndim)
    consts = (gn, w_gla_out, w_out, norm_g, w_route, b_route, counts_in)
    return pl.pallas_call(
        _mix_route_kernel,
        grid=(n // tm,),
        in_specs=[row(D_MODEL)] * 6 + [full(a) for a in consts],
        out_specs=[row(D_MODEL), row(HALF), pl.BlockSpec((SUBLANES, tm), lambda i: (0, i)), row(LANES),
                   pl.BlockSpec((SUBLANES, LANES), lambda i: (0, 0))],
        out_shape=[jax.ShapeDtypeStruct((n, D_MODEL), F32),
                   jax.ShapeDtypeStruct((n, HALF), jnp.uint32),
                   jax.ShapeDtypeStruct((SUBLANES, n), F32),
                   jax.ShapeDtypeStruct((n, LANES), F32),
                   jax.ShapeDtypeStruct((SUBLANES, LANES), F32)],
        scratch_shapes=[pltpu.VMEM((1, LANES), F32)],
        compiler_params=pltpu.CompilerParams(dimension_semantics=("arbitrary",),
                                             vmem_limit_bytes=VMEM_LIMIT),
        name="mix_route",
    )(x2d, conv_out, o, sr, gc, gg, *consts)


def _row_copy(src_ref, src_row, dst_ref, dst_row, sem):
    return pltpu.make_async_copy(src_ref.at[pl.ds(src_row, 1), :], dst_ref.at[pl.ds(dst_row, 1), :], sem)


def _dispatch_kernel(p0_ref, p1_ref, xa_ref, xb_ref, xs_ref, sem, *, tiles_a):
    tm = xa_ref.shape[0]
    i = pl.program_id(0)
    base = i * tm

    def scatter(src_ref):
        def rows(wait):
            def body(t, carry):
                c0 = _row_copy(src_ref, t, xs_ref, p0_ref[base + t], sem)
                c1 = _row_copy(src_ref, t, xs_ref, p1_ref[base + t], sem)
                if wait:
                    c0.wait()
                    c1.wait()
                else:
                    c0.start()
                    c1.start()
                return carry
            lax.fori_loop(0, tm, body, 0, unroll=DMA_UNROLL)

        rows(wait=False)
        rows(wait=True)

    @pl.when(i < tiles_a)
    def _():
        scatter(xa_ref)

    @pl.when(i >= tiles_a)
    def _():
        scatter(xb_ref)


def _dispatch(pos0, pos1, x_a, x_b):
    tm = TOK_TILE
    tiles_a, tiles_b = x_a.shape[0] // tm, x_b.shape[0] // tm
    n = x_a.shape[0] + x_b.shape[0]
    width = x_a.shape[1]
    return pl.pallas_call(
        functools.partial(_dispatch_kernel, tiles_a=tiles_a),
        grid_spec=pltpu.PrefetchScalarGridSpec(
            num_scalar_prefetch=2,
            grid=(tiles_a + tiles_b,),
            in_specs=[pl.BlockSpec((tm, width), lambda i, p0, p1: (jnp.minimum(i, tiles_a - 1), 0)),
                      pl.BlockSpec((tm, width), lambda i, p0, p1: (jnp.maximum(i - tiles_a, 0), 0))],
            out_specs=pl.BlockSpec(memory_space=pl.ANY),
            scratch_shapes=[pltpu.SemaphoreType.DMA(())],
        ),
        out_shape=jax.ShapeDtypeStruct((2 * n, width), x_a.dtype),
        compiler_params=pltpu.CompilerParams(dimension_semantics=("arbitrary",)),
        name="dispatch",
    )(pos0, pos1, x_a, x_b)


def _combine_kernel(p0_ref, p1_ref, h_ref, wf_ref, g_ref, ys_ref, out_ref, b0_ref, b1_ref, sems, *, tok_offset):
    tm = h_ref.shape[0]
    i = pl.program_id(0)

    def gather(tile, slot, wait):
        base = tok_offset + tile * tm

        def rows(t, carry):
            c0 = _row_copy(ys_ref, p0_ref[base + t], b0_ref.at[slot], t, sems.at[slot])
            c1 = _row_copy(ys_ref, p1_ref[base + t], b1_ref.at[slot], t, sems.at[slot])
            if wait:
                c0.wait()
                c1.wait()
            else:
                c0.start()
                c1.start()
            return carry

        lax.fori_loop(0, tm, rows, 0, unroll=DMA_UNROLL)

    slot = lax.rem(i, 2)

    @pl.when(i == 0)
    def _():
        gather(i, slot, wait=False)

    @pl.when(i + 1 < pl.num_programs(0))
    def _():
        gather(i + 1, 1 - slot, wait=False)

    gather(i, slot, wait=True)
    y = h_ref[...] + (b0_ref[slot] * wf_ref[:, 0:1] + b1_ref[slot] * wf_ref[:, 1:2])
    out_ref[...] = y * lax.rsqrt(jnp.mean(y * y, axis=-1, keepdims=True) + EPS) * g_ref[...]


def _combine(pos0, pos1, h, wf, norm_g, ys, tok_offset):
    n_rows = h.shape[0]
    tm = TOK_TILE
    return pl.pallas_call(
        functools.partial(_combine_kernel, tok_offset=tok_offset),
        grid_spec=pltpu.PrefetchScalarGridSpec(
            num_scalar_prefetch=2,
            grid=(n_rows // tm,),
            in_specs=[pl.BlockSpec((tm, D_MODEL), lambda i, p0, p1: (i, 0)),
                      pl.BlockSpec((tm, LANES), lambda i, p0, p1: (i, 0)),
                      pl.BlockSpec((1, D_MODEL), lambda i, p0, p1: (0, 0)),
                      pl.BlockSpec(memory_space=pl.ANY)],
            out_specs=pl.BlockSpec((tm, D_MODEL), lambda i, p0, p1: (i, 0)),
            scratch_shapes=[pltpu.VMEM((2, tm, D_MODEL), F32), pltpu.VMEM((2, tm, D_MODEL), F32),
                            pltpu.SemaphoreType.DMA((2,))],
        ),
        out_shape=jax.ShapeDtypeStruct((n_rows, D_MODEL), F32),
        compiler_params=pltpu.CompilerParams(dimension_semantics=("arbitrary",),
                                             vmem_limit_bytes=VMEM_LIMIT),
        name="combine",
    )(pos0, pos1, h, wf, norm_g, ys)


def _expert_kernel(blk_ref, exp_ref, lo_ref, hi_ref, first_ref, slot_ref, next_ref,
                   xs_ref, wg_hbm, wu_hbm, wd_hbm, ys_ref,
                   wg32_ref, wu32_ref, wd32_ref, wgb_ref, wub_ref, wdb_ref, sems):
    w = pl.program_id(0)
    e = exp_ref[w]
    slot = slot_ref[w]
    changed = jnp.logical_or(w == 0, e != exp_ref[jnp.maximum(w - 1, 0)])

    def weight_copies(expert, s):
        return [pltpu.make_async_copy(src.at[expert], dst.at[s], sems.at[s, j])
                for j, (src, dst) in enumerate(((wg_hbm, wg32_ref), (wu_hbm, wu32_ref), (wd_hbm, wd32_ref)))]

    @pl.when(w == 0)
    def _():
        for c in weight_copies(e, slot):
            c.start()

    @pl.when(changed)
    def _():
        for c in weight_copies(e, slot):
            c.wait()
        wgb_ref[...] = wg32_ref[slot].astype(BF16)
        wub_ref[...] = wu32_ref[slot].astype(BF16)
        wdb_ref[...] = wd32_ref[slot].astype(BF16)

        @pl.when(next_ref[w] >= 0)
        def _():
            for c in weight_copies(next_ref[w], 1 - slot):
                c.start()

    lo = lo_ref[w]
    hi = hi_ref[w]

    @pl.when(hi > lo)
    def _():
        packed = xs_ref[...]
        x_top = lax.bitcast_convert_type(packed & jnp.uint32(0xFFFF0000), F32)
        x_bot = lax.bitcast_convert_type(jnp.left_shift(packed, jnp.uint32(16)), F32)
        x = jnp.concatenate([x_top, x_bot], axis=1).astype(BF16)
        gate = _dot(x, wgb_ref[...])
        hid = (gate * _sigmoid(gate)) * _dot(x, wub_ref[...])
        y = _dot(hid.astype(BF16), wdb_ref[...])
        rows = blk_ref[w] * ROW_BLOCK + lax.broadcasted_iota(jnp.int32, (ROW_BLOCK, 1), 0)
        y = jnp.where((rows >= lo) & (rows < hi), y, 0.0)

        @pl.when(first_ref[w] == 1)
        def _():
            ys_ref[...] = y

        @pl.when(first_ref[w] == 0)
        def _():
            ys_ref[...] = ys_ref[...] + y


def _experts(items, xs, w_gate, w_up, w_down):
    n_rows = xs.shape[0]
    n_items = items[0].shape[0]
    hbm = pl.BlockSpec(memory_space=pl.ANY)
    return pl.pallas_call(
        _expert_kernel,
        grid_spec=pltpu.PrefetchScalarGridSpec(
            num_scalar_prefetch=len(items),
            grid=(n_items,),
            in_specs=[pl.BlockSpec((ROW_BLOCK, HALF), lambda w, blk, *_: (blk[w], 0)), hbm, hbm, hbm],
            out_specs=pl.BlockSpec((ROW_BLOCK, D_MODEL), lambda w, blk, *_: (blk[w], 0)),
            scratch_shapes=[pltpu.VMEM((2, D_MODEL, D_EXPERT), F32), pltpu.VMEM((2, D_MODEL, D_EXPERT), F32),
                            pltpu.VMEM((2, D_EXPERT, D_MODEL), F32),
                            pltpu.VMEM((D_MODEL, D_EXPERT), BF16), pltpu.VMEM((D_MODEL, D_EXPERT), BF16),
                            pltpu.VMEM((D_EXPERT, D_MODEL), BF16),
                            pltpu.SemaphoreType.DMA((2, 3))],
        ),
        out_shape=jax.ShapeDtypeStruct((n_rows, D_MODEL), F32),
        compiler_params=pltpu.CompilerParams(dimension_semantics=("arbitrary",),
                                             vmem_limit_bytes=VMEM_LIMIT),
        name="experts",
    )(*items, xs, w_gate, w_up, w_down)


def _work_items(counts, n_rows):
    n_blocks = n_rows // ROW_BLOCK
    n_items = n_blocks + N_EXPERTS - 1
    ends = jnp.cumsum(counts)
    starts = ends - counts
    blk_lo = starts // ROW_BLOCK
    blk_hi = jnp.maximum(ends - 1, 0) // ROW_BLOCK
    per_e = jnp.where(counts > 0, blk_hi - blk_lo + 1, 0)
    item_end = jnp.cumsum(per_e)
    item_start = item_end - per_e
    total = item_end[-1]
    w = jnp.arange(n_items, dtype=jnp.int32)
    live = w < total
    e = jnp.sum((jnp.minimum(w, total - 1)[:, None] >= item_end[None, :]).astype(jnp.int32), axis=1)
    e = jnp.minimum(e, N_EXPERTS - 1)
    sel = (e[:, None] == jnp.arange(N_EXPERTS, dtype=jnp.int32)[None, :]).astype(jnp.int32)
    pick = lambda table: jnp.sum(sel * table[None, :], axis=1)
    blk = jnp.where(live, pick(blk_lo) + (w - pick(item_start)), n_blocks - 1).astype(jnp.int32)
    lo = jnp.where(live, jnp.maximum(pick(starts), blk * ROW_BLOCK), 0).astype(jnp.int32)
    hi = jnp.where(live, jnp.minimum(pick(ends), (blk + 1) * ROW_BLOCK), 0).astype(jnp.int32)
    prev_blk = jnp.concatenate([jnp.full((1,), -1, jnp.int32), blk[:-1]])
    first = (live & (blk != prev_blk)).astype(jnp.int32)
    ids = jnp.arange(N_EXPERTS, dtype=jnp.int32)
    used = counts > 0
    rank_e = jnp.cumsum(used.astype(jnp.int32)) - 1
    later = used[None, :] & (ids[None, :] > ids[:, None])
    next_e = jnp.min(jnp.where(later, ids[None, :], N_EXPERTS), axis=1)
    next_e = jnp.where(next_e < N_EXPERTS, next_e, -1)
    slot = (pick(rank_e) % 2).astype(jnp.int32)
    nxt = pick(next_e).astype(jnp.int32)
    return (blk, e.astype(jnp.int32), lo, hi, first, slot, nxt), starts


def kernel(x_prompt, x_sample, state_conv, state_gla, norm_mix, w_in, b_gates, w_gla_gate_up, b_gla_gate_up, conv_dw, conv_db, conv_ln_g, conv_ln_b, w_conv_out, gla_norm_g, w_gla_out, w_out, norm_ffn, w_router_group, b_router_group, w_router_expert, b_router_expert, w_expert_gate, w_expert_up, w_expert_down, norm_final):
    depth = norm_mix.shape[0]
    assert depth == 1
    l = 0
    bp, seq_p, _ = x_prompt.shape
    bs, seq_s, _ = x_sample.shape
    n_p, n_s = bp * seq_p, bs * seq_s
    n_tok = n_p + n_s
    row2 = lambda a: a.reshape(1, -1)

    wi = w_in[l]
    w_main = wi[:, :_C_ALOW].astype(BF16)
    w_alow = jnp.pad(wi[:, _C_ALOW:_C_GATES], ((0, 0), (0, LANES - RANK))).astype(BF16)
    w_gates = wi[:, _C_GATES:].astype(BF16)
    w_up_pad = jnp.pad(w_gla_gate_up[l], ((0, LANES - RANK), (0, 0))).astype(BF16)
    w_route = jnp.pad(jnp.concatenate([w_router_expert[l], w_router_group[l]], axis=1),
                      ((0, 0), (0, LANES - N_EXPERTS - N_GROUPS)))
    w_route_hi = w_route.astype(BF16)
    w_route = jnp.stack([w_route_hi, (w_route - w_route_hi.astype(F32)).astype(BF16)])
    b_route = jnp.pad(jnp.concatenate([b_router_expert[l], b_router_group[l]]),
                      (0, LANES - N_EXPERTS - N_GROUPS)).reshape(1, LANES)
    w_pw = w_conv_out[l].astype(BF16)
    w_go = w_gla_out[l].astype(BF16)
    w_o = w_out[l].astype(BF16)
    conv_args = (conv_dw[l], row2(conv_db[l]), row2(conv_ln_g[l]), row2(conv_ln_b[l]), w_pw)
    proj_args = (row2(norm_mix[l]), w_main, w_alow, w_gates, w_up_pad, row2(b_gla_gate_up[l]),
                 row2(b_gates[l]))
    mix_args = (row2(gla_norm_g[l]), w_go, w_o, row2(norm_ffn[l]), w_route, b_route)

    xp = x_prompt.reshape(n_p, D_MODEL)
    u, q, k, v, la, sr, gc, gg = _proj(xp, *proj_args)
    conv_out, conv_p = _conv(u.reshape(bp, seq_p, D_CONV), None, *conv_args, nseq=1)
    o, gla_p = _gla_prompt(q, k, la, v, bp, seq_p)
    h_p, xr_p, rec_p, wf_p, cnt = _mix_route(xp, conv_out.reshape(n_p, D_MODEL), o, sr, gc, gg, *mix_args,
                                             jnp.zeros((SUBLANES, LANES), F32))

    xs_ = x_sample.reshape(n_s, D_MODEL)
    u, q, k, v, la, sr, gc, gg = _proj(xs_, *proj_args)
    conv_out, conv_s = _conv(u.reshape(bs, seq_s, D_CONV), state_conv, *conv_args, nseq=SAMPLE_SEQS)
    seqs = lambda a: a.reshape(bs, seq_s, a.shape[-1])
    o, gla_s = _gla_sample(seqs(q), seqs(k), seqs(la), seqs(v), state_gla)
    h_s, xr_s, rec_s, wf_s, cnt = _mix_route(xs_, conv_out.reshape(n_s, D_MODEL), o.reshape(n_s, V_W), sr, gc, gg,
                                             *mix_args, cnt)

    counts = cnt[0, :N_EXPERTS].astype(jnp.int32)
    items, starts = _work_items(counts, 2 * n_tok)
    rec = jnp.concatenate([rec_p, rec_s], axis=1).astype(jnp.int32)
    e_ids = jnp.arange(N_EXPERTS, dtype=jnp.int32)[:, None]
    start_of = lambda e: jnp.sum(jnp.where(e[None, :] == e_ids, starts[:, None], 0), axis=0)
    pos0 = start_of(rec[_L_E0]) + rec[_L_R0]
    pos1 = start_of(rec[_L_E1]) + rec[_L_R1]
    xs_sorted = _dispatch(pos0, pos1, xr_p, xr_s)
    ys = _experts(items, xs_sorted, w_expert_gate[l], w_expert_up[l], w_expert_down[l])
    y_p = _combine(pos0, pos1, h_p, wf_p, row2(norm_final), ys, 0)
    y_s = _combine(pos0, pos1, h_s, wf_s, row2(norm_final), ys, n_p)

    return (y_p.reshape(bp, seq_p, D_MODEL), y_s.reshape(bs, seq_s, D_MODEL),
            conv_p[None], gla_p[None], conv_s[None], gla_s[None])
```

```python
import functools

import jax
import jax.numpy as jnp
from jax import lax
from jax.experimental import pallas as pl
from jax.experimental.pallas import tpu as pltpu

F32 = jnp.float32
BF16 = jnp.bfloat16

D_MODEL = 1024
D_CONV = 512
CONV_WIDTH = 31
CONV_HIST = CONV_WIDTH - 1
HEADS = 4
DK = 128
DV = 256
QK_W = HEADS * DK
V_W = HEADS * DV
RANK = 16
GATE_NORM = 16.0
N_GROUPS = 4
EXPERTS_PER_GROUP = 8
N_EXPERTS = 32
D_EXPERT = 512
EPS = 1e-6

LANES = 128
SUBLANES = 8
VMEM_LIMIT = 56 * 1024 * 1024

TOK_TILE = 256
GLA_CHUNK = 128
GLA_STEP = 256
SAFE_LOG_DECAY = -80.0
CONV_ROWS = 64
CONV_HALO = 40
ROW_BLOCK = 256
DMA_UNROLL = 8
SAMPLE_SEQS = 32
SAMPLE_GLA_SEQS = 8
SAMPLE_PAD = 16

_C_GLU_A, _C_GLU_B = 0, 512
_C_Q, _C_K, _C_V, _C_R = 1024, 1536, 2048, 3072
_C_ALOW = 4096
_C_GATES = _C_ALOW + RANK


def _sigmoid(x):
    return jax.nn.sigmoid(x)


def _dot(a, b):
    return jnp.dot(a, b, preferred_element_type=F32)


def _normed_input(x_ref, g_ref):
    x = x_ref[...]
    hn = x * lax.rsqrt(jnp.mean(x * x, axis=-1, keepdims=True) + EPS) * g_ref[...]
    return hn.astype(BF16)


def _glu(hb, wm_ref):
    return _dot(hb, wm_ref[:, _C_GLU_A:_C_GLU_B]) * _sigmoid(_dot(hb, wm_ref[:, _C_GLU_B:_C_Q]))


def _project_rest(hb, wm_ref, wa_ref, wg_ref, wup_ref, bup_ref, bg_ref,
                  q_ref, k_ref, v_ref, la_ref, sr_ref, gc_ref, gg_ref):
    def mm(lo, hi):
        return _dot(hb, wm_ref[:, lo:hi])

    q_ref[...] = mm(_C_Q, _C_K) * (DK ** -0.5)
    k_ref[...] = mm(_C_K, _C_V)
    v_ref[...] = mm(_C_V, _C_R).astype(BF16)
    r = mm(_C_R, _C_ALOW)
    sr_ref[...] = (r * _sigmoid(r)).astype(BF16)
    gc_ref[...] = _sigmoid(_dot(hb, wg_ref[:, :D_MODEL]) + bg_ref[:, :D_MODEL]).astype(BF16)
    gg_ref[...] = _sigmoid(_dot(hb, wg_ref[:, D_MODEL:]) + bg_ref[:, D_MODEL:]).astype(BF16)
    a_low = _dot(hb, wa_ref[...])
    z = _dot(a_low.astype(BF16), wup_ref[...]) + bup_ref[...]
    la_ref[...] = (jnp.minimum(z, 0.0) - jnp.log1p(jnp.exp(-jnp.abs(z)))) * (1.0 / GATE_NORM)


def _proj_kernel(x_ref, g_ref, wm_ref, wa_ref, wg_ref, wup_ref, bup_ref, bg_ref,
                 u_ref, q_ref, k_ref, v_ref, la_ref, sr_ref, gc_ref, gg_ref):
    hb = _normed_input(x_ref, g_ref)
    u_ref[...] = _glu(hb, wm_ref)
    _project_rest(hb, wm_ref, wa_ref, wg_ref, wup_ref, bup_ref, bg_ref,
                  q_ref, k_ref, v_ref, la_ref, sr_ref, gc_ref, gg_ref)


_HIST_PAD = 32
_HIST0 = _HIST_PAD - CONV_HIST


def _conv_taps(w, dw_ref, cs, rt):
    win = w.shape[0]
    acc = jnp.zeros((rt, LANES), F32)
    for s in range(SUBLANES):
        ws = w if s == 0 else pltpu.roll(w, win - s, axis=0)
        for a in range(CONV_HALO // SUBLANES):
            j = a * SUBLANES + s - _HIST0
            if 0 <= j < CONV_WIDTH:
                acc = acc + ws[a * SUBLANES:a * SUBLANES + rt, :] * dw_ref[j:j + 1, cs]
    return acc


def _norm_swish_pointwise(y, lg_ref, lb_ref, wpw_ref):
    mu = jnp.mean(y, axis=-1, keepdims=True)
    var = jnp.mean(jnp.square(y - mu), axis=-1, keepdims=True)
    yn = (y - mu) * lax.rsqrt(var + EPS) * lg_ref[...] + lb_ref[...]
    return _dot((yn * _sigmoid(yn)).astype(BF16), wpw_ref[...])


def _proj_conv_kernel(x_ref, g_ref, wm_ref, wa_ref, wg_ref, wup_ref, bup_ref, bg_ref,
                      dw_ref, db_ref, lg_ref, lb_ref, wpw_ref,
                      co_ref, q_ref, k_ref, v_ref, la_ref, sr_ref, gc_ref, gg_ref, ns_ref,
                      full_ref, y_ref):
    c = pl.program_id(1)
    tm = x_ref.shape[0]

    @pl.when(c == 0)
    def _():
        full_ref[0:_HIST_PAD, :] = jnp.zeros((_HIST_PAD, D_CONV), F32)

    @pl.when(c > 0)
    def _():
        full_ref[0:_HIST_PAD, :] = full_ref[tm:tm + _HIST_PAD, :]

    hb = _normed_input(x_ref, g_ref)
    full_ref[_HIST_PAD:_HIST_PAD + tm, :] = _glu(hb, wm_ref)
    full_ref[_HIST_PAD + tm:, :] = jnp.zeros((SUBLANES, D_CONV), F32)
    ns_ref[0] = full_ref[tm + _HIST0:tm + _HIST_PAD, :]
    rt = CONV_ROWS
    for i in range(tm // rt):
        for cc in range(D_CONV // LANES):
            cs = slice(cc * LANES, (cc + 1) * LANES)
            acc = _conv_taps(full_ref[i * rt:i * rt + rt + CONV_HALO, cs], dw_ref, cs, rt)
            y_ref[i * rt:(i + 1) * rt, cs] = acc + db_ref[:, cs]
    _project_rest(hb, wm_ref, wa_ref, wg_ref, wup_ref, bup_ref, bg_ref,
                  q_ref, k_ref, v_ref, la_ref, sr_ref, gc_ref, gg_ref)
    co_ref[...] = _norm_swish_pointwise(y_ref[...], lg_ref, lb_ref, wpw_ref)


def _proj_conv(x2d, nb, seq, proj_consts, conv_consts):
    tm = TOK_TILE
    nc = seq // tm
    row = lambda w: pl.BlockSpec((tm, w), lambda b, c: (b * nc + c, 0))
    full = lambda a: pl.BlockSpec(a.shape, lambda b, c: (0,) * a.ndim)
    widths = (D_MODEL, QK_W, QK_W, V_W, QK_W, V_W, D_MODEL, D_MODEL)
    dtypes = (F32, F32, F32, BF16, F32, BF16, BF16, BF16)
    consts = tuple(proj_consts) + tuple(conv_consts)
    n = nb * seq
    return pl.pallas_call(
        _proj_conv_kernel,
        grid=(nb, nc),
        in_specs=[row(D_MODEL)] + [full(a) for a in consts],
        out_specs=[row(w) for w in widths] + [pl.BlockSpec((1, CONV_HIST, D_CONV), lambda b, c: (b, 0, 0))],
        out_shape=[jax.ShapeDtypeStruct((n, w), dt) for w, dt in zip(widths, dtypes)]
        + [jax.ShapeDtypeStruct((nb, CONV_HIST, D_CONV), F32)],
        scratch_shapes=[pltpu.VMEM((_HIST_PAD + tm + SUBLANES, D_CONV), F32), pltpu.VMEM((tm, D_CONV), F32)],
        compiler_params=pltpu.CompilerParams(dimension_semantics=("parallel", "arbitrary"),
                                             vmem_limit_bytes=VMEM_LIMIT),
        name="proj_conv",
    )(x2d, *consts)


def _proj(x2d, norm_g, w_main, w_alow, w_gates, w_up, b_up, b_gates):
    n = x2d.shape[0]
    tm = TOK_TILE
    row = lambda w: pl.BlockSpec((tm, w), lambda i: (i, 0))
    full = lambda a: pl.BlockSpec(a.shape, lambda i: (0,) * a.ndim)
    widths = (D_CONV, QK_W, QK_W, V_W, QK_W, V_W, D_MODEL, D_MODEL)
    dtypes = (F32, F32, F32, BF16, F32, BF16, BF16, BF16)
    consts = (norm_g, w_main, w_alow, w_gates, w_up, b_up, b_gates)
    return pl.pallas_call(
        _proj_kernel,
        grid=(n // tm,),
        in_specs=[row(D_MODEL)] + [full(a) for a in consts],
        out_specs=[row(w) for w in widths],
        out_shape=[jax.ShapeDtypeStruct((n, w), dt) for w, dt in zip(widths, dtypes)],
        compiler_params=pltpu.CompilerParams(dimension_semantics=("parallel",),
                                             vmem_limit_bytes=VMEM_LIMIT),
        name="proj",
    )(x2d, *consts)


def _conv_kernel(u_ref, st_ref, dw_ref, db_ref, lg_ref, lb_ref, wpw_ref, out_ref, ns_ref, full_ref, y_ref, *, seq):
    nseq = u_ref.shape[0]
    n_rows = y_ref.shape[0] // nseq
    tail = full_ref.shape[0] - (_HIST_PAD + seq)

    def one_seq(b, carry):
        full_ref[0:_HIST_PAD, :] = jnp.zeros((_HIST_PAD, D_CONV), F32)
        full_ref[_HIST_PAD + seq:, :] = jnp.zeros((tail, D_CONV), F32)
        full_ref[_HIST0:_HIST_PAD, :] = st_ref[b]
        full_ref[_HIST_PAD:_HIST_PAD + seq, :] = u_ref[b]
        ns_ref[b] = full_ref[seq + _HIST0:seq + _HIST_PAD, :]
        for c in range(D_CONV // LANES):
            cs = slice(c * LANES, (c + 1) * LANES)
            acc = _conv_taps(full_ref[:, cs], dw_ref, cs, n_rows)
            y_ref[pl.ds(pl.multiple_of(b * n_rows, SUBLANES), n_rows), cs] = acc + db_ref[:, cs]
        return carry

    lax.fori_loop(0, nseq, one_seq, 0)
    res = _norm_swish_pointwise(y_ref[...], lg_ref, lb_ref, wpw_ref)
    for j in range(nseq):
        out_ref[j] = res[j * n_rows:j * n_rows + seq]


def _conv(u3, state, dw, db, lg, lb, w_pw, nseq):
    nb, seq, _ = u3.shape
    n_rows = -(-seq // SUBLANES) * SUBLANES
    full = lambda a: pl.BlockSpec(a.shape, lambda b: (0,) * a.ndim)
    per_b = lambda r, w: pl.BlockSpec((nseq, r, w), lambda b: (b, 0, 0))
    return pl.pallas_call(
        functools.partial(_conv_kernel, seq=seq),
        grid=(nb // nseq,),
        in_specs=[per_b(seq, D_CONV), pl.BlockSpec((None, nseq, CONV_HIST, D_CONV), lambda b: (0, b, 0, 0)),
                  full(dw), full(db), full(lg), full(lb), full(w_pw)],
        out_specs=[per_b(seq, D_MODEL), per_b(CONV_HIST, D_CONV)],
        out_shape=[jax.ShapeDtypeStruct((nb, seq, D_MODEL), F32),
                   jax.ShapeDtypeStruct((nb, CONV_HIST, D_CONV), F32)],
        scratch_shapes=[pltpu.VMEM((n_rows + CONV_HALO, D_CONV), F32),
                        pltpu.VMEM((nseq * n_rows, D_CONV), F32)],
        compiler_params=pltpu.CompilerParams(dimension_semantics=("parallel",),
                                             vmem_limit_bytes=VMEM_LIMIT),
        name="conv_state",
    )(u3, state, dw, db, lg, lb, w_pw)


def _token_step(s, a_col, k_col, q_col, v_row):
    s = s * a_col + k_col * v_row
    return s, jnp.sum(q_col * s, axis=0, keepdims=True)


def _gla_prompt_kernel(q_ref, k_ref, la_ref, v_ref, o_ref, sout_ref, s_ref, vf_ref):
    c = pl.program_id(1)
    n = GLA_CHUNK
    chunks = [slice(i * n, (i + 1) * n) for i in range(q_ref.shape[0] // n)]

    @pl.when(c == 0)
    def _():
        s_ref[...] = jnp.zeros_like(s_ref)

    r_i = lax.broadcasted_iota(jnp.int32, (n, n), 0)
    c_i = lax.broadcasted_iota(jnp.int32, (n, n), 1)
    tri = jnp.where(r_i >= c_i, 1.0, 0.0).astype(BF16)

    def prefix_sum(g):
        hi = g.astype(BF16)
        lo = (g - hi.astype(F32)).astype(BF16)
        return _dot(tri, hi) + _dot(tri, lo)

    bs = [prefix_sum(la_ref[rows, :]) for rows in chunks]
    total = bs[0][n - 1:n, :]
    for b in bs[1:]:
        total = jnp.minimum(total, b[n - 1:n, :])
    fast = jnp.min(total) > SAFE_LOG_DECAY

    @pl.when(fast)
    def _():
        for h in range(HEADS):
            ks = slice(h * DK, (h + 1) * DK)
            vs = slice(h * DV, (h + 1) * DV)
            s = s_ref[h]
            for rows, b in zip(chunks, bs):
                bh = b[:, ks]
                bl = bh[n - 1:n, :]
                kh = k_ref[rows, ks]
                vh = v_ref[rows, vs]
                qp = (q_ref[rows, ks] * jnp.exp(bh)).astype(BF16)
                kp = (kh * jnp.exp(-bh)).astype(BF16)
                kpp = kh * jnp.exp(bl - bh)
                att = lax.dot_general(qp, kp, (((1,), (1,)), ((), ())), preferred_element_type=F32)
                att = jnp.where(r_i >= c_i, att, 0.0).astype(BF16)
                o_ref[rows, vs] = _dot(qp, s.astype(BF16)) + _dot(att, vh)
                d_col = jnp.sum(jnp.where(r_i == c_i, jnp.exp(bl), 0.0), axis=1, keepdims=True)
                s = s * d_col + _dot(kpp.T.astype(BF16), vh)
            s_ref[h] = s

    @pl.when(jnp.logical_not(fast))
    def _():
        lane = lax.broadcasted_iota(jnp.int32, (1, n), 1)
        for rows in chunks:
            vf_ref[...] = v_ref[rows, :].astype(F32)
            for h in range(HEADS):
                ks = slice(h * DK, (h + 1) * DK)
                vs = slice(h * DV, (h + 1) * DV)
                q_t = q_ref[rows, ks].T
                k_t = k_ref[rows, ks].T
                a_t = jnp.exp(la_ref[rows, ks]).T

                def body(t, s):
                    m = lane == t
                    col = lambda x: jnp.sum(jnp.where(m, x, 0.0), axis=1, keepdims=True)
                    s, o_row = _token_step(s, col(a_t), col(k_t), col(q_t), vf_ref[pl.ds(t, 1), vs])
                    o_ref[pl.ds(rows.start + t, 1), vs] = o_row
                    return s

                s_ref[h] = lax.fori_loop(0, n, body, s_ref[h])

    @pl.when(c == pl.num_programs(1) - 1)
    def _():
        sout_ref[0] = s_ref[...]


def _gla_prompt(q, k, la, v, nb, seq):
    nc = seq // GLA_STEP
    tok = lambda w: pl.BlockSpec((GLA_STEP, w), lambda b, c: (b * nc + c, 0))
    return pl.pallas_call(
        _gla_prompt_kernel,
        grid=(nb, nc),
        in_specs=[tok(QK_W), tok(QK_W), tok(QK_W), tok(V_W)],
        out_specs=[tok(V_W), pl.BlockSpec((1, HEADS, DK, DV), lambda b, c: (b, 0, 0, 0))],
        out_shape=[jax.ShapeDtypeStruct((nb * seq, V_W), F32),
                   jax.ShapeDtypeStruct((nb, HEADS, DK, DV), F32)],
        scratch_shapes=[pltpu.VMEM((HEADS, DK, DV), F32), pltpu.VMEM((GLA_CHUNK, V_W), F32)],
        compiler_params=pltpu.CompilerParams(dimension_semantics=("parallel", "arbitrary"),
                                             vmem_limit_bytes=VMEM_LIMIT),
        name="gla_prompt",
    )(q, k, la, v)


def _gla_sample_kernel(q_ref, k_ref, la_ref, v_ref, s_ref, o_ref, sout_ref, *, seq):
    p = SAMPLE_PAD
    row = lax.broadcasted_iota(jnp.int32, (p, 1), 0)

    def padded(x):
        return jnp.concatenate([x, jnp.zeros((p - seq, x.shape[1]), F32)], axis=0)

    for b in range(q_ref.shape[0]):
        q, k, v, g = padded(q_ref[b]), padded(k_ref[b]), padded(v_ref[b].astype(F32)), padded(la_ref[b])
        bc = g
        sh = 1
        while sh < seq:
            bc = bc + jnp.where(row >= sh, pltpu.roll(bc, sh, axis=0), 0.0)
            sh *= 2
        b_last = bc[seq - 1:seq, :]
        bc = jnp.where(row < seq, bc, b_last)
        qp = (q * jnp.exp(bc)).astype(BF16)
        kpp_t = (k * jnp.exp(b_last - bc)).T.astype(BF16)
        d_cols = jnp.broadcast_to(jnp.exp(b_last), (SUBLANES, QK_W)).T
        vb = v.astype(BF16)

        o_in = [jnp.zeros((p, DV), F32) for _ in range(HEADS)]
        for d in range(seq):
            k_d = k if d == 0 else pltpu.roll(k, d, axis=0)
            v_d = v if d == 0 else pltpu.roll(v, d, axis=0)
            b_d = bc if d == 0 else pltpu.roll(bc, d, axis=0)
            pair = q * k_d * jnp.exp(jnp.where(row >= d, bc - b_d, -jnp.inf))
            for h in range(HEADS):
                att = jnp.sum(pair[:, h * DK:(h + 1) * DK], axis=1, keepdims=True)
                o_in[h] = o_in[h] + att * v_d[:, h * DV:(h + 1) * DV]

        for h in range(HEADS):
            ks = slice(h * DK, (h + 1) * DK)
            vs = slice(h * DV, (h + 1) * DV)
            s = s_ref[b, h]
            o = _dot(qp[:, ks], s.astype(BF16)) + o_in[h]
            o_ref[b, :, vs] = o[:seq]
            sout_ref[b, h] = s * d_cols[ks, 0:1] + _dot(kpp_t[ks, :], vb[:, vs])


def _gla_sample(q3, k3, la3, v3, state):
    nb, seq, _ = q3.shape
    assert seq <= SUBLANES
    ns = SAMPLE_GLA_SEQS
    col = pl.BlockSpec((ns, seq, QK_W), lambda b: (b, 0, 0))
    tok = pl.BlockSpec((ns, seq, V_W), lambda b: (b, 0, 0))
    st = pl.BlockSpec((ns, HEADS, DK, DV), lambda b: (b, 0, 0, 0))
    st_in = pl.BlockSpec((None, ns, HEADS, DK, DV), lambda b: (0, b, 0, 0, 0))
    return pl.pallas_call(
        functools.partial(_gla_sample_kernel, seq=seq),
        grid=(nb // ns,),
        in_specs=[col, col, col, tok, st_in],
        out_specs=[tok, st],
        out_shape=[jax.ShapeDtypeStruct((nb, seq, V_W), F32),
                   jax.ShapeDtypeStruct((nb, HEADS, DK, DV), F32)],
        compiler_params=pltpu.CompilerParams(dimension_semantics=("parallel",),
                                             vmem_limit_bytes=VMEM_LIMIT),
        name="gla_sample",
    )(q3, k3, la3, v3, state)


_L_E0, _L_E1, _L_R0, _L_R1 = 0, 1, 2, 3
_GROUP_LANE0 = N_EXPERTS
HALF = D_MODEL // 2


def _mix_route_kernel(x_ref, co_ref, o_ref, sr_ref, gc_ref, gg_ref, gn_ref, wgo_ref, wo_ref,
                      g_ref, wr_ref, br_ref, cin_ref,
                      h_ref, xp_ref, mi_ref, mf_ref, cnt_ref, carry_ref):
    i = pl.program_id(0)
    tm = x_ref.shape[0]

    @pl.when(i == 0)
    def _():
        carry_ref[...] = cin_ref[0:1, :]

    gla_out = None
    for h in range(HEADS):
        vs = slice(h * DV, (h + 1) * DV)
        oh = o_ref[:, vs]
        on = oh * lax.rsqrt(jnp.mean(oh * oh, axis=-1, keepdims=True) + EPS) * gn_ref[:, vs]
        part = _dot((on * sr_ref[:, vs]).astype(BF16), wgo_ref[vs, :])
        gla_out = part if gla_out is None else gla_out + part
    mix = (gc_ref[...] * co_ref[...] + gg_ref[...] * gla_out).astype(BF16)
    x = x_ref[...] + _dot(mix, wo_ref[...])
    h_ref[...] = x

    hn = x * lax.rsqrt(jnp.mean(x * x, axis=-1, keepdims=True) + EPS) * g_ref[...]
    hi = hn.astype(BF16)
    hi_f = hi.astype(F32)
    top = lax.bitcast_convert_type(hi_f[:, :HALF], jnp.uint32)
    bot = lax.bitcast_convert_type(hi_f[:, HALF:], jnp.uint32)
    xp_ref[...] = top | jnp.right_shift(bot, jnp.uint32(16))
    lo = (hn - hi_f).astype(BF16)
    logits = _dot(hi, wr_ref[0]) + (_dot(lo, wr_ref[0]) + _dot(hi, wr_ref[1])) + br_ref[...]
    lane = lax.broadcasted_iota(jnp.int32, (tm, LANES), 1)
    lane_f = lane.astype(F32)
    group_f = jnp.right_shift(lane, 3).astype(F32)
    big = jnp.float32(LANES)
    neg = jnp.float32(-jnp.inf)

    is_g = (lane >= _GROUP_LANE0) & (lane < _GROUP_LANE0 + N_GROUPS)
    lg = jnp.where(is_g, logits, neg)
    mg = jnp.max(lg, axis=1, keepdims=True)
    p_sel = 1.0 / jnp.sum(jnp.exp(lg - mg), axis=1, keepdims=True)
    g_idx = jnp.min(jnp.where(lg == mg, lane_f, big), axis=1, keepdims=True) - _GROUP_LANE0

    is_e = (lane < N_EXPERTS) & (group_f == g_idx)
    le = jnp.where(is_e, logits, neg)
    v0 = jnp.max(le, axis=1, keepdims=True)
    i0 = jnp.min(jnp.where(le == v0, lane_f, big), axis=1, keepdims=True)
    le1 = jnp.where(lane_f == i0, neg, le)
    v1 = jnp.max(le1, axis=1, keepdims=True)
    i1 = jnp.min(jnp.where(le1 == v1, lane_f, big), axis=1, keepdims=True)
    e1 = jnp.exp(v1 - v0)
    den = 1.0 + e1
    w0 = p_sel * (1.0 / den)
    w1 = p_sel * (e1 / den)

    hit0 = lane_f == i0
    hit1 = lane_f == i1
    cnt = jnp.where(hit0 | hit1, 1.0, 0.0)
    r_i = lax.broadcasted_iota(jnp.int32, (tm, tm), 0)
    c_i = lax.broadcasted_iota(jnp.int32, (tm, tm), 1)
    before = jnp.where(r_i > c_i, 1.0, 0.0).astype(BF16)
    seen = _dot(before, cnt.astype(BF16)) + carry_ref[...]
    rank0 = jnp.sum(jnp.where(hit0, seen, 0.0), axis=1, keepdims=True)
    rank1 = jnp.sum(jnp.where(hit1, seen, 0.0), axis=1, keepdims=True)
    carry_ref[...] = carry_ref[...] + jnp.sum(cnt, axis=0, keepdims=True)
    cnt_ref[...] = jnp.broadcast_to(carry_ref[...], cnt_ref.shape)

    rec = jnp.where(lane == _L_E0, i0, 0.0)
    rec = jnp.where(lane == _L_E1, i1, rec)
    rec = jnp.where(lane == _L_R0, rank0, rec)
    rec = jnp.where(lane == _L_R1, rank1, rec)
    mi_ref[...] = rec.T[:SUBLANES, :]
    mf_ref[...] = jnp.where(lane == 0, w0, jnp.where(lane == 1, w1, 0.0))


def _mix_route(x2d, conv_out, o, sr, gc, gg, gn, w_gla_out, w_out, norm_g, w_route, b_route, counts_in):
    n = x2d.shape[0]
    tm = TOK_TILE
    row = lambda w: pl.BlockSpec((tm, w), lambda i: (i, 0))
    full = lambda a: pl.BlockSpec(a.shape, lambda i: (0,) * a.ndim)
    consts = (gn, w_gla_out, w_out, norm_g, w_route, b_route, counts_in)
    return pl.pallas_call(
        _mix_route_kernel,
        grid=(n // tm,),
        in_specs=[row(D_MODEL)] * 6 + [full(a) for a in consts],
        out_specs=[row(D_MODEL), row(HALF), pl.BlockSpec((SUBLANES, tm), lambda i: (0, i)), row(LANES),
                   pl.BlockSpec((SUBLANES, LANES), lambda i: (0, 0))],
        out_shape=[jax.ShapeDtypeStruct((n, D_MODEL), F32),
                   jax.ShapeDtypeStruct((n, HALF), jnp.uint32),
                   jax.ShapeDtypeStruct((SUBLANES, n), F32),
                   jax.ShapeDtypeStruct((n, LANES), F32),
                   jax.ShapeDtypeStruct((SUBLANES, LANES), F32)],
        scratch_shapes=[pltpu.VMEM((1, LANES), F32)],
        compiler_params=pltpu.CompilerParams(dimension_semantics=("arbitrary",),
                                             vmem_limit_bytes=VMEM_LIMIT),
        name="mix_route",
    )(x2d, conv_out, o, sr, gc, gg, *consts)


def _row_copy(src_ref, src_row, dst_ref, dst_row, sem):
    return pltpu.make_async_copy(src_ref.at[pl.ds(src_row, 1), :], dst_ref.at[pl.ds(dst_row, 1), :], sem)


def _dispatch_kernel(p0_ref, p1_ref, xa_ref, xb_ref, xs_ref, sem, *, tiles_a):
    tm = xa_ref.shape[0]
    i = pl.program_id(0)
    base = i * tm

    def scatter(src_ref):
        def rows(wait):
            def body(t, carry):
                c0 = _row_copy(src_ref, t, xs_ref, p0_ref[base + t], sem)
                c1 = _row_copy(src_ref, t, xs_ref, p1_ref[base + t], sem)
                if wait:
                    c0.wait()
                    c1.wait()
                else:
                    c0.start()
                    c1.start()
                return carry
            lax.fori_loop(0, tm, body, 0, unroll=DMA_UNROLL)

        rows(wait=False)
        rows(wait=True)

    @pl.when(i < tiles_a)
    def _():
        scatter(xa_ref)

    @pl.when(i >= tiles_a)
    def _():
        scatter(xb_ref)


def _dispatch(pos0, pos1, x_a, x_b):
    tm = TOK_TILE
    tiles_a, tiles_b = x_a.shape[0] // tm, x_b.shape[0] // tm
    n = x_a.shape[0] + x_b.shape[0]
    width = x_a.shape[1]
    return pl.pallas_call(
        functools.partial(_dispatch_kernel, tiles_a=tiles_a),
        grid_spec=pltpu.PrefetchScalarGridSpec(
            num_scalar_prefetch=2,
            grid=(tiles_a + tiles_b,),
            in_specs=[pl.BlockSpec((tm, width), lambda i, p0, p1: (jnp.minimum(i, tiles_a - 1), 0)),
                      pl.BlockSpec((tm, width), lambda i, p0, p1: (jnp.maximum(i - tiles_a, 0), 0))],
            out_specs=pl.BlockSpec(memory_space=pl.ANY),
            scratch_shapes=[pltpu.SemaphoreType.DMA(())],
        ),
        out_shape=jax.ShapeDtypeStruct((2 * n, width), x_a.dtype),
        compiler_params=pltpu.CompilerParams(dimension_semantics=("arbitrary",)),
        name="dispatch",
    )(pos0, pos1, x_a, x_b)


def _combine_kernel(p0_ref, p1_ref, h_ref, wf_ref, g_ref, ys_ref, out_ref, b0_ref, b1_ref, sems, *, tok_offset):
    tm = h_ref.shape[0]
    i = pl.program_id(0)

    def gather(tile, slot, wait):
        base = tok_offset + tile * tm

        def rows(t, carry):
            c0 = _row_copy(ys_ref, p0_ref[base + t], b0_ref.at[slot], t, sems.at[slot])
            c1 = _row_copy(ys_ref, p1_ref[base + t], b1_ref.at[slot], t, sems.at[slot])
            if wait:
                c0.wait()
                c1.wait()
            else:
                c0.start()
                c1.start()
            return carry

        lax.fori_loop(0, tm, rows, 0, unroll=DMA_UNROLL)

    slot = lax.rem(i, 2)

    @pl.when(i == 0)
    def _():
        gather(i, slot, wait=False)

    @pl.when(i + 1 < pl.num_programs(0))
    def _():
        gather(i + 1, 1 - slot, wait=False)

    gather(i, slot, wait=True)
    y = h_ref[...] + (b0_ref[slot] * wf_ref[:, 0:1] + b1_ref[slot] * wf_ref[:, 1:2])
    out_ref[...] = y * lax.rsqrt(jnp.mean(y * y, axis=-1, keepdims=True) + EPS) * g_ref[...]


def _combine(pos0, pos1, h, wf, norm_g, ys, tok_offset):
    n_rows = h.shape[0]
    tm = TOK_TILE
    return pl.pallas_call(
        functools.partial(_combine_kernel, tok_offset=tok_offset),
        grid_spec=pltpu.PrefetchScalarGridSpec(
            num_scalar_prefetch=2,
            grid=(n_rows // tm,),
            in_specs=[pl.BlockSpec((tm, D_MODEL), lambda i, p0, p1: (i, 0)),
                      pl.BlockSpec((tm, LANES), lambda i, p0, p1: (i, 0)),
                      pl.BlockSpec((1, D_MODEL), lambda i, p0, p1: (0, 0)),
                      pl.BlockSpec(memory_space=pl.ANY)],
            out_specs=pl.BlockSpec((tm, D_MODEL), lambda i, p0, p1: (i, 0)),
            scratch_shapes=[pltpu.VMEM((2, tm, D_MODEL), F32), pltpu.VMEM((2, tm, D_MODEL), F32),
                            pltpu.SemaphoreType.DMA((2,))],
        ),
        out_shape=jax.ShapeDtypeStruct((n_rows, D_MODEL), F32),
        compiler_params=pltpu.CompilerParams(dimension_semantics=("arbitrary",),
                                             vmem_limit_bytes=VMEM_LIMIT),
        name="combine",
    )(pos0, pos1, h, wf, norm_g, ys)


def _expert_kernel(blk_ref, exp_ref, lo_ref, hi_ref, first_ref, slot_ref, next_ref,
                   xs_ref, wg_hbm, wu_hbm, wd_hbm, ys_ref,
                   wg32_ref, wu32_ref, wd32_ref, wgb_ref, wub_ref, wdb_ref, sems):
    w = pl.program_id(0)
    e = exp_ref[w]
    slot = slot_ref[w]
    changed = jnp.logical_or(w == 0, e != exp_ref[jnp.maximum(w - 1, 0)])

    def weight_copies(expert, s):
        return [pltpu.make_async_copy(src.at[expert], dst.at[s], sems.at[s, j])
                for j, (src, dst) in enumerate(((wg_hbm, wg32_ref), (wu_hbm, wu32_ref), (wd_hbm, wd32_ref)))]

    @pl.when(w == 0)
    def _():
        for c in weight_copies(e, slot):
            c.start()

    @pl.when(changed)
    def _():
        for c in weight_copies(e, slot):
            c.wait()
        wgb_ref[...] = wg32_ref[slot].astype(BF16)
        wub_ref[...] = wu32_ref[slot].astype(BF16)
        wdb_ref[...] = wd32_ref[slot].astype(BF16)

        @pl.when(next_ref[w] >= 0)
        def _():
            for c in weight_copies(next_ref[w], 1 - slot):
                c.start()

    lo = lo_ref[w]
    hi = hi_ref[w]

    @pl.when(hi > lo)
    def _():
        packed = xs_ref[...]
        x_top = lax.bitcast_convert_type(packed & jnp.uint32(0xFFFF0000), F32)
        x_bot = lax.bitcast_convert_type(jnp.left_shift(packed, jnp.uint32(16)), F32)
        x = jnp.concatenate([x_top, x_bot], axis=1).astype(BF16)
        gate = _dot(x, wgb_ref[...])
        hid = (gate * _sigmoid(gate)) * _dot(x, wub_ref[...])
        y = _dot(hid.astype(BF16), wdb_ref[...])
        rows = blk_ref[w] * ROW_BLOCK + lax.broadcasted_iota(jnp.int32, (ROW_BLOCK, 1), 0)
        y = jnp.where((rows >= lo) & (rows < hi), y, 0.0)

        @pl.when(first_ref[w] == 1)
        def _():
            ys_ref[...] = y

        @pl.when(first_ref[w] == 0)
        def _():
            ys_ref[...] = ys_ref[...] + y


def _experts(items, xs, w_gate, w_up, w_down):
    n_rows = xs.shape[0]
    n_items = items[0].shape[0]
    hbm = pl.BlockSpec(memory_space=pl.ANY)
    return pl.pallas_call(
        _expert_kernel,
        grid_spec=pltpu.PrefetchScalarGridSpec(
            num_scalar_prefetch=len(items),
            grid=(n_items,),
            in_specs=[pl.BlockSpec((ROW_BLOCK, HALF), lambda w, blk, *_: (blk[w], 0)), hbm, hbm, hbm],
            out_specs=pl.BlockSpec((ROW_BLOCK, D_MODEL), lambda w, blk, *_: (blk[w], 0)),
            scratch_shapes=[pltpu.VMEM((2, D_MODEL, D_EXPERT), F32), pltpu.VMEM((2, D_MODEL, D_EXPERT), F32),
                            pltpu.VMEM((2, D_EXPERT, D_MODEL), F32),
                            pltpu.VMEM((D_MODEL, D_EXPERT), BF16), pltpu.VMEM((D_MODEL, D_EXPERT), BF16),
                            pltpu.VMEM((D_EXPERT, D_MODEL), BF16),
                            pltpu.SemaphoreType.DMA((2, 3))],
        ),
        out_shape=jax.ShapeDtypeStruct((n_rows, D_MODEL), F32),
        compiler_params=pltpu.CompilerParams(dimension_semantics=("arbitrary",),
                                             vmem_limit_bytes=VMEM_LIMIT),
        name="experts",
    )(*items, xs, w_gate, w_up, w_down)


def _work_items(counts, n_rows):
    n_blocks = n_rows // ROW_BLOCK
    n_items = n_blocks + N_EXPERTS - 1
    ends = jnp.cumsum(counts)
    starts = ends - counts
    blk_lo = starts // ROW_BLOCK
    blk_hi = jnp.maximum(ends - 1, 0) // ROW_BLOCK
    per_e = jnp.where(counts > 0, blk_hi - blk_lo + 1, 0)
    item_end = jnp.cumsum(per_e)
    item_start = item_end - per_e
    total = item_end[-1]
    w = jnp.arange(n_items, dtype=jnp.int32)
    live = w < total
    e = jnp.sum((jnp.minimum(w, total - 1)[:, None] >= item_end[None, :]).astype(jnp.int32), axis=1)
    e = jnp.minimum(e, N_EXPERTS - 1)
    sel = (e[:, None] == jnp.arange(N_EXPERTS, dtype=jnp.int32)[None, :]).astype(jnp.int32)
    pick = lambda table: jnp.sum(sel * table[None, :], axis=1)
    blk = jnp.where(live, pick(blk_lo) + (w - pick(item_start)), n_blocks - 1).astype(jnp.int32)
    lo = jnp.where(live, jnp.maximum(pick(starts), blk * ROW_BLOCK), 0).astype(jnp.int32)
    hi = jnp.where(live, jnp.minimum(pick(ends), (blk + 1) * ROW_BLOCK), 0).astype(jnp.int32)
    prev_blk = jnp.concatenate([jnp.full((1,), -1, jnp.int32), blk[:-1]])
    first = (live & (blk != prev_blk)).astype(jnp.int32)
    ids = jnp.arange(N_EXPERTS, dtype=jnp.int32)
    used = counts > 0
    rank_e = jnp.cumsum(used.astype(jnp.int32)) - 1
    later = used[None, :] & (ids[None, :] > ids[:, None])
    next_e = jnp.min(jnp.where(later, ids[None, :], N_EXPERTS), axis=1)
    next_e = jnp.where(next_e < N_EXPERTS, next_e, -1)
    slot = (pick(rank_e) % 2).astype(jnp.int32)
    nxt = pick(next_e).astype(jnp.int32)
    return (blk, e.astype(jnp.int32), lo, hi, first, slot, nxt), starts


def kernel(x_prompt, x_sample, state_conv, state_gla, norm_mix, w_in, b_gates, w_gla_gate_up, b_gla_gate_up, conv_dw, conv_db, conv_ln_g, conv_ln_b, w_conv_out, gla_norm_g, w_gla_out, w_out, norm_ffn, w_router_group, b_router_group, w_router_expert, b_router_expert, w_expert_gate, w_expert_up, w_expert_down, norm_final):
    depth = norm_mix.shape[0]
    assert depth == 1
    l = 0
    bp, seq_p, _ = x_prompt.shape
    bs, seq_s, _ = x_sample.shape
    n_p, n_s = bp * seq_p, bs * seq_s
    n_tok = n_p + n_s
    row2 = lambda a: a.reshape(1, -1)

    wi = w_in[l]
    w_main = wi[:, :_C_ALOW].astype(BF16)
    w_alow = jnp.pad(wi[:, _C_ALOW:_C_GATES], ((0, 0), (0, LANES - RANK))).astype(BF16)
    w_gates = wi[:, _C_GATES:].astype(BF16)
    w_up_pad = jnp.pad(w_gla_gate_up[l], ((0, LANES - RANK), (0, 0))).astype(BF16)
    w_route = jnp.pad(jnp.concatenate([w_router_expert[l], w_router_group[l]], axis=1),
                      ((0, 0), (0, LANES - N_EXPERTS - N_GROUPS)))
    w_route_hi = w_route.astype(BF16)
    w_route = jnp.stack([w_route_hi, (w_route - w_route_hi.astype(F32)).astype(BF16)])
    b_route = jnp.pad(jnp.concatenate([b_router_expert[l], b_router_group[l]]),
                      (0, LANES - N_EXPERTS - N_GROUPS)).reshape(1, LANES)
    w_pw = w_conv_out[l].astype(BF16)
    w_go = w_gla_out[l].astype(BF16)
    w_o = w_out[l].astype(BF16)
    conv_args = (conv_dw[l], row2(conv_db[l]), row2(conv_ln_g[l]), row2(conv_ln_b[l]), w_pw)
    proj_args = (row2(norm_mix[l]), w_main, w_alow, w_gates, w_up_pad, row2(b_gla_gate_up[l]),
                 row2(b_gates[l]))
    mix_args = (row2(gla_norm_g[l]), w_go, w_o, row2(norm_ffn[l]), w_route, b_route)

    xp = x_prompt.reshape(n_p, D_MODEL)
    conv_out, q, k, v, la, sr, gc, gg, conv_p = _proj_conv(xp, bp, seq_p, proj_args, conv_args)
    o, gla_p = _gla_prompt(q, k, la, v, bp, seq_p)
    h_p, xr_p, rec_p, wf_p, cnt = _mix_route(xp, conv_out, o, sr, gc, gg, *mix_args,
                                             jnp.zeros((SUBLANES, LANES), F32))

    xs_ = x_sample.reshape(n_s, D_MODEL)
    u, q, k, v, la, sr, gc, gg = _proj(xs_, *proj_args)
    conv_out, conv_s = _conv(u.reshape(bs, seq_s, D_CONV), state_conv, *conv_args, nseq=SAMPLE_SEQS)
    seqs = lambda a: a.reshape(bs, seq_s, a.shape[-1])
    o, gla_s = _gla_sample(seqs(q), seqs(k), seqs(la), seqs(v), state_gla)
    h_s, xr_s, rec_s, wf_s, cnt = _mix_route(xs_, conv_out.reshape(n_s, D_MODEL), o.reshape(n_s, V_W), sr, gc, gg,
                                             *mix_args, cnt)

    counts = cnt[0, :N_EXPERTS].astype(jnp.int32)
    items, starts = _work_items(counts, 2 * n_tok)
    rec = jnp.concatenate([rec_p, rec_s], axis=1).astype(jnp.int32)
    e_ids = jnp.arange(N_EXPERTS, dtype=jnp.int32)[:, None]
    start_of = lambda e: jnp.sum(jnp.where(e[None, :] == e_ids, starts[:, None], 0), axis=0)
    pos0 = start_of(rec[_L_E0]) + rec[_L_R0]
    pos1 = start_of(rec[_L_E1]) + rec[_L_R1]
    xs_sorted = _dispatch(pos0, pos1, xr_p, xr_s)
    ys = _experts(items, xs_sorted, w_expert_gate[l], w_expert_up[l], w_expert_down[l])
    y_p = _combine(pos0, pos1, h_p, wf_p, row2(norm_final), ys, 0)
    y_s = _combine(pos0, pos1, h_s, wf_s, row2(norm_final), ys, n_p)

    return (y_p.reshape(bp, seq_p, D_MODEL), y_s.reshape(bs, seq_s, D_MODEL),
            conv_p[None], gla_p[None], conv_s[None], gla_s[None])
```

```python
import functools

import jax
import jax.numpy as jnp
from jax import lax
from jax.experimental import pallas as pl
from jax.experimental.pallas import tpu as pltpu

F32 = jnp.float32
BF16 = jnp.bfloat16

D_MODEL = 1024
D_CONV = 512
CONV_WIDTH = 31
CONV_HIST = CONV_WIDTH - 1
HEADS = 4
DK = 128
DV = 256
QK_W = HEADS * DK
V_W = HEADS * DV
RANK = 16
GATE_NORM = 16.0
N_GROUPS = 4
EXPERTS_PER_GROUP = 8
N_EXPERTS = 32
D_EXPERT = 512
EPS = 1e-6

LANES = 128
SUBLANES = 8
VMEM_LIMIT = 56 * 1024 * 1024

TOK_TILE = 256
PROJ_TILE = 512
MIX_TILE = 512
GLA_CHUNK = 128
GLA_STEP = 512
SAFE_LOG_DECAY = -80.0
CONV_ROWS = 64
CONV_HALO = 40
ROW_BLOCK = 256
DMA_UNROLL = 8
SAMPLE_SEQS = 32
SAMPLE_GLA_SEQS = 8
SAMPLE_PAD = 16

_C_GLU_A, _C_GLU_B = 0, 512
_C_Q, _C_K, _C_V, _C_R = 1024, 1536, 2048, 3072
_C_ALOW = 4096
_C_GATES = _C_ALOW + RANK


def _sigmoid(x):
    return jax.nn.sigmoid(x)


def _dot(a, b):
    return jnp.dot(a, b, preferred_element_type=F32)


def _normed_input(x_ref, g_ref):
    x = x_ref[...]
    hn = x * lax.rsqrt(jnp.mean(x * x, axis=-1, keepdims=True) + EPS) * g_ref[...]
    return hn.astype(BF16)


def _glu(hb, wm_ref):
    return _dot(hb, wm_ref[:, _C_GLU_A:_C_GLU_B]) * _sigmoid(_dot(hb, wm_ref[:, _C_GLU_B:_C_Q]))


def _projection_steps(hb, wm_ref, wa_ref, wg_ref, wup_ref, bup_ref, bg_ref,
                      q_ref, k_ref, v_ref, la_ref, sr_ref, gc_ref, gg_ref):
    def mm(lo, hi):
        return _dot(hb, wm_ref[:, lo:hi])

    def q_step():
        q_ref[...] = mm(_C_Q, _C_K) * (DK ** -0.5)

    def k_step():
        k_ref[...] = mm(_C_K, _C_V)

    def v_step():
        v_ref[...] = mm(_C_V, _C_R).astype(BF16)

    def r_step():
        r = mm(_C_R, _C_ALOW)
        sr_ref[...] = (r * _sigmoid(r)).astype(BF16)

    def gc_step():
        gc_ref[...] = _sigmoid(_dot(hb, wg_ref[:, :D_MODEL]) + bg_ref[:, :D_MODEL]).astype(BF16)

    def gg_step():
        gg_ref[...] = _sigmoid(_dot(hb, wg_ref[:, D_MODEL:]) + bg_ref[:, D_MODEL:]).astype(BF16)

    def decay_step():
        a_low = _dot(hb, wa_ref[...])
        z = _dot(a_low.astype(BF16), wup_ref[...]) + bup_ref[...]
        la_ref[...] = (jnp.minimum(z, 0.0) - jnp.log1p(jnp.exp(-jnp.abs(z)))) * (1.0 / GATE_NORM)

    return [q_step, k_step, v_step, r_step, gc_step, gg_step, decay_step]


def _proj_kernel(x_ref, g_ref, wm_ref, wa_ref, wg_ref, wup_ref, bup_ref, bg_ref,
                 u_ref, q_ref, k_ref, v_ref, la_ref, sr_ref, gc_ref, gg_ref):
    hb = _normed_input(x_ref, g_ref)
    u_ref[...] = _glu(hb, wm_ref)
    for step in _projection_steps(hb, wm_ref, wa_ref, wg_ref, wup_ref, bup_ref, bg_ref,
                                  q_ref, k_ref, v_ref, la_ref, sr_ref, gc_ref, gg_ref):
        step()


_HIST_PAD = 32
_HIST0 = _HIST_PAD - CONV_HIST


def _conv_taps(w, dw_ref, cs, rt):
    win = w.shape[0]
    acc = jnp.zeros((rt, LANES), F32)
    for s in range(SUBLANES):
        ws = w if s == 0 else pltpu.roll(w, win - s, axis=0)
        for a in range(CONV_HALO // SUBLANES):
            j = a * SUBLANES + s - _HIST0
            if 0 <= j < CONV_WIDTH:
                acc = acc + ws[a * SUBLANES:a * SUBLANES + rt, :] * dw_ref[j:j + 1, cs]
    return acc


def _norm_swish_pointwise(y, lg_ref, lb_ref, wpw_ref):
    mu = jnp.mean(y, axis=-1, keepdims=True)
    var = jnp.mean(jnp.square(y - mu), axis=-1, keepdims=True)
    yn = (y - mu) * lax.rsqrt(var + EPS) * lg_ref[...] + lb_ref[...]
    return _dot((yn * _sigmoid(yn)).astype(BF16), wpw_ref[...])


def _proj_conv_kernel(x_ref, g_ref, wm_ref, wa_ref, wg_ref, wup_ref, bup_ref, bg_ref,
                      dw_ref, db_ref, lg_ref, lb_ref, wpw_ref,
                      co_ref, q_ref, k_ref, v_ref, la_ref, sr_ref, gc_ref, gg_ref, ns_ref,
                      full_ref, y_ref):
    c = pl.program_id(1)
    tm = x_ref.shape[0]

    @pl.when(c == 0)
    def _():
        full_ref[0:_HIST_PAD, :] = jnp.zeros((_HIST_PAD, D_CONV), F32)

    @pl.when(c > 0)
    def _():
        full_ref[0:_HIST_PAD, :] = full_ref[tm:tm + _HIST_PAD, :]

    hb = _normed_input(x_ref, g_ref)
    full_ref[_HIST_PAD:_HIST_PAD + tm, :] = _glu(hb, wm_ref)
    full_ref[_HIST_PAD + tm:, :] = jnp.zeros((SUBLANES, D_CONV), F32)
    ns_ref[0] = full_ref[tm + _HIST0:tm + _HIST_PAD, :]
    rt = CONV_ROWS

    def conv_tile(i, cc):
        def run():
            cs = slice(cc * LANES, (cc + 1) * LANES)
            acc = _conv_taps(full_ref[i * rt:i * rt + rt + CONV_HALO, cs], dw_ref, cs, rt)
            y_ref[i * rt:(i + 1) * rt, cs] = acc + db_ref[:, cs]
        return run

    conv_tiles = [conv_tile(i, cc) for i in range(tm // rt) for cc in range(D_CONV // LANES)]
    proj_steps = _projection_steps(hb, wm_ref, wa_ref, wg_ref, wup_ref, bup_ref, bg_ref,
                                   q_ref, k_ref, v_ref, la_ref, sr_ref, gc_ref, gg_ref)
    per_step = -(-len(conv_tiles) // len(proj_steps))
    for n, step in enumerate(proj_steps):
        step()
        for tile in conv_tiles[n * per_step:(n + 1) * per_step]:
            tile()
    co_ref[...] = _norm_swish_pointwise(y_ref[...], lg_ref, lb_ref, wpw_ref)


def _proj_conv(x2d, nb, seq, proj_consts, conv_consts):
    tm = PROJ_TILE
    nc = seq // tm
    row = lambda w: pl.BlockSpec((tm, w), lambda b, c: (b * nc + c, 0))
    full = lambda a: pl.BlockSpec(a.shape, lambda b, c: (0,) * a.ndim)
    widths = (D_MODEL, QK_W, QK_W, V_W, QK_W, V_W, D_MODEL, D_MODEL)
    dtypes = (F32, F32, F32, BF16, F32, BF16, BF16, BF16)
    consts = tuple(proj_consts) + tuple(conv_consts)
    n = nb * seq
    return pl.pallas_call(
        _proj_conv_kernel,
        grid=(nb, nc),
        in_specs=[row(D_MODEL)] + [full(a) for a in consts],
        out_specs=[row(w) for w in widths] + [pl.BlockSpec((1, CONV_HIST, D_CONV), lambda b, c: (b, 0, 0))],
        out_shape=[jax.ShapeDtypeStruct((n, w), dt) for w, dt in zip(widths, dtypes)]
        + [jax.ShapeDtypeStruct((nb, CONV_HIST, D_CONV), F32)],
        scratch_shapes=[pltpu.VMEM((_HIST_PAD + tm + SUBLANES, D_CONV), F32), pltpu.VMEM((tm, D_CONV), F32)],
        compiler_params=pltpu.CompilerParams(dimension_semantics=("parallel", "arbitrary"),
                                             vmem_limit_bytes=VMEM_LIMIT),
        name="proj_conv",
    )(x2d, *consts)


def _proj(x2d, norm_g, w_main, w_alow, w_gates, w_up, b_up, b_gates):
    n = x2d.shape[0]
    tm = TOK_TILE
    row = lambda w: pl.BlockSpec((tm, w), lambda i: (i, 0))
    full = lambda a: pl.BlockSpec(a.shape, lambda i: (0,) * a.ndim)
    widths = (D_CONV, QK_W, QK_W, V_W, QK_W, V_W, D_MODEL, D_MODEL)
    dtypes = (F32, F32, F32, BF16, F32, BF16, BF16, BF16)
    consts = (norm_g, w_main, w_alow, w_gates, w_up, b_up, b_gates)
    return pl.pallas_call(
        _proj_kernel,
        grid=(n // tm,),
        in_specs=[row(D_MODEL)] + [full(a) for a in consts],
        out_specs=[row(w) for w in widths],
        out_shape=[jax.ShapeDtypeStruct((n, w), dt) for w, dt in zip(widths, dtypes)],
        compiler_params=pltpu.CompilerParams(dimension_semantics=("parallel",),
                                             vmem_limit_bytes=VMEM_LIMIT),
        name="proj",
    )(x2d, *consts)


def _conv_kernel(u_ref, st_ref, dw_ref, db_ref, lg_ref, lb_ref, wpw_ref, out_ref, ns_ref, full_ref, y_ref, *, seq):
    nseq = u_ref.shape[0]
    n_rows = y_ref.shape[0] // nseq
    tail = full_ref.shape[0] - (_HIST_PAD + seq)

    def one_seq(b, carry):
        full_ref[0:_HIST_PAD, :] = jnp.zeros((_HIST_PAD, D_CONV), F32)
        full_ref[_HIST_PAD + seq:, :] = jnp.zeros((tail, D_CONV), F32)
        full_ref[_HIST0:_HIST_PAD, :] = st_ref[b]
        full_ref[_HIST_PAD:_HIST_PAD + seq, :] = u_ref[b]
        ns_ref[b] = full_ref[seq + _HIST0:seq + _HIST_PAD, :]
        for c in range(D_CONV // LANES):
            cs = slice(c * LANES, (c + 1) * LANES)
            acc = _conv_taps(full_ref[:, cs], dw_ref, cs, n_rows)
            y_ref[pl.ds(pl.multiple_of(b * n_rows, SUBLANES), n_rows), cs] = acc + db_ref[:, cs]
        return carry

    lax.fori_loop(0, nseq, one_seq, 0)
    res = _norm_swish_pointwise(y_ref[...], lg_ref, lb_ref, wpw_ref)
    for j in range(nseq):
        out_ref[j] = res[j * n_rows:j * n_rows + seq]


def _conv(u3, state, dw, db, lg, lb, w_pw, nseq):
    nb, seq, _ = u3.shape
    n_rows = -(-seq // SUBLANES) * SUBLANES
    full = lambda a: pl.BlockSpec(a.shape, lambda b: (0,) * a.ndim)
    per_b = lambda r, w: pl.BlockSpec((nseq, r, w), lambda b: (b, 0, 0))
    return pl.pallas_call(
        functools.partial(_conv_kernel, seq=seq),
        grid=(nb // nseq,),
        in_specs=[per_b(seq, D_CONV), pl.BlockSpec((None, nseq, CONV_HIST, D_CONV), lambda b: (0, b, 0, 0)),
                  full(dw), full(db), full(lg), full(lb), full(w_pw)],
        out_specs=[per_b(seq, D_MODEL), per_b(CONV_HIST, D_CONV)],
        out_shape=[jax.ShapeDtypeStruct((nb, seq, D_MODEL), F32),
                   jax.ShapeDtypeStruct((nb, CONV_HIST, D_CONV), F32)],
        scratch_shapes=[pltpu.VMEM((n_rows + CONV_HALO, D_CONV), F32),
                        pltpu.VMEM((nseq * n_rows, D_CONV), F32)],
        compiler_params=pltpu.CompilerParams(dimension_semantics=("parallel",),
                                             vmem_limit_bytes=VMEM_LIMIT),
        name="conv_state",
    )(u3, state, dw, db, lg, lb, w_pw)


def _token_step(s, a_col, k_col, q_col, v_row):
    s = s * a_col + k_col * v_row
    return s, jnp.sum(q_col * s, axis=0, keepdims=True)


def _gla_prompt_kernel(q_ref, k_ref, la_ref, v_ref, o_ref, sout_ref, s_ref, vf_ref):
    c = pl.program_id(1)
    n = GLA_CHUNK
    chunks = [slice(i * n, (i + 1) * n) for i in range(q_ref.shape[0] // n)]

    @pl.when(c == 0)
    def _():
        s_ref[...] = jnp.zeros_like(s_ref)

    r_i = lax.broadcasted_iota(jnp.int32, (n, n), 0)
    c_i = lax.broadcasted_iota(jnp.int32, (n, n), 1)
    tri = jnp.where(r_i >= c_i, 1.0, 0.0).astype(BF16)

    def prefix_sum(g):
        hi = g.astype(BF16)
        lo = (g - hi.astype(F32)).astype(BF16)
        return _dot(tri, hi) + _dot(tri, lo)

    bs = [prefix_sum(la_ref[rows, :]) for rows in chunks]
    total = bs[0][n - 1:n, :]
    for b in bs[1:]:
        total = jnp.minimum(total, b[n - 1:n, :])
    fast = jnp.min(total) > SAFE_LOG_DECAY

    @pl.when(fast)
    def _():
        for h in range(HEADS):
            ks = slice(h * DK, (h + 1) * DK)
            vs = slice(h * DV, (h + 1) * DV)
            s = s_ref[h]
            for rows, b in zip(chunks, bs):
                bh = b[:, ks]
                bl = bh[n - 1:n, :]
                kh = k_ref[rows, ks]
                vh = v_ref[rows, vs]
                qp = (q_ref[rows, ks] * jnp.exp(bh)).astype(BF16)
                kp = (kh * jnp.exp(-bh)).astype(BF16)
                kpp = kh * jnp.exp(bl - bh)
                att = lax.dot_general(qp, kp, (((1,), (1,)), ((), ())), preferred_element_type=F32)
                att = jnp.where(r_i >= c_i, att, 0.0).astype(BF16)
                o_ref[rows, vs] = _dot(qp, s.astype(BF16)) + _dot(att, vh)
                d_col = jnp.sum(jnp.where(r_i == c_i, jnp.exp(bl), 0.0), axis=1, keepdims=True)
                s = s * d_col + _dot(kpp.T.astype(BF16), vh)
            s_ref[h] = s

    @pl.when(jnp.logical_not(fast))
    def _():
        lane = lax.broadcasted_iota(jnp.int32, (1, n), 1)
        for rows in chunks:
            vf_ref[...] = v_ref[rows, :].astype(F32)
            for h in range(HEADS):
                ks = slice(h * DK, (h + 1) * DK)
                vs = slice(h * DV, (h + 1) * DV)
                q_t = q_ref[rows, ks].T
                k_t = k_ref[rows, ks].T
                a_t = jnp.exp(la_ref[rows, ks]).T

                def body(t, s):
                    m = lane == t
                    col = lambda x: jnp.sum(jnp.where(m, x, 0.0), axis=1, keepdims=True)
                    s, o_row = _token_step(s, col(a_t), col(k_t), col(q_t), vf_ref[pl.ds(t, 1), vs])
                    o_ref[pl.ds(rows.start + t, 1), vs] = o_row
                    return s

                s_ref[h] = lax.fori_loop(0, n, body, s_ref[h])

    @pl.when(c == pl.num_programs(1) - 1)
    def _():
        sout_ref[0] = s_ref[...]


def _gla_prompt(q, k, la, v, nb, seq):
    nc = seq // GLA_STEP
    tok = lambda w: pl.BlockSpec((GLA_STEP, w), lambda b, c: (b * nc + c, 0))
    return pl.pallas_call(
        _gla_prompt_kernel,
        grid=(nb, nc),
        in_specs=[tok(QK_W), tok(QK_W), tok(QK_W), tok(V_W)],
        out_specs=[tok(V_W), pl.BlockSpec((1, HEADS, DK, DV), lambda b, c: (b, 0, 0, 0))],
        out_shape=[jax.ShapeDtypeStruct((nb * seq, V_W), F32),
                   jax.ShapeDtypeStruct((nb, HEADS, DK, DV), F32)],
        scratch_shapes=[pltpu.VMEM((HEADS, DK, DV), F32), pltpu.VMEM((GLA_CHUNK, V_W), F32)],
        compiler_params=pltpu.CompilerParams(dimension_semantics=("parallel", "arbitrary"),
                                             vmem_limit_bytes=VMEM_LIMIT),
        name="gla_prompt",
    )(q, k, la, v)


def _gla_sample_kernel(q_ref, k_ref, la_ref, v_ref, s_ref, o_ref, sout_ref, *, seq):
    p = SAMPLE_PAD
    row = lax.broadcasted_iota(jnp.int32, (p, 1), 0)

    def padded(x):
        return jnp.concatenate([x, jnp.zeros((p - seq, x.shape[1]), F32)], axis=0)

    for b in range(q_ref.shape[0]):
        q, k, v, g = padded(q_ref[b]), padded(k_ref[b]), padded(v_ref[b].astype(F32)), padded(la_ref[b])
        bc = g
        sh = 1
        while sh < seq:
            bc = bc + jnp.where(row >= sh, pltpu.roll(bc, sh, axis=0), 0.0)
            sh *= 2
        b_last = bc[seq - 1:seq, :]
        bc = jnp.where(row < seq, bc, b_last)
        qp = (q * jnp.exp(bc)).astype(BF16)
        kpp_t = (k * jnp.exp(b_last - bc)).T.astype(BF16)
        d_cols = jnp.broadcast_to(jnp.exp(b_last), (SUBLANES, QK_W)).T
        vb = v.astype(BF16)

        o_in = [jnp.zeros((p, DV), F32) for _ in range(HEADS)]
        for d in range(seq):
            k_d = k if d == 0 else pltpu.roll(k, d, axis=0)
            v_d = v if d == 0 else pltpu.roll(v, d, axis=0)
            b_d = bc if d == 0 else pltpu.roll(bc, d, axis=0)
            pair = q * k_d * jnp.exp(jnp.where(row >= d, bc - b_d, -jnp.inf))
            for h in range(HEADS):
                att = jnp.sum(pair[:, h * DK:(h + 1) * DK], axis=1, keepdims=True)
                o_in[h] = o_in[h] + att * v_d[:, h * DV:(h + 1) * DV]

        for h in range(HEADS):
            ks = slice(h * DK, (h + 1) * DK)
            vs = slice(h * DV, (h + 1) * DV)
            s = s_ref[b, h]
            o = _dot(qp[:, ks], s.astype(BF16)) + o_in[h]
            o_ref[b, :, vs] = o[:seq]
            sout_ref[b, h] = s * d_cols[ks, 0:1] + _dot(kpp_t[ks, :], vb[:, vs])


def _gla_sample(q3, k3, la3, v3, state):
    nb, seq, _ = q3.shape
    assert seq <= SUBLANES
    ns = SAMPLE_GLA_SEQS
    col = pl.BlockSpec((ns, seq, QK_W), lambda b: (b, 0, 0))
    tok = pl.BlockSpec((ns, seq, V_W), lambda b: (b, 0, 0))
    st = pl.BlockSpec((ns, HEADS, DK, DV), lambda b: (b, 0, 0, 0))
    st_in = pl.BlockSpec((None, ns, HEADS, DK, DV), lambda b: (0, b, 0, 0, 0))
    return pl.pallas_call(
        functools.partial(_gla_sample_kernel, seq=seq),
        grid=(nb // ns,),
        in_specs=[col, col, col, tok, st_in],
        out_specs=[tok, st],
        out_shape=[jax.ShapeDtypeStruct((nb, seq, V_W), F32),
                   jax.ShapeDtypeStruct((nb, HEADS, DK, DV), F32)],
        compiler_params=pltpu.CompilerParams(dimension_semantics=("parallel",),
                                             vmem_limit_bytes=VMEM_LIMIT),
        name="gla_sample",
    )(q3, k3, la3, v3, state)


_L_E0, _L_E1, _L_R0, _L_R1 = 0, 1, 2, 3
_GROUP_LANE0 = N_EXPERTS
HALF = D_MODEL // 2


def _mix_route_kernel(x_ref, co_ref, o_ref, sr_ref, gc_ref, gg_ref, gn_ref, wgo_ref, wo_ref,
                      g_ref, wr_ref, br_ref, cin_ref,
                      h_ref, xp_ref, mi_ref, mf_ref, cnt_ref, carry_ref):
    i = pl.program_id(0)
    tm = x_ref.shape[0]

    @pl.when(i == 0)
    def _():
        carry_ref[...] = cin_ref[0:1, :]

    gla_out = None
    for h in range(HEADS):
        vs = slice(h * DV, (h + 1) * DV)
        oh = o_ref[:, vs]
        on = oh * lax.rsqrt(jnp.mean(oh * oh, axis=-1, keepdims=True) + EPS) * gn_ref[:, vs]
        part = _dot((on * sr_ref[:, vs]).astype(BF16), wgo_ref[vs, :])
        gla_out = part if gla_out is None else gla_out + part
    mix = (gc_ref[...] * co_ref[...] + gg_ref[...] * gla_out).astype(BF16)
    x = x_ref[...] + _dot(mix, wo_ref[...])
    h_ref[...] = x

    hn = x * lax.rsqrt(jnp.mean(x * x, axis=-1, keepdims=True) + EPS) * g_ref[...]
    hi = hn.astype(BF16)
    hi_f = hi.astype(F32)
    top = lax.bitcast_convert_type(hi_f[:, :HALF], jnp.uint32)
    bot = lax.bitcast_convert_type(hi_f[:, HALF:], jnp.uint32)
    xp_ref[...] = top | jnp.right_shift(bot, jnp.uint32(16))
    lo = (hn - hi_f).astype(BF16)
    logits = _dot(hi, wr_ref[0]) + (_dot(lo, wr_ref[0]) + _dot(hi, wr_ref[1])) + br_ref[...]
    lane = lax.broadcasted_iota(jnp.int32, (tm, LANES), 1)
    lane_f = lane.astype(F32)
    group_f = jnp.right_shift(lane, 3).astype(F32)
    big = jnp.float32(LANES)
    neg = jnp.float32(-jnp.inf)

    is_g = (lane >= _GROUP_LANE0) & (lane < _GROUP_LANE0 + N_GROUPS)
    lg = jnp.where(is_g, logits, neg)
    mg = jnp.max(lg, axis=1, keepdims=True)
    p_sel = 1.0 / jnp.sum(jnp.exp(lg - mg), axis=1, keepdims=True)
    g_idx = jnp.min(jnp.where(lg == mg, lane_f, big), axis=1, keepdims=True) - _GROUP_LANE0

    is_e = (lane < N_EXPERTS) & (group_f == g_idx)
    le = jnp.where(is_e, logits, neg)
    v0 = jnp.max(le, axis=1, keepdims=True)
    i0 = jnp.min(jnp.where(le == v0, lane_f, big), axis=1, keepdims=True)
    le1 = jnp.where(lane_f == i0, neg, le)
    v1 = jnp.max(le1, axis=1, keepdims=True)
    i1 = jnp.min(jnp.where(le1 == v1, lane_f, big), axis=1, keepdims=True)
    e1 = jnp.exp(v1 - v0)
    den = 1.0 + e1
    w0 = p_sel * (1.0 / den)
    w1 = p_sel * (e1 / den)

    hit0 = lane_f == i0
    hit1 = lane_f == i1
    cnt = jnp.where(hit0 | hit1, 1.0, 0.0)
    r_i = lax.broadcasted_iota(jnp.int32, (tm, tm), 0)
    c_i = lax.broadcasted_iota(jnp.int32, (tm, tm), 1)
    before = jnp.where(r_i > c_i, 1.0, 0.0).astype(BF16)
    seen = _dot(before, cnt.astype(BF16)) + carry_ref[...]
    rank0 = jnp.sum(jnp.where(hit0, seen, 0.0), axis=1, keepdims=True)
    rank1 = jnp.sum(jnp.where(hit1, seen, 0.0), axis=1, keepdims=True)
    carry_ref[...] = carry_ref[...] + jnp.sum(cnt, axis=0, keepdims=True)
    cnt_ref[...] = jnp.broadcast_to(carry_ref[...], cnt_ref.shape)

    rec = jnp.where(lane == _L_E0, i0, 0.0)
    rec = jnp.where(lane == _L_E1, i1, rec)
    rec = jnp.where(lane == _L_R0, rank0, rec)
    rec = jnp.where(lane == _L_R1, rank1, rec)
    mi_ref[...] = rec.T[:SUBLANES, :]
    mf_ref[...] = jnp.where(lane == 0, w0, jnp.where(lane == 1, w1, 0.0))


def _mix_route(x2d, conv_out, o, sr, gc, gg, gn, w_gla_out, w_out, norm_g, w_route, b_route, counts_in):
    n = x2d.shape[0]
    tm = MIX_TILE
    row = lambda w: pl.BlockSpec((tm, w), lambda i: (i, 0))
    full = lambda a: pl.BlockSpec(a.shape, lambda i: (0,) * a.ndim)
    consts = (gn, w_gla_out, w_out, norm_g, w_route, b_route, counts_in)
    return pl.pallas_call(
        _mix_route_kernel,
        grid=(n // tm,),
        in_specs=[row(D_MODEL)] * 6 + [full(a) for a in consts],
        out_specs=[row(D_MODEL), row(HALF), pl.BlockSpec((SUBLANES, tm), lambda i: (0, i)), row(LANES),
                   pl.BlockSpec((SUBLANES, LANES), lambda i: (0, 0))],
        out_shape=[jax.ShapeDtypeStruct((n, D_MODEL), F32),
                   jax.ShapeDtypeStruct((n, HALF), jnp.uint32),
                   jax.ShapeDtypeStruct((SUBLANES, n), F32),
                   jax.ShapeDtypeStruct((n, LANES), F32),
                   jax.ShapeDtypeStruct((SUBLANES, LANES), F32)],
        scratch_shapes=[pltpu.VMEM((1, LANES), F32)],
        compiler_params=pltpu.CompilerParams(dimension_semantics=("arbitrary",),
                                             vmem_limit_bytes=VMEM_LIMIT),
        name="mix_route",
    )(x2d, conv_out, o, sr, gc, gg, *consts)


def _row_copy(src_ref, src_row, dst_ref, dst_row, sem):
    return pltpu.make_async_copy(src_ref.at[pl.ds(src_row, 1), :], dst_ref.at[pl.ds(dst_row, 1), :], sem)


def _dispatch_kernel(p0_ref, p1_ref, xa_ref, xb_ref, xs_ref, sem, *, tiles_a):
    tm = xa_ref.shape[0]
    i = pl.program_id(0)
    base = i * tm

    def scatter(src_ref):
        def rows(wait):
            def body(t, carry):
                c0 = _row_copy(src_ref, t, xs_ref, p0_ref[base + t], sem)
                c1 = _row_copy(src_ref, t, xs_ref, p1_ref[base + t], sem)
                if wait:
                    c0.wait()
                    c1.wait()
                else:
                    c0.start()
                    c1.start()
                return carry
            lax.fori_loop(0, tm, body, 0, unroll=DMA_UNROLL)

        rows(wait=False)
        rows(wait=True)

    @pl.when(i < tiles_a)
    def _():
        scatter(xa_ref)

    @pl.when(i >= tiles_a)
    def _():
        scatter(xb_ref)


def _dispatch(pos0, pos1, x_a, x_b):
    tm = TOK_TILE
    tiles_a, tiles_b = x_a.shape[0] // tm, x_b.shape[0] // tm
    n = x_a.shape[0] + x_b.shape[0]
    width = x_a.shape[1]
    return pl.pallas_call(
        functools.partial(_dispatch_kernel, tiles_a=tiles_a),
        grid_spec=pltpu.PrefetchScalarGridSpec(
            num_scalar_prefetch=2,
            grid=(tiles_a + tiles_b,),
            in_specs=[pl.BlockSpec((tm, width), lambda i, p0, p1: (jnp.minimum(i, tiles_a - 1), 0)),
                      pl.BlockSpec((tm, width), lambda i, p0, p1: (jnp.maximum(i - tiles_a, 0), 0))],
            out_specs=pl.BlockSpec(memory_space=pl.ANY),
            scratch_shapes=[pltpu.SemaphoreType.DMA(())],
        ),
        out_shape=jax.ShapeDtypeStruct((2 * n, width), x_a.dtype),
        compiler_params=pltpu.CompilerParams(dimension_semantics=("arbitrary",)),
        name="dispatch",
    )(pos0, pos1, x_a, x_b)


def _combine_kernel(p0_ref, p1_ref, h_ref, wf_ref, g_ref, ys_ref, out_ref, b0_ref, b1_ref, sems, *, tok_offset):
    tm = h_ref.shape[0]
    i = pl.program_id(0)

    def gather(tile, slot, wait):
        base = tok_offset + tile * tm

        def rows(t, carry):
            c0 = _row_copy(ys_ref, p0_ref[base + t], b0_ref.at[slot], t, sems.at[slot])
            c1 = _row_copy(ys_ref, p1_ref[base + t], b1_ref.at[slot], t, sems.at[slot])
            if wait:
                c0.wait()
                c1.wait()
            else:
                c0.start()
                c1.start()
            return carry

        lax.fori_loop(0, tm, rows, 0, unroll=DMA_UNROLL)

    slot = lax.rem(i, 2)

    @pl.when(i == 0)
    def _():
        gather(i, slot, wait=False)

    @pl.when(i + 1 < pl.num_programs(0))
    def _():
        gather(i + 1, 1 - slot, wait=False)

    gather(i, slot, wait=True)
    y = h_ref[...] + (b0_ref[slot] * wf_ref[:, 0:1] + b1_ref[slot] * wf_ref[:, 1:2])
    out_ref[...] = y * lax.rsqrt(jnp.mean(y * y, axis=-1, keepdims=True) + EPS) * g_ref[...]


def _combine(pos0, pos1, h, wf, norm_g, ys, tok_offset):
    n_rows = h.shape[0]
    tm = TOK_TILE
    return pl.pallas_call(
        functools.partial(_combine_kernel, tok_offset=tok_offset),
        grid_spec=pltpu.PrefetchScalarGridSpec(
            num_scalar_prefetch=2,
            grid=(n_rows // tm,),
            in_specs=[pl.BlockSpec((tm, D_MODEL), lambda i, p0, p1: (i, 0)),
                      pl.BlockSpec((tm, LANES), lambda i, p0, p1: (i, 0)),
                      pl.BlockSpec((1, D_MODEL), lambda i, p0, p1: (0, 0)),
                      pl.BlockSpec(memory_space=pl.ANY)],
            out_specs=pl.BlockSpec((tm, D_MODEL), lambda i, p0, p1: (i, 0)),
            scratch_shapes=[pltpu.VMEM((2, tm, D_MODEL), F32), pltpu.VMEM((2, tm, D_MODEL), F32),
                            pltpu.SemaphoreType.DMA((2,))],
        ),
        out_shape=jax.ShapeDtypeStruct((n_rows, D_MODEL), F32),
        compiler_params=pltpu.CompilerParams(dimension_semantics=("arbitrary",),
                                             vmem_limit_bytes=VMEM_LIMIT),
        name="combine",
    )(pos0, pos1, h, wf, norm_g, ys)


def _expert_kernel(blk_ref, exp_ref, lo_ref, hi_ref, first_ref, slot_ref, next_ref,
                   xs_ref, wg_hbm, wu_hbm, wd_hbm, ys_ref,
                   wg32_ref, wu32_ref, wd32_ref, wgb_ref, wub_ref, wdb_ref, sems):
    w = pl.program_id(0)
    e = exp_ref[w]
    slot = slot_ref[w]
    changed = jnp.logical_or(w == 0, e != exp_ref[jnp.maximum(w - 1, 0)])

    def weight_copies(expert, s):
        return [pltpu.make_async_copy(src.at[expert], dst.at[s], sems.at[s, j])
                for j, (src, dst) in enumerate(((wg_hbm, wg32_ref), (wu_hbm, wu32_ref), (wd_hbm, wd32_ref)))]

    @pl.when(w == 0)
    def _():
        for c in weight_copies(e, slot):
            c.start()

    @pl.when(changed)
    def _():
        for c in weight_copies(e, slot):
            c.wait()
        wgb_ref[...] = wg32_ref[slot].astype(BF16)
        wub_ref[...] = wu32_ref[slot].astype(BF16)
        wdb_ref[...] = wd32_ref[slot].astype(BF16)

        @pl.when(next_ref[w] >= 0)
        def _():
            for c in weight_copies(next_ref[w], 1 - slot):
                c.start()

    lo = lo_ref[w]
    hi = hi_ref[w]

    @pl.when(hi > lo)
    def _():
        packed = xs_ref[...]
        x_top = lax.bitcast_convert_type(packed & jnp.uint32(0xFFFF0000), F32)
        x_bot = lax.bitcast_convert_type(jnp.left_shift(packed, jnp.uint32(16)), F32)
        x = jnp.concatenate([x_top, x_bot], axis=1).astype(BF16)
        gate = _dot(x, wgb_ref[...])
        hid = (gate * _sigmoid(gate)) * _dot(x, wub_ref[...])
        y = _dot(hid.astype(BF16), wdb_ref[...])
        rows = blk_ref[w] * ROW_BLOCK + lax.broadcasted_iota(jnp.int32, (ROW_BLOCK, 1), 0)
        y = jnp.where((rows >= lo) & (rows < hi), y, 0.0)

        @pl.when(first_ref[w] == 1)
        def _():
            ys_ref[...] = y

        @pl.when(first_ref[w] == 0)
        def _():
            ys_ref[...] = ys_ref[...] + y


def _experts(items, xs, w_gate, w_up, w_down):
    n_rows = xs.shape[0]
    n_items = items[0].shape[0]
    hbm = pl.BlockSpec(memory_space=pl.ANY)
    return pl.pallas_call(
        _expert_kernel,
        grid_spec=pltpu.PrefetchScalarGridSpec(
            num_scalar_prefetch=len(items),
            grid=(n_items,),
            in_specs=[pl.BlockSpec((ROW_BLOCK, HALF), lambda w, blk, *_: (blk[w], 0)), hbm, hbm, hbm],
            out_specs=pl.BlockSpec((ROW_BLOCK, D_MODEL), lambda w, blk, *_: (blk[w], 0)),
            scratch_shapes=[pltpu.VMEM((2, D_MODEL, D_EXPERT), F32), pltpu.VMEM((2, D_MODEL, D_EXPERT), F32),
                            pltpu.VMEM((2, D_EXPERT, D_MODEL), F32),
                            pltpu.VMEM((D_MODEL, D_EXPERT), BF16), pltpu.VMEM((D_MODEL, D_EXPERT), BF16),
                            pltpu.VMEM((D_EXPERT, D_MODEL), BF16),
                            pltpu.SemaphoreType.DMA((2, 3))],
        ),
        out_shape=jax.ShapeDtypeStruct((n_rows, D_MODEL), F32),
        compiler_params=pltpu.CompilerParams(dimension_semantics=("arbitrary",),
                                             vmem_limit_bytes=VMEM_LIMIT),
        name="experts",
    )(*items, xs, w_gate, w_up, w_down)


def _work_items(counts, n_rows):
    n_blocks = n_rows // ROW_BLOCK
    n_items = n_blocks + N_EXPERTS - 1
    ends = jnp.cumsum(counts)
    starts = ends - counts
    blk_lo = starts // ROW_BLOCK
    blk_hi = jnp.maximum(ends - 1, 0) // ROW_BLOCK
    per_e = jnp.where(counts > 0, blk_hi - blk_lo + 1, 0)
    item_end = jnp.cumsum(per_e)
    item_start = item_end - per_e
    total = item_end[-1]
    w = jnp.arange(n_items, dtype=jnp.int32)
    live = w < total
    e = jnp.sum((jnp.minimum(w, total - 1)[:, None] >= item_end[None, :]).astype(jnp.int32), axis=1)
    e = jnp.minimum(e, N_EXPERTS - 1)
    sel = (e[:, None] == jnp.arange(N_EXPERTS, dtype=jnp.int32)[None, :]).astype(jnp.int32)
    pick = lambda table: jnp.sum(sel * table[None, :], axis=1)
    blk = jnp.where(live, pick(blk_lo) + (w - pick(item_start)), n_blocks - 1).astype(jnp.int32)
    lo = jnp.where(live, jnp.maximum(pick(starts), blk * ROW_BLOCK), 0).astype(jnp.int32)
    hi = jnp.where(live, jnp.minimum(pick(ends), (blk + 1) * ROW_BLOCK), 0).astype(jnp.int32)
    prev_blk = jnp.concatenate([jnp.full((1,), -1, jnp.int32), blk[:-1]])
    first = (live & (blk != prev_blk)).astype(jnp.int32)
    ids = jnp.arange(N_EXPERTS, dtype=jnp.int32)
    used = counts > 0
    rank_e = jnp.cumsum(used.astype(jnp.int32)) - 1
    later = used[None, :] & (ids[None, :] > ids[:, None])
    next_e = jnp.min(jnp.where(later, ids[None, :], N_EXPERTS), axis=1)
    next_e = jnp.where(next_e < N_EXPERTS, next_e, -1)
    slot = (pick(rank_e) % 2).astype(jnp.int32)
    nxt = pick(next_e).astype(jnp.int32)
    return (blk, e.astype(jnp.int32), lo, hi, first, slot, nxt), starts


def kernel(x_prompt, x_sample, state_conv, state_gla, norm_mix, w_in, b_gates, w_gla_gate_up, b_gla_gate_up, conv_dw, conv_db, conv_ln_g, conv_ln_b, w_conv_out, gla_norm_g, w_gla_out, w_out, norm_ffn, w_router_group, b_router_group, w_router_expert, b_router_expert, w_expert_gate, w_expert_up, w_expert_down, norm_final):
    depth = norm_mix.shape[0]
    assert depth == 1
    l = 0
    bp, seq_p, _ = x_prompt.shape
    bs, seq_s, _ = x_sample.shape
    n_p, n_s = bp * seq_p, bs * seq_s
    n_tok = n_p + n_s
    row2 = lambda a: a.reshape(1, -1)

    wi = w_in[l]
    w_main = wi[:, :_C_ALOW].astype(BF16)
    w_alow = jnp.pad(wi[:, _C_ALOW:_C_GATES], ((0, 0), (0, LANES - RANK))).astype(BF16)
    w_gates = wi[:, _C_GATES:].astype(BF16)
    w_up_pad = jnp.pad(w_gla_gate_up[l], ((0, LANES - RANK), (0, 0))).astype(BF16)
    w_route = jnp.pad(jnp.concatenate([w_router_expert[l], w_router_group[l]], axis=1),
                      ((0, 0), (0, LANES - N_EXPERTS - N_GROUPS)))
    w_route_hi = w_route.astype(BF16)
    w_route = jnp.stack([w_route_hi, (w_route - w_route_hi.astype(F32)).astype(BF16)])
    b_route = jnp.pad(jnp.concatenate([b_router_expert[l], b_router_group[l]]),
                      (0, LANES - N_EXPERTS - N_GROUPS)).reshape(1, LANES)
    w_pw = w_conv_out[l].astype(BF16)
    w_go = w_gla_out[l].astype(BF16)
    w_o = w_out[l].astype(BF16)
    conv_args = (conv_dw[l], row2(conv_db[l]), row2(conv_ln_g[l]), row2(conv_ln_b[l]), w_pw)
    proj_args = (row2(norm_mix[l]), w_main, w_alow, w_gates, w_up_pad, row2(b_gla_gate_up[l]),
                 row2(b_gates[l]))
    mix_args = (row2(gla_norm_g[l]), w_go, w_o, row2(norm_ffn[l]), w_route, b_route)

    xp = x_prompt.reshape(n_p, D_MODEL)
    conv_out, q, k, v, la, sr, gc, gg, conv_p = _proj_conv(xp, bp, seq_p, proj_args, conv_args)
    o, gla_p = _gla_prompt(q, k, la, v, bp, seq_p)
    h_p, xr_p, rec_p, wf_p, cnt = _mix_route(xp, conv_out, o, sr, gc, gg, *mix_args,
                                             jnp.zeros((SUBLANES, LANES), F32))

    xs_ = x_sample.reshape(n_s, D_MODEL)
    u, q, k, v, la, sr, gc, gg = _proj(xs_, *proj_args)
    conv_out, conv_s = _conv(u.reshape(bs, seq_s, D_CONV), state_conv, *conv_args, nseq=SAMPLE_SEQS)
    seqs = lambda a: a.reshape(bs, seq_s, a.shape[-1])
    o, gla_s = _gla_sample(seqs(q), seqs(k), seqs(la), seqs(v), state_gla)
    h_s, xr_s, rec_s, wf_s, cnt = _mix_route(xs_, conv_out.reshape(n_s, D_MODEL), o.reshape(n_s, V_W), sr, gc, gg,
                                             *mix_args, cnt)

    counts = cnt[0, :N_EXPERTS].astype(jnp.int32)
    items, starts = _work_items(counts, 2 * n_tok)
    rec = jnp.concatenate([rec_p, rec_s], axis=1).astype(jnp.int32)
    e_ids = jnp.arange(N_EXPERTS, dtype=jnp.int32)[:, None]
    start_of = lambda e: jnp.sum(jnp.where(e[None, :] == e_ids, starts[:, None], 0), axis=0)
    pos0 = start_of(rec[_L_E0]) + rec[_L_R0]
    pos1 = start_of(rec[_L_E1]) + rec[_L_R1]
    xs_sorted = _dispatch(pos0, pos1, xr_p, xr_s)
    ys = _experts(items, xs_sorted, w_expert_gate[l], w_expert_up[l], w_expert_down[l])
    y_p = _combine(pos0, pos1, h_p, wf_p, row2(norm_final), ys, 0)
    y_s = _combine(pos0, pos1, h_s, wf_s, row2(norm_final), ys, n_p)

    return (y_p.reshape(bp, seq_p, D_MODEL), y_s.reshape(bs, seq_s, D_MODEL),
            conv_p[None], gla_p[None], conv_s[None], gla_s[None])
```

```python
import functools

import jax
import jax.numpy as jnp
from jax import lax
from jax.experimental import pallas as pl
from jax.experimental.pallas import tpu as pltpu

F32 = jnp.float32
BF16 = jnp.bfloat16

D_MODEL = 1024
D_CONV = 512
CONV_WIDTH = 31
CONV_HIST = CONV_WIDTH - 1
HEADS = 4
DK = 128
DV = 256
QK_W = HEADS * DK
V_W = HEADS * DV
RANK = 16
GATE_NORM = 16.0
N_GROUPS = 4
EXPERTS_PER_GROUP = 8
N_EXPERTS = 32
D_EXPERT = 512
EPS = 1e-6

LANES = 128
SUBLANES = 8
VMEM_LIMIT = 56 * 1024 * 1024

TOK_TILE = 256
PROJ_TILE = 512
MIX_TILE = 512
GLA_CHUNK = 128
GLA_STEP = 512
SAFE_LOG_DECAY = -80.0
CONV_ROWS = 64
CONV_HALO = 40
ROW_BLOCK = 256
DMA_UNROLL = 8
SAMPLE_SEQS = 32
SAMPLE_GLA_SEQS = 8
SAMPLE_PAD = 16

_C_GLU_A, _C_GLU_B = 0, 512
_C_Q, _C_K, _C_V, _C_R = 1024, 1536, 2048, 3072
_C_ALOW = 4096
_C_GATES = _C_ALOW + RANK


def _sigmoid(x):
    return jax.nn.sigmoid(x)


def _dot(a, b):
    return jnp.dot(a, b, preferred_element_type=F32)


def _normed_input(x_ref, g_ref):
    x = x_ref[...]
    hn = x * lax.rsqrt(jnp.mean(x * x, axis=-1, keepdims=True) + EPS) * g_ref[...]
    return hn.astype(BF16)


def _glu(hb, wm_ref):
    return _dot(hb, wm_ref[:, _C_GLU_A:_C_GLU_B]) * _sigmoid(_dot(hb, wm_ref[:, _C_GLU_B:_C_Q]))


def _projection_steps(hb, wm_ref, wa_ref, wg_ref, wup_ref, bup_ref, bg_ref,
                      q_ref, k_ref, v_ref, la_ref, sr_ref, gc_ref, gg_ref):
    def mm(lo, hi):
        return _dot(hb, wm_ref[:, lo:hi])

    def q_step():
        q_ref[...] = mm(_C_Q, _C_K) * (DK ** -0.5)

    def k_step():
        k_ref[...] = mm(_C_K, _C_V)

    def v_step():
        v_ref[...] = mm(_C_V, _C_R).astype(BF16)

    def r_step():
        r = mm(_C_R, _C_ALOW)
        sr_ref[...] = (r * _sigmoid(r)).astype(BF16)

    def gc_step():
        gc_ref[...] = _sigmoid(_dot(hb, wg_ref[:, :D_MODEL]) + bg_ref[:, :D_MODEL]).astype(BF16)

    def gg_step():
        gg_ref[...] = _sigmoid(_dot(hb, wg_ref[:, D_MODEL:]) + bg_ref[:, D_MODEL:]).astype(BF16)

    def decay_step():
        a_low = _dot(hb, wa_ref[...])
        z = _dot(a_low.astype(BF16), wup_ref[...]) + bup_ref[...]
        la_ref[...] = (jnp.minimum(z, 0.0) - jnp.log1p(jnp.exp(-jnp.abs(z)))) * (1.0 / GATE_NORM)

    return [q_step, k_step, v_step, r_step, gc_step, gg_step, decay_step]


def _proj_kernel(x_ref, g_ref, wm_ref, wa_ref, wg_ref, wup_ref, bup_ref, bg_ref,
                 u_ref, q_ref, k_ref, v_ref, la_ref, sr_ref, gc_ref, gg_ref):
    hb = _normed_input(x_ref, g_ref)
    u_ref[...] = _glu(hb, wm_ref)
    for step in _projection_steps(hb, wm_ref, wa_ref, wg_ref, wup_ref, bup_ref, bg_ref,
                                  q_ref, k_ref, v_ref, la_ref, sr_ref, gc_ref, gg_ref):
        step()


_HIST_PAD = 32
_HIST0 = _HIST_PAD - CONV_HIST


def _conv_taps(w, dw_ref, cs, rt):
    win = w.shape[0]
    acc = jnp.zeros((rt, LANES), F32)
    for s in range(SUBLANES):
        ws = w if s == 0 else pltpu.roll(w, win - s, axis=0)
        for a in range(CONV_HALO // SUBLANES):
            j = a * SUBLANES + s - _HIST0
            if 0 <= j < CONV_WIDTH:
                acc = acc + ws[a * SUBLANES:a * SUBLANES + rt, :] * dw_ref[j:j + 1, cs]
    return acc


def _norm_swish_pointwise(y, lg_ref, lb_ref, wpw_ref):
    mu = jnp.mean(y, axis=-1, keepdims=True)
    var = jnp.mean(jnp.square(y - mu), axis=-1, keepdims=True)
    yn = (y - mu) * lax.rsqrt(var + EPS) * lg_ref[...] + lb_ref[...]
    return _dot((yn * _sigmoid(yn)).astype(BF16), wpw_ref[...])


def _proj_conv_kernel(x_ref, g_ref, wm_ref, wa_ref, wg_ref, wup_ref, bup_ref, bg_ref,
                      dw_ref, db_ref, lg_ref, lb_ref, wpw_ref,
                      co_ref, q_ref, k_ref, v_ref, la_ref, sr_ref, gc_ref, gg_ref, ns_ref,
                      full_ref, y_ref):
    c = pl.program_id(1)
    tm = x_ref.shape[0]

    @pl.when(c == 0)
    def _():
        full_ref[0:_HIST_PAD, :] = jnp.zeros((_HIST_PAD, D_CONV), F32)

    @pl.when(c > 0)
    def _():
        full_ref[0:_HIST_PAD, :] = full_ref[tm:tm + _HIST_PAD, :]

    hb = _normed_input(x_ref, g_ref)
    full_ref[_HIST_PAD:_HIST_PAD + tm, :] = _glu(hb, wm_ref)
    full_ref[_HIST_PAD + tm:, :] = jnp.zeros((SUBLANES, D_CONV), F32)
    ns_ref[0] = full_ref[tm + _HIST0:tm + _HIST_PAD, :]
    rt = CONV_ROWS

    def conv_tile(i, cc):
        def run():
            cs = slice(cc * LANES, (cc + 1) * LANES)
            acc = _conv_taps(full_ref[i * rt:i * rt + rt + CONV_HALO, cs], dw_ref, cs, rt)
            y_ref[i * rt:(i + 1) * rt, cs] = acc + db_ref[:, cs]
        return run

    conv_tiles = [conv_tile(i, cc) for i in range(tm // rt) for cc in range(D_CONV // LANES)]
    proj_steps = _projection_steps(hb, wm_ref, wa_ref, wg_ref, wup_ref, bup_ref, bg_ref,
                                   q_ref, k_ref, v_ref, la_ref, sr_ref, gc_ref, gg_ref)
    per_step = -(-len(conv_tiles) // len(proj_steps))
    for n, step in enumerate(proj_steps):
        step()
        for tile in conv_tiles[n * per_step:(n + 1) * per_step]:
            tile()
    co_ref[...] = _norm_swish_pointwise(y_ref[...], lg_ref, lb_ref, wpw_ref)


def _proj_conv(x2d, nb, seq, proj_consts, conv_consts):
    tm = PROJ_TILE
    nc = seq // tm
    row = lambda w: pl.BlockSpec((tm, w), lambda b, c: (b * nc + c, 0))
    full = lambda a: pl.BlockSpec(a.shape, lambda b, c: (0,) * a.ndim)
    widths = (D_MODEL, QK_W, QK_W, V_W, QK_W, V_W, D_MODEL, D_MODEL)
    dtypes = (F32, F32, F32, BF16, F32, BF16, BF16, BF16)
    consts = tuple(proj_consts) + tuple(conv_consts)
    n = nb * seq
    return pl.pallas_call(
        _proj_conv_kernel,
        grid=(nb, nc),
        in_specs=[row(D_MODEL)] + [full(a) for a in consts],
        out_specs=[row(w) for w in widths] + [pl.BlockSpec((1, CONV_HIST, D_CONV), lambda b, c: (b, 0, 0))],
        out_shape=[jax.ShapeDtypeStruct((n, w), dt) for w, dt in zip(widths, dtypes)]
        + [jax.ShapeDtypeStruct((nb, CONV_HIST, D_CONV), F32)],
        scratch_shapes=[pltpu.VMEM((_HIST_PAD + tm + SUBLANES, D_CONV), F32), pltpu.VMEM((tm, D_CONV), F32)],
        compiler_params=pltpu.CompilerParams(dimension_semantics=("parallel", "arbitrary"),
                                             vmem_limit_bytes=VMEM_LIMIT),
        name="proj_conv",
    )(x2d, *consts)


def _proj(x2d, norm_g, w_main, w_alow, w_gates, w_up, b_up, b_gates):
    n = x2d.shape[0]
    tm = TOK_TILE
    row = lambda w: pl.BlockSpec((tm, w), lambda i: (i, 0))
    full = lambda a: pl.BlockSpec(a.shape, lambda i: (0,) * a.ndim)
    widths = (D_CONV, QK_W, QK_W, V_W, QK_W, V_W, D_MODEL, D_MODEL)
    dtypes = (F32, F32, F32, BF16, F32, BF16, BF16, BF16)
    consts = (norm_g, w_main, w_alow, w_gates, w_up, b_up, b_gates)
    return pl.pallas_call(
        _proj_kernel,
        grid=(n // tm,),
        in_specs=[row(D_MODEL)] + [full(a) for a in consts],
        out_specs=[row(w) for w in widths],
        out_shape=[jax.ShapeDtypeStruct((n, w), dt) for w, dt in zip(widths, dtypes)],
        compiler_params=pltpu.CompilerParams(dimension_semantics=("parallel",),
                                             vmem_limit_bytes=VMEM_LIMIT),
        name="proj",
    )(x2d, *consts)


def _conv_kernel(u_ref, st_ref, dw_ref, db_ref, lg_ref, lb_ref, wpw_ref, out_ref, ns_ref, full_ref, y_ref, *, seq):
    nseq = u_ref.shape[0]
    n_rows = y_ref.shape[0] // nseq
    tail = full_ref.shape[0] - (_HIST_PAD + seq)

    def one_seq(b, carry):
        full_ref[0:_HIST_PAD, :] = jnp.zeros((_HIST_PAD, D_CONV), F32)
        full_ref[_HIST_PAD + seq:, :] = jnp.zeros((tail, D_CONV), F32)
        full_ref[_HIST0:_HIST_PAD, :] = st_ref[b]
        full_ref[_HIST_PAD:_HIST_PAD + seq, :] = u_ref[b]
        ns_ref[b] = full_ref[seq + _HIST0:seq + _HIST_PAD, :]
        for c in range(D_CONV // LANES):
            cs = slice(c * LANES, (c + 1) * LANES)
            acc = _conv_taps(full_ref[:, cs], dw_ref, cs, n_rows)
            y_ref[pl.ds(pl.multiple_of(b * n_rows, SUBLANES), n_rows), cs] = acc + db_ref[:, cs]
        return carry

    lax.fori_loop(0, nseq, one_seq, 0)
    res = _norm_swish_pointwise(y_ref[...], lg_ref, lb_ref, wpw_ref)
    for j in range(nseq):
        out_ref[j] = res[j * n_rows:j * n_rows + seq]


def _conv(u3, state, dw, db, lg, lb, w_pw, nseq):
    nb, seq, _ = u3.shape
    n_rows = -(-seq // SUBLANES) * SUBLANES
    full = lambda a: pl.BlockSpec(a.shape, lambda b: (0,) * a.ndim)
    per_b = lambda r, w: pl.BlockSpec((nseq, r, w), lambda b: (b, 0, 0))
    return pl.pallas_call(
        functools.partial(_conv_kernel, seq=seq),
        grid=(nb // nseq,),
        in_specs=[per_b(seq, D_CONV), pl.BlockSpec((None, nseq, CONV_HIST, D_CONV), lambda b: (0, b, 0, 0)),
                  full(dw), full(db), full(lg), full(lb), full(w_pw)],
        out_specs=[per_b(seq, D_MODEL), per_b(CONV_HIST, D_CONV)],
        out_shape=[jax.ShapeDtypeStruct((nb, seq, D_MODEL), F32),
                   jax.ShapeDtypeStruct((nb, CONV_HIST, D_CONV), F32)],
        scratch_shapes=[pltpu.VMEM((n_rows + CONV_HALO, D_CONV), F32),
                        pltpu.VMEM((nseq * n_rows, D_CONV), F32)],
        compiler_params=pltpu.CompilerParams(dimension_semantics=("parallel",),
                                             vmem_limit_bytes=VMEM_LIMIT),
        name="conv_state",
    )(u3, state, dw, db, lg, lb, w_pw)


def _token_step(s, a_col, k_col, q_col, v_row):
    s = s * a_col + k_col * v_row
    return s, jnp.sum(q_col * s, axis=0, keepdims=True)


def _gla_prompt_kernel(q_ref, k_ref, la_ref, v_ref, o_ref, sout_ref, s_ref, vf_ref):
    c = pl.program_id(1)
    n = GLA_CHUNK
    chunks = [slice(i * n, (i + 1) * n) for i in range(q_ref.shape[0] // n)]

    @pl.when(c == 0)
    def _():
        s_ref[...] = jnp.zeros_like(s_ref)

    r_i = lax.broadcasted_iota(jnp.int32, (n, n), 0)
    c_i = lax.broadcasted_iota(jnp.int32, (n, n), 1)
    tri = jnp.where(r_i >= c_i, 1.0, 0.0).astype(BF16)

    def prefix_sum(g):
        hi = g.astype(BF16)
        lo = (g - hi.astype(F32)).astype(BF16)
        return _dot(tri, hi) + _dot(tri, lo)

    bs = [prefix_sum(la_ref[rows, :]) for rows in chunks]
    total = bs[0][n - 1:n, :]
    for b in bs[1:]:
        total = jnp.minimum(total, b[n - 1:n, :])
    fast = jnp.min(total) > SAFE_LOG_DECAY

    @pl.when(fast)
    def _():
        for h in range(HEADS):
            ks = slice(h * DK, (h + 1) * DK)
            vs = slice(h * DV, (h + 1) * DV)
            s = s_ref[h]
            for rows, b in zip(chunks, bs):
                bh = b[:, ks]
                bl = bh[n - 1:n, :]
                kh = k_ref[rows, ks]
                vh = v_ref[rows, vs]
                qp = (q_ref[rows, ks] * jnp.exp(bh)).astype(BF16)
                kp = (kh * jnp.exp(-bh)).astype(BF16)
                kpp = kh * jnp.exp(bl - bh)
                att = lax.dot_general(qp, kp, (((1,), (1,)), ((), ())), preferred_element_type=F32)
                att = jnp.where(r_i >= c_i, att, 0.0).astype(BF16)
                o_ref[rows, vs] = _dot(qp, s.astype(BF16)) + _dot(att, vh)
                d_col = jnp.sum(jnp.where(r_i == c_i, jnp.exp(bl), 0.0), axis=1, keepdims=True)
                s = s * d_col + _dot(kpp.T.astype(BF16), vh)
            s_ref[h] = s

    @pl.when(jnp.logical_not(fast))
    def _():
        lane = lax.broadcasted_iota(jnp.int32, (1, n), 1)
        for rows in chunks:
            vf_ref[...] = v_ref[rows, :].astype(F32)
            for h in range(HEADS):
                ks = slice(h * DK, (h + 1) * DK)
                vs = slice(h * DV, (h + 1) * DV)
                q_t = q_ref[rows, ks].T
                k_t = k_ref[rows, ks].T
                a_t = jnp.exp(la_ref[rows, ks]).T

                def body(t, s):
                    m = lane == t
                    col = lambda x: jnp.sum(jnp.where(m, x, 0.0), axis=1, keepdims=True)
                    s, o_row = _token_step(s, col(a_t), col(k_t), col(q_t), vf_ref[pl.ds(t, 1), vs])
                    o_ref[pl.ds(rows.start + t, 1), vs] = o_row
                    return s

                s_ref[h] = lax.fori_loop(0, n, body, s_ref[h])

    @pl.when(c == pl.num_programs(1) - 1)
    def _():
        sout_ref[0] = s_ref[...]


def _gla_prompt(q, k, la, v, nb, seq):
    nc = seq // GLA_STEP
    tok = lambda w: pl.BlockSpec((GLA_STEP, w), lambda b, c: (b * nc + c, 0))
    return pl.pallas_call(
        _gla_prompt_kernel,
        grid=(nb, nc),
        in_specs=[tok(QK_W), tok(QK_W), tok(QK_W), tok(V_W)],
        out_specs=[tok(V_W), pl.BlockSpec((1, HEADS, DK, DV), lambda b, c: (b, 0, 0, 0))],
        out_shape=[jax.ShapeDtypeStruct((nb * seq, V_W), F32),
                   jax.ShapeDtypeStruct((nb, HEADS, DK, DV), F32)],
        scratch_shapes=[pltpu.VMEM((HEADS, DK, DV), F32), pltpu.VMEM((GLA_CHUNK, V_W), F32)],
        compiler_params=pltpu.CompilerParams(dimension_semantics=("parallel", "arbitrary"),
                                             vmem_limit_bytes=VMEM_LIMIT),
        name="gla_prompt",
    )(q, k, la, v)


def _gla_sample_kernel(q_ref, k_ref, la_ref, v_ref, s_ref, o_ref, sout_ref, *, seq):
    p = SAMPLE_PAD
    row = lax.broadcasted_iota(jnp.int32, (p, 1), 0)

    def padded(x):
        return jnp.concatenate([x, jnp.zeros((p - seq, x.shape[1]), F32)], axis=0)

    for b in range(q_ref.shape[0]):
        q, k, v, g = padded(q_ref[b]), padded(k_ref[b]), padded(v_ref[b].astype(F32)), padded(la_ref[b])
        bc = g
        sh = 1
        while sh < seq:
            bc = bc + jnp.where(row >= sh, pltpu.roll(bc, sh, axis=0), 0.0)
            sh *= 2
        b_last = bc[seq - 1:seq, :]
        bc = jnp.where(row < seq, bc, b_last)
        qp = (q * jnp.exp(bc)).astype(BF16)
        kpp_t = (k * jnp.exp(b_last - bc)).T.astype(BF16)
        d_cols = jnp.broadcast_to(jnp.exp(b_last), (SUBLANES, QK_W)).T
        vb = v.astype(BF16)

        o_in = [jnp.zeros((p, DV), F32) for _ in range(HEADS)]
        for d in range(seq):
            k_d = k if d == 0 else pltpu.roll(k, d, axis=0)
            v_d = v if d == 0 else pltpu.roll(v, d, axis=0)
            b_d = bc if d == 0 else pltpu.roll(bc, d, axis=0)
            pair = q * k_d * jnp.exp(jnp.where(row >= d, bc - b_d, -jnp.inf))
            for h in range(HEADS):
                att = jnp.sum(pair[:, h * DK:(h + 1) * DK], axis=1, keepdims=True)
                o_in[h] = o_in[h] + att * v_d[:, h * DV:(h + 1) * DV]

        for h in range(HEADS):
            ks = slice(h * DK, (h + 1) * DK)
            vs = slice(h * DV, (h + 1) * DV)
            s = s_ref[b, h]
            o = _dot(qp[:, ks], s.astype(BF16)) + o_in[h]
            o_ref[b, :, vs] = o[:seq]
            sout_ref[b, h] = s * d_cols[ks, 0:1] + _dot(kpp_t[ks, :], vb[:, vs])


def _gla_sample(q3, k3, la3, v3, state):
    nb, seq, _ = q3.shape
    assert seq <= SUBLANES
    ns = SAMPLE_GLA_SEQS
    col = pl.BlockSpec((ns, seq, QK_W), lambda b: (b, 0, 0))
    tok = pl.BlockSpec((ns, seq, V_W), lambda b: (b, 0, 0))
    st = pl.BlockSpec((ns, HEADS, DK, DV), lambda b: (b, 0, 0, 0))
    st_in = pl.BlockSpec((None, ns, HEADS, DK, DV), lambda b: (0, b, 0, 0, 0))
    return pl.pallas_call(
        functools.partial(_gla_sample_kernel, seq=seq),
        grid=(nb // ns,),
        in_specs=[col, col, col, tok, st_in],
        out_specs=[tok, st],
        out_shape=[jax.ShapeDtypeStruct((nb, seq, V_W), F32),
                   jax.ShapeDtypeStruct((nb, HEADS, DK, DV), F32)],
        compiler_params=pltpu.CompilerParams(dimension_semantics=("parallel",),
                                             vmem_limit_bytes=VMEM_LIMIT),
        name="gla_sample",
    )(q3, k3, la3, v3, state)


_L_E0, _L_E1, _L_R0, _L_R1 = 0, 1, 2, 3
_GROUP_LANE0 = N_EXPERTS
HALF = D_MODEL // 2


def _mix_route_kernel(x_ref, co_ref, o_ref, sr_ref, gc_ref, gg_ref, gn_ref, wgo_ref, wo_ref,
                      g_ref, wr_ref, br_ref, cin_ref,
                      h_ref, xp_ref, mi_ref, mf_ref, cnt_ref, carry_ref):
    i = pl.program_id(0)
    tm = x_ref.shape[0]

    @pl.when(i == 0)
    def _():
        carry_ref[...] = cin_ref[0:1, :]

    gla_out = None
    for h in range(HEADS):
        vs = slice(h * DV, (h + 1) * DV)
        oh = o_ref[:, vs]
        on = oh * lax.rsqrt(jnp.mean(oh * oh, axis=-1, keepdims=True) + EPS) * gn_ref[:, vs]
        part = _dot((on * sr_ref[:, vs]).astype(BF16), wgo_ref[vs, :])
        gla_out = part if gla_out is None else gla_out + part
    mix = (gc_ref[...] * co_ref[...] + gg_ref[...] * gla_out).astype(BF16)
    x = x_ref[...] + _dot(mix, wo_ref[...])
    h_ref[...] = x

    hn = x * lax.rsqrt(jnp.mean(x * x, axis=-1, keepdims=True) + EPS) * g_ref[...]
    hi = hn.astype(BF16)
    hi_f = hi.astype(F32)
    top = lax.bitcast_convert_type(hi_f[:, :HALF], jnp.uint32)
    bot = lax.bitcast_convert_type(hi_f[:, HALF:], jnp.uint32)
    xp_ref[...] = top | jnp.right_shift(bot, jnp.uint32(16))
    lo = (hn - hi_f).astype(BF16)
    logits = _dot(hi, wr_ref[0]) + (_dot(lo, wr_ref[0]) + _dot(hi, wr_ref[1])) + br_ref[...]
    lane = lax.broadcasted_iota(jnp.int32, (tm, LANES), 1)
    lane_f = lane.astype(F32)
    group_f = jnp.right_shift(lane, 3).astype(F32)
    big = jnp.float32(LANES)
    neg = jnp.float32(-jnp.inf)

    is_g = (lane >= _GROUP_LANE0) & (lane < _GROUP_LANE0 + N_GROUPS)
    lg = jnp.where(is_g, logits, neg)
    mg = jnp.max(lg, axis=1, keepdims=True)
    p_sel = 1.0 / jnp.sum(jnp.exp(lg - mg), axis=1, keepdims=True)
    g_idx = jnp.min(jnp.where(lg == mg, lane_f, big), axis=1, keepdims=True) - _GROUP_LANE0

    is_e = (lane < N_EXPERTS) & (group_f == g_idx)
    le = jnp.where(is_e, logits, neg)
    v0 = jnp.max(le, axis=1, keepdims=True)
    i0 = jnp.min(jnp.where(le == v0, lane_f, big), axis=1, keepdims=True)
    le1 = jnp.where(lane_f == i0, neg, le)
    v1 = jnp.max(le1, axis=1, keepdims=True)
    i1 = jnp.min(jnp.where(le1 == v1, lane_f, big), axis=1, keepdims=True)
    e1 = jnp.exp(v1 - v0)
    den = 1.0 + e1
    w0 = p_sel * (1.0 / den)
    w1 = p_sel * (e1 / den)

    hit0 = lane_f == i0
    hit1 = lane_f == i1
    cnt = jnp.where(hit0 | hit1, 1.0, 0.0)
    r_i = lax.broadcasted_iota(jnp.int32, (tm, tm), 0)
    c_i = lax.broadcasted_iota(jnp.int32, (tm, tm), 1)
    before = jnp.where(r_i > c_i, 1.0, 0.0).astype(BF16)
    seen = _dot(before, cnt.astype(BF16)) + carry_ref[...]
    rank0 = jnp.sum(jnp.where(hit0, seen, 0.0), axis=1, keepdims=True)
    rank1 = jnp.sum(jnp.where(hit1, seen, 0.0), axis=1, keepdims=True)
    carry_ref[...] = carry_ref[...] + jnp.sum(cnt, axis=0, keepdims=True)
    cnt_ref[...] = jnp.broadcast_to(carry_ref[...], cnt_ref.shape)

    rec = jnp.where(lane == _L_E0, i0, 0.0)
    rec = jnp.where(lane == _L_E1, i1, rec)
    rec = jnp.where(lane == _L_R0, rank0, rec)
    rec = jnp.where(lane == _L_R1, rank1, rec)
    mi_ref[...] = rec.T[:SUBLANES, :]
    mf_ref[...] = jnp.where(lane == 0, w0, jnp.where(lane == 1, w1, 0.0))


def _mix_route(x2d, conv_out, o, sr, gc, gg, gn, w_gla_out, w_out, norm_g, w_route, b_route, counts_in):
    n = x2d.shape[0]
    tm = MIX_TILE
    row = lambda w: pl.BlockSpec((tm, w), lambda i: (i, 0))
    full = lambda a: pl.BlockSpec(a.shape, lambda i: (0,) * a.ndim)
    consts = (gn, w_gla_out, w_out, norm_g, w_route, b_route, counts_in)
    return pl.pallas_call(
        _mix_route_kernel,
        grid=(n // tm,),
        in_specs=[row(D_MODEL)] * 6 + [full(a) for a in consts],
        out_specs=[row(D_MODEL), row(HALF), pl.BlockSpec((SUBLANES, tm), lambda i: (0, i)), row(LANES),
                   pl.BlockSpec((SUBLANES, LANES), lambda i: (0, 0))],
        out_shape=[jax.ShapeDtypeStruct((n, D_MODEL), F32),
                   jax.ShapeDtypeStruct((n, HALF), jnp.uint32),
                   jax.ShapeDtypeStruct((SUBLANES, n), F32),
                   jax.ShapeDtypeStruct((n, LANES), F32),
                   jax.ShapeDtypeStruct((SUBLANES, LANES), F32)],
        scratch_shapes=[pltpu.VMEM((1, LANES), F32)],
        compiler_params=pltpu.CompilerParams(dimension_semantics=("arbitrary",),
                                             vmem_limit_bytes=VMEM_LIMIT),
        name="mix_route",
    )(x2d, conv_out, o, sr, gc, gg, *consts)


def _row_copy(src_ref, src_row, dst_ref, dst_row, sem):
    return pltpu.make_async_copy(src_ref.at[pl.ds(src_row, 1), :], dst_ref.at[pl.ds(dst_row, 1), :], sem)


def _dispatch_kernel(p0_ref, p1_ref, xa_ref, xb_ref, xs_ref, sem, *, tiles_a):
    tm = xa_ref.shape[0]
    i = pl.program_id(0)
    base = i * tm

    def scatter(src_ref):
        def rows(wait):
            def body(t, carry):
                c0 = _row_copy(src_ref, t, xs_ref, p0_ref[base + t], sem)
                c1 = _row_copy(src_ref, t, xs_ref, p1_ref[base + t], sem)
                if wait:
                    c0.wait()
                    c1.wait()
                else:
                    c0.start()
                    c1.start()
                return carry
            if wait:
                lax.fori_loop(0, tm, body, 0, unroll=DMA_UNROLL)
            else:
                for t in range(tm):
                    body(t, 0)

        rows(wait=False)
        rows(wait=True)

    @pl.when(i < tiles_a)
    def _():
        scatter(xa_ref)

    @pl.when(i >= tiles_a)
    def _():
        scatter(xb_ref)


def _dispatch(pos0, pos1, x_a, x_b):
    tm = TOK_TILE
    tiles_a, tiles_b = x_a.shape[0] // tm, x_b.shape[0] // tm
    n = x_a.shape[0] + x_b.shape[0]
    width = x_a.shape[1]
    return pl.pallas_call(
        functools.partial(_dispatch_kernel, tiles_a=tiles_a),
        grid_spec=pltpu.PrefetchScalarGridSpec(
            num_scalar_prefetch=2,
            grid=(tiles_a + tiles_b,),
            in_specs=[pl.BlockSpec((tm, width), lambda i, p0, p1: (jnp.minimum(i, tiles_a - 1), 0)),
                      pl.BlockSpec((tm, width), lambda i, p0, p1: (jnp.maximum(i - tiles_a, 0), 0))],
            out_specs=pl.BlockSpec(memory_space=pl.ANY),
            scratch_shapes=[pltpu.SemaphoreType.DMA(())],
        ),
        out_shape=jax.ShapeDtypeStruct((2 * n, width), x_a.dtype),
        compiler_params=pltpu.CompilerParams(dimension_semantics=("arbitrary",)),
        name="dispatch",
    )(pos0, pos1, x_a, x_b)


def _combine_kernel(p0_ref, p1_ref, h_ref, wf_ref, g_ref, ys_ref, out_ref, b0a, b1a, b0b, b1b, sems, *, tok_offset):
    tm = h_ref.shape[0]
    i = pl.program_id(0)
    bufs = ((b0a, b1a), (b0b, b1b))

    def copies(tile, slot, t):
        base = tok_offset + tile * tm
        return (_row_copy(ys_ref, p0_ref[base + t], bufs[slot][0], t, sems.at[slot]),
                _row_copy(ys_ref, p1_ref[base + t], bufs[slot][1], t, sems.at[slot]))

    def issue(tile, slot):
        for t in range(tm):
            for c in copies(tile, slot, t):
                c.start()

    def drain(tile, slot):
        def rows(t, carry):
            for c in copies(tile, slot, t):
                c.wait()
            return carry
        lax.fori_loop(0, tm, rows, 0, unroll=DMA_UNROLL)

    def finish(slot):
        b0, b1 = bufs[slot]
        y = h_ref[...] + (b0[...] * wf_ref[:, 0:1] + b1[...] * wf_ref[:, 1:2])
        out_ref[...] = y * lax.rsqrt(jnp.mean(y * y, axis=-1, keepdims=True) + EPS) * g_ref[...]

    @pl.when(i == 0)
    def _():
        issue(i, 0)

    for slot in range(2):
        @pl.when(lax.rem(i, 2) == slot)
        def _(slot=slot):
            drain(i, slot)

            @pl.when(i + 1 < pl.num_programs(0))
            def _():
                issue(i + 1, 1 - slot)
                finish(slot)

            @pl.when(i + 1 >= pl.num_programs(0))
            def _():
                finish(slot)


def _combine(pos0, pos1, h, wf, norm_g, ys, tok_offset):
    n_rows = h.shape[0]
    tm = TOK_TILE
    return pl.pallas_call(
        functools.partial(_combine_kernel, tok_offset=tok_offset),
        grid_spec=pltpu.PrefetchScalarGridSpec(
            num_scalar_prefetch=2,
            grid=(n_rows // tm,),
            in_specs=[pl.BlockSpec((tm, D_MODEL), lambda i, p0, p1: (i, 0)),
                      pl.BlockSpec((tm, LANES), lambda i, p0, p1: (i, 0)),
                      pl.BlockSpec((1, D_MODEL), lambda i, p0, p1: (0, 0)),
                      pl.BlockSpec(memory_space=pl.ANY)],
            out_specs=pl.BlockSpec((tm, D_MODEL), lambda i, p0, p1: (i, 0)),
            scratch_shapes=[pltpu.VMEM((tm, D_MODEL), F32)] * 4 + [pltpu.SemaphoreType.DMA((2,))],
        ),
        out_shape=jax.ShapeDtypeStruct((n_rows, D_MODEL), F32),
        compiler_params=pltpu.CompilerParams(dimension_semantics=("arbitrary",),
                                             vmem_limit_bytes=VMEM_LIMIT),
        name="combine",
    )(pos0, pos1, h, wf, norm_g, ys)


def _expert_kernel(blk_ref, exp_ref, lo_ref, hi_ref, first_ref, slot_ref, next_ref,
                   xs_ref, wg_hbm, wu_hbm, wd_hbm, ys_ref,
                   wg32_ref, wu32_ref, wd32_ref, wgb_ref, wub_ref, wdb_ref, sems):
    w = pl.program_id(0)
    e = exp_ref[w]
    slot = slot_ref[w]
    changed = jnp.logical_or(w == 0, e != exp_ref[jnp.maximum(w - 1, 0)])

    def weight_copies(expert, s):
        return [pltpu.make_async_copy(src.at[expert], dst.at[s], sems.at[s, j])
                for j, (src, dst) in enumerate(((wg_hbm, wg32_ref), (wu_hbm, wu32_ref), (wd_hbm, wd32_ref)))]

    @pl.when(w == 0)
    def _():
        for c in weight_copies(e, slot):
            c.start()

    @pl.when(changed)
    def _():
        for c in weight_copies(e, slot):
            c.wait()
        wgb_ref[...] = wg32_ref[slot].astype(BF16)
        wub_ref[...] = wu32_ref[slot].astype(BF16)
        wdb_ref[...] = wd32_ref[slot].astype(BF16)

        @pl.when(next_ref[w] >= 0)
        def _():
            for c in weight_copies(next_ref[w], 1 - slot):
                c.start()

    lo = lo_ref[w]
    hi = hi_ref[w]

    @pl.when(hi > lo)
    def _():
        packed = xs_ref[...]
        x_top = lax.bitcast_convert_type(packed & jnp.uint32(0xFFFF0000), F32)
        x_bot = lax.bitcast_convert_type(jnp.left_shift(packed, jnp.uint32(16)), F32)
        x = jnp.concatenate([x_top, x_bot], axis=1).astype(BF16)
        gate = _dot(x, wgb_ref[...])
        hid = (gate * _sigmoid(gate)) * _dot(x, wub_ref[...])
        y = _dot(hid.astype(BF16), wdb_ref[...])
        rows = blk_ref[w] * ROW_BLOCK + lax.broadcasted_iota(jnp.int32, (ROW_BLOCK, 1), 0)
        y = jnp.where((rows >= lo) & (rows < hi), y, 0.0)

        @pl.when(first_ref[w] == 1)
        def _():
            ys_ref[...] = y

        @pl.when(first_ref[w] == 0)
        def _():
            ys_ref[...] = ys_ref[...] + y


def _experts(items, xs, w_gate, w_up, w_down):
    n_rows = xs.shape[0]
    n_items = items[0].shape[0]
    hbm = pl.BlockSpec(memory_space=pl.ANY)
    return pl.pallas_call(
        _expert_kernel,
        grid_spec=pltpu.PrefetchScalarGridSpec(
            num_scalar_prefetch=len(items),
            grid=(n_items,),
            in_specs=[pl.BlockSpec((ROW_BLOCK, HALF), lambda w, blk, *_: (blk[w], 0)), hbm, hbm, hbm],
            out_specs=pl.BlockSpec((ROW_BLOCK, D_MODEL), lambda w, blk, *_: (blk[w], 0)),
            scratch_shapes=[pltpu.VMEM((2, D_MODEL, D_EXPERT), F32), pltpu.VMEM((2, D_MODEL, D_EXPERT), F32),
                            pltpu.VMEM((2, D_EXPERT, D_MODEL), F32),
                            pltpu.VMEM((D_MODEL, D_EXPERT), BF16), pltpu.VMEM((D_MODEL, D_EXPERT), BF16),
                            pltpu.VMEM((D_EXPERT, D_MODEL), BF16),
                            pltpu.SemaphoreType.DMA((2, 3))],
        ),
        out_shape=jax.ShapeDtypeStruct((n_rows, D_MODEL), F32),
        compiler_params=pltpu.CompilerParams(dimension_semantics=("arbitrary",),
                                             vmem_limit_bytes=VMEM_LIMIT),
        name="experts",
    )(*items, xs, w_gate, w_up, w_down)


def _work_items(counts, n_rows):
    n_blocks = n_rows // ROW_BLOCK
    n_items = n_blocks + N_EXPERTS - 1
    ends = jnp.cumsum(counts)
    starts = ends - counts
    blk_lo = starts // ROW_BLOCK
    blk_hi = jnp.maximum(ends - 1, 0) // ROW_BLOCK
    per_e = jnp.where(counts > 0, blk_hi - blk_lo + 1, 0)
    item_end = jnp.cumsum(per_e)
    item_start = item_end - per_e
    total = item_end[-1]
    w = jnp.arange(n_items, dtype=jnp.int32)
    live = w < total
    e = jnp.sum((jnp.minimum(w, total - 1)[:, None] >= item_end[None, :]).astype(jnp.int32), axis=1)
    e = jnp.minimum(e, N_EXPERTS - 1)
    sel = (e[:, None] == jnp.arange(N_EXPERTS, dtype=jnp.int32)[None, :]).astype(jnp.int32)
    pick = lambda table: jnp.sum(sel * table[None, :], axis=1)
    blk = jnp.where(live, pick(blk_lo) + (w - pick(item_start)), n_blocks - 1).astype(jnp.int32)
    lo = jnp.where(live, jnp.maximum(pick(starts), blk * ROW_BLOCK), 0).astype(jnp.int32)
    hi = jnp.where(live, jnp.minimum(pick(ends), (blk + 1) * ROW_BLOCK), 0).astype(jnp.int32)
    prev_blk = jnp.concatenate([jnp.full((1,), -1, jnp.int32), blk[:-1]])
    first = (live & (blk != prev_blk)).astype(jnp.int32)
    ids = jnp.arange(N_EXPERTS, dtype=jnp.int32)
    used = counts > 0
    rank_e = jnp.cumsum(used.astype(jnp.int32)) - 1
    later = used[None, :] & (ids[None, :] > ids[:, None])
    next_e = jnp.min(jnp.where(later, ids[None, :], N_EXPERTS), axis=1)
    next_e = jnp.where(next_e < N_EXPERTS, next_e, -1)
    slot = (pick(rank_e) % 2).astype(jnp.int32)
    nxt = pick(next_e).astype(jnp.int32)
    return (blk, e.astype(jnp.int32), lo, hi, first, slot, nxt), starts


def kernel(x_prompt, x_sample, state_conv, state_gla, norm_mix, w_in, b_gates, w_gla_gate_up, b_gla_gate_up, conv_dw, conv_db, conv_ln_g, conv_ln_b, w_conv_out, gla_norm_g, w_gla_out, w_out, norm_ffn, w_router_group, b_router_group, w_router_expert, b_router_expert, w_expert_gate, w_expert_up, w_expert_down, norm_final):
    depth = norm_mix.shape[0]
    assert depth == 1
    l = 0
    bp, seq_p, _ = x_prompt.shape
    bs, seq_s, _ = x_sample.shape
    n_p, n_s = bp * seq_p, bs * seq_s
    n_tok = n_p + n_s
    row2 = lambda a: a.reshape(1, -1)

    wi = w_in[l]
    w_main = wi[:, :_C_ALOW].astype(BF16)
    w_alow = jnp.pad(wi[:, _C_ALOW:_C_GATES], ((0, 0), (0, LANES - RANK))).astype(BF16)
    w_gates = wi[:, _C_GATES:].astype(BF16)
    w_up_pad = jnp.pad(w_gla_gate_up[l], ((0, LANES - RANK), (0, 0))).astype(BF16)
    w_route = jnp.pad(jnp.concatenate([w_router_expert[l], w_router_group[l]], axis=1),
                      ((0, 0), (0, LANES - N_EXPERTS - N_GROUPS)))
    w_route_hi = w_route.astype(BF16)
    w_route = jnp.stack([w_route_hi, (w_route - w_route_hi.astype(F32)).astype(BF16)])
    b_route = jnp.pad(jnp.concatenate([b_router_expert[l], b_router_group[l]]),
                      (0, LANES - N_EXPERTS - N_GROUPS)).reshape(1, LANES)
    w_pw = w_conv_out[l].astype(BF16)
    w_go = w_gla_out[l].astype(BF16)
    w_o = w_out[l].astype(BF16)
    conv_args = (conv_dw[l], row2(conv_db[l]), row2(conv_ln_g[l]), row2(conv_ln_b[l]), w_pw)
    proj_args = (row2(norm_mix[l]), w_main, w_alow, w_gates, w_up_pad, row2(b_gla_gate_up[l]),
                 row2(b_gates[l]))
    mix_args = (row2(gla_norm_g[l]), w_go, w_o, row2(norm_ffn[l]), w_route, b_route)

    xp = x_prompt.reshape(n_p, D_MODEL)
    conv_out, q, k, v, la, sr, gc, gg, conv_p = _proj_conv(xp, bp, seq_p, proj_args, conv_args)
    o, gla_p = _gla_prompt(q, k, la, v, bp, seq_p)
    h_p, xr_p, rec_p, wf_p, cnt = _mix_route(xp, conv_out, o, sr, gc, gg, *mix_args,
                                             jnp.zeros((SUBLANES, LANES), F32))

    xs_ = x_sample.reshape(n_s, D_MODEL)
    u, q, k, v, la, sr, gc, gg = _proj(xs_, *proj_args)
    conv_out, conv_s = _conv(u.reshape(bs, seq_s, D_CONV), state_conv, *conv_args, nseq=SAMPLE_SEQS)
    seqs = lambda a: a.reshape(bs, seq_s, a.shape[-1])
    o, gla_s = _gla_sample(seqs(q), seqs(k), seqs(la), seqs(v), state_gla)
    h_s, xr_s, rec_s, wf_s, cnt = _mix_route(xs_, conv_out.reshape(n_s, D_MODEL), o.reshape(n_s, V_W), sr, gc, gg,
                                             *mix_args, cnt)

    counts = cnt[0, :N_EXPERTS].astype(jnp.int32)
    items, starts = _work_items(counts, 2 * n_tok)
    rec = jnp.concatenate([rec_p, rec_s], axis=1).astype(jnp.int32)
    e_ids = jnp.arange(N_EXPERTS, dtype=jnp.int32)[:, None]
    start_of = lambda e: jnp.sum(jnp.where(e[None, :] == e_ids, starts[:, None], 0), axis=0)
    pos0 = start_of(rec[_L_E0]) + rec[_L_R0]
    pos1 = start_of(rec[_L_E1]) + rec[_L_R1]
    xs_sorted = _dispatch(pos0, pos1, xr_p, xr_s)
    ys = _experts(items, xs_sorted, w_expert_gate[l], w_expert_up[l], w_expert_down[l])
    y_p = _combine(pos0, pos1, h_p, wf_p, row2(norm_final), ys, 0)
    y_s = _combine(pos0, pos1, h_s, wf_s, row2(norm_final), ys, n_p)

    return (y_p.reshape(bp, seq_p, D_MODEL), y_s.reshape(bs, seq_s, D_MODEL),
            conv_p[None], gla_p[None], conv_s[None], gla_s[None])
```

```python
import functools

import jax
import jax.numpy as jnp
from jax import lax
from jax.experimental import pallas as pl
from jax.experimental.pallas import tpu as pltpu

F32 = jnp.float32
BF16 = jnp.bfloat16

D_MODEL = 1024
D_CONV = 512
CONV_WIDTH = 31
CONV_HIST = CONV_WIDTH - 1
HEADS = 4
DK = 128
DV = 256
QK_W = HEADS * DK
V_W = HEADS * DV
RANK = 16
GATE_NORM = 16.0
N_GROUPS = 4
EXPERTS_PER_GROUP = 8
N_EXPERTS = 32
D_EXPERT = 512
EPS = 1e-6

LANES = 128
SUBLANES = 8
VMEM_LIMIT = 56 * 1024 * 1024

TOK_TILE = 256
PROJ_TILE = 512
MIX_TILE = 512
GLA_CHUNK = 128
GLA_STEP = 512
SAFE_LOG_DECAY = -80.0
CONV_ROWS = 64
CONV_HALO = 40
ROW_BLOCK = 256
DMA_UNROLL = 8
SAMPLE_SEQS = 32
SAMPLE_GLA_SEQS = 8
SAMPLE_PAD = 16

_C_GLU_A, _C_GLU_B = 0, 512
_C_Q, _C_K, _C_V, _C_R = 1024, 1536, 2048, 3072
_C_ALOW = 4096
_C_GATES = _C_ALOW + RANK


def _sigmoid(x):
    return jax.nn.sigmoid(x)


def _dot(a, b):
    return jnp.dot(a, b, preferred_element_type=F32)


def _normed_input(x_ref, g_ref):
    x = x_ref[...]
    hn = x * lax.rsqrt(jnp.mean(x * x, axis=-1, keepdims=True) + EPS) * g_ref[...]
    return hn.astype(BF16)


def _glu(hb, wm_ref):
    return _dot(hb, wm_ref[:, _C_GLU_A:_C_GLU_B]) * _sigmoid(_dot(hb, wm_ref[:, _C_GLU_B:_C_Q]))


def _projection_steps(hb, wm_ref, wa_ref, wg_ref, wup_ref, bup_ref, bg_ref,
                      q_ref, k_ref, v_ref, la_ref, sr_ref, gc_ref, gg_ref):
    def mm(lo, hi):
        return _dot(hb, wm_ref[:, lo:hi])

    def q_step():
        q_ref[...] = mm(_C_Q, _C_K) * (DK ** -0.5)

    def k_step():
        k_ref[...] = mm(_C_K, _C_V)

    def v_step():
        v_ref[...] = mm(_C_V, _C_R).astype(BF16)

    def r_step():
        r = mm(_C_R, _C_ALOW)
        sr_ref[...] = (r * _sigmoid(r)).astype(BF16)

    def gc_step():
        gc_ref[...] = _sigmoid(_dot(hb, wg_ref[:, :D_MODEL]) + bg_ref[:, :D_MODEL]).astype(BF16)

    def gg_step():
        gg_ref[...] = _sigmoid(_dot(hb, wg_ref[:, D_MODEL:]) + bg_ref[:, D_MODEL:]).astype(BF16)

    def decay_step():
        a_low = _dot(hb, wa_ref[...])
        z = _dot(a_low.astype(BF16), wup_ref[...]) + bup_ref[...]
        la_ref[...] = (jnp.minimum(z, 0.0) - jnp.log1p(jnp.exp(-jnp.abs(z)))) * (1.0 / GATE_NORM)

    return [q_step, k_step, v_step, r_step, gc_step, gg_step, decay_step]


def _proj_kernel(x_ref, g_ref, wm_ref, wa_ref, wg_ref, wup_ref, bup_ref, bg_ref,
                 u_ref, q_ref, k_ref, v_ref, la_ref, sr_ref, gc_ref, gg_ref):
    hb = _normed_input(x_ref, g_ref)
    u_ref[...] = _glu(hb, wm_ref)
    for step in _projection_steps(hb, wm_ref, wa_ref, wg_ref, wup_ref, bup_ref, bg_ref,
                                  q_ref, k_ref, v_ref, la_ref, sr_ref, gc_ref, gg_ref):
        step()


_HIST_PAD = 32
_HIST0 = _HIST_PAD - CONV_HIST


def _conv_taps(w, dw_ref, cs, rt):
    win = w.shape[0]
    acc = jnp.zeros((rt, LANES), F32)
    for s in range(SUBLANES):
        ws = w if s == 0 else pltpu.roll(w, win - s, axis=0)
        for a in range(CONV_HALO // SUBLANES):
            j = a * SUBLANES + s - _HIST0
            if 0 <= j < CONV_WIDTH:
                acc = acc + ws[a * SUBLANES:a * SUBLANES + rt, :] * dw_ref[j:j + 1, cs]
    return acc


def _norm_swish_pointwise(y, lg_ref, lb_ref, wpw_ref):
    mu = jnp.mean(y, axis=-1, keepdims=True)
    var = jnp.mean(jnp.square(y - mu), axis=-1, keepdims=True)
    yn = (y - mu) * lax.rsqrt(var + EPS) * lg_ref[...] + lb_ref[...]
    return _dot((yn * _sigmoid(yn)).astype(BF16), wpw_ref[...])


def _proj_conv_kernel(x_ref, g_ref, wm_ref, wa_ref, wg_ref, wup_ref, bup_ref, bg_ref,
                      dw_ref, db_ref, lg_ref, lb_ref, wpw_ref,
                      co_ref, q_ref, k_ref, v_ref, la_ref, sr_ref, gc_ref, gg_ref, ns_ref,
                      full_ref, y_ref):
    c = pl.program_id(1)
    tm = x_ref.shape[0]

    @pl.when(c == 0)
    def _():
        full_ref[0:_HIST_PAD, :] = jnp.zeros((_HIST_PAD, D_CONV), F32)

    @pl.when(c > 0)
    def _():
        full_ref[0:_HIST_PAD, :] = full_ref[tm:tm + _HIST_PAD, :]

    hb = _normed_input(x_ref, g_ref)
    full_ref[_HIST_PAD:_HIST_PAD + tm, :] = _glu(hb, wm_ref)
    full_ref[_HIST_PAD + tm:, :] = jnp.zeros((SUBLANES, D_CONV), F32)
    ns_ref[0] = full_ref[tm + _HIST0:tm + _HIST_PAD, :]
    rt = CONV_ROWS

    def conv_tile(i, cc):
        def run():
            cs = slice(cc * LANES, (cc + 1) * LANES)
            acc = _conv_taps(full_ref[i * rt:i * rt + rt + CONV_HALO, cs], dw_ref, cs, rt)
            y_ref[i * rt:(i + 1) * rt, cs] = acc + db_ref[:, cs]
        return run

    conv_tiles = [conv_tile(i, cc) for i in range(tm // rt) for cc in range(D_CONV // LANES)]
    proj_steps = _projection_steps(hb, wm_ref, wa_ref, wg_ref, wup_ref, bup_ref, bg_ref,
                                   q_ref, k_ref, v_ref, la_ref, sr_ref, gc_ref, gg_ref)
    per_step = -(-len(conv_tiles) // len(proj_steps))
    for n, step in enumerate(proj_steps):
        step()
        for tile in conv_tiles[n * per_step:(n + 1) * per_step]:
            tile()
    co_ref[...] = _norm_swish_pointwise(y_ref[...], lg_ref, lb_ref, wpw_ref)


def _proj_conv(x2d, nb, seq, proj_consts, conv_consts):
    tm = PROJ_TILE
    nc = seq // tm
    row = lambda w: pl.BlockSpec((tm, w), lambda b, c: (b * nc + c, 0))
    full = lambda a: pl.BlockSpec(a.shape, lambda b, c: (0,) * a.ndim)
    widths = (D_MODEL, QK_W, QK_W, V_W, QK_W, V_W, D_MODEL, D_MODEL)
    dtypes = (F32, F32, F32, BF16, F32, BF16, BF16, BF16)
    consts = tuple(proj_consts) + tuple(conv_consts)
    n = nb * seq
    return pl.pallas_call(
        _proj_conv_kernel,
        grid=(nb, nc),
        in_specs=[row(D_MODEL)] + [full(a) for a in consts],
        out_specs=[row(w) for w in widths] + [pl.BlockSpec((1, CONV_HIST, D_CONV), lambda b, c: (b, 0, 0))],
        out_shape=[jax.ShapeDtypeStruct((n, w), dt) for w, dt in zip(widths, dtypes)]
        + [jax.ShapeDtypeStruct((nb, CONV_HIST, D_CONV), F32)],
        scratch_shapes=[pltpu.VMEM((_HIST_PAD + tm + SUBLANES, D_CONV), F32), pltpu.VMEM((tm, D_CONV), F32)],
        compiler_params=pltpu.CompilerParams(dimension_semantics=("parallel", "arbitrary"),
                                             vmem_limit_bytes=VMEM_LIMIT),
        name="proj_conv",
    )(x2d, *consts)


def _proj(x2d, norm_g, w_main, w_alow, w_gates, w_up, b_up, b_gates):
    n = x2d.shape[0]
    tm = TOK_TILE
    row = lambda w: pl.BlockSpec((tm, w), lambda i: (i, 0))
    full = lambda a: pl.BlockSpec(a.shape, lambda i: (0,) * a.ndim)
    widths = (D_CONV, QK_W, QK_W, V_W, QK_W, V_W, D_MODEL, D_MODEL)
    dtypes = (F32, F32, F32, BF16, F32, BF16, BF16, BF16)
    consts = (norm_g, w_main, w_alow, w_gates, w_up, b_up, b_gates)
    return pl.pallas_call(
        _proj_kernel,
        grid=(n // tm,),
        in_specs=[row(D_MODEL)] + [full(a) for a in consts],
        out_specs=[row(w) for w in widths],
        out_shape=[jax.ShapeDtypeStruct((n, w), dt) for w, dt in zip(widths, dtypes)],
        compiler_params=pltpu.CompilerParams(dimension_semantics=("parallel",),
                                             vmem_limit_bytes=VMEM_LIMIT),
        name="proj",
    )(x2d, *consts)


def _conv_kernel(u_ref, st_ref, dw_ref, db_ref, lg_ref, lb_ref, wpw_ref, out_ref, ns_ref, full_ref, y_ref, *, seq):
    nseq = u_ref.shape[0]
    n_rows = y_ref.shape[0] // nseq
    tail = full_ref.shape[0] - (_HIST_PAD + seq)

    def one_seq(b, carry):
        full_ref[0:_HIST_PAD, :] = jnp.zeros((_HIST_PAD, D_CONV), F32)
        full_ref[_HIST_PAD + seq:, :] = jnp.zeros((tail, D_CONV), F32)
        full_ref[_HIST0:_HIST_PAD, :] = st_ref[b]
        full_ref[_HIST_PAD:_HIST_PAD + seq, :] = u_ref[b]
        ns_ref[b] = full_ref[seq + _HIST0:seq + _HIST_PAD, :]
        for c in range(D_CONV // LANES):
            cs = slice(c * LANES, (c + 1) * LANES)
            acc = _conv_taps(full_ref[:, cs], dw_ref, cs, n_rows)
            y_ref[pl.ds(pl.multiple_of(b * n_rows, SUBLANES), n_rows), cs] = acc + db_ref[:, cs]
        return carry

    lax.fori_loop(0, nseq, one_seq, 0)
    res = _norm_swish_pointwise(y_ref[...], lg_ref, lb_ref, wpw_ref)
    for j in range(nseq):
        out_ref[j] = res[j * n_rows:j * n_rows + seq]


def _conv(u3, state, dw, db, lg, lb, w_pw, nseq):
    nb, seq, _ = u3.shape
    n_rows = -(-seq // SUBLANES) * SUBLANES
    full = lambda a: pl.BlockSpec(a.shape, lambda b: (0,) * a.ndim)
    per_b = lambda r, w: pl.BlockSpec((nseq, r, w), lambda b: (b, 0, 0))
    return pl.pallas_call(
        functools.partial(_conv_kernel, seq=seq),
        grid=(nb // nseq,),
        in_specs=[per_b(seq, D_CONV), pl.BlockSpec((None, nseq, CONV_HIST, D_CONV), lambda b: (0, b, 0, 0)),
                  full(dw), full(db), full(lg), full(lb), full(w_pw)],
        out_specs=[per_b(seq, D_MODEL), per_b(CONV_HIST, D_CONV)],
        out_shape=[jax.ShapeDtypeStruct((nb, seq, D_MODEL), F32),
                   jax.ShapeDtypeStruct((nb, CONV_HIST, D_CONV), F32)],
        scratch_shapes=[pltpu.VMEM((n_rows + CONV_HALO, D_CONV), F32),
                        pltpu.VMEM((nseq * n_rows, D_CONV), F32)],
        compiler_params=pltpu.CompilerParams(dimension_semantics=("parallel",),
                                             vmem_limit_bytes=VMEM_LIMIT),
        name="conv_state",
    )(u3, state, dw, db, lg, lb, w_pw)


def _token_step(s, a_col, k_col, q_col, v_row):
    s = s * a_col + k_col * v_row
    return s, jnp.sum(q_col * s, axis=0, keepdims=True)


def _gla_prompt_kernel(q_ref, k_ref, la_ref, v_ref, o_ref, sout_ref, s_ref, vf_ref):
    c = pl.program_id(1)
    n = GLA_CHUNK
    chunks = [slice(i * n, (i + 1) * n) for i in range(q_ref.shape[0] // n)]

    @pl.when(c == 0)
    def _():
        s_ref[...] = jnp.zeros_like(s_ref)

    r_i = lax.broadcasted_iota(jnp.int32, (n, n), 0)
    c_i = lax.broadcasted_iota(jnp.int32, (n, n), 1)
    tri = jnp.where(r_i >= c_i, 1.0, 0.0).astype(BF16)

    def prefix_sum(g):
        hi = g.astype(BF16)
        lo = (g - hi.astype(F32)).astype(BF16)
        return _dot(tri, hi) + _dot(tri, lo)

    bs = [prefix_sum(la_ref[rows, :]) for rows in chunks]
    total = bs[0][n - 1:n, :]
    for b in bs[1:]:
        total = jnp.minimum(total, b[n - 1:n, :])
    fast = jnp.min(total) > SAFE_LOG_DECAY

    @pl.when(fast)
    def _():
        for h in range(HEADS):
            ks = slice(h * DK, (h + 1) * DK)
            vs = slice(h * DV, (h + 1) * DV)
            s = s_ref[h]
            for rows, b in zip(chunks, bs):
                bh = b[:, ks]
                bl = bh[n - 1:n, :]
                kh = k_ref[rows, ks]
                vh = v_ref[rows, vs]
                qp = (q_ref[rows, ks] * jnp.exp(bh)).astype(BF16)
                kp = (kh * jnp.exp(-bh)).astype(BF16)
                kpp = kh * jnp.exp(bl - bh)
                att = lax.dot_general(qp, kp, (((1,), (1,)), ((), ())), preferred_element_type=F32)
                att = jnp.where(r_i >= c_i, att, 0.0).astype(BF16)
                o_ref[rows, vs] = _dot(qp, s.astype(BF16)) + _dot(att, vh)
                d_col = jnp.sum(jnp.where(r_i == c_i, jnp.exp(bl), 0.0), axis=1, keepdims=True)
                s = s * d_col + _dot(kpp.T.astype(BF16), vh)
            s_ref[h] = s

    @pl.when(jnp.logical_not(fast))
    def _():
        lane = lax.broadcasted_iota(jnp.int32, (1, n), 1)
        for rows in chunks:
            vf_ref[...] = v_ref[rows, :].astype(F32)
            for h in range(HEADS):
                ks = slice(h * DK, (h + 1) * DK)
                vs = slice(h * DV, (h + 1) * DV)
                q_t = q_ref[rows, ks].T
                k_t = k_ref[rows, ks].T
                a_t = jnp.exp(la_ref[rows, ks]).T

                def body(t, s):
                    m = lane == t
                    col = lambda x: jnp.sum(jnp.where(m, x, 0.0), axis=1, keepdims=True)
                    s, o_row = _token_step(s, col(a_t), col(k_t), col(q_t), vf_ref[pl.ds(t, 1), vs])
                    o_ref[pl.ds(rows.start + t, 1), vs] = o_row
                    return s

                s_ref[h] = lax.fori_loop(0, n, body, s_ref[h])

    @pl.when(c == pl.num_programs(1) - 1)
    def _():
        sout_ref[0] = s_ref[...]


def _gla_prompt(q, k, la, v, nb, seq):
    nc = seq // GLA_STEP
    tok = lambda w: pl.BlockSpec((GLA_STEP, w), lambda b, c: (b * nc + c, 0))
    return pl.pallas_call(
        _gla_prompt_kernel,
        grid=(nb, nc),
        in_specs=[tok(QK_W), tok(QK_W), tok(QK_W), tok(V_W)],
        out_specs=[tok(V_W), pl.BlockSpec((1, HEADS, DK, DV), lambda b, c: (b, 0, 0, 0))],
        out_shape=[jax.ShapeDtypeStruct((nb * seq, V_W), F32),
                   jax.ShapeDtypeStruct((nb, HEADS, DK, DV), F32)],
        scratch_shapes=[pltpu.VMEM((HEADS, DK, DV), F32), pltpu.VMEM((GLA_CHUNK, V_W), F32)],
        compiler_params=pltpu.CompilerParams(dimension_semantics=("parallel", "arbitrary"),
                                             vmem_limit_bytes=VMEM_LIMIT),
        name="gla_prompt",
    )(q, k, la, v)


def _gla_sample_kernel(q_ref, k_ref, la_ref, v_ref, s_ref, o_ref, sout_ref, *, seq):
    p = SAMPLE_PAD
    row = lax.broadcasted_iota(jnp.int32, (p, 1), 0)

    def padded(x):
        return jnp.concatenate([x, jnp.zeros((p - seq, x.shape[1]), F32)], axis=0)

    for b in range(q_ref.shape[0]):
        q, k, v, g = padded(q_ref[b]), padded(k_ref[b]), padded(v_ref[b].astype(F32)), padded(la_ref[b])
        bc = g
        sh = 1
        while sh < seq:
            bc = bc + jnp.where(row >= sh, pltpu.roll(bc, sh, axis=0), 0.0)
            sh *= 2
        b_last = bc[seq - 1:seq, :]
        bc = jnp.where(row < seq, bc, b_last)
        qp = (q * jnp.exp(bc)).astype(BF16)
        kpp_t = (k * jnp.exp(b_last - bc)).T.astype(BF16)
        d_cols = jnp.broadcast_to(jnp.exp(b_last), (SUBLANES, QK_W)).T
        vb = v.astype(BF16)

        o_in = [jnp.zeros((p, DV), F32) for _ in range(HEADS)]
        for d in range(seq):
            k_d = k if d == 0 else pltpu.roll(k, d, axis=0)
            v_d = v if d == 0 else pltpu.roll(v, d, axis=0)
            b_d = bc if d == 0 else pltpu.roll(bc, d, axis=0)
            pair = q * k_d * jnp.exp(jnp.where(row >= d, bc - b_d, -jnp.inf))
            for h in range(HEADS):
                att = jnp.sum(pair[:, h * DK:(h + 1) * DK], axis=1, keepdims=True)
                o_in[h] = o_in[h] + att * v_d[:, h * DV:(h + 1) * DV]

        for h in range(HEADS):
            ks = slice(h * DK, (h + 1) * DK)
            vs = slice(h * DV, (h + 1) * DV)
            s = s_ref[b, h]
            o = _dot(qp[:, ks], s.astype(BF16)) + o_in[h]
            o_ref[b, :, vs] = o[:seq]
            sout_ref[b, h] = s * d_cols[ks, 0:1] + _dot(kpp_t[ks, :], vb[:, vs])


def _gla_sample(q3, k3, la3, v3, state):
    nb, seq, _ = q3.shape
    assert seq <= SUBLANES
    ns = SAMPLE_GLA_SEQS
    col = pl.BlockSpec((ns, seq, QK_W), lambda b: (b, 0, 0))
    tok = pl.BlockSpec((ns, seq, V_W), lambda b: (b, 0, 0))
    st = pl.BlockSpec((ns, HEADS, DK, DV), lambda b: (b, 0, 0, 0))
    st_in = pl.BlockSpec((None, ns, HEADS, DK, DV), lambda b: (0, b, 0, 0, 0))
    return pl.pallas_call(
        functools.partial(_gla_sample_kernel, seq=seq),
        grid=(nb // ns,),
        in_specs=[col, col, col, tok, st_in],
        out_specs=[tok, st],
        out_shape=[jax.ShapeDtypeStruct((nb, seq, V_W), F32),
                   jax.ShapeDtypeStruct((nb, HEADS, DK, DV), F32)],
        compiler_params=pltpu.CompilerParams(dimension_semantics=("parallel",),
                                             vmem_limit_bytes=VMEM_LIMIT),
        name="gla_sample",
    )(q3, k3, la3, v3, state)


_L_E0, _L_E1, _L_R0, _L_R1 = 0, 1, 2, 3
_GROUP_LANE0 = N_EXPERTS
HALF = D_MODEL // 2


def _mix_route_kernel(x_ref, co_ref, o_ref, sr_ref, gc_ref, gg_ref, gn_ref, wgo_ref, wo_ref,
                      g_ref, wr_ref, br_ref, cin_ref,
                      h_ref, xp_ref, mi_ref, mf_ref, cnt_ref, carry_ref):
    i = pl.program_id(0)
    tm = x_ref.shape[0]

    @pl.when(i == 0)
    def _():
        carry_ref[...] = cin_ref[0:1, :]

    gla_out = None
    for h in range(HEADS):
        vs = slice(h * DV, (h + 1) * DV)
        oh = o_ref[:, vs]
        on = oh * lax.rsqrt(jnp.mean(oh * oh, axis=-1, keepdims=True) + EPS) * gn_ref[:, vs]
        part = _dot((on * sr_ref[:, vs]).astype(BF16), wgo_ref[vs, :])
        gla_out = part if gla_out is None else gla_out + part
    mix = (gc_ref[...] * co_ref[...] + gg_ref[...] * gla_out).astype(BF16)
    x = x_ref[...] + _dot(mix, wo_ref[...])
    h_ref[...] = x

    hn = x * lax.rsqrt(jnp.mean(x * x, axis=-1, keepdims=True) + EPS) * g_ref[...]
    hi = hn.astype(BF16)
    hi_f = hi.astype(F32)
    top = lax.bitcast_convert_type(hi_f[:, :HALF], jnp.uint32)
    bot = lax.bitcast_convert_type(hi_f[:, HALF:], jnp.uint32)
    xp_ref[...] = top | jnp.right_shift(bot, jnp.uint32(16))
    lo = (hn - hi_f).astype(BF16)
    logits = _dot(hi, wr_ref[0]) + (_dot(lo, wr_ref[0]) + _dot(hi, wr_ref[1])) + br_ref[...]
    lane = lax.broadcasted_iota(jnp.int32, (tm, LANES), 1)
    lane_f = lane.astype(F32)
    group_f = jnp.right_shift(lane, 3).astype(F32)
    big = jnp.float32(LANES)
    neg = jnp.float32(-jnp.inf)

    is_g = (lane >= _GROUP_LANE0) & (lane < _GROUP_LANE0 + N_GROUPS)
    lg = jnp.where(is_g, logits, neg)
    mg = jnp.max(lg, axis=1, keepdims=True)
    p_sel = 1.0 / jnp.sum(jnp.exp(lg - mg), axis=1, keepdims=True)
    g_idx = jnp.min(jnp.where(lg == mg, lane_f, big), axis=1, keepdims=True) - _GROUP_LANE0

    is_e = (lane < N_EXPERTS) & (group_f == g_idx)
    le = jnp.where(is_e, logits, neg)
    v0 = jnp.max(le, axis=1, keepdims=True)
    i0 = jnp.min(jnp.where(le == v0, lane_f, big), axis=1, keepdims=True)
    le1 = jnp.where(lane_f == i0, neg, le)
    v1 = jnp.max(le1, axis=1, keepdims=True)
    i1 = jnp.min(jnp.where(le1 == v1, lane_f, big), axis=1, keepdims=True)
    e1 = jnp.exp(v1 - v0)
    den = 1.0 + e1
    w0 = p_sel * (1.0 / den)
    w1 = p_sel * (e1 / den)

    hit0 = lane_f == i0
    hit1 = lane_f == i1
    cnt = jnp.where(hit0 | hit1, 1.0, 0.0)
    r_i = lax.broadcasted_iota(jnp.int32, (tm, tm), 0)
    c_i = lax.broadcasted_iota(jnp.int32, (tm, tm), 1)
    before = jnp.where(r_i > c_i, 1.0, 0.0).astype(BF16)
    seen = _dot(before, cnt.astype(BF16)) + carry_ref[...]
    rank0 = jnp.sum(jnp.where(hit0, seen, 0.0), axis=1, keepdims=True)
    rank1 = jnp.sum(jnp.where(hit1, seen, 0.0), axis=1, keepdims=True)
    carry_ref[...] = carry_ref[...] + jnp.sum(cnt, axis=0, keepdims=True)
    cnt_ref[...] = jnp.broadcast_to(carry_ref[...], cnt_ref.shape)

    rec = jnp.where(lane == _L_E0, i0, 0.0)
    rec = jnp.where(lane == _L_E1, i1, rec)
    rec = jnp.where(lane == _L_R0, rank0, rec)
    rec = jnp.where(lane == _L_R1, rank1, rec)
    mi_ref[...] = rec.T[:SUBLANES, :]
    mf_ref[...] = jnp.where(lane == 0, w0, jnp.where(lane == 1, w1, 0.0))


def _mix_route(x2d, conv_out, o, sr, gc, gg, gn, w_gla_out, w_out, norm_g, w_route, b_route, counts_in):
    n = x2d.shape[0]
    tm = MIX_TILE
    row = lambda w: pl.BlockSpec((tm, w), lambda i: (i, 0))
    full = lambda a: pl.BlockSpec(a.shape, lambda i: (0,) * a.ndim)
    consts = (gn, w_gla_out, w_out, norm_g, w_route, b_route, counts_in)
    return pl.pallas_call(
        _mix_route_kernel,
        grid=(n // tm,),
        in_specs=[row(D_MODEL)] * 6 + [full(a) for a in consts],
        out_specs=[row(D_MODEL), row(HALF), pl.BlockSpec((SUBLANES, tm), lambda i: (0, i)), row(LANES),
                   pl.BlockSpec((SUBLANES, LANES), lambda i: (0, 0))],
        out_shape=[jax.ShapeDtypeStruct((n, D_MODEL), F32),
                   jax.ShapeDtypeStruct((n, HALF), jnp.uint32),
                   jax.ShapeDtypeStruct((SUBLANES, n), F32),
                   jax.ShapeDtypeStruct((n, LANES), F32),
                   jax.ShapeDtypeStruct((SUBLANES, LANES), F32)],
        scratch_shapes=[pltpu.VMEM((1, LANES), F32)],
        compiler_params=pltpu.CompilerParams(dimension_semantics=("arbitrary",),
                                             vmem_limit_bytes=VMEM_LIMIT),
        name="mix_route",
    )(x2d, conv_out, o, sr, gc, gg, *consts)


def _row_copy(src_ref, src_row, dst_ref, dst_row, sem):
    return pltpu.make_async_copy(src_ref.at[pl.ds(src_row, 1), :], dst_ref.at[pl.ds(dst_row, 1), :], sem)


def _dispatch_kernel(p0_ref, p1_ref, xa_ref, xb_ref, xs_ref, sem, *, tiles_a):
    tm = xa_ref.shape[0]
    i = pl.program_id(0)
    base = i * tm

    def scatter(src_ref):
        def rows(wait):
            def body(t, carry):
                c0 = _row_copy(src_ref, t, xs_ref, p0_ref[base + t], sem)
                c1 = _row_copy(src_ref, t, xs_ref, p1_ref[base + t], sem)
                if wait:
                    c0.wait()
                    c1.wait()
                else:
                    c0.start()
                    c1.start()
                return carry
            lax.fori_loop(0, tm, body, 0, unroll=DMA_UNROLL)

        rows(wait=False)
        rows(wait=True)

    @pl.when(i < tiles_a)
    def _():
        scatter(xa_ref)

    @pl.when(i >= tiles_a)
    def _():
        scatter(xb_ref)


def _dispatch(pos0, pos1, x_a, x_b):
    tm = TOK_TILE
    tiles_a, tiles_b = x_a.shape[0] // tm, x_b.shape[0] // tm
    n = x_a.shape[0] + x_b.shape[0]
    width = x_a.shape[1]
    return pl.pallas_call(
        functools.partial(_dispatch_kernel, tiles_a=tiles_a),
        grid_spec=pltpu.PrefetchScalarGridSpec(
            num_scalar_prefetch=2,
            grid=(tiles_a + tiles_b,),
            in_specs=[pl.BlockSpec((tm, width), lambda i, p0, p1: (jnp.minimum(i, tiles_a - 1), 0)),
                      pl.BlockSpec((tm, width), lambda i, p0, p1: (jnp.maximum(i - tiles_a, 0), 0))],
            out_specs=pl.BlockSpec(memory_space=pl.ANY),
            scratch_shapes=[pltpu.SemaphoreType.DMA(())],
        ),
        out_shape=jax.ShapeDtypeStruct((2 * n, width), x_a.dtype),
        compiler_params=pltpu.CompilerParams(dimension_semantics=("arbitrary",)),
        name="dispatch",
    )(pos0, pos1, x_a, x_b)


def _combine_kernel(h_ref, wf_ref, g_ref, y0_ref, y1_ref, out_ref):
    y = h_ref[...] + (y0_ref[...] * wf_ref[:, 0:1] + y1_ref[...] * wf_ref[:, 1:2])
    out_ref[...] = y * lax.rsqrt(jnp.mean(y * y, axis=-1, keepdims=True) + EPS) * g_ref[...]


def _combine(h, wf, norm_g, y2, tok_offset):
    n_rows = h.shape[0]
    tm = TOK_TILE
    off = tok_offset // tm
    row = lambda w: pl.BlockSpec((tm, w), lambda i: (i, 0))
    pick = lambda s: pl.BlockSpec((None, tm, D_MODEL), lambda i: (s, i + off, 0))
    return pl.pallas_call(
        _combine_kernel,
        grid=(n_rows // tm,),
        in_specs=[row(D_MODEL), row(LANES), pl.BlockSpec((1, D_MODEL), lambda i: (0, 0)), pick(0), pick(1)],
        out_specs=row(D_MODEL),
        out_shape=jax.ShapeDtypeStruct((n_rows, D_MODEL), F32),
        compiler_params=pltpu.CompilerParams(dimension_semantics=("parallel",),
                                             vmem_limit_bytes=VMEM_LIMIT),
        name="combine",
    )(h, wf, norm_g, y2, y2)


def _expert_kernel(blk_ref, exp_ref, lo_ref, hi_ref, slot_ref, next_ref, dst_ref,
                   xs_ref, wg_hbm, wu_hbm, wd_hbm, y2_ref,
                   wg32_ref, wu32_ref, wd32_ref, wgb_ref, wub_ref, wdb_ref, ya_ref, yb_ref, sems, ysems,
                   *, trash_row):
    w = pl.program_id(0)
    n_steps = pl.num_programs(0)
    e = exp_ref[w]
    slot = slot_ref[w]
    changed = jnp.logical_or(w == 0, e != exp_ref[jnp.maximum(w - 1, 0)])
    ybufs = (ya_ref, yb_ref)

    def rows_wait(p):
        pltpu.make_async_copy(ybufs[p], y2_ref.at[pl.ds(0, ROW_BLOCK), :], ysems.at[p]).wait()

    def scatter(k, p, live):
        kk = jnp.maximum(k, 0)
        base = blk_ref[kk] * ROW_BLOCK
        lo_k = jnp.where(live, lo_ref[kk], 0)
        hi_k = jnp.where(live, hi_ref[kk], 0)
        for r in range(ROW_BLOCK):
            g = base + r
            valid = jnp.logical_and(g >= lo_k, g < hi_k)
            dst = jnp.where(valid, dst_ref[g], trash_row + p * ROW_BLOCK + r)
            _row_copy(ybufs[p], r, y2_ref, dst, ysems.at[p]).start()

    @pl.when(w == 0)
    def _():
        yb_ref[...] = jnp.zeros_like(yb_ref)
        plane = trash_row + 2 * ROW_BLOCK
        pads = [pltpu.make_async_copy(yb_ref, y2_ref.at[pl.ds(start, ROW_BLOCK), :], ysems.at[0])
                for start in (trash_row, plane + trash_row, plane + trash_row + ROW_BLOCK)]
        for c in pads:
            c.start()
        for c in pads:
            c.wait()

    @pl.when(w >= 1)
    def _():
        for p in range(2):
            @pl.when(lax.rem(w, 2) == p)
            def _(p=p):
                rows_wait(p)

    def weight_copies(expert, s):
        return [pltpu.make_async_copy(src.at[expert], dst.at[s], sems.at[s, j])
                for j, (src, dst) in enumerate(((wg_hbm, wg32_ref), (wu_hbm, wu32_ref), (wd_hbm, wd32_ref)))]

    @pl.when(w == 0)
    def _():
        for c in weight_copies(e, slot):
            c.start()

    @pl.when(changed)
    def _():
        for c in weight_copies(e, slot):
            c.wait()
        wgb_ref[...] = wg32_ref[slot].astype(BF16)
        wub_ref[...] = wu32_ref[slot].astype(BF16)
        wdb_ref[...] = wd32_ref[slot].astype(BF16)

        @pl.when(next_ref[w] >= 0)
        def _():
            for c in weight_copies(next_ref[w], 1 - slot):
                c.start()

    def compute(y_ref):
        packed = xs_ref[...]
        x_top = lax.bitcast_convert_type(packed & jnp.uint32(0xFFFF0000), F32)
        x_bot = lax.bitcast_convert_type(jnp.left_shift(packed, jnp.uint32(16)), F32)
        x = jnp.concatenate([x_top, x_bot], axis=1).astype(BF16)
        gate = _dot(x, wgb_ref[...])
        hid = (gate * _sigmoid(gate)) * _dot(x, wub_ref[...])
        y_ref[...] = _dot(hid.astype(BF16), wdb_ref[...])

    for p in range(2):
        @pl.when(lax.rem(w, 2) == p)
        def _(p=p):
            compute(ybufs[p])
            scatter(w - 1, 1 - p, w >= 1)

            @pl.when(w == n_steps - 1)
            def _():
                scatter(w, p, True)
                rows_wait(1 - p)
                rows_wait(p)


def _experts(items, row_dst, xs, w_gate, w_up, w_down, n_tok_pad):
    n_items = items[0].shape[0]
    hbm = pl.BlockSpec(memory_space=pl.ANY)
    return pl.pallas_call(
        functools.partial(_expert_kernel, trash_row=n_tok_pad - 2 * ROW_BLOCK),
        grid_spec=pltpu.PrefetchScalarGridSpec(
            num_scalar_prefetch=len(items) + 1,
            grid=(n_items,),
            in_specs=[pl.BlockSpec((ROW_BLOCK, HALF), lambda w, blk, *_: (blk[w], 0)), hbm, hbm, hbm],
            out_specs=hbm,
            scratch_shapes=[pltpu.VMEM((2, D_MODEL, D_EXPERT), F32), pltpu.VMEM((2, D_MODEL, D_EXPERT), F32),
                            pltpu.VMEM((2, D_EXPERT, D_MODEL), F32),
                            pltpu.VMEM((D_MODEL, D_EXPERT), BF16), pltpu.VMEM((D_MODEL, D_EXPERT), BF16),
                            pltpu.VMEM((D_EXPERT, D_MODEL), BF16),
                            pltpu.VMEM((ROW_BLOCK, D_MODEL), F32), pltpu.VMEM((ROW_BLOCK, D_MODEL), F32),
                            pltpu.SemaphoreType.DMA((2, 3)), pltpu.SemaphoreType.DMA((2,))],
        ),
        out_shape=jax.ShapeDtypeStruct((2 * n_tok_pad, D_MODEL), F32),
        compiler_params=pltpu.CompilerParams(dimension_semantics=("arbitrary",),
                                             vmem_limit_bytes=VMEM_LIMIT),
        name="experts",
    )(*items, row_dst, xs, w_gate, w_up, w_down)


def _invert_kernel(p0_ref, p1_ref, out_ref, *, n_tok, plane):
    def body(t, carry):
        out_ref[p0_ref[t]] = t
        out_ref[p1_ref[t]] = plane + t
        return carry

    lax.fori_loop(0, n_tok, body, 0, unroll=DMA_UNROLL)


def _invert(pos0, pos1, plane):
    n_tok = pos0.shape[0]
    return pl.pallas_call(
        functools.partial(_invert_kernel, n_tok=n_tok, plane=plane),
        grid_spec=pltpu.PrefetchScalarGridSpec(
            num_scalar_prefetch=2,
            grid=(1,),
            in_specs=[],
            out_specs=pl.BlockSpec(memory_space=pltpu.SMEM),
        ),
        out_shape=jax.ShapeDtypeStruct((2 * n_tok,), jnp.int32),
        name="invert",
    )(pos0, pos1)


def _work_items(counts, n_rows):
    n_blocks = n_rows // ROW_BLOCK
    n_items = n_blocks + N_EXPERTS - 1
    ends = jnp.cumsum(counts)
    starts = ends - counts
    blk_lo = starts // ROW_BLOCK
    blk_hi = jnp.maximum(ends - 1, 0) // ROW_BLOCK
    per_e = jnp.where(counts > 0, blk_hi - blk_lo + 1, 0)
    item_end = jnp.cumsum(per_e)
    item_start = item_end - per_e
    total = item_end[-1]
    w = jnp.arange(n_items, dtype=jnp.int32)
    live = w < total
    e = jnp.sum((jnp.minimum(w, total - 1)[:, None] >= item_end[None, :]).astype(jnp.int32), axis=1)
    e = jnp.minimum(e, N_EXPERTS - 1)
    sel = (e[:, None] == jnp.arange(N_EXPERTS, dtype=jnp.int32)[None, :]).astype(jnp.int32)
    pick = lambda table: jnp.sum(sel * table[None, :], axis=1)
    blk = jnp.where(live, pick(blk_lo) + (w - pick(item_start)), n_blocks - 1).astype(jnp.int32)
    lo = jnp.where(live, jnp.maximum(pick(starts), blk * ROW_BLOCK), 0).astype(jnp.int32)
    hi = jnp.where(live, jnp.minimum(pick(ends), (blk + 1) * ROW_BLOCK), 0).astype(jnp.int32)
    ids = jnp.arange(N_EXPERTS, dtype=jnp.int32)
    used = counts > 0
    rank_e = jnp.cumsum(used.astype(jnp.int32)) - 1
    later = used[None, :] & (ids[None, :] > ids[:, None])
    next_e = jnp.min(jnp.where(later, ids[None, :], N_EXPERTS), axis=1)
    next_e = jnp.where(next_e < N_EXPERTS, next_e, -1)
    slot = (pick(rank_e) % 2).astype(jnp.int32)
    nxt = pick(next_e).astype(jnp.int32)
    return (blk, e.astype(jnp.int32), lo, hi, slot, nxt), starts


def kernel(x_prompt, x_sample, state_conv, state_gla, norm_mix, w_in, b_gates, w_gla_gate_up, b_gla_gate_up, conv_dw, conv_db, conv_ln_g, conv_ln_b, w_conv_out, gla_norm_g, w_gla_out, w_out, norm_ffn, w_router_group, b_router_group, w_router_expert, b_router_expert, w_expert_gate, w_expert_up, w_expert_down, norm_final):
    depth = norm_mix.shape[0]
    assert depth == 1
    l = 0
    bp, seq_p, _ = x_prompt.shape
    bs, seq_s, _ = x_sample.shape
    n_p, n_s = bp * seq_p, bs * seq_s
    n_tok = n_p + n_s
    row2 = lambda a: a.reshape(1, -1)

    wi = w_in[l]
    w_main = wi[:, :_C_ALOW].astype(BF16)
    w_alow = jnp.pad(wi[:, _C_ALOW:_C_GATES], ((0, 0), (0, LANES - RANK))).astype(BF16)
    w_gates = wi[:, _C_GATES:].astype(BF16)
    w_up_pad = jnp.pad(w_gla_gate_up[l], ((0, LANES - RANK), (0, 0))).astype(BF16)
    w_route = jnp.pad(jnp.concatenate([w_router_expert[l], w_router_group[l]], axis=1),
                      ((0, 0), (0, LANES - N_EXPERTS - N_GROUPS)))
    w_route_hi = w_route.astype(BF16)
    w_route = jnp.stack([w_route_hi, (w_route - w_route_hi.astype(F32)).astype(BF16)])
    b_route = jnp.pad(jnp.concatenate([b_router_expert[l], b_router_group[l]]),
                      (0, LANES - N_EXPERTS - N_GROUPS)).reshape(1, LANES)
    w_pw = w_conv_out[l].astype(BF16)
    w_go = w_gla_out[l].astype(BF16)
    w_o = w_out[l].astype(BF16)
    conv_args = (conv_dw[l], row2(conv_db[l]), row2(conv_ln_g[l]), row2(conv_ln_b[l]), w_pw)
    proj_args = (row2(norm_mix[l]), w_main, w_alow, w_gates, w_up_pad, row2(b_gla_gate_up[l]),
                 row2(b_gates[l]))
    mix_args = (row2(gla_norm_g[l]), w_go, w_o, row2(norm_ffn[l]), w_route, b_route)

    xp = x_prompt.reshape(n_p, D_MODEL)
    conv_out, q, k, v, la, sr, gc, gg, conv_p = _proj_conv(xp, bp, seq_p, proj_args, conv_args)
    o, gla_p = _gla_prompt(q, k, la, v, bp, seq_p)
    h_p, xr_p, rec_p, wf_p, cnt = _mix_route(xp, conv_out, o, sr, gc, gg, *mix_args,
                                             jnp.zeros((SUBLANES, LANES), F32))

    xs_ = x_sample.reshape(n_s, D_MODEL)
    u, q, k, v, la, sr, gc, gg = _proj(xs_, *proj_args)
    conv_out, conv_s = _conv(u.reshape(bs, seq_s, D_CONV), state_conv, *conv_args, nseq=SAMPLE_SEQS)
    seqs = lambda a: a.reshape(bs, seq_s, a.shape[-1])
    o, gla_s = _gla_sample(seqs(q), seqs(k), seqs(la), seqs(v), state_gla)
    h_s, xr_s, rec_s, wf_s, cnt = _mix_route(xs_, conv_out.reshape(n_s, D_MODEL), o.reshape(n_s, V_W), sr, gc, gg,
                                             *mix_args, cnt)

    counts = cnt[0, :N_EXPERTS].astype(jnp.int32)
    items, starts = _work_items(counts, 2 * n_tok)
    rec = jnp.concatenate([rec_p, rec_s], axis=1).astype(jnp.int32)
    e_ids = jnp.arange(N_EXPERTS, dtype=jnp.int32)[:, None]
    start_of = lambda e: jnp.sum(jnp.where(e[None, :] == e_ids, starts[:, None], 0), axis=0)
    pos0 = start_of(rec[_L_E0]) + rec[_L_R0]
    pos1 = start_of(rec[_L_E1]) + rec[_L_R1]
    xs_sorted = _dispatch(pos0, pos1, xr_p, xr_s)
    n_pad = n_tok + 2 * ROW_BLOCK
    row_dst = _invert(pos0, pos1, n_pad)
    y2 = _experts(items, row_dst, xs_sorted, w_expert_gate[l], w_expert_up[l], w_expert_down[l], n_pad)
    y2 = y2.reshape(2, n_pad, D_MODEL)
    y_p = _combine(h_p, wf_p, row2(norm_final), y2, 0)
    y_s = _combine(h_s, wf_s, row2(norm_final), y2, n_p)

    return (y_p.reshape(bp, seq_p, D_MODEL), y_s.reshape(bs, seq_s, D_MODEL),
            conv_p[None], gla_p[None], conv_s[None], gla_s[None])
```

```python
import functools

import jax
import jax.numpy as jnp
from jax import lax
from jax.experimental import pallas as pl
from jax.experimental.pallas import tpu as pltpu

F32 = jnp.float32
BF16 = jnp.bfloat16

D_MODEL = 1024
D_CONV = 512
CONV_WIDTH = 31
CONV_HIST = CONV_WIDTH - 1
HEADS = 4
DK = 128
DV = 256
QK_W = HEADS * DK
V_W = HEADS * DV
RANK = 16
GATE_NORM = 16.0
N_GROUPS = 4
EXPERTS_PER_GROUP = 8
N_EXPERTS = 32
D_EXPERT = 512
EPS = 1e-6

LANES = 128
SUBLANES = 8
VMEM_LIMIT = 56 * 1024 * 1024

TOK_TILE = 512
PROJ_TILE = 512
MIX_TILE = 512
GLA_CHUNK = 128
GLA_STEP = 512
SAFE_LOG_DECAY = -80.0
CONV_ROWS = 64
CONV_HALO = 40
ROW_BLOCK = 256
DMA_UNROLL = 8
SAMPLE_SEQS = 32
SAMPLE_GLA_SEQS = 8
SAMPLE_PAD = 16

_C_GLU_A, _C_GLU_B = 0, 512
_C_Q, _C_K, _C_V, _C_R = 1024, 1536, 2048, 3072
_C_ALOW = 4096
_C_GATES = _C_ALOW + RANK


def _sigmoid(x):
    return jax.nn.sigmoid(x)


def _dot(a, b):
    return jnp.dot(a, b, preferred_element_type=F32)


def _normed_input(x_ref, g_ref):
    x = x_ref[...]
    hn = x * lax.rsqrt(jnp.mean(x * x, axis=-1, keepdims=True) + EPS) * g_ref[...]
    return hn.astype(BF16)


def _glu(hb, wm_ref):
    return _dot(hb, wm_ref[:, _C_GLU_A:_C_GLU_B]) * _sigmoid(_dot(hb, wm_ref[:, _C_GLU_B:_C_Q]))


def _projection_steps(hb, wm_ref, wa_ref, wg_ref, wup_ref, bup_ref, bg_ref,
                      q_ref, k_ref, v_ref, la_ref, sr_ref, gc_ref, gg_ref):
    def mm(lo, hi):
        return _dot(hb, wm_ref[:, lo:hi])

    def q_step():
        q_ref[...] = mm(_C_Q, _C_K) * (DK ** -0.5)

    def k_step():
        k_ref[...] = mm(_C_K, _C_V)

    def v_step():
        v_ref[...] = mm(_C_V, _C_R).astype(BF16)

    def r_step():
        r = mm(_C_R, _C_ALOW)
        sr_ref[...] = (r * _sigmoid(r)).astype(BF16)

    def gc_step():
        gc_ref[...] = _sigmoid(_dot(hb, wg_ref[:, :D_MODEL]) + bg_ref[:, :D_MODEL]).astype(BF16)

    def gg_step():
        gg_ref[...] = _sigmoid(_dot(hb, wg_ref[:, D_MODEL:]) + bg_ref[:, D_MODEL:]).astype(BF16)

    def decay_step():
        a_low = _dot(hb, wa_ref[...])
        z = _dot(a_low.astype(BF16), wup_ref[...]) + bup_ref[...]
        la_ref[...] = (jnp.minimum(z, 0.0) - jnp.log1p(jnp.exp(-jnp.abs(z)))) * (1.0 / GATE_NORM)

    return [q_step, k_step, v_step, r_step, gc_step, gg_step, decay_step]


def _proj_kernel(x_ref, g_ref, wm_ref, wa_ref, wg_ref, wup_ref, bup_ref, bg_ref,
                 u_ref, q_ref, k_ref, v_ref, la_ref, sr_ref, gc_ref, gg_ref):
    hb = _normed_input(x_ref, g_ref)
    u_ref[...] = _glu(hb, wm_ref)
    for step in _projection_steps(hb, wm_ref, wa_ref, wg_ref, wup_ref, bup_ref, bg_ref,
                                  q_ref, k_ref, v_ref, la_ref, sr_ref, gc_ref, gg_ref):
        step()


_HIST_PAD = 32
_HIST0 = _HIST_PAD - CONV_HIST


def _conv_taps(w, dw_ref, cs, rt):
    win = w.shape[0]
    acc = jnp.zeros((rt, LANES), F32)
    for s in range(SUBLANES):
        ws = w if s == 0 else pltpu.roll(w, win - s, axis=0)
        for a in range(CONV_HALO // SUBLANES):
            j = a * SUBLANES + s - _HIST0
            if 0 <= j < CONV_WIDTH:
                acc = acc + ws[a * SUBLANES:a * SUBLANES + rt, :] * dw_ref[j:j + 1, cs]
    return acc


def _norm_swish_pointwise(y, lg_ref, lb_ref, wpw_ref):
    mu = jnp.mean(y, axis=-1, keepdims=True)
    var = jnp.mean(jnp.square(y - mu), axis=-1, keepdims=True)
    yn = (y - mu) * lax.rsqrt(var + EPS) * lg_ref[...] + lb_ref[...]
    return _dot((yn * _sigmoid(yn)).astype(BF16), wpw_ref[...])


def _proj_conv_kernel(x_ref, g_ref, wm_ref, wa_ref, wg_ref, wup_ref, bup_ref, bg_ref,
                      dw_ref, db_ref, lg_ref, lb_ref, wpw_ref,
                      co_ref, q_ref, k_ref, v_ref, la_ref, sr_ref, gc_ref, gg_ref, ns_ref,
                      full_ref, y_ref):
    c = pl.program_id(1)
    tm = x_ref.shape[0]

    @pl.when(c == 0)
    def _():
        full_ref[0:_HIST_PAD, :] = jnp.zeros((_HIST_PAD, D_CONV), F32)

    @pl.when(c > 0)
    def _():
        full_ref[0:_HIST_PAD, :] = full_ref[tm:tm + _HIST_PAD, :]

    hb = _normed_input(x_ref, g_ref)
    full_ref[_HIST_PAD:_HIST_PAD + tm, :] = _glu(hb, wm_ref)
    full_ref[_HIST_PAD + tm:, :] = jnp.zeros((SUBLANES, D_CONV), F32)
    ns_ref[0] = full_ref[tm + _HIST0:tm + _HIST_PAD, :]
    rt = CONV_ROWS

    def conv_tile(i, cc):
        def run():
            cs = slice(cc * LANES, (cc + 1) * LANES)
            acc = _conv_taps(full_ref[i * rt:i * rt + rt + CONV_HALO, cs], dw_ref, cs, rt)
            y_ref[i * rt:(i + 1) * rt, cs] = acc + db_ref[:, cs]
        return run

    conv_tiles = [conv_tile(i, cc) for i in range(tm // rt) for cc in range(D_CONV // LANES)]
    proj_steps = _projection_steps(hb, wm_ref, wa_ref, wg_ref, wup_ref, bup_ref, bg_ref,
                                   q_ref, k_ref, v_ref, la_ref, sr_ref, gc_ref, gg_ref)
    per_step = -(-len(conv_tiles) // len(proj_steps))
    for n, step in enumerate(proj_steps):
        step()
        for tile in conv_tiles[n * per_step:(n + 1) * per_step]:
            tile()
    co_ref[...] = _norm_swish_pointwise(y_ref[...], lg_ref, lb_ref, wpw_ref)


def _proj_conv(x2d, nb, seq, proj_consts, conv_consts):
    tm = PROJ_TILE
    nc = seq // tm
    row = lambda w: pl.BlockSpec((tm, w), lambda b, c: (b * nc + c, 0))
    full = lambda a: pl.BlockSpec(a.shape, lambda b, c: (0,) * a.ndim)
    widths = (D_MODEL, QK_W, QK_W, V_W, QK_W, V_W, D_MODEL, D_MODEL)
    dtypes = (F32, F32, F32, BF16, F32, BF16, BF16, BF16)
    consts = tuple(proj_consts) + tuple(conv_consts)
    n = nb * seq
    return pl.pallas_call(
        _proj_conv_kernel,
        grid=(nb, nc),
        in_specs=[row(D_MODEL)] + [full(a) for a in consts],
        out_specs=[row(w) for w in widths] + [pl.BlockSpec((1, CONV_HIST, D_CONV), lambda b, c: (b, 0, 0))],
        out_shape=[jax.ShapeDtypeStruct((n, w), dt) for w, dt in zip(widths, dtypes)]
        + [jax.ShapeDtypeStruct((nb, CONV_HIST, D_CONV), F32)],
        scratch_shapes=[pltpu.VMEM((_HIST_PAD + tm + SUBLANES, D_CONV), F32), pltpu.VMEM((tm, D_CONV), F32)],
        compiler_params=pltpu.CompilerParams(dimension_semantics=("parallel", "arbitrary"),
                                             vmem_limit_bytes=VMEM_LIMIT),
        name="proj_conv",
    )(x2d, *consts)


def _proj(x2d, norm_g, w_main, w_alow, w_gates, w_up, b_up, b_gates):
    n = x2d.shape[0]
    tm = TOK_TILE
    row = lambda w: pl.BlockSpec((tm, w), lambda i: (i, 0))
    full = lambda a: pl.BlockSpec(a.shape, lambda i: (0,) * a.ndim)
    widths = (D_CONV, QK_W, QK_W, V_W, QK_W, V_W, D_MODEL, D_MODEL)
    dtypes = (F32, F32, F32, BF16, F32, BF16, BF16, BF16)
    consts = (norm_g, w_main, w_alow, w_gates, w_up, b_up, b_gates)
    return pl.pallas_call(
        _proj_kernel,
        grid=(n // tm,),
        in_specs=[row(D_MODEL)] + [full(a) for a in consts],
        out_specs=[row(w) for w in widths],
        out_shape=[jax.ShapeDtypeStruct((n, w), dt) for w, dt in zip(widths, dtypes)],
        compiler_params=pltpu.CompilerParams(dimension_semantics=("parallel",),
                                             vmem_limit_bytes=VMEM_LIMIT),
        name="proj",
    )(x2d, *consts)


def _conv_kernel(u_ref, st_ref, dw_ref, db_ref, lg_ref, lb_ref, wpw_ref, out_ref, ns_ref, full_ref, y_ref, *, seq):
    nseq = u_ref.shape[0]
    n_rows = y_ref.shape[0] // nseq
    tail = full_ref.shape[0] - (_HIST_PAD + seq)

    def one_seq(b, carry):
        full_ref[0:_HIST_PAD, :] = jnp.zeros((_HIST_PAD, D_CONV), F32)
        full_ref[_HIST_PAD + seq:, :] = jnp.zeros((tail, D_CONV), F32)
        full_ref[_HIST0:_HIST_PAD, :] = st_ref[b]
        full_ref[_HIST_PAD:_HIST_PAD + seq, :] = u_ref[b]
        ns_ref[b] = full_ref[seq + _HIST0:seq + _HIST_PAD, :]
        for c in range(D_CONV // LANES):
            cs = slice(c * LANES, (c + 1) * LANES)
            acc = _conv_taps(full_ref[:, cs], dw_ref, cs, n_rows)
            y_ref[pl.ds(pl.multiple_of(b * n_rows, SUBLANES), n_rows), cs] = acc + db_ref[:, cs]
        return carry

    lax.fori_loop(0, nseq, one_seq, 0)
    res = _norm_swish_pointwise(y_ref[...], lg_ref, lb_ref, wpw_ref)
    for j in range(nseq):
        out_ref[j] = res[j * n_rows:j * n_rows + seq]


def _conv(u3, state, dw, db, lg, lb, w_pw, nseq):
    nb, seq, _ = u3.shape
    n_rows = -(-seq // SUBLANES) * SUBLANES
    full = lambda a: pl.BlockSpec(a.shape, lambda b: (0,) * a.ndim)
    per_b = lambda r, w: pl.BlockSpec((nseq, r, w), lambda b: (b, 0, 0))
    return pl.pallas_call(
        functools.partial(_conv_kernel, seq=seq),
        grid=(nb // nseq,),
        in_specs=[per_b(seq, D_CONV), pl.BlockSpec((None, nseq, CONV_HIST, D_CONV), lambda b: (0, b, 0, 0)),
                  full(dw), full(db), full(lg), full(lb), full(w_pw)],
        out_specs=[per_b(seq, D_MODEL), per_b(CONV_HIST, D_CONV)],
        out_shape=[jax.ShapeDtypeStruct((nb, seq, D_MODEL), F32),
                   jax.ShapeDtypeStruct((nb, CONV_HIST, D_CONV), F32)],
        scratch_shapes=[pltpu.VMEM((n_rows + CONV_HALO, D_CONV), F32),
                        pltpu.VMEM((nseq * n_rows, D_CONV), F32)],
        compiler_params=pltpu.CompilerParams(dimension_semantics=("parallel",),
                                             vmem_limit_bytes=VMEM_LIMIT),
        name="conv_state",
    )(u3, state, dw, db, lg, lb, w_pw)


def _token_step(s, a_col, k_col, q_col, v_row):
    s = s * a_col + k_col * v_row
    return s, jnp.sum(q_col * s, axis=0, keepdims=True)


def _gla_prompt_kernel(q_ref, k_ref, la_ref, v_ref, o_ref, sout_ref, s_ref, vf_ref):
    c = pl.program_id(1)
    n = GLA_CHUNK
    chunks = [slice(i * n, (i + 1) * n) for i in range(q_ref.shape[0] // n)]

    @pl.when(c == 0)
    def _():
        s_ref[...] = jnp.zeros_like(s_ref)

    r_i = lax.broadcasted_iota(jnp.int32, (n, n), 0)
    c_i = lax.broadcasted_iota(jnp.int32, (n, n), 1)
    tri = jnp.where(r_i >= c_i, 1.0, 0.0).astype(BF16)

    def prefix_sum(g):
        hi = g.astype(BF16)
        lo = (g - hi.astype(F32)).astype(BF16)
        return _dot(tri, hi) + _dot(tri, lo)

    bs = [prefix_sum(la_ref[rows, :]) for rows in chunks]
    total = bs[0][n - 1:n, :]
    for b in bs[1:]:
        total = jnp.minimum(total, b[n - 1:n, :])
    fast = jnp.min(total) > SAFE_LOG_DECAY

    @pl.when(fast)
    def _():
        for h in range(HEADS):
            ks = slice(h * DK, (h + 1) * DK)
            vs = slice(h * DV, (h + 1) * DV)
            s = s_ref[h]
            for rows, b in zip(chunks, bs):
                bh = b[:, ks]
                bl = bh[n - 1:n, :]
                kh = k_ref[rows, ks]
                vh = v_ref[rows, vs]
                qp = (q_ref[rows, ks] * jnp.exp(bh)).astype(BF16)
                kp = (kh * jnp.exp(-bh)).astype(BF16)
                kpp = kh * jnp.exp(bl - bh)
                att = lax.dot_general(qp, kp, (((1,), (1,)), ((), ())), preferred_element_type=F32)
                att = jnp.where(r_i >= c_i, att, 0.0).astype(BF16)
                o_ref[rows, vs] = _dot(qp, s.astype(BF16)) + _dot(att, vh)
                d_col = jnp.sum(jnp.where(r_i == c_i, jnp.exp(bl), 0.0), axis=1, keepdims=True)
                s = s * d_col + _dot(kpp.T.astype(BF16), vh)
            s_ref[h] = s

    @pl.when(jnp.logical_not(fast))
    def _():
        lane = lax.broadcasted_iota(jnp.int32, (1, n), 1)
        for rows in chunks:
            vf_ref[...] = v_ref[rows, :].astype(F32)
            for h in range(HEADS):
                ks = slice(h * DK, (h + 1) * DK)
                vs = slice(h * DV, (h + 1) * DV)
                q_t = q_ref[rows, ks].T
                k_t = k_ref[rows, ks].T
                a_t = jnp.exp(la_ref[rows, ks]).T

                def body(t, s):
                    m = lane == t
                    col = lambda x: jnp.sum(jnp.where(m, x, 0.0), axis=1, keepdims=True)
                    s, o_row = _token_step(s, col(a_t), col(k_t), col(q_t), vf_ref[pl.ds(t, 1), vs])
                    o_ref[pl.ds(rows.start + t, 1), vs] = o_row
                    return s

                s_ref[h] = lax.fori_loop(0, n, body, s_ref[h])

    @pl.when(c == pl.num_programs(1) - 1)
    def _():
        sout_ref[0] = s_ref[...]


def _gla_prompt(q, k, la, v, nb, seq):
    nc = seq // GLA_STEP
    tok = lambda w: pl.BlockSpec((GLA_STEP, w), lambda b, c: (b * nc + c, 0))
    return pl.pallas_call(
        _gla_prompt_kernel,
        grid=(nb, nc),
        in_specs=[tok(QK_W), tok(QK_W), tok(QK_W), tok(V_W)],
        out_specs=[tok(V_W), pl.BlockSpec((1, HEADS, DK, DV), lambda b, c: (b, 0, 0, 0))],
        out_shape=[jax.ShapeDtypeStruct((nb * seq, V_W), F32),
                   jax.ShapeDtypeStruct((nb, HEADS, DK, DV), F32)],
        scratch_shapes=[pltpu.VMEM((HEADS, DK, DV), F32), pltpu.VMEM((GLA_CHUNK, V_W), F32)],
        compiler_params=pltpu.CompilerParams(dimension_semantics=("parallel", "arbitrary"),
                                             vmem_limit_bytes=VMEM_LIMIT),
        name="gla_prompt",
    )(q, k, la, v)


def _gla_sample_kernel(q_ref, k_ref, la_ref, v_ref, s_ref, o_ref, sout_ref, *, seq):
    p = SAMPLE_PAD
    row = lax.broadcasted_iota(jnp.int32, (p, 1), 0)

    def padded(x):
        return jnp.concatenate([x, jnp.zeros((p - seq, x.shape[1]), F32)], axis=0)

    for b in range(q_ref.shape[0]):
        q, k, v, g = padded(q_ref[b]), padded(k_ref[b]), padded(v_ref[b].astype(F32)), padded(la_ref[b])
        bc = g
        sh = 1
        while sh < seq:
            bc = bc + jnp.where(row >= sh, pltpu.roll(bc, sh, axis=0), 0.0)
            sh *= 2
        b_last = bc[seq - 1:seq, :]
        bc = jnp.where(row < seq, bc, b_last)
        qp = (q * jnp.exp(bc)).astype(BF16)
        kpp_t = (k * jnp.exp(b_last - bc)).T.astype(BF16)
        d_cols = jnp.broadcast_to(jnp.exp(b_last), (SUBLANES, QK_W)).T
        vb = v.astype(BF16)

        o_in = [jnp.zeros((p, DV), F32) for _ in range(HEADS)]
        for d in range(seq):
            k_d = k if d == 0 else pltpu.roll(k, d, axis=0)
            v_d = v if d == 0 else pltpu.roll(v, d, axis=0)
            b_d = bc if d == 0 else pltpu.roll(bc, d, axis=0)
            pair = q * k_d * jnp.exp(jnp.where(row >= d, bc - b_d, -jnp.inf))
            for h in range(HEADS):
                att = jnp.sum(pair[:, h * DK:(h + 1) * DK], axis=1, keepdims=True)
                o_in[h] = o_in[h] + att * v_d[:, h * DV:(h + 1) * DV]

        for h in range(HEADS):
            ks = slice(h * DK, (h + 1) * DK)
            vs = slice(h * DV, (h + 1) * DV)
            s = s_ref[b, h]
            o = _dot(qp[:, ks], s.astype(BF16)) + o_in[h]
            o_ref[b, :, vs] = o[:seq]
            sout_ref[b, h] = s * d_cols[ks, 0:1] + _dot(kpp_t[ks, :], vb[:, vs])


def _gla_sample(q3, k3, la3, v3, state):
    nb, seq, _ = q3.shape
    assert seq <= SUBLANES
    ns = SAMPLE_GLA_SEQS
    col = pl.BlockSpec((ns, seq, QK_W), lambda b: (b, 0, 0))
    tok = pl.BlockSpec((ns, seq, V_W), lambda b: (b, 0, 0))
    st = pl.BlockSpec((ns, HEADS, DK, DV), lambda b: (b, 0, 0, 0))
    st_in = pl.BlockSpec((None, ns, HEADS, DK, DV), lambda b: (0, b, 0, 0, 0))
    return pl.pallas_call(
        functools.partial(_gla_sample_kernel, seq=seq),
        grid=(nb // ns,),
        in_specs=[col, col, col, tok, st_in],
        out_specs=[tok, st],
        out_shape=[jax.ShapeDtypeStruct((nb, seq, V_W), F32),
                   jax.ShapeDtypeStruct((nb, HEADS, DK, DV), F32)],
        compiler_params=pltpu.CompilerParams(dimension_semantics=("parallel",),
                                             vmem_limit_bytes=VMEM_LIMIT),
        name="gla_sample",
    )(q3, k3, la3, v3, state)


_L_E0, _L_E1, _L_R0, _L_R1 = 0, 1, 2, 3
_GROUP_LANE0 = N_EXPERTS
HALF = D_MODEL // 2


def _mix_route_kernel(x_ref, co_ref, o_ref, sr_ref, gc_ref, gg_ref, gn_ref, wgo_ref, wo_ref,
                      g_ref, wr_ref, br_ref, cin_ref,
                      h_ref, xp_ref, mi_ref, mf_ref, cnt_ref, carry_ref):
    i = pl.program_id(0)
    tm = x_ref.shape[0]

    @pl.when(i == 0)
    def _():
        carry_ref[...] = cin_ref[0:1, :]

    gla_out = None
    for h in range(HEADS):
        vs = slice(h * DV, (h + 1) * DV)
        oh = o_ref[:, vs]
        on = oh * lax.rsqrt(jnp.mean(oh * oh, axis=-1, keepdims=True) + EPS) * gn_ref[:, vs]
        part = _dot((on * sr_ref[:, vs]).astype(BF16), wgo_ref[vs, :])
        gla_out = part if gla_out is None else gla_out + part
    mix = (gc_ref[...] * co_ref[...] + gg_ref[...] * gla_out).astype(BF16)
    x = x_ref[...] + _dot(mix, wo_ref[...])
    h_ref[...] = x

    hn = x * lax.rsqrt(jnp.mean(x * x, axis=-1, keepdims=True) + EPS) * g_ref[...]
    hi = hn.astype(BF16)
    hi_f = hi.astype(F32)
    top = lax.bitcast_convert_type(hi_f[:, :HALF], jnp.uint32)
    bot = lax.bitcast_convert_type(hi_f[:, HALF:], jnp.uint32)
    xp_ref[...] = top | jnp.right_shift(bot, jnp.uint32(16))
    lo = (hn - hi_f).astype(BF16)
    logits = _dot(hi, wr_ref[0]) + (_dot(lo, wr_ref[0]) + _dot(hi, wr_ref[1])) + br_ref[...]
    lane = lax.broadcasted_iota(jnp.int32, (tm, LANES), 1)
    lane_f = lane.astype(F32)
    group_f = jnp.right_shift(lane, 3).astype(F32)
    big = jnp.float32(LANES)
    neg = jnp.float32(-jnp.inf)

    is_g = (lane >= _GROUP_LANE0) & (lane < _GROUP_LANE0 + N_GROUPS)
    lg = jnp.where(is_g, logits, neg)
    mg = jnp.max(lg, axis=1, keepdims=True)
    p_sel = 1.0 / jnp.sum(jnp.exp(lg - mg), axis=1, keepdims=True)
    g_idx = jnp.min(jnp.where(lg == mg, lane_f, big), axis=1, keepdims=True) - _GROUP_LANE0

    is_e = (lane < N_EXPERTS) & (group_f == g_idx)
    le = jnp.where(is_e, logits, neg)
    v0 = jnp.max(le, axis=1, keepdims=True)
    i0 = jnp.min(jnp.where(le == v0, lane_f, big), axis=1, keepdims=True)
    le1 = jnp.where(lane_f == i0, neg, le)
    v1 = jnp.max(le1, axis=1, keepdims=True)
    i1 = jnp.min(jnp.where(le1 == v1, lane_f, big), axis=1, keepdims=True)
    e1 = jnp.exp(v1 - v0)
    den = 1.0 + e1
    w0 = p_sel * (1.0 / den)
    w1 = p_sel * (e1 / den)

    hit0 = lane_f == i0
    hit1 = lane_f == i1
    cnt = jnp.where(hit0 | hit1, 1.0, 0.0)
    r_i = lax.broadcasted_iota(jnp.int32, (tm, tm), 0)
    c_i = lax.broadcasted_iota(jnp.int32, (tm, tm), 1)
    before = jnp.where(r_i > c_i, 1.0, 0.0).astype(BF16)
    seen = _dot(before, cnt.astype(BF16)) + carry_ref[...]
    rank0 = jnp.sum(jnp.where(hit0, seen, 0.0), axis=1, keepdims=True)
    rank1 = jnp.sum(jnp.where(hit1, seen, 0.0), axis=1, keepdims=True)
    carry_ref[...] = carry_ref[...] + jnp.sum(cnt, axis=0, keepdims=True)
    cnt_ref[...] = jnp.broadcast_to(carry_ref[...], cnt_ref.shape)

    rec = jnp.where(lane == _L_E0, i0, 0.0)
    rec = jnp.where(lane == _L_E1, i1, rec)
    rec = jnp.where(lane == _L_R0, rank0, rec)
    rec = jnp.where(lane == _L_R1, rank1, rec)
    mi_ref[...] = rec.T[:SUBLANES, :]
    mf_ref[...] = jnp.where(lane == 0, w0, jnp.where(lane == 1, w1, 0.0))


def _mix_route(x2d, conv_out, o, sr, gc, gg, gn, w_gla_out, w_out, norm_g, w_route, b_route, counts_in):
    n = x2d.shape[0]
    tm = MIX_TILE
    row = lambda w: pl.BlockSpec((tm, w), lambda i: (i, 0))
    full = lambda a: pl.BlockSpec(a.shape, lambda i: (0,) * a.ndim)
    consts = (gn, w_gla_out, w_out, norm_g, w_route, b_route, counts_in)
    return pl.pallas_call(
        _mix_route_kernel,
        grid=(n // tm,),
        in_specs=[row(D_MODEL)] * 6 + [full(a) for a in consts],
        out_specs=[row(D_MODEL), row(HALF), pl.BlockSpec((SUBLANES, tm), lambda i: (0, i)), row(LANES),
                   pl.BlockSpec((SUBLANES, LANES), lambda i: (0, 0))],
        out_shape=[jax.ShapeDtypeStruct((n, D_MODEL), F32),
                   jax.ShapeDtypeStruct((n, HALF), jnp.uint32),
                   jax.ShapeDtypeStruct((SUBLANES, n), F32),
                   jax.ShapeDtypeStruct((n, LANES), F32),
                   jax.ShapeDtypeStruct((SUBLANES, LANES), F32)],
        scratch_shapes=[pltpu.VMEM((1, LANES), F32)],
        compiler_params=pltpu.CompilerParams(dimension_semantics=("arbitrary",),
                                             vmem_limit_bytes=VMEM_LIMIT),
        name="mix_route",
    )(x2d, conv_out, o, sr, gc, gg, *consts)


def _row_copy(src_ref, src_row, dst_ref, dst_row, sem):
    return pltpu.make_async_copy(src_ref.at[pl.ds(src_row, 1), :], dst_ref.at[pl.ds(dst_row, 1), :], sem)


def _dispatch_kernel(p0_ref, p1_ref, xa_ref, xb_ref, xs_ref, sem, *, tiles_a):
    tm = xa_ref.shape[0]
    i = pl.program_id(0)
    base = i * tm

    def scatter(src_ref):
        def rows(wait):
            def body(t, carry):
                c0 = _row_copy(src_ref, t, xs_ref, p0_ref[base + t], sem)
                c1 = _row_copy(src_ref, t, xs_ref, p1_ref[base + t], sem)
                if wait:
                    c0.wait()
                    c1.wait()
                else:
                    c0.start()
                    c1.start()
                return carry
            lax.fori_loop(0, tm, body, 0, unroll=DMA_UNROLL)

        rows(wait=False)
        rows(wait=True)

    @pl.when(i < tiles_a)
    def _():
        scatter(xa_ref)

    @pl.when(i >= tiles_a)
    def _():
        scatter(xb_ref)


def _dispatch(pos0, pos1, x_a, x_b):
    tm = TOK_TILE
    tiles_a, tiles_b = x_a.shape[0] // tm, x_b.shape[0] // tm
    n = x_a.shape[0] + x_b.shape[0]
    width = x_a.shape[1]
    return pl.pallas_call(
        functools.partial(_dispatch_kernel, tiles_a=tiles_a),
        grid_spec=pltpu.PrefetchScalarGridSpec(
            num_scalar_prefetch=2,
            grid=(tiles_a + tiles_b,),
            in_specs=[pl.BlockSpec((tm, width), lambda i, p0, p1: (jnp.minimum(i, tiles_a - 1), 0)),
                      pl.BlockSpec((tm, width), lambda i, p0, p1: (jnp.maximum(i - tiles_a, 0), 0))],
            out_specs=pl.BlockSpec(memory_space=pl.ANY),
            scratch_shapes=[pltpu.SemaphoreType.DMA(())],
        ),
        out_shape=jax.ShapeDtypeStruct((2 * n, width), x_a.dtype),
        compiler_params=pltpu.CompilerParams(dimension_semantics=("arbitrary",)),
        name="dispatch",
    )(pos0, pos1, x_a, x_b)


def _combine_kernel(p0_ref, p1_ref, h_ref, wf_ref, g_ref, ys_ref, out_ref, b0_ref, b1_ref, sems, *, tok_offset):
    tm = h_ref.shape[0]
    i = pl.program_id(0)

    def gather(tile, slot, wait):
        base = tok_offset + tile * tm

        def rows(t, carry):
            c0 = _row_copy(ys_ref, p0_ref[base + t], b0_ref.at[slot], t, sems.at[slot])
            c1 = _row_copy(ys_ref, p1_ref[base + t], b1_ref.at[slot], t, sems.at[slot])
            if wait:
                c0.wait()
                c1.wait()
            else:
                c0.start()
                c1.start()
            return carry

        lax.fori_loop(0, tm, rows, 0, unroll=DMA_UNROLL)

    slot = lax.rem(i, 2)

    @pl.when(i == 0)
    def _():
        gather(i, slot, wait=False)

    @pl.when(i + 1 < pl.num_programs(0))
    def _():
        gather(i + 1, 1 - slot, wait=False)

    gather(i, slot, wait=True)
    y = h_ref[...] + (b0_ref[slot] * wf_ref[:, 0:1] + b1_ref[slot] * wf_ref[:, 1:2])
    out_ref[...] = y * lax.rsqrt(jnp.mean(y * y, axis=-1, keepdims=True) + EPS) * g_ref[...]


def _combine(pos0, pos1, h, wf, norm_g, ys, tok_offset):
    n_rows = h.shape[0]
    tm = TOK_TILE
    return pl.pallas_call(
        functools.partial(_combine_kernel, tok_offset=tok_offset),
        grid_spec=pltpu.PrefetchScalarGridSpec(
            num_scalar_prefetch=2,
            grid=(n_rows // tm,),
            in_specs=[pl.BlockSpec((tm, D_MODEL), lambda i, p0, p1: (i, 0)),
                      pl.BlockSpec((tm, LANES), lambda i, p0, p1: (i, 0)),
                      pl.BlockSpec((1, D_MODEL), lambda i, p0, p1: (0, 0)),
                      pl.BlockSpec(memory_space=pl.ANY)],
            out_specs=pl.BlockSpec((tm, D_MODEL), lambda i, p0, p1: (i, 0)),
            scratch_shapes=[pltpu.VMEM((2, tm, D_MODEL), F32), pltpu.VMEM((2, tm, D_MODEL), F32),
                            pltpu.SemaphoreType.DMA((2,))],
        ),
        out_shape=jax.ShapeDtypeStruct((n_rows, D_MODEL), F32),
        compiler_params=pltpu.CompilerParams(dimension_semantics=("arbitrary",),
                                             vmem_limit_bytes=VMEM_LIMIT),
        name="combine",
    )(pos0, pos1, h, wf, norm_g, ys)


def _expert_kernel(blk_ref, exp_ref, lo_ref, hi_ref, first_ref, slot_ref, next_ref,
                   xs_ref, wg_hbm, wu_hbm, wd_hbm, ys_ref,
                   wg32_ref, wu32_ref, wd32_ref, wgb_ref, wub_ref, wdb_ref, sems):
    w = pl.program_id(0)
    e = exp_ref[w]
    slot = slot_ref[w]
    changed = jnp.logical_or(w == 0, e != exp_ref[jnp.maximum(w - 1, 0)])

    def weight_copies(expert, s):
        return [pltpu.make_async_copy(src.at[expert], dst.at[s], sems.at[s, j])
                for j, (src, dst) in enumerate(((wg_hbm, wg32_ref), (wu_hbm, wu32_ref), (wd_hbm, wd32_ref)))]

    @pl.when(w == 0)
    def _():
        for c in weight_copies(e, slot):
            c.start()

    @pl.when(changed)
    def _():
        for c in weight_copies(e, slot):
            c.wait()
        wgb_ref[...] = wg32_ref[slot].astype(BF16)
        wub_ref[...] = wu32_ref[slot].astype(BF16)
        wdb_ref[...] = wd32_ref[slot].astype(BF16)

        @pl.when(next_ref[w] >= 0)
        def _():
            for c in weight_copies(next_ref[w], 1 - slot):
                c.start()

    lo = lo_ref[w]
    hi = hi_ref[w]

    @pl.when(hi > lo)
    def _():
        packed = xs_ref[...]
        x_top = lax.bitcast_convert_type(packed & jnp.uint32(0xFFFF0000), F32)
        x_bot = lax.bitcast_convert_type(jnp.left_shift(packed, jnp.uint32(16)), F32)
        x = jnp.concatenate([x_top, x_bot], axis=1).astype(BF16)
        gate = _dot(x, wgb_ref[...])
        hid = (gate * _sigmoid(gate)) * _dot(x, wub_ref[...])
        y = _dot(hid.astype(BF16), wdb_ref[...])
        rows = blk_ref[w] * ROW_BLOCK + lax.broadcasted_iota(jnp.int32, (ROW_BLOCK, 1), 0)
        y = jnp.where((rows >= lo) & (rows < hi), y, 0.0)

        @pl.when(first_ref[w] == 1)
        def _():
            ys_ref[...] = y

        @pl.when(first_ref[w] == 0)
        def _():
            ys_ref[...] = ys_ref[...] + y


def _experts(items, xs, w_gate, w_up, w_down):
    n_rows = xs.shape[0]
    n_items = items[0].shape[0]
    hbm = pl.BlockSpec(memory_space=pl.ANY)
    return pl.pallas_call(
        _expert_kernel,
        grid_spec=pltpu.PrefetchScalarGridSpec(
            num_scalar_prefetch=len(items),
            grid=(n_items,),
            in_specs=[pl.BlockSpec((ROW_BLOCK, HALF), lambda w, blk, *_: (blk[w], 0)), hbm, hbm, hbm],
            out_specs=pl.BlockSpec((ROW_BLOCK, D_MODEL), lambda w, blk, *_: (blk[w], 0)),
            scratch_shapes=[pltpu.VMEM((2, D_MODEL, D_EXPERT), F32), pltpu.VMEM((2, D_MODEL, D_EXPERT), F32),
                            pltpu.VMEM((2, D_EXPERT, D_MODEL), F32),
                            pltpu.VMEM((D_MODEL, D_EXPERT), BF16), pltpu.VMEM((D_MODEL, D_EXPERT), BF16),
                            pltpu.VMEM((D_EXPERT, D_MODEL), BF16),
                            pltpu.SemaphoreType.DMA((2, 3))],
        ),
        out_shape=jax.ShapeDtypeStruct((n_rows, D_MODEL), F32),
        compiler_params=pltpu.CompilerParams(dimension_semantics=("arbitrary",),
                                             vmem_limit_bytes=VMEM_LIMIT),
        name="experts",
    )(*items, xs, w_gate, w_up, w_down)


def _work_items(counts, n_rows):
    n_blocks = n_rows // ROW_BLOCK
    n_items = n_blocks + N_EXPERTS - 1
    ends = jnp.cumsum(counts)
    starts = ends - counts
    blk_lo = starts // ROW_BLOCK
    blk_hi = jnp.maximum(ends - 1, 0) // ROW_BLOCK
    per_e = jnp.where(counts > 0, blk_hi - blk_lo + 1, 0)
    item_end = jnp.cumsum(per_e)
    item_start = item_end - per_e
    total = item_end[-1]
    w = jnp.arange(n_items, dtype=jnp.int32)
    live = w < total
    e = jnp.sum((jnp.minimum(w, total - 1)[:, None] >= item_end[None, :]).astype(jnp.int32), axis=1)
    e = jnp.minimum(e, N_EXPERTS - 1)
    sel = (e[:, None] == jnp.arange(N_EXPERTS, dtype=jnp.int32)[None, :]).astype(jnp.int32)
    pick = lambda table: jnp.sum(sel * table[None, :], axis=1)
    blk = jnp.where(live, pick(blk_lo) + (w - pick(item_start)), n_blocks - 1).astype(jnp.int32)
    lo = jnp.where(live, jnp.maximum(pick(starts), blk * ROW_BLOCK), 0).astype(jnp.int32)
    hi = jnp.where(live, jnp.minimum(pick(ends), (blk + 1) * ROW_BLOCK), 0).astype(jnp.int32)
    prev_blk = jnp.concatenate([jnp.full((1,), -1, jnp.int32), blk[:-1]])
    first = (live & (blk != prev_blk)).astype(jnp.int32)
    ids = jnp.arange(N_EXPERTS, dtype=jnp.int32)
    used = counts > 0
    rank_e = jnp.cumsum(used.astype(jnp.int32)) - 1
    later = used[None, :] & (ids[None, :] > ids[:, None])
    next_e = jnp.min(jnp.where(later, ids[None, :], N_EXPERTS), axis=1)
    next_e = jnp.where(next_e < N_EXPERTS, next_e, -1)
    slot = (pick(rank_e) % 2).astype(jnp.int32)
    nxt = pick(next_e).astype(jnp.int32)
    return (blk, e.astype(jnp.int32), lo, hi, first, slot, nxt), starts


def kernel(x_prompt, x_sample, state_conv, state_gla, norm_mix, w_in, b_gates, w_gla_gate_up, b_gla_gate_up, conv_dw, conv_db, conv_ln_g, conv_ln_b, w_conv_out, gla_norm_g, w_gla_out, w_out, norm_ffn, w_router_group, b_router_group, w_router_expert, b_router_expert, w_expert_gate, w_expert_up, w_expert_down, norm_final):
    depth = norm_mix.shape[0]
    assert depth == 1
    l = 0
    bp, seq_p, _ = x_prompt.shape
    bs, seq_s, _ = x_sample.shape
    n_p, n_s = bp * seq_p, bs * seq_s
    n_tok = n_p + n_s
    row2 = lambda a: a.reshape(1, -1)

    wi = w_in[l]
    w_main = wi[:, :_C_ALOW].astype(BF16)
    w_alow = jnp.pad(wi[:, _C_ALOW:_C_GATES], ((0, 0), (0, LANES - RANK))).astype(BF16)
    w_gates = wi[:, _C_GATES:].astype(BF16)
    w_up_pad = jnp.pad(w_gla_gate_up[l], ((0, LANES - RANK), (0, 0))).astype(BF16)
    w_route = jnp.pad(jnp.concatenate([w_router_expert[l], w_router_group[l]], axis=1),
                      ((0, 0), (0, LANES - N_EXPERTS - N_GROUPS)))
    w_route_hi = w_route.astype(BF16)
    w_route = jnp.stack([w_route_hi, (w_route - w_route_hi.astype(F32)).astype(BF16)])
    b_route = jnp.pad(jnp.concatenate([b_router_expert[l], b_router_group[l]]),
                      (0, LANES - N_EXPERTS - N_GROUPS)).reshape(1, LANES)
    w_pw = w_conv_out[l].astype(BF16)
    w_go = w_gla_out[l].astype(BF16)
    w_o = w_out[l].astype(BF16)
    conv_args = (conv_dw[l], row2(conv_db[l]), row2(conv_ln_g[l]), row2(conv_ln_b[l]), w_pw)
    proj_args = (row2(norm_mix[l]), w_main, w_alow, w_gates, w_up_pad, row2(b_gla_gate_up[l]),
                 row2(b_gates[l]))
    mix_args = (row2(gla_norm_g[l]), w_go, w_o, row2(norm_ffn[l]), w_route, b_route)

    xp = x_prompt.reshape(n_p, D_MODEL)
    conv_out, q, k, v, la, sr, gc, gg, conv_p = _proj_conv(xp, bp, seq_p, proj_args, conv_args)
    o, gla_p = _gla_prompt(q, k, la, v, bp, seq_p)
    h_p, xr_p, rec_p, wf_p, cnt = _mix_route(xp, conv_out, o, sr, gc, gg, *mix_args,
                                             jnp.zeros((SUBLANES, LANES), F32))

    xs_ = x_sample.reshape(n_s, D_MODEL)
    u, q, k, v, la, sr, gc, gg = _proj(xs_, *proj_args)
    conv_out, conv_s = _conv(u.reshape(bs, seq_s, D_CONV), state_conv, *conv_args, nseq=SAMPLE_SEQS)
    seqs = lambda a: a.reshape(bs, seq_s, a.shape[-1])
    o, gla_s = _gla_sample(seqs(q), seqs(k), seqs(la), seqs(v), state_gla)
    h_s, xr_s, rec_s, wf_s, cnt = _mix_route(xs_, conv_out.reshape(n_s, D_MODEL), o.reshape(n_s, V_W), sr, gc, gg,
                                             *mix_args, cnt)

    counts = cnt[0, :N_EXPERTS].astype(jnp.int32)
    items, starts = _work_items(counts, 2 * n_tok)
    rec = jnp.concatenate([rec_p, rec_s], axis=1).astype(jnp.int32)
    e_ids = jnp.arange(N_EXPERTS, dtype=jnp.int32)[:, None]
    start_of = lambda e: jnp.sum(jnp.where(e[None, :] == e_ids, starts[:, None], 0), axis=0)
    pos0 = start_of(rec[_L_E0]) + rec[_L_R0]
    pos1 = start_of(rec[_L_E1]) + rec[_L_R1]
    xs_sorted = _dispatch(pos0, pos1, xr_p, xr_s)
    ys = _experts(items, xs_sorted, w_expert_gate[l], w_expert_up[l], w_expert_down[l])
    y_p = _combine(pos0, pos1, h_p, wf_p, row2(norm_final), ys, 0)
    y_s = _combine(pos0, pos1, h_s, wf_s, row2(norm_final), ys, n_p)

    return (y_p.reshape(bp, seq_p, D_MODEL), y_s.reshape(bs, seq_s, D_MODEL),
            conv_p[None], gla_p[None], conv_s[None], gla_s[None])
```

```python
import functools

import jax
import jax.numpy as jnp
from jax import lax
from jax.experimental import pallas as pl
from jax.experimental.pallas import tpu as pltpu

F32 = jnp.float32
BF16 = jnp.bfloat16

D_MODEL = 1024
D_CONV = 512
CONV_WIDTH = 31
CONV_HIST = CONV_WIDTH - 1
HEADS = 4
DK = 128
DV = 256
QK_W = HEADS * DK
V_W = HEADS * DV
RANK = 16
GATE_NORM = 16.0
N_GROUPS = 4
EXPERTS_PER_GROUP = 8
N_EXPERTS = 32
D_EXPERT = 512
EPS = 1e-6

LANES = 128
SUBLANES = 8
VMEM_LIMIT = 56 * 1024 * 1024

TOK_TILE = 512
PROJ_TILE = 512
MIX_TILE = 512
GLA_CHUNK = 128
GLA_STEP = 512
SAFE_LOG_DECAY = -80.0
CONV_ROWS = 64
CONV_HALO = 40
ROW_BLOCK = 256
DMA_UNROLL = 8
SAMPLE_SEQS = 32
SAMPLE_GLA_SEQS = 8
SAMPLE_PAD = 16

_C_GLU_A, _C_GLU_B = 0, 512
_C_Q, _C_K, _C_V, _C_R = 1024, 1536, 2048, 3072
_C_ALOW = 4096
_C_GATES = _C_ALOW + RANK
_MAIN_W = 1


def _sigmoid(x):
    return jax.nn.sigmoid(x)


def _dot(a, b):
    return jnp.dot(a, b, preferred_element_type=F32)


def _normed_input(x_ref, g_ref):
    x = x_ref[...]
    hn = x * lax.rsqrt(jnp.mean(x * x, axis=-1, keepdims=True) + EPS) * g_ref[...]
    return hn.astype(BF16)


def _glu(hb, wm_ref):
    return _dot(hb, wm_ref[:, _C_GLU_A:_C_GLU_B]) * _sigmoid(_dot(hb, wm_ref[:, _C_GLU_B:_C_Q]))


def _projection_steps(hb, wm_ref, wa_ref, wg_ref, wup_ref, bup_ref, bg_ref,
                      q_ref, k_ref, v_ref, la_ref, sr_ref, gc_ref, gg_ref):
    def mm(lo, hi):
        return _dot(hb, wm_ref[:, lo:hi])

    def q_step():
        q_ref[...] = mm(_C_Q, _C_K) * (DK ** -0.5)

    def k_step():
        k_ref[...] = mm(_C_K, _C_V)

    def v_step():
        v_ref[...] = mm(_C_V, _C_R).astype(BF16)

    def r_step():
        r = mm(_C_R, _C_ALOW)
        sr_ref[...] = (r * _sigmoid(r)).astype(BF16)

    def gc_step():
        gc_ref[...] = _sigmoid(_dot(hb, wg_ref[:, :D_MODEL]) + bg_ref[:, :D_MODEL]).astype(BF16)

    def gg_step():
        gg_ref[...] = _sigmoid(_dot(hb, wg_ref[:, D_MODEL:]) + bg_ref[:, D_MODEL:]).astype(BF16)

    def decay_step():
        a_low = _dot(hb, wa_ref[...])
        z = _dot(a_low.astype(BF16), wup_ref[...]) + bup_ref[...]
        la_ref[...] = (jnp.minimum(z, 0.0) - jnp.log1p(jnp.exp(-jnp.abs(z)))) * (1.0 / GATE_NORM)

    return [q_step, k_step, v_step, r_step, gc_step, gg_step, decay_step]


def _proj_kernel(x_ref, g_ref, wm_ref, wa_ref, wg_ref, wup_ref, bup_ref, bg_ref,
                 u_ref, q_ref, k_ref, v_ref, la_ref, sr_ref, gc_ref, gg_ref):
    hb = _normed_input(x_ref, g_ref)
    u_ref[...] = _glu(hb, wm_ref)
    for step in _projection_steps(hb, wm_ref, wa_ref, wg_ref, wup_ref, bup_ref, bg_ref,
                                  q_ref, k_ref, v_ref, la_ref, sr_ref, gc_ref, gg_ref):
        step()


_HIST_PAD = 32
_HIST0 = _HIST_PAD - CONV_HIST


def _conv_taps(w, dw_ref, cs, rt):
    win = w.shape[0]
    acc = jnp.zeros((rt, LANES), F32)
    for s in range(SUBLANES):
        ws = w if s == 0 else pltpu.roll(w, win - s, axis=0)
        for a in range(CONV_HALO // SUBLANES):
            j = a * SUBLANES + s - _HIST0
            if 0 <= j < CONV_WIDTH:
                acc = acc + ws[a * SUBLANES:a * SUBLANES + rt, :] * dw_ref[j:j + 1, cs]
    return acc


def _norm_swish_pointwise(y, lg_ref, lb_ref, wpw_ref):
    mu = jnp.mean(y, axis=-1, keepdims=True)
    var = jnp.mean(jnp.square(y - mu), axis=-1, keepdims=True)
    yn = (y - mu) * lax.rsqrt(var + EPS) * lg_ref[...] + lb_ref[...]
    return _dot((yn * _sigmoid(yn)).astype(BF16), wpw_ref[...])


def _proj_conv_kernel(x_ref, g_ref, wm_ref, wa_ref, wg_ref, wup_ref, bup_ref, bg_ref,
                      dw_ref, db_ref, lg_ref, lb_ref, wpw_ref,
                      co_ref, q_ref, k_ref, v_ref, la_ref, sr_ref, gc_ref, gg_ref, ns_ref,
                      full_ref, y_ref):
    c = pl.program_id(1)
    tm = x_ref.shape[0]

    @pl.when(c == 0)
    def _():
        full_ref[0:_HIST_PAD, :] = jnp.zeros((_HIST_PAD, D_CONV), F32)

    @pl.when(c > 0)
    def _():
        full_ref[0:_HIST_PAD, :] = full_ref[tm:tm + _HIST_PAD, :]

    hb = _normed_input(x_ref, g_ref)
    full_ref[_HIST_PAD:_HIST_PAD + tm, :] = _glu(hb, wm_ref)
    full_ref[_HIST_PAD + tm:, :] = jnp.zeros((SUBLANES, D_CONV), F32)
    ns_ref[0] = full_ref[tm + _HIST0:tm + _HIST_PAD, :]
    rt = CONV_ROWS

    def conv_tile(i, cc):
        def run():
            cs = slice(cc * LANES, (cc + 1) * LANES)
            acc = _conv_taps(full_ref[i * rt:i * rt + rt + CONV_HALO, cs], dw_ref, cs, rt)
            y_ref[i * rt:(i + 1) * rt, cs] = acc + db_ref[:, cs]
        return run

    conv_tiles = [conv_tile(i, cc) for i in range(tm // rt) for cc in range(D_CONV // LANES)]
    proj_steps = _projection_steps(hb, wm_ref, wa_ref, wg_ref, wup_ref, bup_ref, bg_ref,
                                   q_ref, k_ref, v_ref, la_ref, sr_ref, gc_ref, gg_ref)
    per_step = -(-len(conv_tiles) // len(proj_steps))
    for n, step in enumerate(proj_steps):
        step()
        for tile in conv_tiles[n * per_step:(n + 1) * per_step]:
            tile()
    co_ref[...] = _norm_swish_pointwise(y_ref[...], lg_ref, lb_ref, wpw_ref)


def _proj_conv(x2d, nb, seq, proj_consts, conv_consts):
    tm = PROJ_TILE
    nc = seq // tm
    row = lambda w: pl.BlockSpec((tm, w), lambda b, c: (b * nc + c, 0))
    full = lambda a: pl.BlockSpec(a.shape, lambda b, c: (0,) * a.ndim)
    widths = (D_MODEL, QK_W, QK_W, V_W, QK_W, V_W, D_MODEL, D_MODEL)
    dtypes = (F32, F32, F32, BF16, F32, BF16, BF16, BF16)
    consts = tuple(proj_consts) + tuple(conv_consts)
    n = nb * seq
    const_specs = [full(a) for a in consts]
    const_specs[_MAIN_W] = pl.BlockSpec((D_MODEL, _C_ALOW), lambda b, c: (0, 0))
    return pl.pallas_call(
        _proj_conv_kernel,
        grid=(nb, nc),
        in_specs=[row(D_MODEL)] + const_specs,
        out_specs=[row(w) for w in widths] + [pl.BlockSpec((1, CONV_HIST, D_CONV), lambda b, c: (b, 0, 0))],
        out_shape=[jax.ShapeDtypeStruct((n, w), dt) for w, dt in zip(widths, dtypes)]
        + [jax.ShapeDtypeStruct((nb, CONV_HIST, D_CONV), F32)],
        scratch_shapes=[pltpu.VMEM((_HIST_PAD + tm + SUBLANES, D_CONV), F32), pltpu.VMEM((tm, D_CONV), F32)],
        compiler_params=pltpu.CompilerParams(dimension_semantics=("parallel", "arbitrary"),
                                             vmem_limit_bytes=VMEM_LIMIT),
        name="proj_conv",
    )(x2d, *consts)


def _proj(x2d, norm_g, w_main, w_alow, w_gates, w_up, b_up, b_gates):
    n = x2d.shape[0]
    tm = TOK_TILE
    row = lambda w: pl.BlockSpec((tm, w), lambda i: (i, 0))
    full = lambda a: pl.BlockSpec(a.shape, lambda i: (0,) * a.ndim)
    widths = (D_CONV, QK_W, QK_W, V_W, QK_W, V_W, D_MODEL, D_MODEL)
    dtypes = (F32, F32, F32, BF16, F32, BF16, BF16, BF16)
    consts = (norm_g, w_main, w_alow, w_gates, w_up, b_up, b_gates)
    const_specs = [full(a) for a in consts]
    const_specs[_MAIN_W] = pl.BlockSpec((D_MODEL, _C_ALOW), lambda i: (0, 0))
    return pl.pallas_call(
        _proj_kernel,
        grid=(n // tm,),
        in_specs=[row(D_MODEL)] + const_specs,
        out_specs=[row(w) for w in widths],
        out_shape=[jax.ShapeDtypeStruct((n, w), dt) for w, dt in zip(widths, dtypes)],
        compiler_params=pltpu.CompilerParams(dimension_semantics=("parallel",),
                                             vmem_limit_bytes=VMEM_LIMIT),
        name="proj",
    )(x2d, *consts)


def _conv_kernel(u_ref, st_ref, dw_ref, db_ref, lg_ref, lb_ref, wpw_ref, out_ref, ns_ref, full_ref, y_ref, *, seq):
    nseq = u_ref.shape[0]
    n_rows = y_ref.shape[0] // nseq
    tail = full_ref.shape[0] - (_HIST_PAD + seq)

    def one_seq(b, carry):
        full_ref[0:_HIST_PAD, :] = jnp.zeros((_HIST_PAD, D_CONV), F32)
        full_ref[_HIST_PAD + seq:, :] = jnp.zeros((tail, D_CONV), F32)
        full_ref[_HIST0:_HIST_PAD, :] = st_ref[b]
        full_ref[_HIST_PAD:_HIST_PAD + seq, :] = u_ref[b]
        ns_ref[b] = full_ref[seq + _HIST0:seq + _HIST_PAD, :]
        for c in range(D_CONV // LANES):
            cs = slice(c * LANES, (c + 1) * LANES)
            acc = _conv_taps(full_ref[:, cs], dw_ref, cs, n_rows)
            y_ref[pl.ds(pl.multiple_of(b * n_rows, SUBLANES), n_rows), cs] = acc + db_ref[:, cs]
        return carry

    lax.fori_loop(0, nseq, one_seq, 0)
    res = _norm_swish_pointwise(y_ref[...], lg_ref, lb_ref, wpw_ref)
    for j in range(nseq):
        out_ref[j] = res[j * n_rows:j * n_rows + seq]


def _conv(u3, state, dw, db, lg, lb, w_pw, nseq):
    nb, seq, _ = u3.shape
    n_rows = -(-seq // SUBLANES) * SUBLANES
    full = lambda a: pl.BlockSpec(a.shape, lambda b: (0,) * a.ndim)
    per_b = lambda r, w: pl.BlockSpec((nseq, r, w), lambda b: (b, 0, 0))
    return pl.pallas_call(
        functools.partial(_conv_kernel, seq=seq),
        grid=(nb // nseq,),
        in_specs=[per_b(seq, D_CONV), pl.BlockSpec((None, nseq, CONV_HIST, D_CONV), lambda b: (0, b, 0, 0)),
                  full(dw), full(db), full(lg), full(lb), full(w_pw)],
        out_specs=[per_b(seq, D_MODEL), per_b(CONV_HIST, D_CONV)],
        out_shape=[jax.ShapeDtypeStruct((nb, seq, D_MODEL), F32),
                   jax.ShapeDtypeStruct((nb, CONV_HIST, D_CONV), F32)],
        scratch_shapes=[pltpu.VMEM((n_rows + CONV_HALO, D_CONV), F32),
                        pltpu.VMEM((nseq * n_rows, D_CONV), F32)],
        compiler_params=pltpu.CompilerParams(dimension_semantics=("parallel",),
                                             vmem_limit_bytes=VMEM_LIMIT),
        name="conv_state",
    )(u3, state, dw, db, lg, lb, w_pw)


def _token_step(s, a_col, k_col, q_col, v_row):
    s = s * a_col + k_col * v_row
    return s, jnp.sum(q_col * s, axis=0, keepdims=True)


def _gla_prompt_kernel(q_ref, k_ref, la_ref, v_ref, o_ref, sout_ref, s_ref, vf_ref):
    c = pl.program_id(1)
    n = GLA_CHUNK
    chunks = [slice(i * n, (i + 1) * n) for i in range(q_ref.shape[0] // n)]

    @pl.when(c == 0)
    def _():
        s_ref[...] = jnp.zeros_like(s_ref)

    r_i = lax.broadcasted_iota(jnp.int32, (n, n), 0)
    c_i = lax.broadcasted_iota(jnp.int32, (n, n), 1)
    tri = jnp.where(r_i >= c_i, 1.0, 0.0).astype(BF16)

    def prefix_sum(g):
        hi = g.astype(BF16)
        lo = (g - hi.astype(F32)).astype(BF16)
        return _dot(tri, hi) + _dot(tri, lo)

    bs = [prefix_sum(la_ref[rows, :]) for rows in chunks]
    total = bs[0][n - 1:n, :]
    for b in bs[1:]:
        total = jnp.minimum(total, b[n - 1:n, :])
    fast = jnp.min(total) > SAFE_LOG_DECAY

    @pl.when(fast)
    def _():
        for h in range(HEADS):
            ks = slice(h * DK, (h + 1) * DK)
            vs = slice(h * DV, (h + 1) * DV)
            s = s_ref[h]
            for rows, b in zip(chunks, bs):
                bh = b[:, ks]
                bl = bh[n - 1:n, :]
                kh = k_ref[rows, ks]
                vh = v_ref[rows, vs]
                qp = (q_ref[rows, ks] * jnp.exp(bh)).astype(BF16)
                kp = (kh * jnp.exp(-bh)).astype(BF16)
                kpp = kh * jnp.exp(bl - bh)
                att = lax.dot_general(qp, kp, (((1,), (1,)), ((), ())), preferred_element_type=F32)
                att = jnp.where(r_i >= c_i, att, 0.0).astype(BF16)
                o_ref[rows, vs] = _dot(qp, s.astype(BF16)) + _dot(att, vh)
                d_col = jnp.sum(jnp.where(r_i == c_i, jnp.exp(bl), 0.0), axis=1, keepdims=True)
                s = s * d_col + _dot(kpp.T.astype(BF16), vh)
            s_ref[h] = s

    @pl.when(jnp.logical_not(fast))
    def _():
        lane = lax.broadcasted_iota(jnp.int32, (1, n), 1)
        for rows in chunks:
            vf_ref[...] = v_ref[rows, :].astype(F32)
            for h in range(HEADS):
                ks = slice(h * DK, (h + 1) * DK)
                vs = slice(h * DV, (h + 1) * DV)
                q_t = q_ref[rows, ks].T
                k_t = k_ref[rows, ks].T
                a_t = jnp.exp(la_ref[rows, ks]).T

                def body(t, s):
                    m = lane == t
                    col = lambda x: jnp.sum(jnp.where(m, x, 0.0), axis=1, keepdims=True)
                    s, o_row = _token_step(s, col(a_t), col(k_t), col(q_t), vf_ref[pl.ds(t, 1), vs])
                    o_ref[pl.ds(rows.start + t, 1), vs] = o_row
                    return s

                s_ref[h] = lax.fori_loop(0, n, body, s_ref[h])

    @pl.when(c == pl.num_programs(1) - 1)
    def _():
        sout_ref[0] = s_ref[...]


def _gla_prompt(q, k, la, v, nb, seq):
    nc = seq // GLA_STEP
    tok = lambda w: pl.BlockSpec((GLA_STEP, w), lambda b, c: (b * nc + c, 0))
    return pl.pallas_call(
        _gla_prompt_kernel,
        grid=(nb, nc),
        in_specs=[tok(QK_W), tok(QK_W), tok(QK_W), tok(V_W)],
        out_specs=[tok(V_W), pl.BlockSpec((1, HEADS, DK, DV), lambda b, c: (b, 0, 0, 0))],
        out_shape=[jax.ShapeDtypeStruct((nb * seq, V_W), F32),
                   jax.ShapeDtypeStruct((nb, HEADS, DK, DV), F32)],
        scratch_shapes=[pltpu.VMEM((HEADS, DK, DV), F32), pltpu.VMEM((GLA_CHUNK, V_W), F32)],
        compiler_params=pltpu.CompilerParams(dimension_semantics=("parallel", "arbitrary"),
                                             vmem_limit_bytes=VMEM_LIMIT),
        name="gla_prompt",
    )(q, k, la, v)


def _gla_sample_kernel(q_ref, k_ref, la_ref, v_ref, s_ref, o_ref, sout_ref, *, seq):
    p = SAMPLE_PAD
    row = lax.broadcasted_iota(jnp.int32, (p, 1), 0)

    def padded(x):
        return jnp.concatenate([x, jnp.zeros((p - seq, x.shape[1]), F32)], axis=0)

    for b in range(q_ref.shape[0]):
        q, k, v, g = padded(q_ref[b]), padded(k_ref[b]), padded(v_ref[b].astype(F32)), padded(la_ref[b])
        bc = g
        sh = 1
        while sh < seq:
            bc = bc + jnp.where(row >= sh, pltpu.roll(bc, sh, axis=0), 0.0)
            sh *= 2
        b_last = bc[seq - 1:seq, :]
        bc = jnp.where(row < seq, bc, b_last)
        qp = (q * jnp.exp(bc)).astype(BF16)
        kpp_t = (k * jnp.exp(b_last - bc)).T.astype(BF16)
        d_cols = jnp.broadcast_to(jnp.exp(b_last), (SUBLANES, QK_W)).T
        vb = v.astype(BF16)

        o_in = [jnp.zeros((p, DV), F32) for _ in range(HEADS)]
        for d in range(seq):
            k_d = k if d == 0 else pltpu.roll(k, d, axis=0)
            v_d = v if d == 0 else pltpu.roll(v, d, axis=0)
            b_d = bc if d == 0 else pltpu.roll(bc, d, axis=0)
            pair = q * k_d * jnp.exp(jnp.where(row >= d, bc - b_d, -jnp.inf))
            for h in range(HEADS):
                att = jnp.sum(pair[:, h * DK:(h + 1) * DK], axis=1, keepdims=True)
                o_in[h] = o_in[h] + att * v_d[:, h * DV:(h + 1) * DV]

        for h in range(HEADS):
            ks = slice(h * DK, (h + 1) * DK)
            vs = slice(h * DV, (h + 1) * DV)
            s = s_ref[b, h]
            o = _dot(qp[:, ks], s.astype(BF16)) + o_in[h]
            o_ref[b, :, vs] = o[:seq]
            sout_ref[b, h] = s * d_cols[ks, 0:1] + _dot(kpp_t[ks, :], vb[:, vs])


def _gla_sample(q3, k3, la3, v3, state):
    nb, seq, _ = q3.shape
    assert seq <= SUBLANES
    ns = SAMPLE_GLA_SEQS
    col = pl.BlockSpec((ns, seq, QK_W), lambda b: (b, 0, 0))
    tok = pl.BlockSpec((ns, seq, V_W), lambda b: (b, 0, 0))
    st = pl.BlockSpec((ns, HEADS, DK, DV), lambda b: (b, 0, 0, 0))
    st_in = pl.BlockSpec((None, ns, HEADS, DK, DV), lambda b: (0, b, 0, 0, 0))
    return pl.pallas_call(
        functools.partial(_gla_sample_kernel, seq=seq),
        grid=(nb // ns,),
        in_specs=[col, col, col, tok, st_in],
        out_specs=[tok, st],
        out_shape=[jax.ShapeDtypeStruct((nb, seq, V_W), F32),
                   jax.ShapeDtypeStruct((nb, HEADS, DK, DV), F32)],
        compiler_params=pltpu.CompilerParams(dimension_semantics=("parallel",),
                                             vmem_limit_bytes=VMEM_LIMIT),
        name="gla_sample",
    )(q3, k3, la3, v3, state)


_L_E0, _L_E1, _L_R0, _L_R1 = 0, 1, 2, 3
_GROUP_LANE0 = N_EXPERTS
HALF = D_MODEL // 2


def _pack_rows(x):
    r = x.astype(BF16).astype(F32)
    top = lax.bitcast_convert_type(r[:, :HALF], jnp.uint32)
    bot = lax.bitcast_convert_type(r[:, HALF:], jnp.uint32)
    return top | jnp.right_shift(bot, jnp.uint32(16))


def _unpack_rows(packed):
    top = lax.bitcast_convert_type(packed & jnp.uint32(0xFFFF0000), F32)
    bot = lax.bitcast_convert_type(jnp.left_shift(packed, jnp.uint32(16)), F32)
    return top, bot


def _mix_route_kernel(x_ref, co_ref, o_ref, sr_ref, gc_ref, gg_ref, gn_ref, wgo_ref, wo_ref,
                      g_ref, wr_ref, br_ref, cin_ref,
                      h_ref, xp_ref, mi_ref, mf_ref, cnt_ref, carry_ref):
    i = pl.program_id(0)
    tm = x_ref.shape[0]

    @pl.when(i == 0)
    def _():
        carry_ref[...] = cin_ref[0:1, :]

    gla_out = None
    for h in range(HEADS):
        vs = slice(h * DV, (h + 1) * DV)
        oh = o_ref[:, vs]
        on = oh * lax.rsqrt(jnp.mean(oh * oh, axis=-1, keepdims=True) + EPS) * gn_ref[:, vs]
        part = _dot((on * sr_ref[:, vs]).astype(BF16), wgo_ref[vs, :])
        gla_out = part if gla_out is None else gla_out + part
    mix = (gc_ref[...] * co_ref[...] + gg_ref[...] * gla_out).astype(BF16)
    x = x_ref[...] + _dot(mix, wo_ref[...])
    h_ref[...] = x

    hn = x * lax.rsqrt(jnp.mean(x * x, axis=-1, keepdims=True) + EPS) * g_ref[...]
    hi = hn.astype(BF16)
    hi_f = hi.astype(F32)
    xp_ref[...] = _pack_rows(hn)
    lo = (hn - hi_f).astype(BF16)
    logits = _dot(hi, wr_ref[0]) + (_dot(lo, wr_ref[0]) + _dot(hi, wr_ref[1])) + br_ref[...]
    lane = lax.broadcasted_iota(jnp.int32, (tm, LANES), 1)
    lane_f = lane.astype(F32)
    group_f = jnp.right_shift(lane, 3).astype(F32)
    big = jnp.float32(LANES)
    neg = jnp.float32(-jnp.inf)

    is_g = (lane >= _GROUP_LANE0) & (lane < _GROUP_LANE0 + N_GROUPS)
    lg = jnp.where(is_g, logits, neg)
    mg = jnp.max(lg, axis=1, keepdims=True)
    p_sel = 1.0 / jnp.sum(jnp.exp(lg - mg), axis=1, keepdims=True)
    g_idx = jnp.min(jnp.where(lg == mg, lane_f, big), axis=1, keepdims=True) - _GROUP_LANE0

    is_e = (lane < N_EXPERTS) & (group_f == g_idx)
    le = jnp.where(is_e, logits, neg)
    v0 = jnp.max(le, axis=1, keepdims=True)
    i0 = jnp.min(jnp.where(le == v0, lane_f, big), axis=1, keepdims=True)
    le1 = jnp.where(lane_f == i0, neg, le)
    v1 = jnp.max(le1, axis=1, keepdims=True)
    i1 = jnp.min(jnp.where(le1 == v1, lane_f, big), axis=1, keepdims=True)
    e1 = jnp.exp(v1 - v0)
    den = 1.0 + e1
    w0 = p_sel * (1.0 / den)
    w1 = p_sel * (e1 / den)

    hit0 = lane_f == i0
    hit1 = lane_f == i1
    cnt = jnp.where(hit0 | hit1, 1.0, 0.0)
    r_i = lax.broadcasted_iota(jnp.int32, (tm, tm), 0)
    c_i = lax.broadcasted_iota(jnp.int32, (tm, tm), 1)
    before = jnp.where(r_i > c_i, 1.0, 0.0).astype(BF16)
    seen = _dot(before, cnt.astype(BF16)) + carry_ref[...]
    rank0 = jnp.sum(jnp.where(hit0, seen, 0.0), axis=1, keepdims=True)
    rank1 = jnp.sum(jnp.where(hit1, seen, 0.0), axis=1, keepdims=True)
    carry_ref[...] = carry_ref[...] + jnp.sum(cnt, axis=0, keepdims=True)
    cnt_ref[...] = jnp.broadcast_to(carry_ref[...], cnt_ref.shape)

    rec = jnp.where(lane == _L_E0, i0, 0.0)
    rec = jnp.where(lane == _L_E1, i1, rec)
    rec = jnp.where(lane == _L_R0, rank0, rec)
    rec = jnp.where(lane == _L_R1, rank1, rec)
    mi_ref[...] = rec.T[:SUBLANES, :]
    mf_ref[...] = jnp.where(lane == 0, w0, jnp.where(lane == 1, w1, 0.0))


def _mix_route(x2d, conv_out, o, sr, gc, gg, gn, w_gla_out, w_out, norm_g, w_route, b_route, counts_in):
    n = x2d.shape[0]
    tm = MIX_TILE
    row = lambda w: pl.BlockSpec((tm, w), lambda i: (i, 0))
    full = lambda a: pl.BlockSpec(a.shape, lambda i: (0,) * a.ndim)
    consts = (gn, w_gla_out, w_out, norm_g, w_route, b_route, counts_in)
    return pl.pallas_call(
        _mix_route_kernel,
        grid=(n // tm,),
        in_specs=[row(D_MODEL)] * 6 + [full(a) for a in consts],
        out_specs=[row(D_MODEL), row(HALF), pl.BlockSpec((SUBLANES, tm), lambda i: (0, i)), row(LANES),
                   pl.BlockSpec((SUBLANES, LANES), lambda i: (0, 0))],
        out_shape=[jax.ShapeDtypeStruct((n, D_MODEL), F32),
                   jax.ShapeDtypeStruct((n, HALF), jnp.uint32),
                   jax.ShapeDtypeStruct((SUBLANES, n), F32),
                   jax.ShapeDtypeStruct((n, LANES), F32),
                   jax.ShapeDtypeStruct((SUBLANES, LANES), F32)],
        scratch_shapes=[pltpu.VMEM((1, LANES), F32)],
        compiler_params=pltpu.CompilerParams(dimension_semantics=("arbitrary",),
                                             vmem_limit_bytes=VMEM_LIMIT),
        name="mix_route",
    )(x2d, conv_out, o, sr, gc, gg, *consts)


def _row_copy(src_ref, src_row, dst_ref, dst_row, sem):
    return pltpu.make_async_copy(src_ref.at[pl.ds(src_row, 1), :], dst_ref.at[pl.ds(dst_row, 1), :], sem)


def _dispatch_kernel(p0_ref, p1_ref, xa_ref, xb_ref, xs_ref, sem, *, tiles_a):
    tm = xa_ref.shape[0]
    i = pl.program_id(0)
    base = i * tm

    def scatter(src_ref):
        def rows(wait):
            def body(t, carry):
                c0 = _row_copy(src_ref, t, xs_ref, p0_ref[base + t], sem)
                c1 = _row_copy(src_ref, t, xs_ref, p1_ref[base + t], sem)
                if wait:
                    c0.wait()
                    c1.wait()
                else:
                    c0.start()
                    c1.start()
                return carry
            lax.fori_loop(0, tm, body, 0, unroll=DMA_UNROLL)

        rows(wait=False)
        rows(wait=True)

    @pl.when(i < tiles_a)
    def _():
        scatter(xa_ref)

    @pl.when(i >= tiles_a)
    def _():
        scatter(xb_ref)


def _dispatch(pos0, pos1, x_a, x_b):
    tm = TOK_TILE
    tiles_a, tiles_b = x_a.shape[0] // tm, x_b.shape[0] // tm
    n = x_a.shape[0] + x_b.shape[0]
    width = x_a.shape[1]
    return pl.pallas_call(
        functools.partial(_dispatch_kernel, tiles_a=tiles_a),
        grid_spec=pltpu.PrefetchScalarGridSpec(
            num_scalar_prefetch=2,
            grid=(tiles_a + tiles_b,),
            in_specs=[pl.BlockSpec((tm, width), lambda i, p0, p1: (jnp.minimum(i, tiles_a - 1), 0)),
                      pl.BlockSpec((tm, width), lambda i, p0, p1: (jnp.maximum(i - tiles_a, 0), 0))],
            out_specs=pl.BlockSpec(memory_space=pl.ANY),
            scratch_shapes=[pltpu.SemaphoreType.DMA(())],
        ),
        out_shape=jax.ShapeDtypeStruct((2 * n, width), x_a.dtype),
        compiler_params=pltpu.CompilerParams(dimension_semantics=("arbitrary",)),
        name="dispatch",
    )(pos0, pos1, x_a, x_b)


def _combine_kernel(p0_ref, p1_ref, h_ref, wf_ref, g_ref, ys_ref, out_ref, b0_ref, b1_ref, sems, *, tok_offset):
    tm = h_ref.shape[0]
    i = pl.program_id(0)

    def gather(tile, slot, wait):
        base = tok_offset + tile * tm

        def rows(t, carry):
            c0 = _row_copy(ys_ref, p0_ref[base + t], b0_ref.at[slot], t, sems.at[slot])
            c1 = _row_copy(ys_ref, p1_ref[base + t], b1_ref.at[slot], t, sems.at[slot])
            if wait:
                c0.wait()
                c1.wait()
            else:
                c0.start()
                c1.start()
            return carry

        lax.fori_loop(0, tm, rows, 0, unroll=DMA_UNROLL)

    slot = lax.rem(i, 2)

    @pl.when(i == 0)
    def _():
        gather(i, slot, wait=False)

    @pl.when(i + 1 < pl.num_programs(0))
    def _():
        gather(i + 1, 1 - slot, wait=False)

    gather(i, slot, wait=True)
    top0, bot0 = _unpack_rows(b0_ref[slot])
    top1, bot1 = _unpack_rows(b1_ref[slot])
    w0, w1 = wf_ref[:, 0:1], wf_ref[:, 1:2]
    y = h_ref[...] + jnp.concatenate([top0 * w0 + top1 * w1, bot0 * w0 + bot1 * w1], axis=1)
    out_ref[...] = y * lax.rsqrt(jnp.mean(y * y, axis=-1, keepdims=True) + EPS) * g_ref[...]


def _combine(pos0, pos1, h, wf, norm_g, ys, tok_offset):
    n_rows = h.shape[0]
    tm = TOK_TILE
    return pl.pallas_call(
        functools.partial(_combine_kernel, tok_offset=tok_offset),
        grid_spec=pltpu.PrefetchScalarGridSpec(
            num_scalar_prefetch=2,
            grid=(n_rows // tm,),
            in_specs=[pl.BlockSpec((tm, D_MODEL), lambda i, p0, p1: (i, 0)),
                      pl.BlockSpec((tm, LANES), lambda i, p0, p1: (i, 0)),
                      pl.BlockSpec((1, D_MODEL), lambda i, p0, p1: (0, 0)),
                      pl.BlockSpec(memory_space=pl.ANY)],
            out_specs=pl.BlockSpec((tm, D_MODEL), lambda i, p0, p1: (i, 0)),
            scratch_shapes=[pltpu.VMEM((2, tm, HALF), jnp.uint32), pltpu.VMEM((2, tm, HALF), jnp.uint32),
                            pltpu.SemaphoreType.DMA((2,))],
        ),
        out_shape=jax.ShapeDtypeStruct((n_rows, D_MODEL), F32),
        compiler_params=pltpu.CompilerParams(dimension_semantics=("arbitrary",),
                                             vmem_limit_bytes=VMEM_LIMIT),
        name="combine",
    )(pos0, pos1, h, wf, norm_g, ys)


def _expert_kernel(blk_ref, exp_ref, lo_ref, hi_ref, first_ref, slot_ref, next_ref,
                   xs_ref, wg_hbm, wu_hbm, wd_hbm, ys_ref,
                   wg32_ref, wu32_ref, wd32_ref, wgb_ref, wub_ref, wdb_ref, sems):
    w = pl.program_id(0)
    e = exp_ref[w]
    slot = slot_ref[w]
    changed = jnp.logical_or(w == 0, e != exp_ref[jnp.maximum(w - 1, 0)])

    def weight_copies(expert, s):
        return [pltpu.make_async_copy(src.at[expert], dst.at[s], sems.at[s, j])
                for j, (src, dst) in enumerate(((wg_hbm, wg32_ref), (wu_hbm, wu32_ref), (wd_hbm, wd32_ref)))]

    @pl.when(w == 0)
    def _():
        for c in weight_copies(e, slot):
            c.start()

    @pl.when(changed)
    def _():
        for c in weight_copies(e, slot):
            c.wait()
        wgb_ref[...] = wg32_ref[slot].astype(BF16)
        wub_ref[...] = wu32_ref[slot].astype(BF16)
        wdb_ref[...] = wd32_ref[slot].astype(BF16)

        @pl.when(next_ref[w] >= 0)
        def _():
            for c in weight_copies(next_ref[w], 1 - slot):
                c.start()

    lo = lo_ref[w]
    hi = hi_ref[w]

    @pl.when(hi > lo)
    def _():
        x = jnp.concatenate(_unpack_rows(xs_ref[...]), axis=1).astype(BF16)
        gate = _dot(x, wgb_ref[...])
        hid = (gate * _sigmoid(gate)) * _dot(x, wub_ref[...])
        y = _pack_rows(_dot(hid.astype(BF16), wdb_ref[...]))

        @pl.when(first_ref[w] == 1)
        def _():
            ys_ref[...] = y

        @pl.when(first_ref[w] == 0)
        def _():
            rows = blk_ref[w] * ROW_BLOCK + lax.broadcasted_iota(jnp.int32, (ROW_BLOCK, 1), 0)
            ys_ref[...] = jnp.where((rows >= lo) & (rows < hi), y, ys_ref[...])


def _experts(items, xs, w_gate, w_up, w_down):
    n_rows = xs.shape[0]
    n_items = items[0].shape[0]
    hbm = pl.BlockSpec(memory_space=pl.ANY)
    return pl.pallas_call(
        _expert_kernel,
        grid_spec=pltpu.PrefetchScalarGridSpec(
            num_scalar_prefetch=len(items),
            grid=(n_items,),
            in_specs=[pl.BlockSpec((ROW_BLOCK, HALF), lambda w, blk, *_: (blk[w], 0)), hbm, hbm, hbm],
            out_specs=pl.BlockSpec((ROW_BLOCK, HALF), lambda w, blk, *_: (blk[w], 0)),
            scratch_shapes=[pltpu.VMEM((2, D_MODEL, D_EXPERT), F32), pltpu.VMEM((2, D_MODEL, D_EXPERT), F32),
                            pltpu.VMEM((2, D_EXPERT, D_MODEL), F32),
                            pltpu.VMEM((D_MODEL, D_EXPERT), BF16), pltpu.VMEM((D_MODEL, D_EXPERT), BF16),
                            pltpu.VMEM((D_EXPERT, D_MODEL), BF16),
                            pltpu.SemaphoreType.DMA((2, 3))],
        ),
        out_shape=jax.ShapeDtypeStruct((n_rows, HALF), jnp.uint32),
        compiler_params=pltpu.CompilerParams(dimension_semantics=("arbitrary",),
                                             vmem_limit_bytes=VMEM_LIMIT),
        name="experts",
    )(*items, xs, w_gate, w_up, w_down)


def _work_items(counts, n_rows):
    n_blocks = n_rows // ROW_BLOCK
    n_items = n_blocks + N_EXPERTS - 1
    ends = jnp.cumsum(counts)
    starts = ends - counts
    blk_lo = starts // ROW_BLOCK
    blk_hi = jnp.maximum(ends - 1, 0) // ROW_BLOCK
    per_e = jnp.where(counts > 0, blk_hi - blk_lo + 1, 0)
    item_end = jnp.cumsum(per_e)
    item_start = item_end - per_e
    total = item_end[-1]
    w = jnp.arange(n_items, dtype=jnp.int32)
    live = w < total
    e = jnp.sum((jnp.minimum(w, total - 1)[:, None] >= item_end[None, :]).astype(jnp.int32), axis=1)
    e = jnp.minimum(e, N_EXPERTS - 1)
    sel = (e[:, None] == jnp.arange(N_EXPERTS, dtype=jnp.int32)[None, :]).astype(jnp.int32)
    pick = lambda table: jnp.sum(sel * table[None, :], axis=1)
    blk = jnp.where(live, pick(blk_lo) + (w - pick(item_start)), n_blocks - 1).astype(jnp.int32)
    lo = jnp.where(live, jnp.maximum(pick(starts), blk * ROW_BLOCK), 0).astype(jnp.int32)
    hi = jnp.where(live, jnp.minimum(pick(ends), (blk + 1) * ROW_BLOCK), 0).astype(jnp.int32)
    prev_blk = jnp.concatenate([jnp.full((1,), -1, jnp.int32), blk[:-1]])
    first = (live & (blk != prev_blk)).astype(jnp.int32)
    ids = jnp.arange(N_EXPERTS, dtype=jnp.int32)
    used = counts > 0
    rank_e = jnp.cumsum(used.astype(jnp.int32)) - 1
    later = used[None, :] & (ids[None, :] > ids[:, None])
    next_e = jnp.min(jnp.where(later, ids[None, :], N_EXPERTS), axis=1)
    next_e = jnp.where(next_e < N_EXPERTS, next_e, -1)
    slot = (pick(rank_e) % 2).astype(jnp.int32)
    nxt = pick(next_e).astype(jnp.int32)
    return (blk, e.astype(jnp.int32), lo, hi, first, slot, nxt), starts


def kernel(x_prompt, x_sample, state_conv, state_gla, norm_mix, w_in, b_gates, w_gla_gate_up, b_gla_gate_up, conv_dw, conv_db, conv_ln_g, conv_ln_b, w_conv_out, gla_norm_g, w_gla_out, w_out, norm_ffn, w_router_group, b_router_group, w_router_expert, b_router_expert, w_expert_gate, w_expert_up, w_expert_down, norm_final):
    depth = norm_mix.shape[0]
    assert depth == 1
    l = 0
    bp, seq_p, _ = x_prompt.shape
    bs, seq_s, _ = x_sample.shape
    n_p, n_s = bp * seq_p, bs * seq_s
    n_tok = n_p + n_s
    row2 = lambda a: a.reshape(1, -1)

    wi = w_in[l]
    w_main = wi.astype(BF16)
    w_alow = jnp.pad(w_main[:, _C_ALOW:_C_GATES], ((0, 0), (0, LANES - RANK)))
    w_gates = w_main[:, _C_GATES:]
    w_up_pad = jnp.pad(w_gla_gate_up[l], ((0, LANES - RANK), (0, 0))).astype(BF16)
    w_route = jnp.pad(jnp.concatenate([w_router_expert[l], w_router_group[l]], axis=1),
                      ((0, 0), (0, LANES - N_EXPERTS - N_GROUPS)))
    w_route_hi = w_route.astype(BF16)
    w_route = jnp.stack([w_route_hi, (w_route - w_route_hi.astype(F32)).astype(BF16)])
    b_route = jnp.pad(jnp.concatenate([b_router_expert[l], b_router_group[l]]),
                      (0, LANES - N_EXPERTS - N_GROUPS)).reshape(1, LANES)
    w_pw = w_conv_out[l].astype(BF16)
    w_go = w_gla_out[l].astype(BF16)
    w_o = w_out[l].astype(BF16)
    conv_args = (conv_dw[l], row2(conv_db[l]), row2(conv_ln_g[l]), row2(conv_ln_b[l]), w_pw)
    proj_args = (row2(norm_mix[l]), w_main, w_alow, w_gates, w_up_pad, row2(b_gla_gate_up[l]),
                 row2(b_gates[l]))
    mix_args = (row2(gla_norm_g[l]), w_go, w_o, row2(norm_ffn[l]), w_route, b_route)

    xp = x_prompt.reshape(n_p, D_MODEL)
    conv_out, q, k, v, la, sr, gc, gg, conv_p = _proj_conv(xp, bp, seq_p, proj_args, conv_args)
    o, gla_p = _gla_prompt(q, k, la, v, bp, seq_p)
    h_p, xr_p, rec_p, wf_p, cnt = _mix_route(xp, conv_out, o, sr, gc, gg, *mix_args,
                                             jnp.zeros((SUBLANES, LANES), F32))

    xs_ = x_sample.reshape(n_s, D_MODEL)
    u, q, k, v, la, sr, gc, gg = _proj(xs_, *proj_args)
    conv_out, conv_s = _conv(u.reshape(bs, seq_s, D_CONV), state_conv, *conv_args, nseq=SAMPLE_SEQS)
    seqs = lambda a: a.reshape(bs, seq_s, a.shape[-1])
    o, gla_s = _gla_sample(seqs(q), seqs(k), seqs(la), seqs(v), state_gla)
    h_s, xr_s, rec_s, wf_s, cnt = _mix_route(xs_, conv_out.reshape(n_s, D_MODEL), o.reshape(n_s, V_W), sr, gc, gg,
                                             *mix_args, cnt)

    counts = cnt[0, :N_EXPERTS].astype(jnp.int32)
    items, starts = _work_items(counts, 2 * n_tok)
    rec = jnp.concatenate([rec_p, rec_s], axis=1).astype(jnp.int32)
    e_ids = jnp.arange(N_EXPERTS, dtype=jnp.int32)[:, None]
    start_of = lambda e: jnp.sum(jnp.where(e[None, :] == e_ids, starts[:, None], 0), axis=0)
    pos0 = start_of(rec[_L_E0]) + rec[_L_R0]
    pos1 = start_of(rec[_L_E1]) + rec[_L_R1]
    xs_sorted = _dispatch(pos0, pos1, xr_p, xr_s)
    ys = _experts(items, xs_sorted, w_expert_gate[l], w_expert_up[l], w_expert_down[l])
    y_p = _combine(pos0, pos1, h_p, wf_p, row2(norm_final), ys, 0)
    y_s = _combine(pos0, pos1, h_s, wf_s, row2(norm_final), ys, n_p)

    return (y_p.reshape(bp, seq_p, D_MODEL), y_s.reshape(bs, seq_s, D_MODEL),
            conv_p[None], gla_p[None], conv_s[None], gla_s[None])
```

```python
import functools

import jax
import jax.numpy as jnp
from jax import lax
from jax.experimental import pallas as pl
from jax.experimental.pallas import tpu as pltpu

F32 = jnp.float32
BF16 = jnp.bfloat16

D_MODEL = 1024
D_CONV = 512
CONV_WIDTH = 31
CONV_HIST = CONV_WIDTH - 1
HEADS = 4
DK = 128
DV = 256
QK_W = HEADS * DK
V_W = HEADS * DV
RANK = 16
GATE_NORM = 16.0
N_GROUPS = 4
EXPERTS_PER_GROUP = 8
N_EXPERTS = 32
D_EXPERT = 512
EPS = 1e-6

LANES = 128
SUBLANES = 8
VMEM_LIMIT = 56 * 1024 * 1024

TOK_TILE = 512
PROJ_TILE = 512
MIX_TILE = 512
GLA_CHUNK = 128
GLA_STEP = 512
SAFE_LOG_DECAY = -80.0
CONV_ROWS = 64
CONV_HALO = 40
ROW_BLOCK = 256
DMA_UNROLL = 8
SAMPLE_SEQS = 32
SAMPLE_GLA_SEQS = 8
SAMPLE_PAD = 16

_C_GLU_A, _C_GLU_B = 0, 512
_C_Q, _C_K, _C_V, _C_R = 1024, 1536, 2048, 3072
_C_ALOW = 4096
_C_GATES = _C_ALOW + RANK
_MAIN_W = 1


def _sigmoid(x):
    return jax.nn.sigmoid(x)


def _dot(a, b):
    return jnp.dot(a, b, preferred_element_type=F32)


def _normed_input(x_ref, g_ref):
    x = x_ref[...]
    hn = x * lax.rsqrt(jnp.mean(x * x, axis=-1, keepdims=True) + EPS) * g_ref[...]
    return hn.astype(BF16)


def _glu(hb, wm_ref):
    return _dot(hb, wm_ref[:, _C_GLU_A:_C_GLU_B]) * _sigmoid(_dot(hb, wm_ref[:, _C_GLU_B:_C_Q]))


def _projection_steps(hb, wm_ref, wa_ref, wg_ref, wup_ref, bup_ref, bg_ref,
                      q_ref, k_ref, v_ref, la_ref, sr_ref, gc_ref, gg_ref):
    def mm(lo, hi):
        return _dot(hb, wm_ref[:, lo:hi])

    def q_step():
        q_ref[...] = mm(_C_Q, _C_K) * (DK ** -0.5)

    def k_step():
        k_ref[...] = mm(_C_K, _C_V)

    def v_step():
        v_ref[...] = mm(_C_V, _C_R).astype(BF16)

    def r_step():
        r = mm(_C_R, _C_ALOW)
        sr_ref[...] = (r * _sigmoid(r)).astype(BF16)

    def gc_step():
        gc_ref[...] = _sigmoid(_dot(hb, wg_ref[:, :D_MODEL]) + bg_ref[:, :D_MODEL]).astype(BF16)

    def gg_step():
        gg_ref[...] = _sigmoid(_dot(hb, wg_ref[:, D_MODEL:]) + bg_ref[:, D_MODEL:]).astype(BF16)

    def decay_step():
        a_low = _dot(hb, wa_ref[...])
        z = _dot(a_low.astype(BF16), wup_ref[...]) + bup_ref[...]
        la_ref[...] = (jnp.minimum(z, 0.0) - jnp.log1p(jnp.exp(-jnp.abs(z)))) * (1.0 / GATE_NORM)

    return [q_step, k_step, v_step, r_step, gc_step, gg_step, decay_step]


def _proj_kernel(x_ref, g_ref, wm_ref, wa_ref, wg_ref, wup_ref, bup_ref, bg_ref,
                 u_ref, q_ref, k_ref, v_ref, la_ref, sr_ref, gc_ref, gg_ref):
    hb = _normed_input(x_ref, g_ref)
    u_ref[...] = _glu(hb, wm_ref)
    for step in _projection_steps(hb, wm_ref, wa_ref, wg_ref, wup_ref, bup_ref, bg_ref,
                                  q_ref, k_ref, v_ref, la_ref, sr_ref, gc_ref, gg_ref):
        step()


_HIST_PAD = 32
_HIST0 = _HIST_PAD - CONV_HIST


def _conv_taps(w, dw_ref, cs, rt):
    win = w.shape[0]
    acc = jnp.zeros((rt, LANES), F32)
    for s in range(SUBLANES):
        ws = w if s == 0 else pltpu.roll(w, win - s, axis=0)
        for a in range(CONV_HALO // SUBLANES):
            j = a * SUBLANES + s - _HIST0
            if 0 <= j < CONV_WIDTH:
                acc = acc + ws[a * SUBLANES:a * SUBLANES + rt, :] * dw_ref[j:j + 1, cs]
    return acc


def _norm_swish_pointwise(y, lg_ref, lb_ref, wpw_ref):
    mu = jnp.mean(y, axis=-1, keepdims=True)
    var = jnp.mean(jnp.square(y - mu), axis=-1, keepdims=True)
    yn = (y - mu) * lax.rsqrt(var + EPS) * lg_ref[...] + lb_ref[...]
    return _dot((yn * _sigmoid(yn)).astype(BF16), wpw_ref[...])


def _proj_conv_kernel(x_ref, g_ref, wm_ref, wa_ref, wg_ref, wup_ref, bup_ref, bg_ref,
                      dw_ref, db_ref, lg_ref, lb_ref, wpw_ref,
                      co_ref, q_ref, k_ref, v_ref, la_ref, sr_ref, gc_ref, gg_ref, ns_ref,
                      full_ref, y_ref):
    c = pl.program_id(1)
    tm = x_ref.shape[0]

    @pl.when(c == 0)
    def _():
        full_ref[0:_HIST_PAD, :] = jnp.zeros((_HIST_PAD, D_CONV), F32)

    @pl.when(c > 0)
    def _():
        full_ref[0:_HIST_PAD, :] = full_ref[tm:tm + _HIST_PAD, :]

    hb = _normed_input(x_ref, g_ref)
    full_ref[_HIST_PAD:_HIST_PAD + tm, :] = _glu(hb, wm_ref)
    full_ref[_HIST_PAD + tm:, :] = jnp.zeros((SUBLANES, D_CONV), F32)
    ns_ref[0] = full_ref[tm + _HIST0:tm + _HIST_PAD, :]
    rt = CONV_ROWS

    def conv_tile(i, cc):
        def run():
            cs = slice(cc * LANES, (cc + 1) * LANES)
            acc = _conv_taps(full_ref[i * rt:i * rt + rt + CONV_HALO, cs], dw_ref, cs, rt)
            y_ref[i * rt:(i + 1) * rt, cs] = acc + db_ref[:, cs]
        return run

    conv_tiles = [conv_tile(i, cc) for i in range(tm // rt) for cc in range(D_CONV // LANES)]
    proj_steps = _projection_steps(hb, wm_ref, wa_ref, wg_ref, wup_ref, bup_ref, bg_ref,
                                   q_ref, k_ref, v_ref, la_ref, sr_ref, gc_ref, gg_ref)
    per_step = -(-len(conv_tiles) // len(proj_steps))
    for n, step in enumerate(proj_steps):
        step()
        for tile in conv_tiles[n * per_step:(n + 1) * per_step]:
            tile()
    co_ref[...] = _norm_swish_pointwise(y_ref[...], lg_ref, lb_ref, wpw_ref)


def _proj_conv(x2d, nb, seq, proj_consts, conv_consts):
    tm = PROJ_TILE
    nc = seq // tm
    row = lambda w: pl.BlockSpec((tm, w), lambda b, c: (b * nc + c, 0))
    full = lambda a: pl.BlockSpec(a.shape, lambda b, c: (0,) * a.ndim)
    widths = (D_MODEL, QK_W, QK_W, V_W, QK_W, V_W, D_MODEL, D_MODEL)
    dtypes = (F32, F32, F32, BF16, F32, BF16, BF16, BF16)
    consts = tuple(proj_consts) + tuple(conv_consts)
    n = nb * seq
    const_specs = [full(a) for a in consts]
    const_specs[_MAIN_W] = pl.BlockSpec((D_MODEL, _C_ALOW), lambda b, c: (0, 0))
    return pl.pallas_call(
        _proj_conv_kernel,
        grid=(nb, nc),
        in_specs=[row(D_MODEL)] + const_specs,
        out_specs=[row(w) for w in widths] + [pl.BlockSpec((1, CONV_HIST, D_CONV), lambda b, c: (b, 0, 0))],
        out_shape=[jax.ShapeDtypeStruct((n, w), dt) for w, dt in zip(widths, dtypes)]
        + [jax.ShapeDtypeStruct((nb, CONV_HIST, D_CONV), F32)],
        scratch_shapes=[pltpu.VMEM((_HIST_PAD + tm + SUBLANES, D_CONV), F32), pltpu.VMEM((tm, D_CONV), F32)],
        compiler_params=pltpu.CompilerParams(dimension_semantics=("parallel", "arbitrary"),
                                             vmem_limit_bytes=VMEM_LIMIT),
        name="proj_conv",
    )(x2d, *consts)


def _proj(x2d, norm_g, w_main, w_alow, w_gates, w_up, b_up, b_gates):
    n = x2d.shape[0]
    tm = TOK_TILE
    row = lambda w: pl.BlockSpec((tm, w), lambda i: (i, 0))
    full = lambda a: pl.BlockSpec(a.shape, lambda i: (0,) * a.ndim)
    widths = (D_CONV, QK_W, QK_W, V_W, QK_W, V_W, D_MODEL, D_MODEL)
    dtypes = (F32, F32, F32, BF16, F32, BF16, BF16, BF16)
    consts = (norm_g, w_main, w_alow, w_gates, w_up, b_up, b_gates)
    const_specs = [full(a) for a in consts]
    const_specs[_MAIN_W] = pl.BlockSpec((D_MODEL, _C_ALOW), lambda i: (0, 0))
    return pl.pallas_call(
        _proj_kernel,
        grid=(n // tm,),
        in_specs=[row(D_MODEL)] + const_specs,
        out_specs=[row(w) for w in widths],
        out_shape=[jax.ShapeDtypeStruct((n, w), dt) for w, dt in zip(widths, dtypes)],
        compiler_params=pltpu.CompilerParams(dimension_semantics=("parallel",),
                                             vmem_limit_bytes=VMEM_LIMIT),
        name="proj",
    )(x2d, *consts)


def _conv_kernel(u_ref, st_ref, dw_ref, db_ref, lg_ref, lb_ref, wpw_ref, out_ref, ns_ref, full_ref, y_ref, *, seq):
    nseq = u_ref.shape[0]
    n_rows = y_ref.shape[0] // nseq
    tail = full_ref.shape[0] - (_HIST_PAD + seq)

    def one_seq(b, carry):
        full_ref[0:_HIST_PAD, :] = jnp.zeros((_HIST_PAD, D_CONV), F32)
        full_ref[_HIST_PAD + seq:, :] = jnp.zeros((tail, D_CONV), F32)
        full_ref[_HIST0:_HIST_PAD, :] = st_ref[b]
        full_ref[_HIST_PAD:_HIST_PAD + seq, :] = u_ref[b]
        ns_ref[b] = full_ref[seq + _HIST0:seq + _HIST_PAD, :]
        for c in range(D_CONV // LANES):
            cs = slice(c * LANES, (c + 1) * LANES)
            acc = _conv_taps(full_ref[:, cs], dw_ref, cs, n_rows)
            y_ref[pl.ds(pl.multiple_of(b * n_rows, SUBLANES), n_rows), cs] = acc + db_ref[:, cs]
        return carry

    lax.fori_loop(0, nseq, one_seq, 0)
    res = _norm_swish_pointwise(y_ref[...], lg_ref, lb_ref, wpw_ref)
    for j in range(nseq):
        out_ref[j] = res[j * n_rows:j * n_rows + seq]


def _conv(u3, state, dw, db, lg, lb, w_pw, nseq):
    nb, seq, _ = u3.shape
    n_rows = -(-seq // SUBLANES) * SUBLANES
    full = lambda a: pl.BlockSpec(a.shape, lambda b: (0,) * a.ndim)
    per_b = lambda r, w: pl.BlockSpec((nseq, r, w), lambda b: (b, 0, 0))
    return pl.pallas_call(
        functools.partial(_conv_kernel, seq=seq),
        grid=(nb // nseq,),
        in_specs=[per_b(seq, D_CONV), pl.BlockSpec((None, nseq, CONV_HIST, D_CONV), lambda b: (0, b, 0, 0)),
                  full(dw), full(db), full(lg), full(lb), full(w_pw)],
        out_specs=[per_b(seq, D_MODEL), per_b(CONV_HIST, D_CONV)],
        out_shape=[jax.ShapeDtypeStruct((nb, seq, D_MODEL), F32),
                   jax.ShapeDtypeStruct((nb, CONV_HIST, D_CONV), F32)],
        scratch_shapes=[pltpu.VMEM((n_rows + CONV_HALO, D_CONV), F32),
                        pltpu.VMEM((nseq * n_rows, D_CONV), F32)],
        compiler_params=pltpu.CompilerParams(dimension_semantics=("parallel",),
                                             vmem_limit_bytes=VMEM_LIMIT),
        name="conv_state",
    )(u3, state, dw, db, lg, lb, w_pw)


def _token_step(s, a_col, k_col, q_col, v_row):
    s = s * a_col + k_col * v_row
    return s, jnp.sum(q_col * s, axis=0, keepdims=True)


def _gla_prompt_kernel(q_ref, k_ref, la_ref, v_ref, o_ref, sout_ref, s_ref, vf_ref):
    c = pl.program_id(1)
    n = GLA_CHUNK
    chunks = [slice(i * n, (i + 1) * n) for i in range(q_ref.shape[0] // n)]

    @pl.when(c == 0)
    def _():
        s_ref[...] = jnp.zeros_like(s_ref)

    r_i = lax.broadcasted_iota(jnp.int32, (n, n), 0)
    c_i = lax.broadcasted_iota(jnp.int32, (n, n), 1)
    tri = jnp.where(r_i >= c_i, 1.0, 0.0).astype(BF16)

    def prefix_sum(g):
        hi = g.astype(BF16)
        lo = (g - hi.astype(F32)).astype(BF16)
        return _dot(tri, hi) + _dot(tri, lo)

    bs = [prefix_sum(la_ref[rows, :]) for rows in chunks]
    total = bs[0][n - 1:n, :]
    for b in bs[1:]:
        total = jnp.minimum(total, b[n - 1:n, :])
    fast = jnp.min(total) > SAFE_LOG_DECAY

    @pl.when(fast)
    def _():
        for h in range(HEADS):
            ks = slice(h * DK, (h + 1) * DK)
            vs = slice(h * DV, (h + 1) * DV)
            s = s_ref[h]
            for rows, b in zip(chunks, bs):
                bh = b[:, ks]
                bl = bh[n - 1:n, :]
                kh = k_ref[rows, ks]
                vh = v_ref[rows, vs]
                qp = (q_ref[rows, ks] * jnp.exp(bh)).astype(BF16)
                kp = (kh * jnp.exp(-bh)).astype(BF16)
                kpp = kh * jnp.exp(bl - bh)
                att = lax.dot_general(qp, kp, (((1,), (1,)), ((), ())), preferred_element_type=F32)
                att = jnp.where(r_i >= c_i, att, 0.0).astype(BF16)
                o_ref[rows, vs] = _dot(qp, s.astype(BF16)) + _dot(att, vh)
                d_col = jnp.sum(jnp.where(r_i == c_i, jnp.exp(bl), 0.0), axis=1, keepdims=True)
                s = s * d_col + _dot(kpp.T.astype(BF16), vh)
            s_ref[h] = s

    @pl.when(jnp.logical_not(fast))
    def _():
        lane = lax.broadcasted_iota(jnp.int32, (1, n), 1)
        for rows in chunks:
            vf_ref[...] = v_ref[rows, :].astype(F32)
            for h in range(HEADS):
                ks = slice(h * DK, (h + 1) * DK)
                vs = slice(h * DV, (h + 1) * DV)
                q_t = q_ref[rows, ks].T
                k_t = k_ref[rows, ks].T
                a_t = jnp.exp(la_ref[rows, ks]).T

                def body(t, s):
                    m = lane == t
                    col = lambda x: jnp.sum(jnp.where(m, x, 0.0), axis=1, keepdims=True)
                    s, o_row = _token_step(s, col(a_t), col(k_t), col(q_t), vf_ref[pl.ds(t, 1), vs])
                    o_ref[pl.ds(rows.start + t, 1), vs] = o_row
                    return s

                s_ref[h] = lax.fori_loop(0, n, body, s_ref[h])

    @pl.when(c == pl.num_programs(1) - 1)
    def _():
        sout_ref[0] = s_ref[...]


def _gla_prompt(q, k, la, v, nb, seq):
    nc = seq // GLA_STEP
    tok = lambda w: pl.BlockSpec((GLA_STEP, w), lambda b, c: (b * nc + c, 0))
    return pl.pallas_call(
        _gla_prompt_kernel,
        grid=(nb, nc),
        in_specs=[tok(QK_W), tok(QK_W), tok(QK_W), tok(V_W)],
        out_specs=[tok(V_W), pl.BlockSpec((1, HEADS, DK, DV), lambda b, c: (b, 0, 0, 0))],
        out_shape=[jax.ShapeDtypeStruct((nb * seq, V_W), F32),
                   jax.ShapeDtypeStruct((nb, HEADS, DK, DV), F32)],
        scratch_shapes=[pltpu.VMEM((HEADS, DK, DV), F32), pltpu.VMEM((GLA_CHUNK, V_W), F32)],
        compiler_params=pltpu.CompilerParams(dimension_semantics=("parallel", "arbitrary"),
                                             vmem_limit_bytes=VMEM_LIMIT),
        name="gla_prompt",
    )(q, k, la, v)


def _gla_sample_kernel(q_ref, k_ref, la_ref, v_ref, s_ref, o_ref, sout_ref, *, seq):
    p = SAMPLE_PAD
    row = lax.broadcasted_iota(jnp.int32, (p, 1), 0)

    def padded(x):
        return jnp.concatenate([x, jnp.zeros((p - seq, x.shape[1]), F32)], axis=0)

    for b in range(q_ref.shape[0]):
        q, k, v, g = padded(q_ref[b]), padded(k_ref[b]), padded(v_ref[b].astype(F32)), padded(la_ref[b])
        bc = g
        sh = 1
        while sh < seq:
            bc = bc + jnp.where(row >= sh, pltpu.roll(bc, sh, axis=0), 0.0)
            sh *= 2
        b_last = bc[seq - 1:seq, :]
        bc = jnp.where(row < seq, bc, b_last)
        qp = (q * jnp.exp(bc)).astype(BF16)
        kpp_t = (k * jnp.exp(b_last - bc)).T.astype(BF16)
        d_cols = jnp.broadcast_to(jnp.exp(b_last), (SUBLANES, QK_W)).T
        vb = v.astype(BF16)

        o_in = [jnp.zeros((p, DV), F32) for _ in range(HEADS)]
        for d in range(seq):
            k_d = k if d == 0 else pltpu.roll(k, d, axis=0)
            v_d = v if d == 0 else pltpu.roll(v, d, axis=0)
            b_d = bc if d == 0 else pltpu.roll(bc, d, axis=0)
            pair = q * k_d * jnp.exp(jnp.where(row >= d, bc - b_d, -jnp.inf))
            for h in range(HEADS):
                att = jnp.sum(pair[:, h * DK:(h + 1) * DK], axis=1, keepdims=True)
                o_in[h] = o_in[h] + att * v_d[:, h * DV:(h + 1) * DV]

        for h in range(HEADS):
            ks = slice(h * DK, (h + 1) * DK)
            vs = slice(h * DV, (h + 1) * DV)
            s = s_ref[b, h]
            o = _dot(qp[:, ks], s.astype(BF16)) + o_in[h]
            o_ref[b, :, vs] = o[:seq]
            sout_ref[b, h] = s * d_cols[ks, 0:1] + _dot(kpp_t[ks, :], vb[:, vs])


def _gla_sample(q3, k3, la3, v3, state):
    nb, seq, _ = q3.shape
    assert seq <= SUBLANES
    ns = SAMPLE_GLA_SEQS
    col = pl.BlockSpec((ns, seq, QK_W), lambda b: (b, 0, 0))
    tok = pl.BlockSpec((ns, seq, V_W), lambda b: (b, 0, 0))
    st = pl.BlockSpec((ns, HEADS, DK, DV), lambda b: (b, 0, 0, 0))
    st_in = pl.BlockSpec((None, ns, HEADS, DK, DV), lambda b: (0, b, 0, 0, 0))
    return pl.pallas_call(
        functools.partial(_gla_sample_kernel, seq=seq),
        grid=(nb // ns,),
        in_specs=[col, col, col, tok, st_in],
        out_specs=[tok, st],
        out_shape=[jax.ShapeDtypeStruct((nb, seq, V_W), F32),
                   jax.ShapeDtypeStruct((nb, HEADS, DK, DV), F32)],
        compiler_params=pltpu.CompilerParams(dimension_semantics=("parallel",),
                                             vmem_limit_bytes=VMEM_LIMIT),
        name="gla_sample",
    )(q3, k3, la3, v3, state)


_L_E0, _L_E1, _L_R0, _L_R1 = 0, 1, 2, 3
_GROUP_LANE0 = N_EXPERTS
HALF = D_MODEL // 2


def _pack_rows(x):
    r = x.astype(BF16).astype(F32)
    top = lax.bitcast_convert_type(r[:, :HALF], jnp.uint32)
    bot = lax.bitcast_convert_type(r[:, HALF:], jnp.uint32)
    return top | jnp.right_shift(bot, jnp.uint32(16))


def _unpack_rows(packed):
    top = lax.bitcast_convert_type(packed & jnp.uint32(0xFFFF0000), F32)
    bot = lax.bitcast_convert_type(jnp.left_shift(packed, jnp.uint32(16)), F32)
    return top, bot


def _mix_route_kernel(x_ref, co_ref, o_ref, sr_ref, gc_ref, gg_ref, gn_ref, wgo_ref, wo_ref,
                      g_ref, wr_ref, br_ref, cin_ref,
                      h_ref, xp_ref, mi_ref, mf_ref, cnt_ref, carry_ref):
    i = pl.program_id(0)
    tm = x_ref.shape[0]

    @pl.when(i == 0)
    def _():
        carry_ref[...] = cin_ref[0:1, :]

    gla_out = None
    for h in range(HEADS):
        vs = slice(h * DV, (h + 1) * DV)
        oh = o_ref[:, vs]
        on = oh * lax.rsqrt(jnp.mean(oh * oh, axis=-1, keepdims=True) + EPS) * gn_ref[:, vs]
        part = _dot((on * sr_ref[:, vs]).astype(BF16), wgo_ref[vs, :])
        gla_out = part if gla_out is None else gla_out + part
    mix = (gc_ref[...] * co_ref[...] + gg_ref[...] * gla_out).astype(BF16)
    x = x_ref[...] + _dot(mix, wo_ref[...])
    h_ref[...] = x

    hn = x * lax.rsqrt(jnp.mean(x * x, axis=-1, keepdims=True) + EPS) * g_ref[...]
    hi = hn.astype(BF16)
    hi_f = hi.astype(F32)
    xp_ref[...] = _pack_rows(hn)
    lo = (hn - hi_f).astype(BF16)
    logits = _dot(hi, wr_ref[0]) + (_dot(lo, wr_ref[0]) + _dot(hi, wr_ref[1])) + br_ref[...]
    lane = lax.broadcasted_iota(jnp.int32, (tm, LANES), 1)
    lane_f = lane.astype(F32)
    group_f = jnp.right_shift(lane, 3).astype(F32)
    big = jnp.float32(LANES)
    neg = jnp.float32(-jnp.inf)

    is_g = (lane >= _GROUP_LANE0) & (lane < _GROUP_LANE0 + N_GROUPS)
    lg = jnp.where(is_g, logits, neg)
    mg = jnp.max(lg, axis=1, keepdims=True)
    p_sel = 1.0 / jnp.sum(jnp.exp(lg - mg), axis=1, keepdims=True)
    g_idx = jnp.min(jnp.where(lg == mg, lane_f, big), axis=1, keepdims=True) - _GROUP_LANE0

    is_e = (lane < N_EXPERTS) & (group_f == g_idx)
    le = jnp.where(is_e, logits, neg)
    v0 = jnp.max(le, axis=1, keepdims=True)
    i0 = jnp.min(jnp.where(le == v0, lane_f, big), axis=1, keepdims=True)
    le1 = jnp.where(lane_f == i0, neg, le)
    v1 = jnp.max(le1, axis=1, keepdims=True)
    i1 = jnp.min(jnp.where(le1 == v1, lane_f, big), axis=1, keepdims=True)
    e1 = jnp.exp(v1 - v0)
    den = 1.0 + e1
    w0 = p_sel * (1.0 / den)
    w1 = p_sel * (e1 / den)

    hit0 = lane_f == i0
    hit1 = lane_f == i1
    cnt = jnp.where(hit0 | hit1, 1.0, 0.0)
    r_i = lax.broadcasted_iota(jnp.int32, (tm, tm), 0)
    c_i = lax.broadcasted_iota(jnp.int32, (tm, tm), 1)
    before = jnp.where(r_i > c_i, 1.0, 0.0).astype(BF16)
    seen = _dot(before, cnt.astype(BF16)) + carry_ref[...]
    rank0 = jnp.sum(jnp.where(hit0, seen, 0.0), axis=1, keepdims=True)
    rank1 = jnp.sum(jnp.where(hit1, seen, 0.0), axis=1, keepdims=True)
    carry_ref[...] = carry_ref[...] + jnp.sum(cnt, axis=0, keepdims=True)
    cnt_ref[...] = jnp.broadcast_to(carry_ref[...], cnt_ref.shape)

    rec = jnp.where(lane == _L_E0, i0, 0.0)
    rec = jnp.where(lane == _L_E1, i1, rec)
    rec = jnp.where(lane == _L_R0, rank0, rec)
    rec = jnp.where(lane == _L_R1, rank1, rec)
    mi_ref[...] = rec.T[:SUBLANES, :]
    mf_ref[...] = jnp.where(lane == 0, w0, jnp.where(lane == 1, w1, 0.0))


def _mix_route(x2d, conv_out, o, sr, gc, gg, gn, w_gla_out, w_out, norm_g, w_route, b_route, counts_in):
    n = x2d.shape[0]
    tm = MIX_TILE
    row = lambda w: pl.BlockSpec((tm, w), lambda i: (i, 0))
    full = lambda a: pl.BlockSpec(a.shape, lambda i: (0,) * a.ndim)
    consts = (gn, w_gla_out, w_out, norm_g, w_route, b_route, counts_in)
    return pl.pallas_call(
        _mix_route_kernel,
        grid=(n // tm,),
        in_specs=[row(D_MODEL)] * 6 + [full(a) for a in consts],
        out_specs=[row(D_MODEL), row(HALF), pl.BlockSpec((SUBLANES, tm), lambda i: (0, i)), row(LANES),
                   pl.BlockSpec((SUBLANES, LANES), lambda i: (0, 0))],
        out_shape=[jax.ShapeDtypeStruct((n, D_MODEL), F32),
                   jax.ShapeDtypeStruct((n, HALF), jnp.uint32),
                   jax.ShapeDtypeStruct((SUBLANES, n), F32),
                   jax.ShapeDtypeStruct((n, LANES), F32),
                   jax.ShapeDtypeStruct((SUBLANES, LANES), F32)],
        scratch_shapes=[pltpu.VMEM((1, LANES), F32)],
        compiler_params=pltpu.CompilerParams(dimension_semantics=("arbitrary",),
                                             vmem_limit_bytes=VMEM_LIMIT),
        name="mix_route",
    )(x2d, conv_out, o, sr, gc, gg, *consts)


def _row_copy(src_ref, src_row, dst_ref, dst_row, sem):
    return pltpu.make_async_copy(src_ref.at[pl.ds(src_row, 1), :], dst_ref.at[pl.ds(dst_row, 1), :], sem)


def _dispatch_kernel(p0_ref, p1_ref, xa_ref, xb_ref, xs_ref, sem, *, tiles_a):
    tm = xa_ref.shape[0]
    i = pl.program_id(0)
    base = i * tm

    def scatter(src_ref):
        def rows(wait):
            def body(t, carry):
                c0 = _row_copy(src_ref, t, xs_ref, p0_ref[base + t], sem)
                c1 = _row_copy(src_ref, t, xs_ref, p1_ref[base + t], sem)
                if wait:
                    c0.wait()
                    c1.wait()
                else:
                    c0.start(priority=0)
                    c1.start(priority=1)
                return carry
            lax.fori_loop(0, tm, body, 0, unroll=DMA_UNROLL)

        rows(wait=False)
        rows(wait=True)

    @pl.when(i < tiles_a)
    def _():
        scatter(xa_ref)

    @pl.when(i >= tiles_a)
    def _():
        scatter(xb_ref)


def _dispatch(pos0, pos1, x_a, x_b):
    tm = TOK_TILE
    tiles_a, tiles_b = x_a.shape[0] // tm, x_b.shape[0] // tm
    n = x_a.shape[0] + x_b.shape[0]
    width = x_a.shape[1]
    return pl.pallas_call(
        functools.partial(_dispatch_kernel, tiles_a=tiles_a),
        grid_spec=pltpu.PrefetchScalarGridSpec(
            num_scalar_prefetch=2,
            grid=(tiles_a + tiles_b,),
            in_specs=[pl.BlockSpec((tm, width), lambda i, p0, p1: (jnp.minimum(i, tiles_a - 1), 0)),
                      pl.BlockSpec((tm, width), lambda i, p0, p1: (jnp.maximum(i - tiles_a, 0), 0))],
            out_specs=pl.BlockSpec(memory_space=pl.ANY),
            scratch_shapes=[pltpu.SemaphoreType.DMA(())],
        ),
        out_shape=jax.ShapeDtypeStruct((2 * n, width), x_a.dtype),
        compiler_params=pltpu.CompilerParams(dimension_semantics=("arbitrary",)),
        name="dispatch",
    )(pos0, pos1, x_a, x_b)


def _combine_kernel(p0_ref, p1_ref, h_ref, wf_ref, g_ref, ys_ref, out_ref, b0_ref, b1_ref, sems, *, tok_offset):
    tm = h_ref.shape[0]
    i = pl.program_id(0)

    def gather(tile, slot, wait):
        base = tok_offset + tile * tm

        def rows(t, carry):
            c0 = _row_copy(ys_ref, p0_ref[base + t], b0_ref.at[slot], t, sems.at[slot])
            c1 = _row_copy(ys_ref, p1_ref[base + t], b1_ref.at[slot], t, sems.at[slot])
            if wait:
                c0.wait()
                c1.wait()
            else:
                c0.start(priority=0)
                c1.start(priority=1)
            return carry

        lax.fori_loop(0, tm, rows, 0, unroll=DMA_UNROLL)

    slot = lax.rem(i, 2)

    @pl.when(i == 0)
    def _():
        gather(i, slot, wait=False)

    @pl.when(i + 1 < pl.num_programs(0))
    def _():
        gather(i + 1, 1 - slot, wait=False)

    gather(i, slot, wait=True)
    top0, bot0 = _unpack_rows(b0_ref[slot])
    top1, bot1 = _unpack_rows(b1_ref[slot])
    w0, w1 = wf_ref[:, 0:1], wf_ref[:, 1:2]
    y = h_ref[...] + jnp.concatenate([top0 * w0 + top1 * w1, bot0 * w0 + bot1 * w1], axis=1)
    out_ref[...] = y * lax.rsqrt(jnp.mean(y * y, axis=-1, keepdims=True) + EPS) * g_ref[...]


def _combine(pos0, pos1, h, wf, norm_g, ys, tok_offset):
    n_rows = h.shape[0]
    tm = TOK_TILE
    return pl.pallas_call(
        functools.partial(_combine_kernel, tok_offset=tok_offset),
        grid_spec=pltpu.PrefetchScalarGridSpec(
            num_scalar_prefetch=2,
            grid=(n_rows // tm,),
            in_specs=[pl.BlockSpec((tm, D_MODEL), lambda i, p0, p1: (i, 0)),
                      pl.BlockSpec((tm, LANES), lambda i, p0, p1: (i, 0)),
                      pl.BlockSpec((1, D_MODEL), lambda i, p0, p1: (0, 0)),
                      pl.BlockSpec(memory_space=pl.ANY)],
            out_specs=pl.BlockSpec((tm, D_MODEL), lambda i, p0, p1: (i, 0)),
            scratch_shapes=[pltpu.VMEM((2, tm, HALF), jnp.uint32), pltpu.VMEM((2, tm, HALF), jnp.uint32),
                            pltpu.SemaphoreType.DMA((2,))],
        ),
        out_shape=jax.ShapeDtypeStruct((n_rows, D_MODEL), F32),
        compiler_params=pltpu.CompilerParams(dimension_semantics=("arbitrary",),
                                             vmem_limit_bytes=VMEM_LIMIT),
        name="combine",
    )(pos0, pos1, h, wf, norm_g, ys)


def _expert_kernel(blk_ref, exp_ref, lo_ref, hi_ref, first_ref, slot_ref, next_ref,
                   xs_ref, wg_hbm, wu_hbm, wd_hbm, ys_ref,
                   wg32_ref, wu32_ref, wd32_ref, wgb_ref, wub_ref, wdb_ref, sems):
    w = pl.program_id(0)
    e = exp_ref[w]
    slot = slot_ref[w]
    changed = jnp.logical_or(w == 0, e != exp_ref[jnp.maximum(w - 1, 0)])

    def weight_copies(expert, s):
        return [pltpu.make_async_copy(src.at[expert], dst.at[s], sems.at[s, j])
                for j, (src, dst) in enumerate(((wg_hbm, wg32_ref), (wu_hbm, wu32_ref), (wd_hbm, wd32_ref)))]

    @pl.when(w == 0)
    def _():
        for c in weight_copies(e, slot):
            c.start()

    @pl.when(changed)
    def _():
        for c in weight_copies(e, slot):
            c.wait()
        wgb_ref[...] = wg32_ref[slot].astype(BF16)
        wub_ref[...] = wu32_ref[slot].astype(BF16)
        wdb_ref[...] = wd32_ref[slot].astype(BF16)

        @pl.when(next_ref[w] >= 0)
        def _():
            for c in weight_copies(next_ref[w], 1 - slot):
                c.start()

    lo = lo_ref[w]
    hi = hi_ref[w]

    @pl.when(hi > lo)
    def _():
        x = jnp.concatenate(_unpack_rows(xs_ref[...]), axis=1).astype(BF16)
        gate = _dot(x, wgb_ref[...])
        hid = (gate * _sigmoid(gate)) * _dot(x, wub_ref[...])
        y = _pack_rows(_dot(hid.astype(BF16), wdb_ref[...]))

        @pl.when(first_ref[w] == 1)
        def _():
            ys_ref[...] = y

        @pl.when(first_ref[w] == 0)
        def _():
            rows = blk_ref[w] * ROW_BLOCK + lax.broadcasted_iota(jnp.int32, (ROW_BLOCK, 1), 0)
            ys_ref[...] = jnp.where((rows >= lo) & (rows < hi), y, ys_ref[...])


def _experts(items, xs, w_gate, w_up, w_down):
    n_rows = xs.shape[0]
    n_items = items[0].shape[0]
    hbm = pl.BlockSpec(memory_space=pl.ANY)
    return pl.pallas_call(
        _expert_kernel,
        grid_spec=pltpu.PrefetchScalarGridSpec(
            num_scalar_prefetch=len(items),
            grid=(n_items,),
            in_specs=[pl.BlockSpec((ROW_BLOCK, HALF), lambda w, blk, *_: (blk[w], 0)), hbm, hbm, hbm],
            out_specs=pl.BlockSpec((ROW_BLOCK, HALF), lambda w, blk, *_: (blk[w], 0)),
            scratch_shapes=[pltpu.VMEM((2, D_MODEL, D_EXPERT), F32), pltpu.VMEM((2, D_MODEL, D_EXPERT), F32),
                            pltpu.VMEM((2, D_EXPERT, D_MODEL), F32),
                            pltpu.VMEM((D_MODEL, D_EXPERT), BF16), pltpu.VMEM((D_MODEL, D_EXPERT), BF16),
                            pltpu.VMEM((D_EXPERT, D_MODEL), BF16),
                            pltpu.SemaphoreType.DMA((2, 3))],
        ),
        out_shape=jax.ShapeDtypeStruct((n_rows, HALF), jnp.uint32),
        compiler_params=pltpu.CompilerParams(dimension_semantics=("arbitrary",),
                                             vmem_limit_bytes=VMEM_LIMIT),
        name="experts",
    )(*items, xs, w_gate, w_up, w_down)


def _work_items(counts, n_rows):
    n_blocks = n_rows // ROW_BLOCK
    n_items = n_blocks + N_EXPERTS - 1
    ends = jnp.cumsum(counts)
    starts = ends - counts
    blk_lo = starts // ROW_BLOCK
    blk_hi = jnp.maximum(ends - 1, 0) // ROW_BLOCK
    per_e = jnp.where(counts > 0, blk_hi - blk_lo + 1, 0)
    item_end = jnp.cumsum(per_e)
    item_start = item_end - per_e
    total = item_end[-1]
    w = jnp.arange(n_items, dtype=jnp.int32)
    live = w < total
    e = jnp.sum((jnp.minimum(w, total - 1)[:, None] >= item_end[None, :]).astype(jnp.int32), axis=1)
    e = jnp.minimum(e, N_EXPERTS - 1)
    sel = (e[:, None] == jnp.arange(N_EXPERTS, dtype=jnp.int32)[None, :]).astype(jnp.int32)
    pick = lambda table: jnp.sum(sel * table[None, :], axis=1)
    blk = jnp.where(live, pick(blk_lo) + (w - pick(item_start)), n_blocks - 1).astype(jnp.int32)
    lo = jnp.where(live, jnp.maximum(pick(starts), blk * ROW_BLOCK), 0).astype(jnp.int32)
    hi = jnp.where(live, jnp.minimum(pick(ends), (blk + 1) * ROW_BLOCK), 0).astype(jnp.int32)
    prev_blk = jnp.concatenate([jnp.full((1,), -1, jnp.int32), blk[:-1]])
    first = (live & (blk != prev_blk)).astype(jnp.int32)
    ids = jnp.arange(N_EXPERTS, dtype=jnp.int32)
    used = counts > 0
    rank_e = jnp.cumsum(used.astype(jnp.int32)) - 1
    later = used[None, :] & (ids[None, :] > ids[:, None])
    next_e = jnp.min(jnp.where(later, ids[None, :], N_EXPERTS), axis=1)
    next_e = jnp.where(next_e < N_EXPERTS, next_e, -1)
    slot = (pick(rank_e) % 2).astype(jnp.int32)
    nxt = pick(next_e).astype(jnp.int32)
    return (blk, e.astype(jnp.int32), lo, hi, first, slot, nxt), starts


def kernel(x_prompt, x_sample, state_conv, state_gla, norm_mix, w_in, b_gates, w_gla_gate_up, b_gla_gate_up, conv_dw, conv_db, conv_ln_g, conv_ln_b, w_conv_out, gla_norm_g, w_gla_out, w_out, norm_ffn, w_router_group, b_router_group, w_router_expert, b_router_expert, w_expert_gate, w_expert_up, w_expert_down, norm_final):
    depth = norm_mix.shape[0]
    assert depth == 1
    l = 0
    bp, seq_p, _ = x_prompt.shape
    bs, seq_s, _ = x_sample.shape
    n_p, n_s = bp * seq_p, bs * seq_s
    n_tok = n_p + n_s
    row2 = lambda a: a.reshape(1, -1)

    wi = w_in[l]
    w_main = wi.astype(BF16)
    w_alow = jnp.pad(w_main[:, _C_ALOW:_C_GATES], ((0, 0), (0, LANES - RANK)))
    w_gates = w_main[:, _C_GATES:]
    w_up_pad = jnp.pad(w_gla_gate_up[l], ((0, LANES - RANK), (0, 0))).astype(BF16)
    w_route = jnp.pad(jnp.concatenate([w_router_expert[l], w_router_group[l]], axis=1),
                      ((0, 0), (0, LANES - N_EXPERTS - N_GROUPS)))
    w_route_hi = w_route.astype(BF16)
    w_route = jnp.stack([w_route_hi, (w_route - w_route_hi.astype(F32)).astype(BF16)])
    b_route = jnp.pad(jnp.concatenate([b_router_expert[l], b_router_group[l]]),
                      (0, LANES - N_EXPERTS - N_GROUPS)).reshape(1, LANES)
    w_pw = w_conv_out[l].astype(BF16)
    w_go = w_gla_out[l].astype(BF16)
    w_o = w_out[l].astype(BF16)
    conv_args = (conv_dw[l], row2(conv_db[l]), row2(conv_ln_g[l]), row2(conv_ln_b[l]), w_pw)
    proj_args = (row2(norm_mix[l]), w_main, w_alow, w_gates, w_up_pad, row2(b_gla_gate_up[l]),
                 row2(b_gates[l]))
    mix_args = (row2(gla_norm_g[l]), w_go, w_o, row2(norm_ffn[l]), w_route, b_route)

    xp = x_prompt.reshape(n_p, D_MODEL)
    conv_out, q, k, v, la, sr, gc, gg, conv_p = _proj_conv(xp, bp, seq_p, proj_args, conv_args)
    o, gla_p = _gla_prompt(q, k, la, v, bp, seq_p)
    h_p, xr_p, rec_p, wf_p, cnt = _mix_route(xp, conv_out, o, sr, gc, gg, *mix_args,
                                             jnp.zeros((SUBLANES, LANES), F32))

    xs_ = x_sample.reshape(n_s, D_MODEL)
    u, q, k, v, la, sr, gc, gg = _proj(xs_, *proj_args)
    conv_out, conv_s = _conv(u.reshape(bs, seq_s, D_CONV), state_conv, *conv_args, nseq=SAMPLE_SEQS)
    seqs = lambda a: a.reshape(bs, seq_s, a.shape[-1])
    o, gla_s = _gla_sample(seqs(q), seqs(k), seqs(la), seqs(v), state_gla)
    h_s, xr_s, rec_s, wf_s, cnt = _mix_route(xs_, conv_out.reshape(n_s, D_MODEL), o.reshape(n_s, V_W), sr, gc, gg,
                                             *mix_args, cnt)

    counts = cnt[0, :N_EXPERTS].astype(jnp.int32)
    items, starts = _work_items(counts, 2 * n_tok)
    rec = jnp.concatenate([rec_p, rec_s], axis=1).astype(jnp.int32)
    e_ids = jnp.arange(N_EXPERTS, dtype=jnp.int32)[:, None]
    start_of = lambda e: jnp.sum(jnp.where(e[None, :] == e_ids, starts[:, None], 0), axis=0)
    pos0 = start_of(rec[_L_E0]) + rec[_L_R0]
    pos1 = start_of(rec[_L_E1]) + rec[_L_R1]
    xs_sorted = _dispatch(pos0, pos1, xr_p, xr_s)
    ys = _experts(items, xs_sorted, w_expert_gate[l], w_expert_up[l], w_expert_down[l])
    y_p = _combine(pos0, pos1, h_p, wf_p, row2(norm_final), ys, 0)
    y_s = _combine(pos0, pos1, h_s, wf_s, row2(norm_final), ys, n_p)

    return (y_p.reshape(bp, seq_p, D_MODEL), y_s.reshape(bs, seq_s, D_MODEL),
            conv_p[None], gla_p[None], conv_s[None], gla_s[None])
```

```python
import functools

import jax
import jax.numpy as jnp
from jax import lax
from jax.experimental import pallas as pl
from jax.experimental.pallas import tpu as pltpu

F32 = jnp.float32
BF16 = jnp.bfloat16

D_MODEL = 1024
D_CONV = 512
CONV_WIDTH = 31
CONV_HIST = CONV_WIDTH - 1
HEADS = 4
DK = 128
DV = 256
QK_W = HEADS * DK
V_W = HEADS * DV
RANK = 16
GATE_NORM = 16.0
N_GROUPS = 4
EXPERTS_PER_GROUP = 8
N_EXPERTS = 32
D_EXPERT = 512
EPS = 1e-6

LANES = 128
SUBLANES = 8
VMEM_LIMIT = 56 * 1024 * 1024

TOK_TILE = 512
PROJ_TILE = 512
MIX_TILE = 512
GLA_CHUNK = 128
GLA_STEP = 512
SAFE_LOG_DECAY = -80.0
CONV_ROWS = 64
CONV_HALO = 40
ROW_BLOCK = 256
DMA_UNROLL = 8
DMA_SEMS = 8
SAMPLE_SEQS = 32
SAMPLE_GLA_SEQS = 8
SAMPLE_PAD = 16

_C_GLU_A, _C_GLU_B = 0, 512
_C_Q, _C_K, _C_V, _C_R = 1024, 1536, 2048, 3072
_C_ALOW = 4096
_C_GATES = _C_ALOW + RANK
_MAIN_W = 1


def _sigmoid(x):
    return jax.nn.sigmoid(x)


def _dot(a, b):
    return jnp.dot(a, b, preferred_element_type=F32)


def _normed_input(x_ref, g_ref):
    x = x_ref[...]
    hn = x * lax.rsqrt(jnp.mean(x * x, axis=-1, keepdims=True) + EPS) * g_ref[...]
    return hn.astype(BF16)


def _glu(hb, wm_ref):
    return _dot(hb, wm_ref[:, _C_GLU_A:_C_GLU_B]) * _sigmoid(_dot(hb, wm_ref[:, _C_GLU_B:_C_Q]))


def _projection_steps(hb, wm_ref, wa_ref, wg_ref, wup_ref, bup_ref, bg_ref,
                      q_ref, k_ref, v_ref, la_ref, sr_ref, gc_ref, gg_ref):
    def mm(lo, hi):
        return _dot(hb, wm_ref[:, lo:hi])

    def q_step():
        q_ref[...] = mm(_C_Q, _C_K) * (DK ** -0.5)

    def k_step():
        k_ref[...] = mm(_C_K, _C_V)

    def v_step():
        v_ref[...] = mm(_C_V, _C_R).astype(BF16)

    def r_step():
        r = mm(_C_R, _C_ALOW)
        sr_ref[...] = (r * _sigmoid(r)).astype(BF16)

    def gc_step():
        gc_ref[...] = _sigmoid(_dot(hb, wg_ref[:, :D_MODEL]) + bg_ref[:, :D_MODEL]).astype(BF16)

    def gg_step():
        gg_ref[...] = _sigmoid(_dot(hb, wg_ref[:, D_MODEL:]) + bg_ref[:, D_MODEL:]).astype(BF16)

    def decay_step():
        a_low = _dot(hb, wa_ref[...])
        z = _dot(a_low.astype(BF16), wup_ref[...]) + bup_ref[...]
        la_ref[...] = (jnp.minimum(z, 0.0) - jnp.log1p(jnp.exp(-jnp.abs(z)))) * (1.0 / GATE_NORM)

    return [q_step, k_step, v_step, r_step, gc_step, gg_step, decay_step]


def _proj_kernel(x_ref, g_ref, wm_ref, wa_ref, wg_ref, wup_ref, bup_ref, bg_ref,
                 u_ref, q_ref, k_ref, v_ref, la_ref, sr_ref, gc_ref, gg_ref):
    hb = _normed_input(x_ref, g_ref)
    u_ref[...] = _glu(hb, wm_ref)
    for step in _projection_steps(hb, wm_ref, wa_ref, wg_ref, wup_ref, bup_ref, bg_ref,
                                  q_ref, k_ref, v_ref, la_ref, sr_ref, gc_ref, gg_ref):
        step()


_HIST_PAD = 32
_HIST0 = _HIST_PAD - CONV_HIST


def _conv_taps(w, dw_ref, cs, rt):
    win = w.shape[0]
    acc = jnp.zeros((rt, LANES), F32)
    for s in range(SUBLANES):
        ws = w if s == 0 else pltpu.roll(w, win - s, axis=0)
        for a in range(CONV_HALO // SUBLANES):
            j = a * SUBLANES + s - _HIST0
            if 0 <= j < CONV_WIDTH:
                acc = acc + ws[a * SUBLANES:a * SUBLANES + rt, :] * dw_ref[j:j + 1, cs]
    return acc


def _norm_swish_pointwise(y, lg_ref, lb_ref, wpw_ref):
    mu = jnp.mean(y, axis=-1, keepdims=True)
    var = jnp.mean(jnp.square(y - mu), axis=-1, keepdims=True)
    yn = (y - mu) * lax.rsqrt(var + EPS) * lg_ref[...] + lb_ref[...]
    return _dot((yn * _sigmoid(yn)).astype(BF16), wpw_ref[...])


def _proj_conv_kernel(x_ref, g_ref, wm_ref, wa_ref, wg_ref, wup_ref, bup_ref, bg_ref,
                      dw_ref, db_ref, lg_ref, lb_ref, wpw_ref,
                      co_ref, q_ref, k_ref, v_ref, la_ref, sr_ref, gc_ref, gg_ref, ns_ref,
                      full_ref, y_ref):
    c = pl.program_id(1)
    tm = x_ref.shape[0]

    @pl.when(c == 0)
    def _():
        full_ref[0:_HIST_PAD, :] = jnp.zeros((_HIST_PAD, D_CONV), F32)

    @pl.when(c > 0)
    def _():
        full_ref[0:_HIST_PAD, :] = full_ref[tm:tm + _HIST_PAD, :]

    hb = _normed_input(x_ref, g_ref)
    full_ref[_HIST_PAD:_HIST_PAD + tm, :] = _glu(hb, wm_ref)
    full_ref[_HIST_PAD + tm:, :] = jnp.zeros((SUBLANES, D_CONV), F32)
    ns_ref[0] = full_ref[tm + _HIST0:tm + _HIST_PAD, :]
    rt = CONV_ROWS

    def conv_tile(i, cc):
        def run():
            cs = slice(cc * LANES, (cc + 1) * LANES)
            acc = _conv_taps(full_ref[i * rt:i * rt + rt + CONV_HALO, cs], dw_ref, cs, rt)
            y_ref[i * rt:(i + 1) * rt, cs] = acc + db_ref[:, cs]
        return run

    conv_tiles = [conv_tile(i, cc) for i in range(tm // rt) for cc in range(D_CONV // LANES)]
    proj_steps = _projection_steps(hb, wm_ref, wa_ref, wg_ref, wup_ref, bup_ref, bg_ref,
                                   q_ref, k_ref, v_ref, la_ref, sr_ref, gc_ref, gg_ref)
    per_step = -(-len(conv_tiles) // len(proj_steps))
    for n, step in enumerate(proj_steps):
        step()
        for tile in conv_tiles[n * per_step:(n + 1) * per_step]:
            tile()
    co_ref[...] = _norm_swish_pointwise(y_ref[...], lg_ref, lb_ref, wpw_ref)


def _proj_conv(x2d, nb, seq, proj_consts, conv_consts):
    tm = PROJ_TILE
    nc = seq // tm
    row = lambda w: pl.BlockSpec((tm, w), lambda b, c: (b * nc + c, 0))
    full = lambda a: pl.BlockSpec(a.shape, lambda b, c: (0,) * a.ndim)
    widths = (D_MODEL, QK_W, QK_W, V_W, QK_W, V_W, D_MODEL, D_MODEL)
    dtypes = (F32, F32, F32, BF16, F32, BF16, BF16, BF16)
    consts = tuple(proj_consts) + tuple(conv_consts)
    n = nb * seq
    const_specs = [full(a) for a in consts]
    const_specs[_MAIN_W] = pl.BlockSpec((D_MODEL, _C_ALOW), lambda b, c: (0, 0))
    return pl.pallas_call(
        _proj_conv_kernel,
        grid=(nb, nc),
        in_specs=[row(D_MODEL)] + const_specs,
        out_specs=[row(w) for w in widths] + [pl.BlockSpec((1, CONV_HIST, D_CONV), lambda b, c: (b, 0, 0))],
        out_shape=[jax.ShapeDtypeStruct((n, w), dt) for w, dt in zip(widths, dtypes)]
        + [jax.ShapeDtypeStruct((nb, CONV_HIST, D_CONV), F32)],
        scratch_shapes=[pltpu.VMEM((_HIST_PAD + tm + SUBLANES, D_CONV), F32), pltpu.VMEM((tm, D_CONV), F32)],
        compiler_params=pltpu.CompilerParams(dimension_semantics=("parallel", "arbitrary"),
                                             vmem_limit_bytes=VMEM_LIMIT),
        name="proj_conv",
    )(x2d, *consts)


def _proj(x2d, norm_g, w_main, w_alow, w_gates, w_up, b_up, b_gates):
    n = x2d.shape[0]
    tm = TOK_TILE
    row = lambda w: pl.BlockSpec((tm, w), lambda i: (i, 0))
    full = lambda a: pl.BlockSpec(a.shape, lambda i: (0,) * a.ndim)
    widths = (D_CONV, QK_W, QK_W, V_W, QK_W, V_W, D_MODEL, D_MODEL)
    dtypes = (F32, F32, F32, BF16, F32, BF16, BF16, BF16)
    consts = (norm_g, w_main, w_alow, w_gates, w_up, b_up, b_gates)
    const_specs = [full(a) for a in consts]
    const_specs[_MAIN_W] = pl.BlockSpec((D_MODEL, _C_ALOW), lambda i: (0, 0))
    return pl.pallas_call(
        _proj_kernel,
        grid=(n // tm,),
        in_specs=[row(D_MODEL)] + const_specs,
        out_specs=[row(w) for w in widths],
        out_shape=[jax.ShapeDtypeStruct((n, w), dt) for w, dt in zip(widths, dtypes)],
        compiler_params=pltpu.CompilerParams(dimension_semantics=("parallel",),
                                             vmem_limit_bytes=VMEM_LIMIT),
        name="proj",
    )(x2d, *consts)


def _conv_kernel(u_ref, st_ref, dw_ref, db_ref, lg_ref, lb_ref, wpw_ref, out_ref, ns_ref, full_ref, y_ref, *, seq):
    nseq = u_ref.shape[0]
    n_rows = y_ref.shape[0] // nseq
    tail = full_ref.shape[0] - (_HIST_PAD + seq)

    def one_seq(b, carry):
        full_ref[0:_HIST_PAD, :] = jnp.zeros((_HIST_PAD, D_CONV), F32)
        full_ref[_HIST_PAD + seq:, :] = jnp.zeros((tail, D_CONV), F32)
        full_ref[_HIST0:_HIST_PAD, :] = st_ref[b]
        full_ref[_HIST_PAD:_HIST_PAD + seq, :] = u_ref[b]
        ns_ref[b] = full_ref[seq + _HIST0:seq + _HIST_PAD, :]
        for c in range(D_CONV // LANES):
            cs = slice(c * LANES, (c + 1) * LANES)
            acc = _conv_taps(full_ref[:, cs], dw_ref, cs, n_rows)
            y_ref[pl.ds(pl.multiple_of(b * n_rows, SUBLANES), n_rows), cs] = acc + db_ref[:, cs]
        return carry

    lax.fori_loop(0, nseq, one_seq, 0)
    res = _norm_swish_pointwise(y_ref[...], lg_ref, lb_ref, wpw_ref)
    for j in range(nseq):
        out_ref[j] = res[j * n_rows:j * n_rows + seq]


def _conv(u3, state, dw, db, lg, lb, w_pw, nseq):
    nb, seq, _ = u3.shape
    n_rows = -(-seq // SUBLANES) * SUBLANES
    full = lambda a: pl.BlockSpec(a.shape, lambda b: (0,) * a.ndim)
    per_b = lambda r, w: pl.BlockSpec((nseq, r, w), lambda b: (b, 0, 0))
    return pl.pallas_call(
        functools.partial(_conv_kernel, seq=seq),
        grid=(nb // nseq,),
        in_specs=[per_b(seq, D_CONV), pl.BlockSpec((None, nseq, CONV_HIST, D_CONV), lambda b: (0, b, 0, 0)),
                  full(dw), full(db), full(lg), full(lb), full(w_pw)],
        out_specs=[per_b(seq, D_MODEL), per_b(CONV_HIST, D_CONV)],
        out_shape=[jax.ShapeDtypeStruct((nb, seq, D_MODEL), F32),
                   jax.ShapeDtypeStruct((nb, CONV_HIST, D_CONV), F32)],
        scratch_shapes=[pltpu.VMEM((n_rows + CONV_HALO, D_CONV), F32),
                        pltpu.VMEM((nseq * n_rows, D_CONV), F32)],
        compiler_params=pltpu.CompilerParams(dimension_semantics=("parallel",),
                                             vmem_limit_bytes=VMEM_LIMIT),
        name="conv_state",
    )(u3, state, dw, db, lg, lb, w_pw)


def _token_step(s, a_col, k_col, q_col, v_row):
    s = s * a_col + k_col * v_row
    return s, jnp.sum(q_col * s, axis=0, keepdims=True)


def _gla_prompt_kernel(q_ref, k_ref, la_ref, v_ref, o_ref, sout_ref, s_ref, vf_ref):
    c = pl.program_id(1)
    n = GLA_CHUNK
    chunks = [slice(i * n, (i + 1) * n) for i in range(q_ref.shape[0] // n)]

    @pl.when(c == 0)
    def _():
        s_ref[...] = jnp.zeros_like(s_ref)

    r_i = lax.broadcasted_iota(jnp.int32, (n, n), 0)
    c_i = lax.broadcasted_iota(jnp.int32, (n, n), 1)
    tri = jnp.where(r_i >= c_i, 1.0, 0.0).astype(BF16)

    def prefix_sum(g):
        hi = g.astype(BF16)
        lo = (g - hi.astype(F32)).astype(BF16)
        return _dot(tri, hi) + _dot(tri, lo)

    bs = [prefix_sum(la_ref[rows, :]) for rows in chunks]
    total = bs[0][n - 1:n, :]
    for b in bs[1:]:
        total = jnp.minimum(total, b[n - 1:n, :])
    fast = jnp.min(total) > SAFE_LOG_DECAY

    @pl.when(fast)
    def _():
        for h in range(HEADS):
            ks = slice(h * DK, (h + 1) * DK)
            vs = slice(h * DV, (h + 1) * DV)
            s = s_ref[h]
            for rows, b in zip(chunks, bs):
                bh = b[:, ks]
                bl = bh[n - 1:n, :]
                kh = k_ref[rows, ks]
                vh = v_ref[rows, vs]
                qp = (q_ref[rows, ks] * jnp.exp(bh)).astype(BF16)
                kp = (kh * jnp.exp(-bh)).astype(BF16)
                kpp = kh * jnp.exp(bl - bh)
                att = lax.dot_general(qp, kp, (((1,), (1,)), ((), ())), preferred_element_type=F32)
                att = jnp.where(r_i >= c_i, att, 0.0).astype(BF16)
                o_ref[rows, vs] = _dot(qp, s.astype(BF16)) + _dot(att, vh)
                d_col = jnp.sum(jnp.where(r_i == c_i, jnp.exp(bl), 0.0), axis=1, keepdims=True)
                s = s * d_col + _dot(kpp.T.astype(BF16), vh)
            s_ref[h] = s

    @pl.when(jnp.logical_not(fast))
    def _():
        lane = lax.broadcasted_iota(jnp.int32, (1, n), 1)
        for rows in chunks:
            vf_ref[...] = v_ref[rows, :].astype(F32)
            for h in range(HEADS):
                ks = slice(h * DK, (h + 1) * DK)
                vs = slice(h * DV, (h + 1) * DV)
                q_t = q_ref[rows, ks].T
                k_t = k_ref[rows, ks].T
                a_t = jnp.exp(la_ref[rows, ks]).T

                def body(t, s):
                    m = lane == t
                    col = lambda x: jnp.sum(jnp.where(m, x, 0.0), axis=1, keepdims=True)
                    s, o_row = _token_step(s, col(a_t), col(k_t), col(q_t), vf_ref[pl.ds(t, 1), vs])
                    o_ref[pl.ds(rows.start + t, 1), vs] = o_row
                    return s

                s_ref[h] = lax.fori_loop(0, n, body, s_ref[h])

    @pl.when(c == pl.num_programs(1) - 1)
    def _():
        sout_ref[0] = s_ref[...]


def _gla_prompt(q, k, la, v, nb, seq):
    nc = seq // GLA_STEP
    tok = lambda w: pl.BlockSpec((GLA_STEP, w), lambda b, c: (b * nc + c, 0))
    return pl.pallas_call(
        _gla_prompt_kernel,
        grid=(nb, nc),
        in_specs=[tok(QK_W), tok(QK_W), tok(QK_W), tok(V_W)],
        out_specs=[tok(V_W), pl.BlockSpec((1, HEADS, DK, DV), lambda b, c: (b, 0, 0, 0))],
        out_shape=[jax.ShapeDtypeStruct((nb * seq, V_W), F32),
                   jax.ShapeDtypeStruct((nb, HEADS, DK, DV), F32)],
        scratch_shapes=[pltpu.VMEM((HEADS, DK, DV), F32), pltpu.VMEM((GLA_CHUNK, V_W), F32)],
        compiler_params=pltpu.CompilerParams(dimension_semantics=("parallel", "arbitrary"),
                                             vmem_limit_bytes=VMEM_LIMIT),
        name="gla_prompt",
    )(q, k, la, v)


def _gla_sample_kernel(q_ref, k_ref, la_ref, v_ref, s_ref, o_ref, sout_ref, *, seq):
    p = SAMPLE_PAD
    row = lax.broadcasted_iota(jnp.int32, (p, 1), 0)

    def padded(x):
        return jnp.concatenate([x, jnp.zeros((p - seq, x.shape[1]), F32)], axis=0)

    for b in range(q_ref.shape[0]):
        q, k, v, g = padded(q_ref[b]), padded(k_ref[b]), padded(v_ref[b].astype(F32)), padded(la_ref[b])
        bc = g
        sh = 1
        while sh < seq:
            bc = bc + jnp.where(row >= sh, pltpu.roll(bc, sh, axis=0), 0.0)
            sh *= 2
        b_last = bc[seq - 1:seq, :]
        bc = jnp.where(row < seq, bc, b_last)
        qp = (q * jnp.exp(bc)).astype(BF16)
        kpp_t = (k * jnp.exp(b_last - bc)).T.astype(BF16)
        d_cols = jnp.broadcast_to(jnp.exp(b_last), (SUBLANES, QK_W)).T
        vb = v.astype(BF16)

        o_in = [jnp.zeros((p, DV), F32) for _ in range(HEADS)]
        for d in range(seq):
            k_d = k if d == 0 else pltpu.roll(k, d, axis=0)
            v_d = v if d == 0 else pltpu.roll(v, d, axis=0)
            b_d = bc if d == 0 else pltpu.roll(bc, d, axis=0)
            pair = q * k_d * jnp.exp(jnp.where(row >= d, bc - b_d, -jnp.inf))
            for h in range(HEADS):
                att = jnp.sum(pair[:, h * DK:(h + 1) * DK], axis=1, keepdims=True)
                o_in[h] = o_in[h] + att * v_d[:, h * DV:(h + 1) * DV]

        for h in range(HEADS):
            ks = slice(h * DK, (h + 1) * DK)
            vs = slice(h * DV, (h + 1) * DV)
            s = s_ref[b, h]
            o = _dot(qp[:, ks], s.astype(BF16)) + o_in[h]
            o_ref[b, :, vs] = o[:seq]
            sout_ref[b, h] = s * d_cols[ks, 0:1] + _dot(kpp_t[ks, :], vb[:, vs])


def _gla_sample(q3, k3, la3, v3, state):
    nb, seq, _ = q3.shape
    assert seq <= SUBLANES
    ns = SAMPLE_GLA_SEQS
    col = pl.BlockSpec((ns, seq, QK_W), lambda b: (b, 0, 0))
    tok = pl.BlockSpec((ns, seq, V_W), lambda b: (b, 0, 0))
    st = pl.BlockSpec((ns, HEADS, DK, DV), lambda b: (b, 0, 0, 0))
    st_in = pl.BlockSpec((None, ns, HEADS, DK, DV), lambda b: (0, b, 0, 0, 0))
    return pl.pallas_call(
        functools.partial(_gla_sample_kernel, seq=seq),
        grid=(nb // ns,),
        in_specs=[col, col, col, tok, st_in],
        out_specs=[tok, st],
        out_shape=[jax.ShapeDtypeStruct((nb, seq, V_W), F32),
                   jax.ShapeDtypeStruct((nb, HEADS, DK, DV), F32)],
        compiler_params=pltpu.CompilerParams(dimension_semantics=("parallel",),
                                             vmem_limit_bytes=VMEM_LIMIT),
        name="gla_sample",
    )(q3, k3, la3, v3, state)


_L_E0, _L_E1, _L_R0, _L_R1 = 0, 1, 2, 3
_GROUP_LANE0 = N_EXPERTS
HALF = D_MODEL // 2


def _pack_rows(x):
    r = x.astype(BF16).astype(F32)
    top = lax.bitcast_convert_type(r[:, :HALF], jnp.uint32)
    bot = lax.bitcast_convert_type(r[:, HALF:], jnp.uint32)
    return top | jnp.right_shift(bot, jnp.uint32(16))


def _unpack_rows(packed):
    top = lax.bitcast_convert_type(packed & jnp.uint32(0xFFFF0000), F32)
    bot = lax.bitcast_convert_type(jnp.left_shift(packed, jnp.uint32(16)), F32)
    return top, bot


def _mix_route_kernel(x_ref, co_ref, o_ref, sr_ref, gc_ref, gg_ref, gn_ref, wgo_ref, wo_ref,
                      g_ref, wr_ref, br_ref, cin_ref,
                      h_ref, xp_ref, mi_ref, mf_ref, cnt_ref, carry_ref):
    i = pl.program_id(0)
    tm = x_ref.shape[0]

    @pl.when(i == 0)
    def _():
        carry_ref[...] = cin_ref[0:1, :]

    gla_out = None
    for h in range(HEADS):
        vs = slice(h * DV, (h + 1) * DV)
        oh = o_ref[:, vs]
        on = oh * lax.rsqrt(jnp.mean(oh * oh, axis=-1, keepdims=True) + EPS) * gn_ref[:, vs]
        part = _dot((on * sr_ref[:, vs]).astype(BF16), wgo_ref[vs, :])
        gla_out = part if gla_out is None else gla_out + part
    mix = (gc_ref[...] * co_ref[...] + gg_ref[...] * gla_out).astype(BF16)
    x = x_ref[...] + _dot(mix, wo_ref[...])
    h_ref[...] = x

    hn = x * lax.rsqrt(jnp.mean(x * x, axis=-1, keepdims=True) + EPS) * g_ref[...]
    hi = hn.astype(BF16)
    hi_f = hi.astype(F32)
    xp_ref[...] = _pack_rows(hn)
    lo = (hn - hi_f).astype(BF16)
    logits = _dot(hi, wr_ref[0]) + (_dot(lo, wr_ref[0]) + _dot(hi, wr_ref[1])) + br_ref[...]
    lane = lax.broadcasted_iota(jnp.int32, (tm, LANES), 1)
    lane_f = lane.astype(F32)
    group_f = jnp.right_shift(lane, 3).astype(F32)
    big = jnp.float32(LANES)
    neg = jnp.float32(-jnp.inf)

    is_g = (lane >= _GROUP_LANE0) & (lane < _GROUP_LANE0 + N_GROUPS)
    lg = jnp.where(is_g, logits, neg)
    mg = jnp.max(lg, axis=1, keepdims=True)
    p_sel = 1.0 / jnp.sum(jnp.exp(lg - mg), axis=1, keepdims=True)
    g_idx = jnp.min(jnp.where(lg == mg, lane_f, big), axis=1, keepdims=True) - _GROUP_LANE0

    is_e = (lane < N_EXPERTS) & (group_f == g_idx)
    le = jnp.where(is_e, logits, neg)
    v0 = jnp.max(le, axis=1, keepdims=True)
    i0 = jnp.min(jnp.where(le == v0, lane_f, big), axis=1, keepdims=True)
    le1 = jnp.where(lane_f == i0, neg, le)
    v1 = jnp.max(le1, axis=1, keepdims=True)
    i1 = jnp.min(jnp.where(le1 == v1, lane_f, big), axis=1, keepdims=True)
    e1 = jnp.exp(v1 - v0)
    den = 1.0 + e1
    w0 = p_sel * (1.0 / den)
    w1 = p_sel * (e1 / den)

    hit0 = lane_f == i0
    hit1 = lane_f == i1
    cnt = jnp.where(hit0 | hit1, 1.0, 0.0)
    r_i = lax.broadcasted_iota(jnp.int32, (tm, tm), 0)
    c_i = lax.broadcasted_iota(jnp.int32, (tm, tm), 1)
    before = jnp.where(r_i > c_i, 1.0, 0.0).astype(BF16)
    seen = _dot(before, cnt.astype(BF16)) + carry_ref[...]
    rank0 = jnp.sum(jnp.where(hit0, seen, 0.0), axis=1, keepdims=True)
    rank1 = jnp.sum(jnp.where(hit1, seen, 0.0), axis=1, keepdims=True)
    carry_ref[...] = carry_ref[...] + jnp.sum(cnt, axis=0, keepdims=True)
    cnt_ref[...] = jnp.broadcast_to(carry_ref[...], cnt_ref.shape)

    rec = jnp.where(lane == _L_E0, i0, 0.0)
    rec = jnp.where(lane == _L_E1, i1, rec)
    rec = jnp.where(lane == _L_R0, rank0, rec)
    rec = jnp.where(lane == _L_R1, rank1, rec)
    mi_ref[...] = rec.T[:SUBLANES, :]
    mf_ref[...] = jnp.where(lane == 0, w0, jnp.where(lane == 1, w1, 0.0))


def _mix_route(x2d, conv_out, o, sr, gc, gg, gn, w_gla_out, w_out, norm_g, w_route, b_route, counts_in):
    n = x2d.shape[0]
    tm = MIX_TILE
    row = lambda w: pl.BlockSpec((tm, w), lambda i: (i, 0))
    full = lambda a: pl.BlockSpec(a.shape, lambda i: (0,) * a.ndim)
    consts = (gn, w_gla_out, w_out, norm_g, w_route, b_route, counts_in)
    return pl.pallas_call(
        _mix_route_kernel,
        grid=(n // tm,),
        in_specs=[row(D_MODEL)] * 6 + [full(a) for a in consts],
        out_specs=[row(D_MODEL), row(HALF), pl.BlockSpec((SUBLANES, tm), lambda i: (0, i)), row(LANES),
                   pl.BlockSpec((SUBLANES, LANES), lambda i: (0, 0))],
        out_shape=[jax.ShapeDtypeStruct((n, D_MODEL), F32),
                   jax.ShapeDtypeStruct((n, HALF), jnp.uint32),
                   jax.ShapeDtypeStruct((SUBLANES, n), F32),
                   jax.ShapeDtypeStruct((n, LANES), F32),
                   jax.ShapeDtypeStruct((SUBLANES, LANES), F32)],
        scratch_shapes=[pltpu.VMEM((1, LANES), F32)],
        compiler_params=pltpu.CompilerParams(dimension_semantics=("arbitrary",),
                                             vmem_limit_bytes=VMEM_LIMIT),
        name="mix_route",
    )(x2d, conv_out, o, sr, gc, gg, *consts)


def _row_copy(src_ref, src_row, dst_ref, dst_row, sem):
    return pltpu.make_async_copy(src_ref.at[pl.ds(src_row, 1), :], dst_ref.at[pl.ds(dst_row, 1), :], sem)


def _dispatch_kernel(p0_ref, p1_ref, xa_ref, xb_ref, xs_ref, sem, *, tiles_a):
    tm = xa_ref.shape[0]
    i = pl.program_id(0)
    base = i * tm

    def scatter(src_ref):
        def rows(wait):
            def body(t, carry):
                s = 2 * lax.rem(t, DMA_SEMS // 2)
                c0 = _row_copy(src_ref, t, xs_ref, p0_ref[base + t], sem.at[s])
                c1 = _row_copy(src_ref, t, xs_ref, p1_ref[base + t], sem.at[s + 1])
                if wait:
                    c0.wait()
                    c1.wait()
                else:
                    c0.start(priority=0)
                    c1.start(priority=1)
                return carry
            lax.fori_loop(0, tm, body, 0, unroll=DMA_UNROLL)

        rows(wait=False)
        rows(wait=True)

    @pl.when(i < tiles_a)
    def _():
        scatter(xa_ref)

    @pl.when(i >= tiles_a)
    def _():
        scatter(xb_ref)


def _dispatch(pos0, pos1, x_a, x_b):
    tm = TOK_TILE
    tiles_a, tiles_b = x_a.shape[0] // tm, x_b.shape[0] // tm
    n = x_a.shape[0] + x_b.shape[0]
    width = x_a.shape[1]
    return pl.pallas_call(
        functools.partial(_dispatch_kernel, tiles_a=tiles_a),
        grid_spec=pltpu.PrefetchScalarGridSpec(
            num_scalar_prefetch=2,
            grid=(tiles_a + tiles_b,),
            in_specs=[pl.BlockSpec((tm, width), lambda i, p0, p1: (jnp.minimum(i, tiles_a - 1), 0)),
                      pl.BlockSpec((tm, width), lambda i, p0, p1: (jnp.maximum(i - tiles_a, 0), 0))],
            out_specs=pl.BlockSpec(memory_space=pl.ANY),
            scratch_shapes=[pltpu.SemaphoreType.DMA((DMA_SEMS,))],
        ),
        out_shape=jax.ShapeDtypeStruct((2 * n, width), x_a.dtype),
        compiler_params=pltpu.CompilerParams(dimension_semantics=("arbitrary",)),
        name="dispatch",
    )(pos0, pos1, x_a, x_b)


def _combine_kernel(p0_ref, p1_ref, h_ref, wf_ref, g_ref, ys_ref, out_ref, b0_ref, b1_ref, sems, *, tok_offset):
    tm = h_ref.shape[0]
    i = pl.program_id(0)

    def gather(tile, slot, wait):
        base = tok_offset + tile * tm

        def rows(t, carry):
            c0 = _row_copy(ys_ref, p0_ref[base + t], b0_ref.at[slot], t, sems.at[slot])
            c1 = _row_copy(ys_ref, p1_ref[base + t], b1_ref.at[slot], t, sems.at[slot])
            if wait:
                c0.wait()
                c1.wait()
            else:
                c0.start(priority=0)
                c1.start(priority=1)
            return carry

        lax.fori_loop(0, tm, rows, 0, unroll=DMA_UNROLL)

    slot = lax.rem(i, 2)

    @pl.when(i == 0)
    def _():
        gather(i, slot, wait=False)

    @pl.when(i + 1 < pl.num_programs(0))
    def _():
        gather(i + 1, 1 - slot, wait=False)

    gather(i, slot, wait=True)
    top0, bot0 = _unpack_rows(b0_ref[slot])
    top1, bot1 = _unpack_rows(b1_ref[slot])
    w0, w1 = wf_ref[:, 0:1], wf_ref[:, 1:2]
    y = h_ref[...] + jnp.concatenate([top0 * w0 + top1 * w1, bot0 * w0 + bot1 * w1], axis=1)
    out_ref[...] = y * lax.rsqrt(jnp.mean(y * y, axis=-1, keepdims=True) + EPS) * g_ref[...]


def _combine(pos0, pos1, h, wf, norm_g, ys, tok_offset):
    n_rows = h.shape[0]
    tm = TOK_TILE
    return pl.pallas_call(
        functools.partial(_combine_kernel, tok_offset=tok_offset),
        grid_spec=pltpu.PrefetchScalarGridSpec(
            num_scalar_prefetch=2,
            grid=(n_rows // tm,),
            in_specs=[pl.BlockSpec((tm, D_MODEL), lambda i, p0, p1: (i, 0)),
                      pl.BlockSpec((tm, LANES), lambda i, p0, p1: (i, 0)),
                      pl.BlockSpec((1, D_MODEL), lambda i, p0, p1: (0, 0)),
                      pl.BlockSpec(memory_space=pl.ANY)],
            out_specs=pl.BlockSpec((tm, D_MODEL), lambda i, p0, p1: (i, 0)),
            scratch_shapes=[pltpu.VMEM((2, tm, HALF), jnp.uint32), pltpu.VMEM((2, tm, HALF), jnp.uint32),
                            pltpu.SemaphoreType.DMA((2,))],
        ),
        out_shape=jax.ShapeDtypeStruct((n_rows, D_MODEL), F32),
        compiler_params=pltpu.CompilerParams(dimension_semantics=("arbitrary",),
                                             vmem_limit_bytes=VMEM_LIMIT),
        name="combine",
    )(pos0, pos1, h, wf, norm_g, ys)


def _expert_kernel(blk_ref, exp_ref, lo_ref, hi_ref, first_ref, slot_ref, next_ref,
                   xs_ref, wg_hbm, wu_hbm, wd_hbm, ys_ref,
                   wg32_ref, wu32_ref, wd32_ref, wgb_ref, wub_ref, wdb_ref, sems):
    w = pl.program_id(0)
    e = exp_ref[w]
    slot = slot_ref[w]
    changed = jnp.logical_or(w == 0, e != exp_ref[jnp.maximum(w - 1, 0)])

    def weight_copies(expert, s):
        return [pltpu.make_async_copy(src.at[expert], dst.at[s], sems.at[s, j])
                for j, (src, dst) in enumerate(((wg_hbm, wg32_ref), (wu_hbm, wu32_ref), (wd_hbm, wd32_ref)))]

    @pl.when(w == 0)
    def _():
        for c in weight_copies(e, slot):
            c.start()

    @pl.when(changed)
    def _():
        for c in weight_copies(e, slot):
            c.wait()
        wgb_ref[...] = wg32_ref[slot].astype(BF16)
        wub_ref[...] = wu32_ref[slot].astype(BF16)
        wdb_ref[...] = wd32_ref[slot].astype(BF16)

        @pl.when(next_ref[w] >= 0)
        def _():
            for c in weight_copies(next_ref[w], 1 - slot):
                c.start()

    lo = lo_ref[w]
    hi = hi_ref[w]

    @pl.when(hi > lo)
    def _():
        x = jnp.concatenate(_unpack_rows(xs_ref[...]), axis=1).astype(BF16)
        gate = _dot(x, wgb_ref[...])
        hid = (gate * _sigmoid(gate)) * _dot(x, wub_ref[...])
        y = _pack_rows(_dot(hid.astype(BF16), wdb_ref[...]))

        @pl.when(first_ref[w] == 1)
        def _():
            ys_ref[...] = y

        @pl.when(first_ref[w] == 0)
        def _():
            rows = blk_ref[w] * ROW_BLOCK + lax.broadcasted_iota(jnp.int32, (ROW_BLOCK, 1), 0)
            ys_ref[...] = jnp.where((rows >= lo) & (rows < hi), y, ys_ref[...])


def _experts(items, xs, w_gate, w_up, w_down):
    n_rows = xs.shape[0]
    n_items = items[0].shape[0]
    hbm = pl.BlockSpec(memory_space=pl.ANY)
    return pl.pallas_call(
        _expert_kernel,
        grid_spec=pltpu.PrefetchScalarGridSpec(
            num_scalar_prefetch=len(items),
            grid=(n_items,),
            in_specs=[pl.BlockSpec((ROW_BLOCK, HALF), lambda w, blk, *_: (blk[w], 0)), hbm, hbm, hbm],
            out_specs=pl.BlockSpec((ROW_BLOCK, HALF), lambda w, blk, *_: (blk[w], 0)),
            scratch_shapes=[pltpu.VMEM((2, D_MODEL, D_EXPERT), F32), pltpu.VMEM((2, D_MODEL, D_EXPERT), F32),
                            pltpu.VMEM((2, D_EXPERT, D_MODEL), F32),
                            pltpu.VMEM((D_MODEL, D_EXPERT), BF16), pltpu.VMEM((D_MODEL, D_EXPERT), BF16),
                            pltpu.VMEM((D_EXPERT, D_MODEL), BF16),
                            pltpu.SemaphoreType.DMA((2, 3))],
        ),
        out_shape=jax.ShapeDtypeStruct((n_rows, HALF), jnp.uint32),
        compiler_params=pltpu.CompilerParams(dimension_semantics=("arbitrary",),
                                             vmem_limit_bytes=VMEM_LIMIT),
        name="experts",
    )(*items, xs, w_gate, w_up, w_down)


def _work_items(counts, n_rows):
    n_blocks = n_rows // ROW_BLOCK
    n_items = n_blocks + N_EXPERTS - 1
    ends = jnp.cumsum(counts)
    starts = ends - counts
    blk_lo = starts // ROW_BLOCK
    blk_hi = jnp.maximum(ends - 1, 0) // ROW_BLOCK
    per_e = jnp.where(counts > 0, blk_hi - blk_lo + 1, 0)
    item_end = jnp.cumsum(per_e)
    item_start = item_end - per_e
    total = item_end[-1]
    w = jnp.arange(n_items, dtype=jnp.int32)
    live = w < total
    e = jnp.sum((jnp.minimum(w, total - 1)[:, None] >= item_end[None, :]).astype(jnp.int32), axis=1)
    e = jnp.minimum(e, N_EXPERTS - 1)
    sel = (e[:, None] == jnp.arange(N_EXPERTS, dtype=jnp.int32)[None, :]).astype(jnp.int32)
    pick = lambda table: jnp.sum(sel * table[None, :], axis=1)
    blk = jnp.where(live, pick(blk_lo) + (w - pick(item_start)), n_blocks - 1).astype(jnp.int32)
    lo = jnp.where(live, jnp.maximum(pick(starts), blk * ROW_BLOCK), 0).astype(jnp.int32)
    hi = jnp.where(live, jnp.minimum(pick(ends), (blk + 1) * ROW_BLOCK), 0).astype(jnp.int32)
    prev_blk = jnp.concatenate([jnp.full((1,), -1, jnp.int32), blk[:-1]])
    first = (live & (blk != prev_blk)).astype(jnp.int32)
    ids = jnp.arange(N_EXPERTS, dtype=jnp.int32)
    used = counts > 0
    rank_e = jnp.cumsum(used.astype(jnp.int32)) - 1
    later = used[None, :] & (ids[None, :] > ids[:, None])
    next_e = jnp.min(jnp.where(later, ids[None, :], N_EXPERTS), axis=1)
    next_e = jnp.where(next_e < N_EXPERTS, next_e, -1)
    slot = (pick(rank_e) % 2).astype(jnp.int32)
    nxt = pick(next_e).astype(jnp.int32)
    return (blk, e.astype(jnp.int32), lo, hi, first, slot, nxt), starts


def kernel(x_prompt, x_sample, state_conv, state_gla, norm_mix, w_in, b_gates, w_gla_gate_up, b_gla_gate_up, conv_dw, conv_db, conv_ln_g, conv_ln_b, w_conv_out, gla_norm_g, w_gla_out, w_out, norm_ffn, w_router_group, b_router_group, w_router_expert, b_router_expert, w_expert_gate, w_expert_up, w_expert_down, norm_final):
    depth = norm_mix.shape[0]
    assert depth == 1
    l = 0
    bp, seq_p, _ = x_prompt.shape
    bs, seq_s, _ = x_sample.shape
    n_p, n_s = bp * seq_p, bs * seq_s
    n_tok = n_p + n_s
    row2 = lambda a: a.reshape(1, -1)

    wi = w_in[l]
    w_main = wi.astype(BF16)
    w_alow = jnp.pad(w_main[:, _C_ALOW:_C_GATES], ((0, 0), (0, LANES - RANK)))
    w_gates = w_main[:, _C_GATES:]
    w_up_pad = jnp.pad(w_gla_gate_up[l], ((0, LANES - RANK), (0, 0))).astype(BF16)
    w_route = jnp.pad(jnp.concatenate([w_router_expert[l], w_router_group[l]], axis=1),
                      ((0, 0), (0, LANES - N_EXPERTS - N_GROUPS)))
    w_route_hi = w_route.astype(BF16)
    w_route = jnp.stack([w_route_hi, (w_route - w_route_hi.astype(F32)).astype(BF16)])
    b_route = jnp.pad(jnp.concatenate([b_router_expert[l], b_router_group[l]]),
                      (0, LANES - N_EXPERTS - N_GROUPS)).reshape(1, LANES)
    w_pw = w_conv_out[l].astype(BF16)
    w_go = w_gla_out[l].astype(BF16)
    w_o = w_out[l].astype(BF16)
    conv_args = (conv_dw[l], row2(conv_db[l]), row2(conv_ln_g[l]), row2(conv_ln_b[l]), w_pw)
    proj_args = (row2(norm_mix[l]), w_main, w_alow, w_gates, w_up_pad, row2(b_gla_gate_up[l]),
                 row2(b_gates[l]))
    mix_args = (row2(gla_norm_g[l]), w_go, w_o, row2(norm_ffn[l]), w_route, b_route)

    xp = x_prompt.reshape(n_p, D_MODEL)
    conv_out, q, k, v, la, sr, gc, gg, conv_p = _proj_conv(xp, bp, seq_p, proj_args, conv_args)
    o, gla_p = _gla_prompt(q, k, la, v, bp, seq_p)
    h_p, xr_p, rec_p, wf_p, cnt = _mix_route(xp, conv_out, o, sr, gc, gg, *mix_args,
                                             jnp.zeros((SUBLANES, LANES), F32))

    xs_ = x_sample.reshape(n_s, D_MODEL)
    u, q, k, v, la, sr, gc, gg = _proj(xs_, *proj_args)
    conv_out, conv_s = _conv(u.reshape(bs, seq_s, D_CONV), state_conv, *conv_args, nseq=SAMPLE_SEQS)
    seqs = lambda a: a.reshape(bs, seq_s, a.shape[-1])
    o, gla_s = _gla_sample(seqs(q), seqs(k), seqs(la), seqs(v), state_gla)
    h_s, xr_s, rec_s, wf_s, cnt = _mix_route(xs_, conv_out.reshape(n_s, D_MODEL), o.reshape(n_s, V_W), sr, gc, gg,
                                             *mix_args, cnt)

    counts = cnt[0, :N_EXPERTS].astype(jnp.int32)
    items, starts = _work_items(counts, 2 * n_tok)
    rec = jnp.concatenate([rec_p, rec_s], axis=1).astype(jnp.int32)
    e_ids = jnp.arange(N_EXPERTS, dtype=jnp.int32)[:, None]
    start_of = lambda e: jnp.sum(jnp.where(e[None, :] == e_ids, starts[:, None], 0), axis=0)
    pos0 = start_of(rec[_L_E0]) + rec[_L_R0]
    pos1 = start_of(rec[_L_E1]) + rec[_L_R1]
    xs_sorted = _dispatch(pos0, pos1, xr_p, xr_s)
    ys = _experts(items, xs_sorted, w_expert_gate[l], w_expert_up[l], w_expert_down[l])
    y_p = _combine(pos0, pos1, h_p, wf_p, row2(norm_final), ys, 0)
    y_s = _combine(pos0, pos1, h_s, wf_s, row2(norm_final), ys, n_p)

    return (y_p.reshape(bp, seq_p, D_MODEL), y_s.reshape(bs, seq_s, D_MODEL),
            conv_p[None], gla_p[None], conv_s[None], gla_s[None])
```

```python
import functools

import jax
import jax.numpy as jnp
from jax import lax
from jax.experimental import pallas as pl
from jax.experimental.pallas import tpu as pltpu

F32 = jnp.float32
BF16 = jnp.bfloat16

D_MODEL = 1024
D_CONV = 512
CONV_WIDTH = 31
CONV_HIST = CONV_WIDTH - 1
HEADS = 4
DK = 128
DV = 256
QK_W = HEADS * DK
V_W = HEADS * DV
RANK = 16
GATE_NORM = 16.0
N_GROUPS = 4
EXPERTS_PER_GROUP = 8
N_EXPERTS = 32
D_EXPERT = 512
EPS = 1e-6

LANES = 128
SUBLANES = 8
VMEM_LIMIT = 56 * 1024 * 1024

TOK_TILE = 512
PROJ_TILE = 512
MIX_TILE = 512
GLA_CHUNK = 128
GLA_STEP = 512
SAFE_LOG_DECAY = -80.0
CONV_ROWS = 64
CONV_HALO = 40
ROW_BLOCK = 256
DMA_UNROLL = 8
SAMPLE_SEQS = 32
SAMPLE_GLA_SEQS = 8
SAMPLE_PAD = 16

_C_GLU_A, _C_GLU_B = 0, 512
_C_Q, _C_K, _C_V, _C_R = 1024, 1536, 2048, 3072
_C_ALOW = 4096
_C_GATES = _C_ALOW + RANK
_MAIN_W = 1


def _sigmoid(x):
    return jax.nn.sigmoid(x)


def _dot(a, b):
    return jnp.dot(a, b, preferred_element_type=F32)


def _normed_input(x_ref, g_ref):
    x = x_ref[...]
    hn = x * lax.rsqrt(jnp.mean(x * x, axis=-1, keepdims=True) + EPS) * g_ref[...]
    return hn.astype(BF16)


def _glu(hb, wm_ref):
    return _dot(hb, wm_ref[:, _C_GLU_A:_C_GLU_B]) * _sigmoid(_dot(hb, wm_ref[:, _C_GLU_B:_C_Q]))


def _projection_steps(hb, wm_ref, wa_ref, wg_ref, wup_ref, bup_ref, bg_ref,
                      q_ref, k_ref, v_ref, la_ref, sr_ref, gc_ref, gg_ref):
    def mm(lo, hi):
        return _dot(hb, wm_ref[:, lo:hi])

    def q_step():
        q_ref[...] = mm(_C_Q, _C_K) * (DK ** -0.5)

    def k_step():
        k_ref[...] = mm(_C_K, _C_V)

    def v_step():
        v_ref[...] = mm(_C_V, _C_R).astype(BF16)

    def r_step():
        r = mm(_C_R, _C_ALOW)
        sr_ref[...] = (r * _sigmoid(r)).astype(BF16)

    def gc_step():
        gc_ref[...] = _sigmoid(_dot(hb, wg_ref[:, :D_MODEL]) + bg_ref[:, :D_MODEL]).astype(BF16)

    def gg_step():
        gg_ref[...] = _sigmoid(_dot(hb, wg_ref[:, D_MODEL:]) + bg_ref[:, D_MODEL:]).astype(BF16)

    def decay_step():
        a_low = _dot(hb, wa_ref[...])
        z = _dot(a_low.astype(BF16), wup_ref[...]) + bup_ref[...]
        la_ref[...] = (jnp.minimum(z, 0.0) - jnp.log1p(jnp.exp(-jnp.abs(z)))) * (1.0 / GATE_NORM)

    return [q_step, k_step, v_step, r_step, gc_step, gg_step, decay_step]


def _proj_kernel(x_ref, g_ref, wm_ref, wa_ref, wg_ref, wup_ref, bup_ref, bg_ref,
                 u_ref, q_ref, k_ref, v_ref, la_ref, sr_ref, gc_ref, gg_ref):
    hb = _normed_input(x_ref, g_ref)
    u_ref[...] = _glu(hb, wm_ref)
    for step in _projection_steps(hb, wm_ref, wa_ref, wg_ref, wup_ref, bup_ref, bg_ref,
                                  q_ref, k_ref, v_ref, la_ref, sr_ref, gc_ref, gg_ref):
        step()


_HIST_PAD = 32
_HIST0 = _HIST_PAD - CONV_HIST


def _conv_taps(w, dw_ref, cs, rt):
    win = w.shape[0]
    acc = jnp.zeros((rt, LANES), F32)
    for s in range(SUBLANES):
        ws = w if s == 0 else pltpu.roll(w, win - s, axis=0)
        for a in range(CONV_HALO // SUBLANES):
            j = a * SUBLANES + s - _HIST0
            if 0 <= j < CONV_WIDTH:
                acc = acc + ws[a * SUBLANES:a * SUBLANES + rt, :] * dw_ref[j:j + 1, cs]
    return acc


def _norm_swish_pointwise(y, lg_ref, lb_ref, wpw_ref):
    mu = jnp.mean(y, axis=-1, keepdims=True)
    var = jnp.mean(jnp.square(y - mu), axis=-1, keepdims=True)
    yn = (y - mu) * lax.rsqrt(var + EPS) * lg_ref[...] + lb_ref[...]
    return _dot((yn * _sigmoid(yn)).astype(BF16), wpw_ref[...])


def _proj_conv_kernel(x_ref, g_ref, wm_ref, wa_ref, wg_ref, wup_ref, bup_ref, bg_ref,
                      dw_ref, db_ref, lg_ref, lb_ref, wpw_ref,
                      co_ref, q_ref, k_ref, v_ref, la_ref, sr_ref, gc_ref, gg_ref, ns_ref,
                      full_ref, y_ref):
    c = pl.program_id(1)
    tm = x_ref.shape[0]

    @pl.when(c == 0)
    def _():
        full_ref[0:_HIST_PAD, :] = jnp.zeros((_HIST_PAD, D_CONV), F32)

    @pl.when(c > 0)
    def _():
        full_ref[0:_HIST_PAD, :] = full_ref[tm:tm + _HIST_PAD, :]

    hb = _normed_input(x_ref, g_ref)
    full_ref[_HIST_PAD:_HIST_PAD + tm, :] = _glu(hb, wm_ref)
    full_ref[_HIST_PAD + tm:, :] = jnp.zeros((SUBLANES, D_CONV), F32)
    ns_ref[0] = full_ref[tm + _HIST0:tm + _HIST_PAD, :]
    rt = CONV_ROWS

    def conv_tile(i, cc):
        def run():
            cs = slice(cc * LANES, (cc + 1) * LANES)
            acc = _conv_taps(full_ref[i * rt:i * rt + rt + CONV_HALO, cs], dw_ref, cs, rt)
            y_ref[i * rt:(i + 1) * rt, cs] = acc + db_ref[:, cs]
        return run

    conv_tiles = [conv_tile(i, cc) for i in range(tm // rt) for cc in range(D_CONV // LANES)]
    proj_steps = _projection_steps(hb, wm_ref, wa_ref, wg_ref, wup_ref, bup_ref, bg_ref,
                                   q_ref, k_ref, v_ref, la_ref, sr_ref, gc_ref, gg_ref)
    per_step = -(-len(conv_tiles) // len(proj_steps))
    for n, step in enumerate(proj_steps):
        step()
        for tile in conv_tiles[n * per_step:(n + 1) * per_step]:
            tile()
    co_ref[...] = _norm_swish_pointwise(y_ref[...], lg_ref, lb_ref, wpw_ref)


def _proj_conv(x2d, nb, seq, proj_consts, conv_consts):
    tm = PROJ_TILE
    nc = seq // tm
    row = lambda w: pl.BlockSpec((tm, w), lambda b, c: (b * nc + c, 0))
    full = lambda a: pl.BlockSpec(a.shape, lambda b, c: (0,) * a.ndim)
    widths = (D_MODEL, QK_W, QK_W, V_W, QK_W, V_W, D_MODEL, D_MODEL)
    dtypes = (F32, F32, F32, BF16, F32, BF16, BF16, BF16)
    consts = tuple(proj_consts) + tuple(conv_consts)
    n = nb * seq
    const_specs = [full(a) for a in consts]
    const_specs[_MAIN_W] = pl.BlockSpec((D_MODEL, _C_ALOW), lambda b, c: (0, 0))
    return pl.pallas_call(
        _proj_conv_kernel,
        grid=(nb, nc),
        in_specs=[row(D_MODEL)] + const_specs,
        out_specs=[row(w) for w in widths] + [pl.BlockSpec((1, CONV_HIST, D_CONV), lambda b, c: (b, 0, 0))],
        out_shape=[jax.ShapeDtypeStruct((n, w), dt) for w, dt in zip(widths, dtypes)]
        + [jax.ShapeDtypeStruct((nb, CONV_HIST, D_CONV), F32)],
        scratch_shapes=[pltpu.VMEM((_HIST_PAD + tm + SUBLANES, D_CONV), F32), pltpu.VMEM((tm, D_CONV), F32)],
        compiler_params=pltpu.CompilerParams(dimension_semantics=("parallel", "arbitrary"),
                                             vmem_limit_bytes=VMEM_LIMIT),
        name="proj_conv",
    )(x2d, *consts)


def _proj(x2d, norm_g, w_main, w_alow, w_gates, w_up, b_up, b_gates):
    n = x2d.shape[0]
    tm = TOK_TILE
    row = lambda w: pl.BlockSpec((tm, w), lambda i: (i, 0))
    full = lambda a: pl.BlockSpec(a.shape, lambda i: (0,) * a.ndim)
    widths = (D_CONV, QK_W, QK_W, V_W, QK_W, V_W, D_MODEL, D_MODEL)
    dtypes = (F32, F32, F32, BF16, F32, BF16, BF16, BF16)
    consts = (norm_g, w_main, w_alow, w_gates, w_up, b_up, b_gates)
    const_specs = [full(a) for a in consts]
    const_specs[_MAIN_W] = pl.BlockSpec((D_MODEL, _C_ALOW), lambda i: (0, 0))
    return pl.pallas_call(
        _proj_kernel,
        grid=(n // tm,),
        in_specs=[row(D_MODEL)] + const_specs,
        out_specs=[row(w) for w in widths],
        out_shape=[jax.ShapeDtypeStruct((n, w), dt) for w, dt in zip(widths, dtypes)],
        compiler_params=pltpu.CompilerParams(dimension_semantics=("parallel",),
                                             vmem_limit_bytes=VMEM_LIMIT),
        name="proj",
    )(x2d, *consts)


def _conv_kernel(u_ref, st_ref, dw_ref, db_ref, lg_ref, lb_ref, wpw_ref, out_ref, ns_ref, full_ref, y_ref, *, seq):
    nseq = u_ref.shape[0]
    n_rows = y_ref.shape[0] // nseq
    tail = full_ref.shape[0] - (_HIST_PAD + seq)

    def one_seq(b, carry):
        full_ref[0:_HIST_PAD, :] = jnp.zeros((_HIST_PAD, D_CONV), F32)
        full_ref[_HIST_PAD + seq:, :] = jnp.zeros((tail, D_CONV), F32)
        full_ref[_HIST0:_HIST_PAD, :] = st_ref[b]
        full_ref[_HIST_PAD:_HIST_PAD + seq, :] = u_ref[b]
        ns_ref[b] = full_ref[seq + _HIST0:seq + _HIST_PAD, :]
        for c in range(D_CONV // LANES):
            cs = slice(c * LANES, (c + 1) * LANES)
            acc = _conv_taps(full_ref[:, cs], dw_ref, cs, n_rows)
            y_ref[pl.ds(pl.multiple_of(b * n_rows, SUBLANES), n_rows), cs] = acc + db_ref[:, cs]
        return carry

    lax.fori_loop(0, nseq, one_seq, 0)
    res = _norm_swish_pointwise(y_ref[...], lg_ref, lb_ref, wpw_ref)
    for j in range(nseq):
        out_ref[j] = res[j * n_rows:j * n_rows + seq]


def _conv(u3, state, dw, db, lg, lb, w_pw, nseq):
    nb, seq, _ = u3.shape
    n_rows = -(-seq // SUBLANES) * SUBLANES
    full = lambda a: pl.BlockSpec(a.shape, lambda b: (0,) * a.ndim)
    per_b = lambda r, w: pl.BlockSpec((nseq, r, w), lambda b: (b, 0, 0))
    return pl.pallas_call(
        functools.partial(_conv_kernel, seq=seq),
        grid=(nb // nseq,),
        in_specs=[per_b(seq, D_CONV), pl.BlockSpec((None, nseq, CONV_HIST, D_CONV), lambda b: (0, b, 0, 0)),
                  full(dw), full(db), full(lg), full(lb), full(w_pw)],
        out_specs=[per_b(seq, D_MODEL), per_b(CONV_HIST, D_CONV)],
        out_shape=[jax.ShapeDtypeStruct((nb, seq, D_MODEL), F32),
                   jax.ShapeDtypeStruct((nb, CONV_HIST, D_CONV), F32)],
        scratch_shapes=[pltpu.VMEM((n_rows + CONV_HALO, D_CONV), F32),
                        pltpu.VMEM((nseq * n_rows, D_CONV), F32)],
        compiler_params=pltpu.CompilerParams(dimension_semantics=("parallel",),
                                             vmem_limit_bytes=VMEM_LIMIT),
        name="conv_state",
    )(u3, state, dw, db, lg, lb, w_pw)


def _token_step(s, a_col, k_col, q_col, v_row):
    s = s * a_col + k_col * v_row
    return s, jnp.sum(q_col * s, axis=0, keepdims=True)


def _gla_prompt_kernel(q_ref, k_ref, la_ref, v_ref, o_ref, sout_ref, s_ref, vf_ref):
    c = pl.program_id(1)
    n = GLA_CHUNK
    chunks = [slice(i * n, (i + 1) * n) for i in range(q_ref.shape[0] // n)]

    @pl.when(c == 0)
    def _():
        s_ref[...] = jnp.zeros_like(s_ref)

    r_i = lax.broadcasted_iota(jnp.int32, (n, n), 0)
    c_i = lax.broadcasted_iota(jnp.int32, (n, n), 1)
    tri = jnp.where(r_i >= c_i, 1.0, 0.0).astype(BF16)

    def prefix_sum(g):
        hi = g.astype(BF16)
        lo = (g - hi.astype(F32)).astype(BF16)
        return _dot(tri, hi) + _dot(tri, lo)

    bs = [prefix_sum(la_ref[rows, :]) for rows in chunks]
    total = bs[0][n - 1:n, :]
    for b in bs[1:]:
        total = jnp.minimum(total, b[n - 1:n, :])
    fast = jnp.min(total) > SAFE_LOG_DECAY

    @pl.when(fast)
    def _():
        for h in range(HEADS):
            ks = slice(h * DK, (h + 1) * DK)
            vs = slice(h * DV, (h + 1) * DV)
            s = s_ref[h]
            for rows, b in zip(chunks, bs):
                bh = b[:, ks]
                bl = bh[n - 1:n, :]
                kh = k_ref[rows, ks]
                vh = v_ref[rows, vs]
                qp = (q_ref[rows, ks] * jnp.exp(bh)).astype(BF16)
                kp = (kh * jnp.exp(-bh)).astype(BF16)
                kpp = kh * jnp.exp(bl - bh)
                att = lax.dot_general(qp, kp, (((1,), (1,)), ((), ())), preferred_element_type=F32)
                att = jnp.where(r_i >= c_i, att, 0.0).astype(BF16)
                o_ref[rows, vs] = _dot(qp, s.astype(BF16)) + _dot(att, vh)
                d_col = jnp.sum(jnp.where(r_i == c_i, jnp.exp(bl), 0.0), axis=1, keepdims=True)
                s = s * d_col + _dot(kpp.T.astype(BF16), vh)
            s_ref[h] = s

    @pl.when(jnp.logical_not(fast))
    def _():
        lane = lax.broadcasted_iota(jnp.int32, (1, n), 1)
        for rows in chunks:
            vf_ref[...] = v_ref[rows, :].astype(F32)
            for h in range(HEADS):
                ks = slice(h * DK, (h + 1) * DK)
                vs = slice(h * DV, (h + 1) * DV)
                q_t = q_ref[rows, ks].T
                k_t = k_ref[rows, ks].T
                a_t = jnp.exp(la_ref[rows, ks]).T

                def body(t, s):
                    m = lane == t
                    col = lambda x: jnp.sum(jnp.where(m, x, 0.0), axis=1, keepdims=True)
                    s, o_row = _token_step(s, col(a_t), col(k_t), col(q_t), vf_ref[pl.ds(t, 1), vs])
                    o_ref[pl.ds(rows.start + t, 1), vs] = o_row
                    return s

                s_ref[h] = lax.fori_loop(0, n, body, s_ref[h])

    @pl.when(c == pl.num_programs(1) - 1)
    def _():
        sout_ref[0] = s_ref[...]


def _gla_prompt(q, k, la, v, nb, seq):
    nc = seq // GLA_STEP
    tok = lambda w: pl.BlockSpec((GLA_STEP, w), lambda b, c: (b * nc + c, 0))
    return pl.pallas_call(
        _gla_prompt_kernel,
        grid=(nb, nc),
        in_specs=[tok(QK_W), tok(QK_W), tok(QK_W), tok(V_W)],
        out_specs=[tok(V_W), pl.BlockSpec((1, HEADS, DK, DV), lambda b, c: (b, 0, 0, 0))],
        out_shape=[jax.ShapeDtypeStruct((nb * seq, V_W), F32),
                   jax.ShapeDtypeStruct((nb, HEADS, DK, DV), F32)],
        scratch_shapes=[pltpu.VMEM((HEADS, DK, DV), F32), pltpu.VMEM((GLA_CHUNK, V_W), F32)],
        compiler_params=pltpu.CompilerParams(dimension_semantics=("parallel", "arbitrary"),
                                             vmem_limit_bytes=VMEM_LIMIT),
        name="gla_prompt",
    )(q, k, la, v)


def _gla_sample_kernel(q_ref, k_ref, la_ref, v_ref, s_ref, o_ref, sout_ref, *, seq):
    p = SAMPLE_PAD
    row = lax.broadcasted_iota(jnp.int32, (p, 1), 0)

    def padded(x):
        return jnp.concatenate([x, jnp.zeros((p - seq, x.shape[1]), F32)], axis=0)

    for b in range(q_ref.shape[0]):
        q, k, v, g = padded(q_ref[b]), padded(k_ref[b]), padded(v_ref[b].astype(F32)), padded(la_ref[b])
        bc = g
        sh = 1
        while sh < seq:
            bc = bc + jnp.where(row >= sh, pltpu.roll(bc, sh, axis=0), 0.0)
            sh *= 2
        b_last = bc[seq - 1:seq, :]
        bc = jnp.where(row < seq, bc, b_last)
        qp = (q * jnp.exp(bc)).astype(BF16)
        kpp_t = (k * jnp.exp(b_last - bc)).T.astype(BF16)
        d_cols = jnp.broadcast_to(jnp.exp(b_last), (SUBLANES, QK_W)).T
        vb = v.astype(BF16)

        o_in = [jnp.zeros((p, DV), F32) for _ in range(HEADS)]
        for d in range(seq):
            k_d = k if d == 0 else pltpu.roll(k, d, axis=0)
            v_d = v if d == 0 else pltpu.roll(v, d, axis=0)
            b_d = bc if d == 0 else pltpu.roll(bc, d, axis=0)
            pair = q * k_d * jnp.exp(jnp.where(row >= d, bc - b_d, -jnp.inf))
            for h in range(HEADS):
                att = jnp.sum(pair[:, h * DK:(h + 1) * DK], axis=1, keepdims=True)
                o_in[h] = o_in[h] + att * v_d[:, h * DV:(h + 1) * DV]

        for h in range(HEADS):
            ks = slice(h * DK, (h + 1) * DK)
            vs = slice(h * DV, (h + 1) * DV)
            s = s_ref[b, h]
            o = _dot(qp[:, ks], s.astype(BF16)) + o_in[h]
            o_ref[b, :, vs] = o[:seq]
            sout_ref[b, h] = s * d_cols[ks, 0:1] + _dot(kpp_t[ks, :], vb[:, vs])


def _gla_sample(q3, k3, la3, v3, state):
    nb, seq, _ = q3.shape
    assert seq <= SUBLANES
    ns = SAMPLE_GLA_SEQS
    col = pl.BlockSpec((ns, seq, QK_W), lambda b: (b, 0, 0))
    tok = pl.BlockSpec((ns, seq, V_W), lambda b: (b, 0, 0))
    st = pl.BlockSpec((ns, HEADS, DK, DV), lambda b: (b, 0, 0, 0))
    st_in = pl.BlockSpec((None, ns, HEADS, DK, DV), lambda b: (0, b, 0, 0, 0))
    return pl.pallas_call(
        functools.partial(_gla_sample_kernel, seq=seq),
        grid=(nb // ns,),
        in_specs=[col, col, col, tok, st_in],
        out_specs=[tok, st],
        out_shape=[jax.ShapeDtypeStruct((nb, seq, V_W), F32),
                   jax.ShapeDtypeStruct((nb, HEADS, DK, DV), F32)],
        compiler_params=pltpu.CompilerParams(dimension_semantics=("parallel",),
                                             vmem_limit_bytes=VMEM_LIMIT),
        name="gla_sample",
    )(q3, k3, la3, v3, state)


_L_E0, _L_E1, _L_R0, _L_R1 = 0, 1, 2, 3
_GROUP_LANE0 = N_EXPERTS
HALF = D_MODEL // 2


def _pack_rows(x):
    r = x.astype(BF16).astype(F32)
    top = lax.bitcast_convert_type(r[:, :HALF], jnp.uint32)
    bot = lax.bitcast_convert_type(r[:, HALF:], jnp.uint32)
    return top | jnp.right_shift(bot, jnp.uint32(16))


def _unpack_rows(packed):
    top = lax.bitcast_convert_type(packed & jnp.uint32(0xFFFF0000), F32)
    bot = lax.bitcast_convert_type(jnp.left_shift(packed, jnp.uint32(16)), F32)
    return top, bot


def _mix_route_kernel(x_ref, co_ref, o_ref, sr_ref, gc_ref, gg_ref, gn_ref, wgo_ref, wo_ref,
                      g_ref, wr_ref, br_ref, cin_ref,
                      h_ref, xp_ref, mi_ref, mf_ref, cnt_ref, carry_ref):
    i = pl.program_id(0)
    tm = x_ref.shape[0]

    @pl.when(i == 0)
    def _():
        carry_ref[...] = cin_ref[0:1, :]

    gla_out = None
    for h in range(HEADS):
        vs = slice(h * DV, (h + 1) * DV)
        oh = o_ref[:, vs]
        on = oh * lax.rsqrt(jnp.mean(oh * oh, axis=-1, keepdims=True) + EPS) * gn_ref[:, vs]
        part = _dot((on * sr_ref[:, vs]).astype(BF16), wgo_ref[vs, :])
        gla_out = part if gla_out is None else gla_out + part
    mix = (gc_ref[...] * co_ref[...] + gg_ref[...] * gla_out).astype(BF16)
    x = x_ref[...] + _dot(mix, wo_ref[...])
    h_ref[...] = x

    hn = x * lax.rsqrt(jnp.mean(x * x, axis=-1, keepdims=True) + EPS) * g_ref[...]
    hi = hn.astype(BF16)
    hi_f = hi.astype(F32)
    xp_ref[...] = _pack_rows(hn)
    lo = (hn - hi_f).astype(BF16)
    logits = _dot(hi, wr_ref[0]) + (_dot(lo, wr_ref[0]) + _dot(hi, wr_ref[1])) + br_ref[...]
    lane = lax.broadcasted_iota(jnp.int32, (tm, LANES), 1)
    lane_f = lane.astype(F32)
    group_f = jnp.right_shift(lane, 3).astype(F32)
    big = jnp.float32(LANES)
    neg = jnp.float32(-jnp.inf)

    is_g = (lane >= _GROUP_LANE0) & (lane < _GROUP_LANE0 + N_GROUPS)
    lg = jnp.where(is_g, logits, neg)
    mg = jnp.max(lg, axis=1, keepdims=True)
    p_sel = 1.0 / jnp.sum(jnp.exp(lg - mg), axis=1, keepdims=True)
    g_idx = jnp.min(jnp.where(lg == mg, lane_f, big), axis=1, keepdims=True) - _GROUP_LANE0

    is_e = (lane < N_EXPERTS) & (group_f == g_idx)
    le = jnp.where(is_e, logits, neg)
    v0 = jnp.max(le, axis=1, keepdims=True)
    i0 = jnp.min(jnp.where(le == v0, lane_f, big), axis=1, keepdims=True)
    le1 = jnp.where(lane_f == i0, neg, le)
    v1 = jnp.max(le1, axis=1, keepdims=True)
    i1 = jnp.min(jnp.where(le1 == v1, lane_f, big), axis=1, keepdims=True)
    e1 = jnp.exp(v1 - v0)
    den = 1.0 + e1
    w0 = p_sel * (1.0 / den)
    w1 = p_sel * (e1 / den)

    hit0 = lane_f == i0
    hit1 = lane_f == i1
    cnt = jnp.where(hit0 | hit1, 1.0, 0.0)
    r_i = lax.broadcasted_iota(jnp.int32, (tm, tm), 0)
    c_i = lax.broadcasted_iota(jnp.int32, (tm, tm), 1)
    before = jnp.where(r_i > c_i, 1.0, 0.0).astype(BF16)
    seen = _dot(before, cnt.astype(BF16)) + carry_ref[...]
    rank0 = jnp.sum(jnp.where(hit0, seen, 0.0), axis=1, keepdims=True)
    rank1 = jnp.sum(jnp.where(hit1, seen, 0.0), axis=1, keepdims=True)
    carry_ref[...] = carry_ref[...] + jnp.sum(cnt, axis=0, keepdims=True)
    cnt_ref[...] = jnp.broadcast_to(carry_ref[...], cnt_ref.shape)

    rec = jnp.where(lane == _L_E0, i0, 0.0)
    rec = jnp.where(lane == _L_E1, i1, rec)
    rec = jnp.where(lane == _L_R0, rank0, rec)
    rec = jnp.where(lane == _L_R1, rank1, rec)
    mi_ref[...] = rec.T[:SUBLANES, :]
    mf_ref[...] = jnp.where(lane == 0, w0, jnp.where(lane == 1, w1, 0.0))


def _mix_route(x2d, conv_out, o, sr, gc, gg, gn, w_gla_out, w_out, norm_g, w_route, b_route, counts_in):
    n = x2d.shape[0]
    tm = MIX_TILE
    row = lambda w: pl.BlockSpec((tm, w), lambda i: (i, 0))
    full = lambda a: pl.BlockSpec(a.shape, lambda i: (0,) * a.ndim)
    consts = (gn, w_gla_out, w_out, norm_g, w_route, b_route, counts_in)
    return pl.pallas_call(
        _mix_route_kernel,
        grid=(n // tm,),
        in_specs=[row(D_MODEL)] * 6 + [full(a) for a in consts],
        out_specs=[row(D_MODEL), row(HALF), pl.BlockSpec((SUBLANES, tm), lambda i: (0, i)), row(LANES),
                   pl.BlockSpec((SUBLANES, LANES), lambda i: (0, 0))],
        out_shape=[jax.ShapeDtypeStruct((n, D_MODEL), F32),
                   jax.ShapeDtypeStruct((n, HALF), jnp.uint32),
                   jax.ShapeDtypeStruct((SUBLANES, n), F32),
                   jax.ShapeDtypeStruct((n, LANES), F32),
                   jax.ShapeDtypeStruct((SUBLANES, LANES), F32)],
        scratch_shapes=[pltpu.VMEM((1, LANES), F32)],
        compiler_params=pltpu.CompilerParams(dimension_semantics=("arbitrary",),
                                             vmem_limit_bytes=VMEM_LIMIT),
        name="mix_route",
    )(x2d, conv_out, o, sr, gc, gg, *consts)


def _row_copy(src_ref, src_row, dst_ref, dst_row, sem):
    return pltpu.make_async_copy(src_ref.at[pl.ds(src_row, 1), :], dst_ref.at[pl.ds(dst_row, 1), :], sem)


def _dispatch_kernel(p0_ref, p1_ref, xa_ref, xb_ref, xs_ref, sem, *, tiles_a):
    tm = xa_ref.shape[0]
    i = pl.program_id(0)
    base = i * tm

    def scatter(src_ref):
        def rows(wait):
            def body(t, carry):
                c0 = _row_copy(src_ref, t, xs_ref, p0_ref[base + t], sem)
                c1 = _row_copy(src_ref, t, xs_ref, p1_ref[base + t], sem)
                if wait:
                    c0.wait()
                    c1.wait()
                else:
                    c0.start()
                    c1.start()
                return carry
            if wait:
                lax.fori_loop(0, tm, body, 0, unroll=DMA_UNROLL)
            else:
                for t in range(tm):
                    body(t, 0)

        rows(wait=False)
        rows(wait=True)

    @pl.when(i < tiles_a)
    def _():
        scatter(xa_ref)

    @pl.when(i >= tiles_a)
    def _():
        scatter(xb_ref)


def _dispatch(pos0, pos1, x_a, x_b):
    tm = TOK_TILE
    tiles_a, tiles_b = x_a.shape[0] // tm, x_b.shape[0] // tm
    n = x_a.shape[0] + x_b.shape[0]
    width = x_a.shape[1]
    return pl.pallas_call(
        functools.partial(_dispatch_kernel, tiles_a=tiles_a),
        grid_spec=pltpu.PrefetchScalarGridSpec(
            num_scalar_prefetch=2,
            grid=(tiles_a + tiles_b,),
            in_specs=[pl.BlockSpec((tm, width), lambda i, p0, p1: (jnp.minimum(i, tiles_a - 1), 0)),
                      pl.BlockSpec((tm, width), lambda i, p0, p1: (jnp.maximum(i - tiles_a, 0), 0))],
            out_specs=pl.BlockSpec(memory_space=pl.ANY),
            scratch_shapes=[pltpu.SemaphoreType.DMA(())],
        ),
        out_shape=jax.ShapeDtypeStruct((2 * n, width), x_a.dtype),
        compiler_params=pltpu.CompilerParams(dimension_semantics=("arbitrary",)),
        name="dispatch",
    )(pos0, pos1, x_a, x_b)


def _combine_kernel(p0_ref, p1_ref, h_ref, wf_ref, g_ref, ys_ref, out_ref, b0a, b1a, b0b, b1b, sems, *, tok_offset):
    tm = h_ref.shape[0]
    i = pl.program_id(0)
    bufs = ((b0a, b1a), (b0b, b1b))

    def copies(tile, slot, t):
        base = tok_offset + tile * tm
        return (_row_copy(ys_ref, p0_ref[base + t], bufs[slot][0], t, sems.at[slot]),
                _row_copy(ys_ref, p1_ref[base + t], bufs[slot][1], t, sems.at[slot]))

    def issue(tile, slot):
        for t in range(tm):
            for c in copies(tile, slot, t):
                c.start()

    def drain(tile, slot):
        def rows(t, carry):
            for c in copies(tile, slot, t):
                c.wait()
            return carry
        lax.fori_loop(0, tm, rows, 0, unroll=DMA_UNROLL)

    def finish(slot):
        top0, bot0 = _unpack_rows(bufs[slot][0][...])
        top1, bot1 = _unpack_rows(bufs[slot][1][...])
        w0, w1 = wf_ref[:, 0:1], wf_ref[:, 1:2]
        y = h_ref[...] + jnp.concatenate([top0 * w0 + top1 * w1, bot0 * w0 + bot1 * w1], axis=1)
        out_ref[...] = y * lax.rsqrt(jnp.mean(y * y, axis=-1, keepdims=True) + EPS) * g_ref[...]

    @pl.when(i == 0)
    def _():
        issue(i, 0)

    for slot in range(2):
        @pl.when(lax.rem(i, 2) == slot)
        def _(slot=slot):
            drain(i, slot)

            @pl.when(i + 1 < pl.num_programs(0))
            def _():
                issue(i + 1, 1 - slot)
                finish(slot)

            @pl.when(i + 1 >= pl.num_programs(0))
            def _():
                finish(slot)


def _combine(pos0, pos1, h, wf, norm_g, ys, tok_offset):
    n_rows = h.shape[0]
    tm = TOK_TILE
    return pl.pallas_call(
        functools.partial(_combine_kernel, tok_offset=tok_offset),
        grid_spec=pltpu.PrefetchScalarGridSpec(
            num_scalar_prefetch=2,
            grid=(n_rows // tm,),
            in_specs=[pl.BlockSpec((tm, D_MODEL), lambda i, p0, p1: (i, 0)),
                      pl.BlockSpec((tm, LANES), lambda i, p0, p1: (i, 0)),
                      pl.BlockSpec((1, D_MODEL), lambda i, p0, p1: (0, 0)),
                      pl.BlockSpec(memory_space=pl.ANY)],
            out_specs=pl.BlockSpec((tm, D_MODEL), lambda i, p0, p1: (i, 0)),
            scratch_shapes=[pltpu.VMEM((tm, HALF), jnp.uint32)] * 4 + [pltpu.SemaphoreType.DMA((2,))],
        ),
        out_shape=jax.ShapeDtypeStruct((n_rows, D_MODEL), F32),
        compiler_params=pltpu.CompilerParams(dimension_semantics=("arbitrary",),
                                             vmem_limit_bytes=VMEM_LIMIT),
        name="combine",
    )(pos0, pos1, h, wf, norm_g, ys)


def _expert_kernel(blk_ref, exp_ref, lo_ref, hi_ref, first_ref, slot_ref, next_ref,
                   xs_ref, wg_hbm, wu_hbm, wd_hbm, ys_ref,
                   wg32_ref, wu32_ref, wd32_ref, wgb_ref, wub_ref, wdb_ref, sems):
    w = pl.program_id(0)
    e = exp_ref[w]
    slot = slot_ref[w]
    changed = jnp.logical_or(w == 0, e != exp_ref[jnp.maximum(w - 1, 0)])

    def weight_copies(expert, s):
        return [pltpu.make_async_copy(src.at[expert], dst.at[s], sems.at[s, j])
                for j, (src, dst) in enumerate(((wg_hbm, wg32_ref), (wu_hbm, wu32_ref), (wd_hbm, wd32_ref)))]

    @pl.when(w == 0)
    def _():
        for c in weight_copies(e, slot):
            c.start()

    @pl.when(changed)
    def _():
        for c in weight_copies(e, slot):
            c.wait()
        wgb_ref[...] = wg32_ref[slot].astype(BF16)
        wub_ref[...] = wu32_ref[slot].astype(BF16)
        wdb_ref[...] = wd32_ref[slot].astype(BF16)

        @pl.when(next_ref[w] >= 0)
        def _():
            for c in weight_copies(next_ref[w], 1 - slot):
                c.start()

    lo = lo_ref[w]
    hi = hi_ref[w]

    @pl.when(hi > lo)
    def _():
        x = jnp.concatenate(_unpack_rows(xs_ref[...]), axis=1).astype(BF16)
        gate = _dot(x, wgb_ref[...])
        hid = (gate * _sigmoid(gate)) * _dot(x, wub_ref[...])
        y = _pack_rows(_dot(hid.astype(BF16), wdb_ref[...]))

        @pl.when(first_ref[w] == 1)
        def _():
            ys_ref[...] = y

        @pl.when(first_ref[w] == 0)
        def _():
            rows = blk_ref[w] * ROW_BLOCK + lax.broadcasted_iota(jnp.int32, (ROW_BLOCK, 1), 0)
            ys_ref[...] = jnp.where((rows >= lo) & (rows < hi), y, ys_ref[...])


def _experts(items, xs, w_gate, w_up, w_down):
    n_rows = xs.shape[0]
    n_items = items[0].shape[0]
    hbm = pl.BlockSpec(memory_space=pl.ANY)
    return pl.pallas_call(
        _expert_kernel,
        grid_spec=pltpu.PrefetchScalarGridSpec(
            num_scalar_prefetch=len(items),
            grid=(n_items,),
            in_specs=[pl.BlockSpec((ROW_BLOCK, HALF), lambda w, blk, *_: (blk[w], 0)), hbm, hbm, hbm],
            out_specs=pl.BlockSpec((ROW_BLOCK, HALF), lambda w, blk, *_: (blk[w], 0)),
            scratch_shapes=[pltpu.VMEM((2, D_MODEL, D_EXPERT), F32), pltpu.VMEM((2, D_MODEL, D_EXPERT), F32),
                            pltpu.VMEM((2, D_EXPERT, D_MODEL), F32),
                            pltpu.VMEM((D_MODEL, D_EXPERT), BF16), pltpu.VMEM((D_MODEL, D_EXPERT), BF16),
                            pltpu.VMEM((D_EXPERT, D_MODEL), BF16),
                            pltpu.SemaphoreType.DMA((2, 3))],
        ),
        out_shape=jax.ShapeDtypeStruct((n_rows, HALF), jnp.uint32),
        compiler_params=pltpu.CompilerParams(dimension_semantics=("arbitrary",),
                                             vmem_limit_bytes=VMEM_LIMIT),
        name="experts",
    )(*items, xs, w_gate, w_up, w_down)


def _work_items(counts, n_rows):
    n_blocks = n_rows // ROW_BLOCK
    n_items = n_blocks + N_EXPERTS - 1
    ends = jnp.cumsum(counts)
    starts = ends - counts
    blk_lo = starts // ROW_BLOCK
    blk_hi = jnp.maximum(ends - 1, 0) // ROW_BLOCK
    per_e = jnp.where(counts > 0, blk_hi - blk_lo + 1, 0)
    item_end = jnp.cumsum(per_e)
    item_start = item_end - per_e
    total = item_end[-1]
    w = jnp.arange(n_items, dtype=jnp.int32)
    live = w < total
    e = jnp.sum((jnp.minimum(w, total - 1)[:, None] >= item_end[None, :]).astype(jnp.int32), axis=1)
    e = jnp.minimum(e, N_EXPERTS - 1)
    sel = (e[:, None] == jnp.arange(N_EXPERTS, dtype=jnp.int32)[None, :]).astype(jnp.int32)
    pick = lambda table: jnp.sum(sel * table[None, :], axis=1)
    blk = jnp.where(live, pick(blk_lo) + (w - pick(item_start)), n_blocks - 1).astype(jnp.int32)
    lo = jnp.where(live, jnp.maximum(pick(starts), blk * ROW_BLOCK), 0).astype(jnp.int32)
    hi = jnp.where(live, jnp.minimum(pick(ends), (blk + 1) * ROW_BLOCK), 0).astype(jnp.int32)
    prev_blk = jnp.concatenate([jnp.full((1,), -1, jnp.int32), blk[:-1]])
    first = (live & (blk != prev_blk)).astype(jnp.int32)
    ids = jnp.arange(N_EXPERTS, dtype=jnp.int32)
    used = counts > 0
    rank_e = jnp.cumsum(used.astype(jnp.int32)) - 1
    later = used[None, :] & (ids[None, :] > ids[:, None])
    next_e = jnp.min(jnp.where(later, ids[None, :], N_EXPERTS), axis=1)
    next_e = jnp.where(next_e < N_EXPERTS, next_e, -1)
    slot = (pick(rank_e) % 2).astype(jnp.int32)
    nxt = pick(next_e).astype(jnp.int32)
    return (blk, e.astype(jnp.int32), lo, hi, first, slot, nxt), starts


def kernel(x_prompt, x_sample, state_conv, state_gla, norm_mix, w_in, b_gates, w_gla_gate_up, b_gla_gate_up, conv_dw, conv_db, conv_ln_g, conv_ln_b, w_conv_out, gla_norm_g, w_gla_out, w_out, norm_ffn, w_router_group, b_router_group, w_router_expert, b_router_expert, w_expert_gate, w_expert_up, w_expert_down, norm_final):
    depth = norm_mix.shape[0]
    assert depth == 1
    l = 0
    bp, seq_p, _ = x_prompt.shape
    bs, seq_s, _ = x_sample.shape
    n_p, n_s = bp * seq_p, bs * seq_s
    n_tok = n_p + n_s
    row2 = lambda a: a.reshape(1, -1)

    wi = w_in[l]
    w_main = wi.astype(BF16)
    w_alow = jnp.pad(w_main[:, _C_ALOW:_C_GATES], ((0, 0), (0, LANES - RANK)))
    w_gates = w_main[:, _C_GATES:]
    w_up_pad = jnp.pad(w_gla_gate_up[l], ((0, LANES - RANK), (0, 0))).astype(BF16)
    w_route = jnp.pad(jnp.concatenate([w_router_expert[l], w_router_group[l]], axis=1),
                      ((0, 0), (0, LANES - N_EXPERTS - N_GROUPS)))
    w_route_hi = w_route.astype(BF16)
    w_route = jnp.stack([w_route_hi, (w_route - w_route_hi.astype(F32)).astype(BF16)])
    b_route = jnp.pad(jnp.concatenate([b_router_expert[l], b_router_group[l]]),
                      (0, LANES - N_EXPERTS - N_GROUPS)).reshape(1, LANES)
    w_pw = w_conv_out[l].astype(BF16)
    w_go = w_gla_out[l].astype(BF16)
    w_o = w_out[l].astype(BF16)
    conv_args = (conv_dw[l], row2(conv_db[l]), row2(conv_ln_g[l]), row2(conv_ln_b[l]), w_pw)
    proj_args = (row2(norm_mix[l]), w_main, w_alow, w_gates, w_up_pad, row2(b_gla_gate_up[l]),
                 row2(b_gates[l]))
    mix_args = (row2(gla_norm_g[l]), w_go, w_o, row2(norm_ffn[l]), w_route, b_route)

    xp = x_prompt.reshape(n_p, D_MODEL)
    conv_out, q, k, v, la, sr, gc, gg, conv_p = _proj_conv(xp, bp, seq_p, proj_args, conv_args)
    o, gla_p = _gla_prompt(q, k, la, v, bp, seq_p)
    h_p, xr_p, rec_p, wf_p, cnt = _mix_route(xp, conv_out, o, sr, gc, gg, *mix_args,
                                             jnp.zeros((SUBLANES, LANES), F32))

    xs_ = x_sample.reshape(n_s, D_MODEL)
    u, q, k, v, la, sr, gc, gg = _proj(xs_, *proj_args)
    conv_out, conv_s = _conv(u.reshape(bs, seq_s, D_CONV), state_conv, *conv_args, nseq=SAMPLE_SEQS)
    seqs = lambda a: a.reshape(bs, seq_s, a.shape[-1])
    o, gla_s = _gla_sample(seqs(q), seqs(k), seqs(la), seqs(v), state_gla)
    h_s, xr_s, rec_s, wf_s, cnt = _mix_route(xs_, conv_out.reshape(n_s, D_MODEL), o.reshape(n_s, V_W), sr, gc, gg,
                                             *mix_args, cnt)

    counts = cnt[0, :N_EXPERTS].astype(jnp.int32)
    items, starts = _work_items(counts, 2 * n_tok)
    rec = jnp.concatenate([rec_p, rec_s], axis=1).astype(jnp.int32)
    e_ids = jnp.arange(N_EXPERTS, dtype=jnp.int32)[:, None]
    start_of = lambda e: jnp.sum(jnp.where(e[None, :] == e_ids, starts[:, None], 0), axis=0)
    pos0 = start_of(rec[_L_E0]) + rec[_L_R0]
    pos1 = start_of(rec[_L_E1]) + rec[_L_R1]
    xs_sorted = _dispatch(pos0, pos1, xr_p, xr_s)
    ys = _experts(items, xs_sorted, w_expert_gate[l], w_expert_up[l], w_expert_down[l])
    y_p = _combine(pos0, pos1, h_p, wf_p, row2(norm_final), ys, 0)
    y_s = _combine(pos0, pos1, h_s, wf_s, row2(norm_final), ys, n_p)

    return (y_p.reshape(bp, seq_p, D_MODEL), y_s.reshape(bs, seq_s, D_MODEL),
            conv_p[None], gla_p[None], conv_s[None], gla_s[None])
```

```python
import functools

import jax
import jax.numpy as jnp
from jax import lax
from jax.experimental import pallas as pl
from jax.experimental.pallas import tpu as pltpu

F32 = jnp.float32
BF16 = jnp.bfloat16

D_MODEL = 1024
D_CONV = 512
CONV_WIDTH = 31
CONV_HIST = CONV_WIDTH - 1
HEADS = 4
DK = 128
DV = 256
QK_W = HEADS * DK
V_W = HEADS * DV
RANK = 16
GATE_NORM = 16.0
N_GROUPS = 4
EXPERTS_PER_GROUP = 8
N_EXPERTS = 32
D_EXPERT = 512
EPS = 1e-6

LANES = 128
SUBLANES = 8
VMEM_LIMIT = 56 * 1024 * 1024

TOK_TILE = 512
PROJ_TILE = 512
MIX_TILE = 512
GLA_CHUNK = 128
GLA_STEP = 1024
SAFE_LOG_DECAY = -80.0
CONV_ROWS = 64
CONV_HALO = 40
ROW_BLOCK = 256
DMA_UNROLL = 8
SAMPLE_SEQS = 32
SAMPLE_GLA_SEQS = 8
SAMPLE_PAD = 16

_C_GLU_A, _C_GLU_B = 0, 512
_C_Q, _C_K, _C_V, _C_R = 1024, 1536, 2048, 3072
_C_ALOW = 4096
_C_GATES = _C_ALOW + RANK
_MAIN_W = 1


def _sigmoid(x):
    return jax.nn.sigmoid(x)


def _dot(a, b):
    return jnp.dot(a, b, preferred_element_type=F32)


def _normed_input(x_ref, g_ref):
    x = x_ref[...]
    hn = x * lax.rsqrt(jnp.mean(x * x, axis=-1, keepdims=True) + EPS) * g_ref[...]
    return hn.astype(BF16)


def _glu(hb, wm_ref):
    return _dot(hb, wm_ref[:, _C_GLU_A:_C_GLU_B]) * _sigmoid(_dot(hb, wm_ref[:, _C_GLU_B:_C_Q]))


def _projection_steps(hb, wm_ref, wa_ref, wg_ref, wup_ref, bup_ref, bg_ref,
                      q_ref, k_ref, v_ref, la_ref, sr_ref, gc_ref, gg_ref):
    def mm(lo, hi):
        return _dot(hb, wm_ref[:, lo:hi])

    def q_step():
        q_ref[...] = mm(_C_Q, _C_K) * (DK ** -0.5)

    def k_step():
        k_ref[...] = mm(_C_K, _C_V)

    def v_step():
        v_ref[...] = mm(_C_V, _C_R).astype(BF16)

    def r_step():
        r = mm(_C_R, _C_ALOW)
        sr_ref[...] = (r * _sigmoid(r)).astype(BF16)

    def gc_step():
        gc_ref[...] = _sigmoid(_dot(hb, wg_ref[:, :D_MODEL]) + bg_ref[:, :D_MODEL]).astype(BF16)

    def gg_step():
        gg_ref[...] = _sigmoid(_dot(hb, wg_ref[:, D_MODEL:]) + bg_ref[:, D_MODEL:]).astype(BF16)

    def decay_step():
        a_low = _dot(hb, wa_ref[...])
        z = _dot(a_low.astype(BF16), wup_ref[...]) + bup_ref[...]
        la_ref[...] = (jnp.minimum(z, 0.0) - jnp.log1p(jnp.exp(-jnp.abs(z)))) * (1.0 / GATE_NORM)

    return [q_step, k_step, v_step, r_step, gc_step, gg_step, decay_step]


def _proj_kernel(x_ref, g_ref, wm_ref, wa_ref, wg_ref, wup_ref, bup_ref, bg_ref,
                 u_ref, q_ref, k_ref, v_ref, la_ref, sr_ref, gc_ref, gg_ref):
    hb = _normed_input(x_ref, g_ref)
    u_ref[...] = _glu(hb, wm_ref)
    for step in _projection_steps(hb, wm_ref, wa_ref, wg_ref, wup_ref, bup_ref, bg_ref,
                                  q_ref, k_ref, v_ref, la_ref, sr_ref, gc_ref, gg_ref):
        step()


_HIST_PAD = 32
_HIST0 = _HIST_PAD - CONV_HIST


def _conv_taps(w, dw_ref, cs, rt):
    win = w.shape[0]
    acc = jnp.zeros((rt, LANES), F32)
    for s in range(SUBLANES):
        ws = w if s == 0 else pltpu.roll(w, win - s, axis=0)
        for a in range(CONV_HALO // SUBLANES):
            j = a * SUBLANES + s - _HIST0
            if 0 <= j < CONV_WIDTH:
                acc = acc + ws[a * SUBLANES:a * SUBLANES + rt, :] * dw_ref[j:j + 1, cs]
    return acc


def _norm_swish_pointwise(y, lg_ref, lb_ref, wpw_ref):
    mu = jnp.mean(y, axis=-1, keepdims=True)
    var = jnp.mean(jnp.square(y - mu), axis=-1, keepdims=True)
    yn = (y - mu) * lax.rsqrt(var + EPS) * lg_ref[...] + lb_ref[...]
    return _dot((yn * _sigmoid(yn)).astype(BF16), wpw_ref[...])


def _proj_conv_kernel(x_ref, g_ref, wm_ref, wa_ref, wg_ref, wup_ref, bup_ref, bg_ref,
                      dw_ref, db_ref, lg_ref, lb_ref, wpw_ref,
                      co_ref, q_ref, k_ref, v_ref, la_ref, sr_ref, gc_ref, gg_ref, ns_ref,
                      full_ref, y_ref):
    c = pl.program_id(1)
    tm = x_ref.shape[0]

    @pl.when(c == 0)
    def _():
        full_ref[0:_HIST_PAD, :] = jnp.zeros((_HIST_PAD, D_CONV), F32)

    @pl.when(c > 0)
    def _():
        full_ref[0:_HIST_PAD, :] = full_ref[tm:tm + _HIST_PAD, :]

    hb = _normed_input(x_ref, g_ref)
    full_ref[_HIST_PAD:_HIST_PAD + tm, :] = _glu(hb, wm_ref)
    full_ref[_HIST_PAD + tm:, :] = jnp.zeros((SUBLANES, D_CONV), F32)
    ns_ref[0] = full_ref[tm + _HIST0:tm + _HIST_PAD, :]
    rt = CONV_ROWS
    for i in range(tm // rt):
        for cc in range(D_CONV // LANES):
            cs = slice(cc * LANES, (cc + 1) * LANES)
            acc = _conv_taps(full_ref[i * rt:i * rt + rt + CONV_HALO, cs], dw_ref, cs, rt)
            y_ref[i * rt:(i + 1) * rt, cs] = acc + db_ref[:, cs]
    for step in _projection_steps(hb, wm_ref, wa_ref, wg_ref, wup_ref, bup_ref, bg_ref,
                                  q_ref, k_ref, v_ref, la_ref, sr_ref, gc_ref, gg_ref):
        step()
    co_ref[...] = _norm_swish_pointwise(y_ref[...], lg_ref, lb_ref, wpw_ref)


def _proj_conv(x2d, nb, seq, proj_consts, conv_consts):
    tm = PROJ_TILE
    nc = seq // tm
    row = lambda w: pl.BlockSpec((tm, w), lambda b, c: (b * nc + c, 0))
    full = lambda a: pl.BlockSpec(a.shape, lambda b, c: (0,) * a.ndim)
    widths = (D_MODEL, QK_W, QK_W, V_W, QK_W, V_W, D_MODEL, D_MODEL)
    dtypes = (F32, F32, F32, BF16, F32, BF16, BF16, BF16)
    consts = tuple(proj_consts) + tuple(conv_consts)
    n = nb * seq
    const_specs = [full(a) for a in consts]
    const_specs[_MAIN_W] = pl.BlockSpec((D_MODEL, _C_ALOW), lambda b, c: (0, 0))
    return pl.pallas_call(
        _proj_conv_kernel,
        grid=(nb, nc),
        in_specs=[row(D_MODEL)] + const_specs,
        out_specs=[row(w) for w in widths] + [pl.BlockSpec((1, CONV_HIST, D_CONV), lambda b, c: (b, 0, 0))],
        out_shape=[jax.ShapeDtypeStruct((n, w), dt) for w, dt in zip(widths, dtypes)]
        + [jax.ShapeDtypeStruct((nb, CONV_HIST, D_CONV), F32)],
        scratch_shapes=[pltpu.VMEM((_HIST_PAD + tm + SUBLANES, D_CONV), F32), pltpu.VMEM((tm, D_CONV), F32)],
        compiler_params=pltpu.CompilerParams(dimension_semantics=("parallel", "arbitrary"),
                                             vmem_limit_bytes=VMEM_LIMIT),
        name="proj_conv",
    )(x2d, *consts)


def _proj(x2d, norm_g, w_main, w_alow, w_gates, w_up, b_up, b_gates):
    n = x2d.shape[0]
    tm = TOK_TILE
    row = lambda w: pl.BlockSpec((tm, w), lambda i: (i, 0))
    full = lambda a: pl.BlockSpec(a.shape, lambda i: (0,) * a.ndim)
    widths = (D_CONV, QK_W, QK_W, V_W, QK_W, V_W, D_MODEL, D_MODEL)
    dtypes = (F32, F32, F32, BF16, F32, BF16, BF16, BF16)
    consts = (norm_g, w_main, w_alow, w_gates, w_up, b_up, b_gates)
    const_specs = [full(a) for a in consts]
    const_specs[_MAIN_W] = pl.BlockSpec((D_MODEL, _C_ALOW), lambda i: (0, 0))
    return pl.pallas_call(
        _proj_kernel,
        grid=(n // tm,),
        in_specs=[row(D_MODEL)] + const_specs,
        out_specs=[row(w) for w in widths],
        out_shape=[jax.ShapeDtypeStruct((n, w), dt) for w, dt in zip(widths, dtypes)],
        compiler_params=pltpu.CompilerParams(dimension_semantics=("parallel",),
                                             vmem_limit_bytes=VMEM_LIMIT),
        name="proj",
    )(x2d, *consts)


def _conv_kernel(u_ref, st_ref, dw_ref, db_ref, lg_ref, lb_ref, wpw_ref, out_ref, ns_ref, full_ref, y_ref, *, seq):
    nseq = u_ref.shape[0]
    n_rows = y_ref.shape[0] // nseq
    tail = full_ref.shape[0] - (_HIST_PAD + seq)

    def one_seq(b, carry):
        full_ref[0:_HIST_PAD, :] = jnp.zeros((_HIST_PAD, D_CONV), F32)
        full_ref[_HIST_PAD + seq:, :] = jnp.zeros((tail, D_CONV), F32)
        full_ref[_HIST0:_HIST_PAD, :] = st_ref[b]
        full_ref[_HIST_PAD:_HIST_PAD + seq, :] = u_ref[b]
        ns_ref[b] = full_ref[seq + _HIST0:seq + _HIST_PAD, :]
        for c in range(D_CONV // LANES):
            cs = slice(c * LANES, (c + 1) * LANES)
            acc = _conv_taps(full_ref[:, cs], dw_ref, cs, n_rows)
            y_ref[pl.ds(pl.multiple_of(b * n_rows, SUBLANES), n_rows), cs] = acc + db_ref[:, cs]
        return carry

    lax.fori_loop(0, nseq, one_seq, 0)
    res = _norm_swish_pointwise(y_ref[...], lg_ref, lb_ref, wpw_ref)
    for j in range(nseq):
        out_ref[j] = res[j * n_rows:j * n_rows + seq]


def _conv(u3, state, dw, db, lg, lb, w_pw, nseq):
    nb, seq, _ = u3.shape
    n_rows = -(-seq // SUBLANES) * SUBLANES
    full = lambda a: pl.BlockSpec(a.shape, lambda b: (0,) * a.ndim)
    per_b = lambda r, w: pl.BlockSpec((nseq, r, w), lambda b: (b, 0, 0))
    return pl.pallas_call(
        functools.partial(_conv_kernel, seq=seq),
        grid=(nb // nseq,),
        in_specs=[per_b(seq, D_CONV), pl.BlockSpec((None, nseq, CONV_HIST, D_CONV), lambda b: (0, b, 0, 0)),
                  full(dw), full(db), full(lg), full(lb), full(w_pw)],
        out_specs=[per_b(seq, D_MODEL), per_b(CONV_HIST, D_CONV)],
        out_shape=[jax.ShapeDtypeStruct((nb, seq, D_MODEL), F32),
                   jax.ShapeDtypeStruct((nb, CONV_HIST, D_CONV), F32)],
        scratch_shapes=[pltpu.VMEM((n_rows + CONV_HALO, D_CONV), F32),
                        pltpu.VMEM((nseq * n_rows, D_CONV), F32)],
        compiler_params=pltpu.CompilerParams(dimension_semantics=("parallel",),
                                             vmem_limit_bytes=VMEM_LIMIT),
        name="conv_state",
    )(u3, state, dw, db, lg, lb, w_pw)


def _token_step(s, a_col, k_col, q_col, v_row):
    s = s * a_col + k_col * v_row
    return s, jnp.sum(q_col * s, axis=0, keepdims=True)


def _gla_prompt_kernel(q_ref, k_ref, la_ref, v_ref, o_ref, sout_ref, s_ref, vf_ref):
    c = pl.program_id(1)
    n = GLA_CHUNK
    chunks = [slice(i * n, (i + 1) * n) for i in range(q_ref.shape[0] // n)]

    @pl.when(c == 0)
    def _():
        s_ref[...] = jnp.zeros_like(s_ref)

    r_i = lax.broadcasted_iota(jnp.int32, (n, n), 0)
    c_i = lax.broadcasted_iota(jnp.int32, (n, n), 1)
    tri = jnp.where(r_i >= c_i, 1.0, 0.0).astype(BF16)

    def prefix_sum(g):
        hi = g.astype(BF16)
        lo = (g - hi.astype(F32)).astype(BF16)
        return _dot(tri, hi) + _dot(tri, lo)

    bs = [prefix_sum(la_ref[rows, :]) for rows in chunks]
    total = bs[0][n - 1:n, :]
    for b in bs[1:]:
        total = jnp.minimum(total, b[n - 1:n, :])
    fast = jnp.min(total) > SAFE_LOG_DECAY

    @pl.when(fast)
    def _():
        for h in range(HEADS):
            ks = slice(h * DK, (h + 1) * DK)
            vs = slice(h * DV, (h + 1) * DV)
            s = s_ref[h]
            for rows, b in zip(chunks, bs):
                bh = b[:, ks]
                bl = bh[n - 1:n, :]
                kh = k_ref[rows, ks]
                vh = v_ref[rows, vs]
                qp = (q_ref[rows, ks] * jnp.exp(bh)).astype(BF16)
                kp = (kh * jnp.exp(-bh)).astype(BF16)
                kpp = kh * jnp.exp(bl - bh)
                att = lax.dot_general(qp, kp, (((1,), (1,)), ((), ())), preferred_element_type=F32)
                att = jnp.where(r_i >= c_i, att, 0.0).astype(BF16)
                o_ref[rows, vs] = _dot(qp, s.astype(BF16)) + _dot(att, vh)
                d_col = jnp.sum(jnp.where(r_i == c_i, jnp.exp(bl), 0.0), axis=1, keepdims=True)
                s = s * d_col + _dot(kpp.T.astype(BF16), vh)
            s_ref[h] = s

    @pl.when(jnp.logical_not(fast))
    def _():
        lane = lax.broadcasted_iota(jnp.int32, (1, n), 1)
        for rows in chunks:
            vf_ref[...] = v_ref[rows, :].astype(F32)
            for h in range(HEADS):
                ks = slice(h * DK, (h + 1) * DK)
                vs = slice(h * DV, (h + 1) * DV)
                q_t = q_ref[rows, ks].T
                k_t = k_ref[rows, ks].T
                a_t = jnp.exp(la_ref[rows, ks]).T

                def body(t, s):
                    m = lane == t
                    col = lambda x: jnp.sum(jnp.where(m, x, 0.0), axis=1, keepdims=True)
                    s, o_row = _token_step(s, col(a_t), col(k_t), col(q_t), vf_ref[pl.ds(t, 1), vs])
                    o_ref[pl.ds(rows.start + t, 1), vs] = o_row
                    return s

                s_ref[h] = lax.fori_loop(0, n, body, s_ref[h])

    @pl.when(c == pl.num_programs(1) - 1)
    def _():
        sout_ref[0] = s_ref[...]


def _gla_prompt(q, k, la, v, nb, seq):
    nc = seq // GLA_STEP
    tok = lambda w: pl.BlockSpec((GLA_STEP, w), lambda b, c: (b * nc + c, 0))
    return pl.pallas_call(
        _gla_prompt_kernel,
        grid=(nb, nc),
        in_specs=[tok(QK_W), tok(QK_W), tok(QK_W), tok(V_W)],
        out_specs=[tok(V_W), pl.BlockSpec((1, HEADS, DK, DV), lambda b, c: (b, 0, 0, 0))],
        out_shape=[jax.ShapeDtypeStruct((nb * seq, V_W), F32),
                   jax.ShapeDtypeStruct((nb, HEADS, DK, DV), F32)],
        scratch_shapes=[pltpu.VMEM((HEADS, DK, DV), F32), pltpu.VMEM((GLA_CHUNK, V_W), F32)],
        compiler_params=pltpu.CompilerParams(dimension_semantics=("parallel", "arbitrary"),
                                             vmem_limit_bytes=VMEM_LIMIT),
        name="gla_prompt",
    )(q, k, la, v)


def _gla_sample_kernel(q_ref, k_ref, la_ref, v_ref, s_ref, o_ref, sout_ref, *, seq):
    p = SAMPLE_PAD
    row = lax.broadcasted_iota(jnp.int32, (p, 1), 0)

    def padded(x):
        return jnp.concatenate([x, jnp.zeros((p - seq, x.shape[1]), F32)], axis=0)

    for b in range(q_ref.shape[0]):
        q, k, v, g = padded(q_ref[b]), padded(k_ref[b]), padded(v_ref[b].astype(F32)), padded(la_ref[b])
        bc = g
        sh = 1
        while sh < seq:
            bc = bc + jnp.where(row >= sh, pltpu.roll(bc, sh, axis=0), 0.0)
            sh *= 2
        b_last = bc[seq - 1:seq, :]
        bc = jnp.where(row < seq, bc, b_last)
        qp = (q * jnp.exp(bc)).astype(BF16)
        kpp_t = (k * jnp.exp(b_last - bc)).T.astype(BF16)
        d_cols = jnp.broadcast_to(jnp.exp(b_last), (SUBLANES, QK_W)).T
        vb = v.astype(BF16)

        o_in = [jnp.zeros((p, DV), F32) for _ in range(HEADS)]
        for d in range(seq):
            k_d = k if d == 0 else pltpu.roll(k, d, axis=0)
            v_d = v if d == 0 else pltpu.roll(v, d, axis=0)
            b_d = bc if d == 0 else pltpu.roll(bc, d, axis=0)
            pair = q * k_d * jnp.exp(jnp.where(row >= d, bc - b_d, -jnp.inf))
            for h in range(HEADS):
                att = jnp.sum(pair[:, h * DK:(h + 1) * DK], axis=1, keepdims=True)
                o_in[h] = o_in[h] + att * v_d[:, h * DV:(h + 1) * DV]

        for h in range(HEADS):
            ks = slice(h * DK, (h + 1) * DK)
            vs = slice(h * DV, (h + 1) * DV)
            s = s_ref[b, h]
            o = _dot(qp[:, ks], s.astype(BF16)) + o_in[h]
            o_ref[b, :, vs] = o[:seq]
            sout_ref[b, h] = s * d_cols[ks, 0:1] + _dot(kpp_t[ks, :], vb[:, vs])


def _gla_sample(q3, k3, la3, v3, state):
    nb, seq, _ = q3.shape
    assert seq <= SUBLANES
    ns = SAMPLE_GLA_SEQS
    col = pl.BlockSpec((ns, seq, QK_W), lambda b: (b, 0, 0))
    tok = pl.BlockSpec((ns, seq, V_W), lambda b: (b, 0, 0))
    st = pl.BlockSpec((ns, HEADS, DK, DV), lambda b: (b, 0, 0, 0))
    st_in = pl.BlockSpec((None, ns, HEADS, DK, DV), lambda b: (0, b, 0, 0, 0))
    return pl.pallas_call(
        functools.partial(_gla_sample_kernel, seq=seq),
        grid=(nb // ns,),
        in_specs=[col, col, col, tok, st_in],
        out_specs=[tok, st],
        out_shape=[jax.ShapeDtypeStruct((nb, seq, V_W), F32),
                   jax.ShapeDtypeStruct((nb, HEADS, DK, DV), F32)],
        compiler_params=pltpu.CompilerParams(dimension_semantics=("parallel",),
                                             vmem_limit_bytes=VMEM_LIMIT),
        name="gla_sample",
    )(q3, k3, la3, v3, state)


_L_E0, _L_E1, _L_R0, _L_R1 = 0, 1, 2, 3
_GROUP_LANE0 = N_EXPERTS
HALF = D_MODEL // 2


def _pack_rows(x):
    r = x.astype(BF16).astype(F32)
    top = lax.bitcast_convert_type(r[:, :HALF], jnp.uint32)
    bot = lax.bitcast_convert_type(r[:, HALF:], jnp.uint32)
    return top | jnp.right_shift(bot, jnp.uint32(16))


def _unpack_rows(packed):
    top = lax.bitcast_convert_type(packed & jnp.uint32(0xFFFF0000), F32)
    bot = lax.bitcast_convert_type(jnp.left_shift(packed, jnp.uint32(16)), F32)
    return top, bot


def _mix_route_kernel(x_ref, co_ref, o_ref, sr_ref, gc_ref, gg_ref, gn_ref, wgo_ref, wo_ref,
                      g_ref, wr_ref, br_ref, cin_ref,
                      h_ref, xp_ref, mi_ref, mf_ref, cnt_ref, carry_ref):
    i = pl.program_id(0)
    tm = x_ref.shape[0]

    @pl.when(i == 0)
    def _():
        carry_ref[...] = cin_ref[0:1, :]

    gla_out = None
    for h in range(HEADS):
        vs = slice(h * DV, (h + 1) * DV)
        oh = o_ref[:, vs]
        on = oh * lax.rsqrt(jnp.mean(oh * oh, axis=-1, keepdims=True) + EPS) * gn_ref[:, vs]
        part = _dot((on * sr_ref[:, vs]).astype(BF16), wgo_ref[vs, :])
        gla_out = part if gla_out is None else gla_out + part
    mix = (gc_ref[...] * co_ref[...] + gg_ref[...] * gla_out).astype(BF16)
    x = x_ref[...] + _dot(mix, wo_ref[...])
    h_ref[...] = x

    hn = x * lax.rsqrt(jnp.mean(x * x, axis=-1, keepdims=True) + EPS) * g_ref[...]
    hi = hn.astype(BF16)
    hi_f = hi.astype(F32)
    xp_ref[...] = _pack_rows(hn)
    lo = (hn - hi_f).astype(BF16)
    logits = _dot(hi, wr_ref[0]) + (_dot(lo, wr_ref[0]) + _dot(hi, wr_ref[1])) + br_ref[...]
    lane = lax.broadcasted_iota(jnp.int32, (tm, LANES), 1)
    lane_f = lane.astype(F32)
    group_f = jnp.right_shift(lane, EXPERTS_PER_GROUP.bit_length() - 1).astype(F32)
    big = jnp.float32(LANES)
    neg = jnp.float32(-jnp.inf)

    is_g = (lane >= _GROUP_LANE0) & (lane < _GROUP_LANE0 + N_GROUPS)
    lg = jnp.where(is_g, logits, neg)
    mg = jnp.max(lg, axis=1, keepdims=True)
    p_sel = 1.0 / jnp.sum(jnp.exp(lg - mg), axis=1, keepdims=True)
    g_idx = jnp.min(jnp.where(lg == mg, lane_f, big), axis=1, keepdims=True) - _GROUP_LANE0

    is_e = (lane < N_EXPERTS) & (group_f == g_idx)
    le = jnp.where(is_e, logits, neg)
    v0 = jnp.max(le, axis=1, keepdims=True)
    i0 = jnp.min(jnp.where(le == v0, lane_f, big), axis=1, keepdims=True)
    le1 = jnp.where(lane_f == i0, neg, le)
    v1 = jnp.max(le1, axis=1, keepdims=True)
    i1 = jnp.min(jnp.where(le1 == v1, lane_f, big), axis=1, keepdims=True)
    e1 = jnp.exp(v1 - v0)
    den = 1.0 + e1
    w0 = p_sel * (1.0 / den)
    w1 = p_sel * (e1 / den)

    hit0 = lane_f == i0
    hit1 = lane_f == i1
    cnt = jnp.where(hit0 | hit1, 1.0, 0.0)
    r_i = lax.broadcasted_iota(jnp.int32, (tm, tm), 0)
    c_i = lax.broadcasted_iota(jnp.int32, (tm, tm), 1)
    before = jnp.where(r_i > c_i, 1.0, 0.0).astype(BF16)
    seen = _dot(before, cnt.astype(BF16)) + carry_ref[...]
    rank0 = jnp.sum(jnp.where(hit0, seen, 0.0), axis=1, keepdims=True)
    rank1 = jnp.sum(jnp.where(hit1, seen, 0.0), axis=1, keepdims=True)
    carry_ref[...] = carry_ref[...] + jnp.sum(cnt, axis=0, keepdims=True)
    cnt_ref[...] = jnp.broadcast_to(carry_ref[...], cnt_ref.shape)

    rec = jnp.where(lane == _L_E0, i0, 0.0)
    rec = jnp.where(lane == _L_E1, i1, rec)
    rec = jnp.where(lane == _L_R0, rank0, rec)
    rec = jnp.where(lane == _L_R1, rank1, rec)
    mi_ref[...] = rec.T[:SUBLANES, :]
    mf_ref[...] = jnp.where(lane == 0, w0, jnp.where(lane == 1, w1, 0.0))


def _mix_route(x2d, conv_out, o, sr, gc, gg, gn, w_gla_out, w_out, norm_g, w_route, b_route, counts_in):
    n = x2d.shape[0]
    tm = MIX_TILE
    row = lambda w: pl.BlockSpec((tm, w), lambda i: (i, 0))
    full = lambda a: pl.BlockSpec(a.shape, lambda i: (0,) * a.ndim)
    consts = (gn, w_gla_out, w_out, norm_g, w_route, b_route, counts_in)
    return pl.pallas_call(
        _mix_route_kernel,
        grid=(n // tm,),
        in_specs=[row(D_MODEL)] * 6 + [full(a) for a in consts],
        out_specs=[row(D_MODEL), row(HALF), pl.BlockSpec((SUBLANES, tm), lambda i: (0, i)), row(LANES),
                   pl.BlockSpec((SUBLANES, LANES), lambda i: (0, 0))],
        out_shape=[jax.ShapeDtypeStruct((n, D_MODEL), F32),
                   jax.ShapeDtypeStruct((n, HALF), jnp.uint32),
                   jax.ShapeDtypeStruct((SUBLANES, n), F32),
                   jax.ShapeDtypeStruct((n, LANES), F32),
                   jax.ShapeDtypeStruct((SUBLANES, LANES), F32)],
        scratch_shapes=[pltpu.VMEM((1, LANES), F32)],
        compiler_params=pltpu.CompilerParams(dimension_semantics=("arbitrary",),
                                             vmem_limit_bytes=VMEM_LIMIT),
        name="mix_route",
    )(x2d, conv_out, o, sr, gc, gg, *consts)


def _row_copy(src_ref, src_row, dst_ref, dst_row, sem):
    return pltpu.make_async_copy(src_ref.at[pl.ds(src_row, 1), :], dst_ref.at[pl.ds(dst_row, 1), :], sem)


def _dispatch_kernel(p0_ref, p1_ref, xa_ref, xb_ref, xs_ref, sem, *, tiles_a):
    tm = xa_ref.shape[0]
    i = pl.program_id(0)
    base = i * tm

    def scatter(src_ref):
        def rows(wait):
            def body(t, carry):
                c0 = _row_copy(src_ref, t, xs_ref, p0_ref[base + t], sem)
                c1 = _row_copy(src_ref, t, xs_ref, p1_ref[base + t], sem)
                if wait:
                    c0.wait()
                    c1.wait()
                else:
                    c0.start()
                    c1.start()
                return carry
            if wait:
                lax.fori_loop(0, tm, body, 0, unroll=DMA_UNROLL)
            else:
                for t in range(tm):
                    body(t, 0)

        rows(wait=False)
        rows(wait=True)

    @pl.when(i < tiles_a)
    def _():
        scatter(xa_ref)

    @pl.when(i >= tiles_a)
    def _():
        scatter(xb_ref)


def _dispatch(pos0, pos1, x_a, x_b):
    tm = TOK_TILE
    tiles_a, tiles_b = x_a.shape[0] // tm, x_b.shape[0] // tm
    n = x_a.shape[0] + x_b.shape[0]
    width = x_a.shape[1]
    return pl.pallas_call(
        functools.partial(_dispatch_kernel, tiles_a=tiles_a),
        grid_spec=pltpu.PrefetchScalarGridSpec(
            num_scalar_prefetch=2,
            grid=(tiles_a + tiles_b,),
            in_specs=[pl.BlockSpec((tm, width), lambda i, p0, p1: (jnp.minimum(i, tiles_a - 1), 0)),
                      pl.BlockSpec((tm, width), lambda i, p0, p1: (jnp.maximum(i - tiles_a, 0), 0))],
            out_specs=pl.BlockSpec(memory_space=pl.ANY),
            scratch_shapes=[pltpu.SemaphoreType.DMA(())],
        ),
        out_shape=jax.ShapeDtypeStruct((2 * n, width), x_a.dtype),
        compiler_params=pltpu.CompilerParams(dimension_semantics=("arbitrary",)),
        name="dispatch",
    )(pos0, pos1, x_a, x_b)


def _combine_kernel(p0_ref, p1_ref, h_ref, wf_ref, g_ref, ys_ref, out_ref, b0a, b1a, b0b, b1b, sems, *, tok_offset):
    tm = h_ref.shape[0]
    i = pl.program_id(0)
    bufs = ((b0a, b1a), (b0b, b1b))

    def copies(tile, slot, t):
        base = tok_offset + tile * tm
        return (_row_copy(ys_ref, p0_ref[base + t], bufs[slot][0], t, sems.at[slot]),
                _row_copy(ys_ref, p1_ref[base + t], bufs[slot][1], t, sems.at[slot]))

    def issue(tile, slot):
        for t in range(tm):
            for c in copies(tile, slot, t):
                c.start()

    def drain(tile, slot):
        def rows(t, carry):
            for c in copies(tile, slot, t):
                c.wait()
            return carry
        lax.fori_loop(0, tm, rows, 0, unroll=DMA_UNROLL)

    def finish(slot):
        top0, bot0 = _unpack_rows(bufs[slot][0][...])
        top1, bot1 = _unpack_rows(bufs[slot][1][...])
        w0, w1 = wf_ref[:, 0:1], wf_ref[:, 1:2]
        y = h_ref[...] + jnp.concatenate([top0 * w0 + top1 * w1, bot0 * w0 + bot1 * w1], axis=1)
        out_ref[...] = y * lax.rsqrt(jnp.mean(y * y, axis=-1, keepdims=True) + EPS) * g_ref[...]

    @pl.when(i == 0)
    def _():
        issue(i, 0)

    for slot in range(2):
        @pl.when(lax.rem(i, 2) == slot)
        def _(slot=slot):
            drain(i, slot)

            @pl.when(i + 1 < pl.num_programs(0))
            def _():
                issue(i + 1, 1 - slot)
                finish(slot)

            @pl.when(i + 1 >= pl.num_programs(0))
            def _():
                finish(slot)


def _combine(pos0, pos1, h, wf, norm_g, ys, tok_offset):
    n_rows = h.shape[0]
    tm = TOK_TILE
    return pl.pallas_call(
        functools.partial(_combine_kernel, tok_offset=tok_offset),
        grid_spec=pltpu.PrefetchScalarGridSpec(
            num_scalar_prefetch=2,
            grid=(n_rows // tm,),
            in_specs=[pl.BlockSpec((tm, D_MODEL), lambda i, p0, p1: (i, 0)),
                      pl.BlockSpec((tm, LANES), lambda i, p0, p1: (i, 0)),
                      pl.BlockSpec((1, D_MODEL), lambda i, p0, p1: (0, 0)),
                      pl.BlockSpec(memory_space=pl.ANY)],
            out_specs=pl.BlockSpec((tm, D_MODEL), lambda i, p0, p1: (i, 0)),
            scratch_shapes=[pltpu.VMEM((tm, HALF), jnp.uint32)] * 4 + [pltpu.SemaphoreType.DMA((2,))],
        ),
        out_shape=jax.ShapeDtypeStruct((n_rows, D_MODEL), F32),
        compiler_params=pltpu.CompilerParams(dimension_semantics=("arbitrary",),
                                             vmem_limit_bytes=VMEM_LIMIT),
        name="combine",
    )(pos0, pos1, h, wf, norm_g, ys)


def _expert_kernel(blk_ref, exp_ref, lo_ref, hi_ref, first_ref, slot_ref, next_ref,
                   xs_ref, wg_hbm, wu_hbm, wd_hbm, ys_ref,
                   wg32_ref, wu32_ref, wd32_ref, wgb_ref, wub_ref, wdb_ref, sems):
    w = pl.program_id(0)
    e = exp_ref[w]
    slot = slot_ref[w]
    changed = jnp.logical_or(w == 0, e != exp_ref[jnp.maximum(w - 1, 0)])

    def weight_copies(expert, s):
        return [pltpu.make_async_copy(src.at[expert], dst.at[s], sems.at[s, j])
                for j, (src, dst) in enumerate(((wg_hbm, wg32_ref), (wu_hbm, wu32_ref), (wd_hbm, wd32_ref)))]

    @pl.when(w == 0)
    def _():
        for c in weight_copies(e, slot):
            c.start()

    @pl.when(changed)
    def _():
        for c in weight_copies(e, slot):
            c.wait()
        wgb_ref[...] = wg32_ref[slot].astype(BF16)
        wub_ref[...] = wu32_ref[slot].astype(BF16)
        wdb_ref[...] = wd32_ref[slot].astype(BF16)

        @pl.when(next_ref[w] >= 0)
        def _():
            for c in weight_copies(next_ref[w], 1 - slot):
                c.start()

    lo = lo_ref[w]
    hi = hi_ref[w]

    @pl.when(hi > lo)
    def _():
        x = jnp.concatenate(_unpack_rows(xs_ref[...]), axis=1).astype(BF16)
        gate = _dot(x, wgb_ref[...])
        hid = (gate * _sigmoid(gate)) * _dot(x, wub_ref[...])
        y = _pack_rows(_dot(hid.astype(BF16), wdb_ref[...]))

        @pl.when(first_ref[w] == 1)
        def _():
            ys_ref[...] = y

        @pl.when(first_ref[w] == 0)
        def _():
            rows = blk_ref[w] * ROW_BLOCK + lax.broadcasted_iota(jnp.int32, (ROW_BLOCK, 1), 0)
            ys_ref[...] = jnp.where((rows >= lo) & (rows < hi), y, ys_ref[...])


def _experts(items, xs, w_gate, w_up, w_down):
    n_rows = xs.shape[0]
    n_items = items[0].shape[0]
    hbm = pl.BlockSpec(memory_space=pl.ANY)
    return pl.pallas_call(
        _expert_kernel,
        grid_spec=pltpu.PrefetchScalarGridSpec(
            num_scalar_prefetch=len(items),
            grid=(n_items,),
            in_specs=[pl.BlockSpec((ROW_BLOCK, HALF), lambda w, blk, *_: (blk[w], 0)), hbm, hbm, hbm],
            out_specs=pl.BlockSpec((ROW_BLOCK, HALF), lambda w, blk, *_: (blk[w], 0)),
            scratch_shapes=[pltpu.VMEM((2, D_MODEL, D_EXPERT), F32), pltpu.VMEM((2, D_MODEL, D_EXPERT), F32),
                            pltpu.VMEM((2, D_EXPERT, D_MODEL), F32),
                            pltpu.VMEM((D_MODEL, D_EXPERT), BF16), pltpu.VMEM((D_MODEL, D_EXPERT), BF16),
                            pltpu.VMEM((D_EXPERT, D_MODEL), BF16),
                            pltpu.SemaphoreType.DMA((2, 3))],
        ),
        out_shape=jax.ShapeDtypeStruct((n_rows, HALF), jnp.uint32),
        compiler_params=pltpu.CompilerParams(dimension_semantics=("arbitrary",),
                                             vmem_limit_bytes=VMEM_LIMIT),
        name="experts",
    )(*items, xs, w_gate, w_up, w_down)


def _work_items(counts, n_rows):
    n_blocks = n_rows // ROW_BLOCK
    n_items = n_blocks + N_EXPERTS - 1
    ends = jnp.cumsum(counts)
    starts = ends - counts
    blk_lo = starts // ROW_BLOCK
    blk_hi = jnp.maximum(ends - 1, 0) // ROW_BLOCK
    per_e = jnp.where(counts > 0, blk_hi - blk_lo + 1, 0)
    item_end = jnp.cumsum(per_e)
    item_start = item_end - per_e
    total = item_end[-1]
    w = jnp.arange(n_items, dtype=jnp.int32)
    live = w < total
    e = jnp.sum((jnp.minimum(w, total - 1)[:, None] >= item_end[None, :]).astype(jnp.int32), axis=1)
    e = jnp.minimum(e, N_EXPERTS - 1)
    sel = (e[:, None] == jnp.arange(N_EXPERTS, dtype=jnp.int32)[None, :]).astype(jnp.int32)
    pick = lambda table: jnp.sum(sel * table[None, :], axis=1)
    blk = jnp.where(live, pick(blk_lo) + (w - pick(item_start)), n_blocks - 1).astype(jnp.int32)
    lo = jnp.where(live, jnp.maximum(pick(starts), blk * ROW_BLOCK), 0).astype(jnp.int32)
    hi = jnp.where(live, jnp.minimum(pick(ends), (blk + 1) * ROW_BLOCK), 0).astype(jnp.int32)
    prev_blk = jnp.concatenate([jnp.full((1,), -1, jnp.int32), blk[:-1]])
    first = (live & (blk != prev_blk)).astype(jnp.int32)
    ids = jnp.arange(N_EXPERTS, dtype=jnp.int32)
    used = counts > 0
    rank_e = jnp.cumsum(used.astype(jnp.int32)) - 1
    later = used[None, :] & (ids[None, :] > ids[:, None])
    next_e = jnp.min(jnp.where(later, ids[None, :], N_EXPERTS), axis=1)
    next_e = jnp.where(next_e < N_EXPERTS, next_e, -1)
    slot = (pick(rank_e) % 2).astype(jnp.int32)
    nxt = pick(next_e).astype(jnp.int32)
    return (blk, e.astype(jnp.int32), lo, hi, first, slot, nxt), starts


def kernel(x_prompt, x_sample, state_conv, state_gla, norm_mix, w_in, b_gates, w_gla_gate_up, b_gla_gate_up, conv_dw, conv_db, conv_ln_g, conv_ln_b, w_conv_out, gla_norm_g, w_gla_out, w_out, norm_ffn, w_router_group, b_router_group, w_router_expert, b_router_expert, w_expert_gate, w_expert_up, w_expert_down, norm_final):
    depth = norm_mix.shape[0]
    assert depth == 1
    l = 0
    bp, seq_p, _ = x_prompt.shape
    bs, seq_s, _ = x_sample.shape
    n_p, n_s = bp * seq_p, bs * seq_s
    n_tok = n_p + n_s
    row2 = lambda a: a.reshape(1, -1)

    wi = w_in[l]
    w_main = wi.astype(BF16)
    w_alow = jnp.pad(w_main[:, _C_ALOW:_C_GATES], ((0, 0), (0, LANES - RANK)))
    w_gates = w_main[:, _C_GATES:]
    w_up_pad = jnp.pad(w_gla_gate_up[l], ((0, LANES - RANK), (0, 0))).astype(BF16)
    w_route = jnp.pad(jnp.concatenate([w_router_expert[l], w_router_group[l]], axis=1),
                      ((0, 0), (0, LANES - N_EXPERTS - N_GROUPS)))
    w_route_hi = w_route.astype(BF16)
    w_route = jnp.stack([w_route_hi, (w_route - w_route_hi.astype(F32)).astype(BF16)])
    b_route = jnp.pad(jnp.concatenate([b_router_expert[l], b_router_group[l]]),
                      (0, LANES - N_EXPERTS - N_GROUPS)).reshape(1, LANES)
    w_pw = w_conv_out[l].astype(BF16)
    w_go = w_gla_out[l].astype(BF16)
    w_o = w_out[l].astype(BF16)
    conv_args = (conv_dw[l], row2(conv_db[l]), row2(conv_ln_g[l]), row2(conv_ln_b[l]), w_pw)
    proj_args = (row2(norm_mix[l]), w_main, w_alow, w_gates, w_up_pad, row2(b_gla_gate_up[l]),
                 row2(b_gates[l]))
    mix_args = (row2(gla_norm_g[l]), w_go, w_o, row2(norm_ffn[l]), w_route, b_route)

    xp = x_prompt.reshape(n_p, D_MODEL)
    conv_out, q, k, v, la, sr, gc, gg, conv_p = _proj_conv(xp, bp, seq_p, proj_args, conv_args)
    o, gla_p = _gla_prompt(q, k, la, v, bp, seq_p)
    h_p, xr_p, rec_p, wf_p, cnt = _mix_route(xp, conv_out, o, sr, gc, gg, *mix_args,
                                             jnp.zeros((SUBLANES, LANES), F32))

    xs_ = x_sample.reshape(n_s, D_MODEL)
    u, q, k, v, la, sr, gc, gg = _proj(xs_, *proj_args)
    conv_out, conv_s = _conv(u.reshape(bs, seq_s, D_CONV), state_conv, *conv_args, nseq=SAMPLE_SEQS)
    seqs = lambda a: a.reshape(bs, seq_s, a.shape[-1])
    o, gla_s = _gla_sample(seqs(q), seqs(k), seqs(la), seqs(v), state_gla)
    h_s, xr_s, rec_s, wf_s, cnt = _mix_route(xs_, conv_out.reshape(n_s, D_MODEL), o.reshape(n_s, V_W), sr, gc, gg,
                                             *mix_args, cnt)

    counts = cnt[0, :N_EXPERTS].astype(jnp.int32)
    items, starts = _work_items(counts, 2 * n_tok)
    rec = jnp.concatenate([rec_p, rec_s], axis=1).astype(jnp.int32)
    e_ids = jnp.arange(N_EXPERTS, dtype=jnp.int32)[:, None]
    start_of = lambda e: jnp.sum(jnp.where(e[None, :] == e_ids, starts[:, None], 0), axis=0)
    pos0 = start_of(rec[_L_E0]) + rec[_L_R0]
    pos1 = start_of(rec[_L_E1]) + rec[_L_R1]
    xs_sorted = _dispatch(pos0, pos1, xr_p, xr_s)
    ys = _experts(items, xs_sorted, w_expert_gate[l], w_expert_up[l], w_expert_down[l])
    y_p = _combine(pos0, pos1, h_p, wf_p, row2(norm_final), ys, 0)
    y_s = _combine(pos0, pos1, h_s, wf_s, row2(norm_final), ys, n_p)

    return (y_p.reshape(bp, seq_p, D_MODEL), y_s.reshape(bs, seq_s, D_MODEL),
            conv_p[None], gla_p[None], conv_s[None], gla_s[None])
```

```python
import functools

import jax
import jax.numpy as jnp
from jax import lax
from jax.experimental import pallas as pl
from jax.experimental.pallas import tpu as pltpu

F32 = jnp.float32
BF16 = jnp.bfloat16

D_MODEL = 1024
D_CONV = 512
CONV_WIDTH = 31
CONV_HIST = CONV_WIDTH - 1
HEADS = 4
DK = 128
DV = 256
QK_W = HEADS * DK
V_W = HEADS * DV
RANK = 16
GATE_NORM = 16.0
N_GROUPS = 4
EXPERTS_PER_GROUP = 8
N_EXPERTS = 32
D_EXPERT = 512
EPS = 1e-6

LANES = 128
SUBLANES = 8
VMEM_LIMIT = 56 * 1024 * 1024

TOK_TILE = 512
PROJ_TILE = 512
MIX_TILE = 512
GLA_CHUNK = 128
GLA_STEP = 1024
SAFE_LOG_DECAY = -80.0
CONV_ROWS = 64
CONV_HALO = 40
ROW_BLOCK = 256
DMA_UNROLL = 8
SAMPLE_SEQS = 32
SAMPLE_GLA_SEQS = 8
SAMPLE_PAD = 16

_C_GLU_A, _C_GLU_B = 0, 512
_C_Q, _C_K, _C_V, _C_R = 1024, 1536, 2048, 3072
_C_ALOW = 4096
_C_GATES = _C_ALOW + RANK
_MAIN_W = 1


def _sigmoid(x):
    return jax.nn.sigmoid(x)


def _dot(a, b):
    return jnp.dot(a, b, preferred_element_type=F32)


def _normed_input(x_ref, g_ref):
    x = x_ref[...]
    hn = x * lax.rsqrt(jnp.mean(x * x, axis=-1, keepdims=True) + EPS) * g_ref[...]
    return hn.astype(BF16)


def _glu(hb, wm_ref):
    return _dot(hb, wm_ref[:, _C_GLU_A:_C_GLU_B]) * _sigmoid(_dot(hb, wm_ref[:, _C_GLU_B:_C_Q]))


def _projection_steps(hb, wm_ref, wa_ref, wg_ref, wup_ref, bup_ref, bg_ref,
                      q_ref, k_ref, v_ref, la_ref, sr_ref, gc_ref, gg_ref):
    def mm(lo, hi):
        return _dot(hb, wm_ref[:, lo:hi])

    def q_step():
        q_ref[...] = mm(_C_Q, _C_K) * (DK ** -0.5)

    def k_step():
        k_ref[...] = mm(_C_K, _C_V)

    def v_step():
        v_ref[...] = mm(_C_V, _C_R).astype(BF16)

    def r_step():
        r = mm(_C_R, _C_ALOW)
        sr_ref[...] = (r * _sigmoid(r)).astype(BF16)

    def gc_step():
        gc_ref[...] = _sigmoid(_dot(hb, wg_ref[:, :D_MODEL]) + bg_ref[:, :D_MODEL]).astype(BF16)

    def gg_step():
        gg_ref[...] = _sigmoid(_dot(hb, wg_ref[:, D_MODEL:]) + bg_ref[:, D_MODEL:]).astype(BF16)

    def decay_step():
        a_low = _dot(hb, wa_ref[...])
        z = _dot(a_low.astype(BF16), wup_ref[...]) + bup_ref[...]
        la_ref[...] = (jnp.minimum(z, 0.0) - jnp.log1p(jnp.exp(-jnp.abs(z)))) * (1.0 / GATE_NORM)

    return [q_step, k_step, v_step, r_step, gc_step, gg_step, decay_step]


def _proj_kernel(x_ref, g_ref, wm_ref, wa_ref, wg_ref, wup_ref, bup_ref, bg_ref,
                 u_ref, q_ref, k_ref, v_ref, la_ref, sr_ref, gc_ref, gg_ref):
    hb = _normed_input(x_ref, g_ref)
    u_ref[...] = _glu(hb, wm_ref)
    for step in _projection_steps(hb, wm_ref, wa_ref, wg_ref, wup_ref, bup_ref, bg_ref,
                                  q_ref, k_ref, v_ref, la_ref, sr_ref, gc_ref, gg_ref):
        step()


_HIST_PAD = 32
_HIST0 = _HIST_PAD - CONV_HIST


def _conv_taps(w, dw_ref, cs, rt):
    win = w.shape[0]
    acc = jnp.zeros((rt, LANES), F32)
    for s in range(SUBLANES):
        ws = w if s == 0 else pltpu.roll(w, win - s, axis=0)
        for a in range(CONV_HALO // SUBLANES):
            j = a * SUBLANES + s - _HIST0
            if 0 <= j < CONV_WIDTH:
                acc = acc + ws[a * SUBLANES:a * SUBLANES + rt, :] * dw_ref[j:j + 1, cs]
    return acc


def _norm_swish_pointwise(y, lg_ref, lb_ref, wpw_ref):
    mu = jnp.mean(y, axis=-1, keepdims=True)
    var = jnp.mean(jnp.square(y - mu), axis=-1, keepdims=True)
    yn = (y - mu) * lax.rsqrt(var + EPS) * lg_ref[...] + lb_ref[...]
    return _dot((yn * _sigmoid(yn)).astype(BF16), wpw_ref[...])


def _proj_conv_kernel(x_ref, g_ref, wm_ref, wa_ref, wg_ref, wup_ref, bup_ref, bg_ref,
                      dw_ref, db_ref, lg_ref, lb_ref, wpw_ref,
                      co_ref, q_ref, k_ref, v_ref, la_ref, sr_ref, gc_ref, gg_ref, ns_ref,
                      full_ref, y_ref):
    c = pl.program_id(1)
    tm = x_ref.shape[0]

    @pl.when(c == 0)
    def _():
        full_ref[0:_HIST_PAD, :] = jnp.zeros((_HIST_PAD, D_CONV), F32)

    @pl.when(c > 0)
    def _():
        full_ref[0:_HIST_PAD, :] = full_ref[tm:tm + _HIST_PAD, :]

    hb = _normed_input(x_ref, g_ref)
    full_ref[_HIST_PAD:_HIST_PAD + tm, :] = _glu(hb, wm_ref)
    full_ref[_HIST_PAD + tm:, :] = jnp.zeros((SUBLANES, D_CONV), F32)
    ns_ref[0] = full_ref[tm + _HIST0:tm + _HIST_PAD, :]
    rt = CONV_ROWS
    for i in range(tm // rt):
        for cc in range(D_CONV // LANES):
            cs = slice(cc * LANES, (cc + 1) * LANES)
            acc = _conv_taps(full_ref[i * rt:i * rt + rt + CONV_HALO, cs], dw_ref, cs, rt)
            y_ref[i * rt:(i + 1) * rt, cs] = acc + db_ref[:, cs]
    for step in _projection_steps(hb, wm_ref, wa_ref, wg_ref, wup_ref, bup_ref, bg_ref,
                                  q_ref, k_ref, v_ref, la_ref, sr_ref, gc_ref, gg_ref):
        step()
    co_ref[...] = _norm_swish_pointwise(y_ref[...], lg_ref, lb_ref, wpw_ref)


def _proj_conv(x2d, nb, seq, proj_consts, conv_consts):
    tm = PROJ_TILE
    nc = seq // tm
    row = lambda w: pl.BlockSpec((tm, w), lambda b, c: (b * nc + c, 0))
    full = lambda a: pl.BlockSpec(a.shape, lambda b, c: (0,) * a.ndim)
    widths = (D_MODEL, QK_W, QK_W, V_W, QK_W, V_W, D_MODEL, D_MODEL)
    dtypes = (F32, F32, F32, BF16, F32, BF16, BF16, BF16)
    consts = tuple(proj_consts) + tuple(conv_consts)
    n = nb * seq
    const_specs = [full(a) for a in consts]
    const_specs[_MAIN_W] = pl.BlockSpec((D_MODEL, _C_ALOW), lambda b, c: (0, 0))
    return pl.pallas_call(
        _proj_conv_kernel,
        grid=(nb, nc),
        in_specs=[row(D_MODEL)] + const_specs,
        out_specs=[row(w) for w in widths] + [pl.BlockSpec((1, CONV_HIST, D_CONV), lambda b, c: (b, 0, 0))],
        out_shape=[jax.ShapeDtypeStruct((n, w), dt) for w, dt in zip(widths, dtypes)]
        + [jax.ShapeDtypeStruct((nb, CONV_HIST, D_CONV), F32)],
        scratch_shapes=[pltpu.VMEM((_HIST_PAD + tm + SUBLANES, D_CONV), F32), pltpu.VMEM((tm, D_CONV), F32)],
        compiler_params=pltpu.CompilerParams(dimension_semantics=("parallel", "arbitrary"),
                                             vmem_limit_bytes=VMEM_LIMIT),
        name="proj_conv",
    )(x2d, *consts)


def _proj(x2d, norm_g, w_main, w_alow, w_gates, w_up, b_up, b_gates):
    n = x2d.shape[0]
    tm = TOK_TILE
    row = lambda w: pl.BlockSpec((tm, w), lambda i: (i, 0))
    full = lambda a: pl.BlockSpec(a.shape, lambda i: (0,) * a.ndim)
    widths = (D_CONV, QK_W, QK_W, V_W, QK_W, V_W, D_MODEL, D_MODEL)
    dtypes = (F32, F32, F32, BF16, F32, BF16, BF16, BF16)
    consts = (norm_g, w_main, w_alow, w_gates, w_up, b_up, b_gates)
    const_specs = [full(a) for a in consts]
    const_specs[_MAIN_W] = pl.BlockSpec((D_MODEL, _C_ALOW), lambda i: (0, 0))
    return pl.pallas_call(
        _proj_kernel,
        grid=(n // tm,),
        in_specs=[row(D_MODEL)] + const_specs,
        out_specs=[row(w) for w in widths],
        out_shape=[jax.ShapeDtypeStruct((n, w), dt) for w, dt in zip(widths, dtypes)],
        compiler_params=pltpu.CompilerParams(dimension_semantics=("parallel",),
                                             vmem_limit_bytes=VMEM_LIMIT),
        name="proj",
    )(x2d, *consts)


def _conv_kernel(u_ref, st_ref, dw_ref, db_ref, lg_ref, lb_ref, wpw_ref, out_ref, ns_ref, full_ref, y_ref, *, seq):
    nseq = u_ref.shape[0]
    n_rows = y_ref.shape[0] // nseq
    tail = full_ref.shape[0] - (_HIST_PAD + seq)

    def one_seq(b, carry):
        full_ref[0:_HIST_PAD, :] = jnp.zeros((_HIST_PAD, D_CONV), F32)
        full_ref[_HIST_PAD + seq:, :] = jnp.zeros((tail, D_CONV), F32)
        full_ref[_HIST0:_HIST_PAD, :] = st_ref[b]
        full_ref[_HIST_PAD:_HIST_PAD + seq, :] = u_ref[b]
        ns_ref[b] = full_ref[seq + _HIST0:seq + _HIST_PAD, :]
        for c in range(D_CONV // LANES):
            cs = slice(c * LANES, (c + 1) * LANES)
            acc = _conv_taps(full_ref[:, cs], dw_ref, cs, n_rows)
            y_ref[pl.ds(pl.multiple_of(b * n_rows, SUBLANES), n_rows), cs] = acc + db_ref[:, cs]
        return carry

    lax.fori_loop(0, nseq, one_seq, 0)
    res = _norm_swish_pointwise(y_ref[...], lg_ref, lb_ref, wpw_ref)
    for j in range(nseq):
        out_ref[j] = res[j * n_rows:j * n_rows + seq]


def _conv(u3, state, dw, db, lg, lb, w_pw, nseq):
    nb, seq, _ = u3.shape
    n_rows = -(-seq // SUBLANES) * SUBLANES
    full = lambda a: pl.BlockSpec(a.shape, lambda b: (0,) * a.ndim)
    per_b = lambda r, w: pl.BlockSpec((nseq, r, w), lambda b: (b, 0, 0))
    return pl.pallas_call(
        functools.partial(_conv_kernel, seq=seq),
        grid=(nb // nseq,),
        in_specs=[per_b(seq, D_CONV), pl.BlockSpec((None, nseq, CONV_HIST, D_CONV), lambda b: (0, b, 0, 0)),
                  full(dw), full(db), full(lg), full(lb), full(w_pw)],
        out_specs=[per_b(seq, D_MODEL), per_b(CONV_HIST, D_CONV)],
        out_shape=[jax.ShapeDtypeStruct((nb, seq, D_MODEL), F32),
                   jax.ShapeDtypeStruct((nb, CONV_HIST, D_CONV), F32)],
        scratch_shapes=[pltpu.VMEM((n_rows + CONV_HALO, D_CONV), F32),
                        pltpu.VMEM((nseq * n_rows, D_CONV), F32)],
        compiler_params=pltpu.CompilerParams(dimension_semantics=("parallel",),
                                             vmem_limit_bytes=VMEM_LIMIT),
        name="conv_state",
    )(u3, state, dw, db, lg, lb, w_pw)


def _token_step(s, a_col, k_col, q_col, v_row):
    s = s * a_col + k_col * v_row
    return s, jnp.sum(q_col * s, axis=0, keepdims=True)


def _gla_prompt_kernel(q_ref, k_ref, la_ref, v_ref, o_ref, sout_ref, s_ref, vf_ref):
    c = pl.program_id(1)
    n = GLA_CHUNK
    chunks = [slice(i * n, (i + 1) * n) for i in range(q_ref.shape[0] // n)]

    @pl.when(c == 0)
    def _():
        s_ref[...] = jnp.zeros_like(s_ref)

    r_i = lax.broadcasted_iota(jnp.int32, (n, n), 0)
    c_i = lax.broadcasted_iota(jnp.int32, (n, n), 1)
    tri = jnp.where(r_i >= c_i, 1.0, 0.0).astype(BF16)

    def prefix_sum(g):
        hi = g.astype(BF16)
        lo = (g - hi.astype(F32)).astype(BF16)
        return _dot(tri, hi) + _dot(tri, lo)

    bs = [prefix_sum(la_ref[rows, :]) for rows in chunks]
    total = bs[0][n - 1:n, :]
    for b in bs[1:]:
        total = jnp.minimum(total, b[n - 1:n, :])
    fast = jnp.min(total) > SAFE_LOG_DECAY

    @pl.when(fast)
    def _():
        for h in range(HEADS):
            ks = slice(h * DK, (h + 1) * DK)
            vs = slice(h * DV, (h + 1) * DV)
            s = s_ref[h]
            for rows, b in zip(chunks, bs):
                bh = b[:, ks]
                bl = bh[n - 1:n, :]
                kh = k_ref[rows, ks]
                vh = v_ref[rows, vs]
                qp = (q_ref[rows, ks] * jnp.exp(bh)).astype(BF16)
                kp = (kh * jnp.exp(-bh)).astype(BF16)
                kpp = kh * jnp.exp(bl - bh)
                att = lax.dot_general(qp, kp, (((1,), (1,)), ((), ())), preferred_element_type=F32)
                att = jnp.where(r_i >= c_i, att, 0.0).astype(BF16)
                o_ref[rows, vs] = _dot(qp, s.astype(BF16)) + _dot(att, vh)
                d_col = jnp.sum(jnp.where(r_i == c_i, jnp.exp(bl), 0.0), axis=1, keepdims=True)
                s = s * d_col + _dot(kpp.T.astype(BF16), vh)
            s_ref[h] = s

    @pl.when(jnp.logical_not(fast))
    def _():
        lane = lax.broadcasted_iota(jnp.int32, (1, n), 1)
        for rows in chunks:
            vf_ref[...] = v_ref[rows, :].astype(F32)
            for h in range(HEADS):
                ks = slice(h * DK, (h + 1) * DK)
                vs = slice(h * DV, (h + 1) * DV)
                q_t = q_ref[rows, ks].T
                k_t = k_ref[rows, ks].T
                a_t = jnp.exp(la_ref[rows, ks]).T

                def body(t, s):
                    m = lane == t
                    col = lambda x: jnp.sum(jnp.where(m, x, 0.0), axis=1, keepdims=True)
                    s, o_row = _token_step(s, col(a_t), col(k_t), col(q_t), vf_ref[pl.ds(t, 1), vs])
                    o_ref[pl.ds(rows.start + t, 1), vs] = o_row
                    return s

                s_ref[h] = lax.fori_loop(0, n, body, s_ref[h])

    @pl.when(c == pl.num_programs(1) - 1)
    def _():
        sout_ref[0] = s_ref[...]


def _gla_prompt(q, k, la, v, nb, seq):
    nc = seq // GLA_STEP
    tok = lambda w: pl.BlockSpec((GLA_STEP, w), lambda b, c: (b * nc + c, 0))
    return pl.pallas_call(
        _gla_prompt_kernel,
        grid=(nb, nc),
        in_specs=[tok(QK_W), tok(QK_W), tok(QK_W), tok(V_W)],
        out_specs=[tok(V_W), pl.BlockSpec((1, HEADS, DK, DV), lambda b, c: (b, 0, 0, 0))],
        out_shape=[jax.ShapeDtypeStruct((nb * seq, V_W), F32),
                   jax.ShapeDtypeStruct((nb, HEADS, DK, DV), F32)],
        scratch_shapes=[pltpu.VMEM((HEADS, DK, DV), F32), pltpu.VMEM((GLA_CHUNK, V_W), F32)],
        compiler_params=pltpu.CompilerParams(dimension_semantics=("parallel", "arbitrary"),
                                             vmem_limit_bytes=VMEM_LIMIT),
        name="gla_prompt",
    )(q, k, la, v)


def _gla_sample_kernel(q_ref, k_ref, la_ref, v_ref, s_ref, o_ref, sout_ref, *, seq):
    p = SAMPLE_PAD
    row = lax.broadcasted_iota(jnp.int32, (p, 1), 0)

    def padded(x):
        return jnp.concatenate([x, jnp.zeros((p - seq, x.shape[1]), F32)], axis=0)

    for b in range(q_ref.shape[0]):
        q, k, v, g = padded(q_ref[b]), padded(k_ref[b]), padded(v_ref[b].astype(F32)), padded(la_ref[b])
        bc = g
        sh = 1
        while sh < seq:
            bc = bc + jnp.where(row >= sh, pltpu.roll(bc, sh, axis=0), 0.0)
            sh *= 2
        b_last = bc[seq - 1:seq, :]
        bc = jnp.where(row < seq, bc, b_last)
        qp = (q * jnp.exp(bc)).astype(BF16)
        kpp_t = (k * jnp.exp(b_last - bc)).T.astype(BF16)
        d_cols = jnp.broadcast_to(jnp.exp(b_last), (SUBLANES, QK_W)).T
        vb = v.astype(BF16)

        o_in = [jnp.zeros((p, DV), F32) for _ in range(HEADS)]
        for d in range(seq):
            k_d = k if d == 0 else pltpu.roll(k, d, axis=0)
            v_d = v if d == 0 else pltpu.roll(v, d, axis=0)
            b_d = bc if d == 0 else pltpu.roll(bc, d, axis=0)
            pair = q * k_d * jnp.exp(jnp.where(row >= d, bc - b_d, -jnp.inf))
            for h in range(HEADS):
                att = jnp.sum(pair[:, h * DK:(h + 1) * DK], axis=1, keepdims=True)
                o_in[h] = o_in[h] + att * v_d[:, h * DV:(h + 1) * DV]

        for h in range(HEADS):
            ks = slice(h * DK, (h + 1) * DK)
            vs = slice(h * DV, (h + 1) * DV)
            s = s_ref[b, h]
            o = _dot(qp[:, ks], s.astype(BF16)) + o_in[h]
            o_ref[b, :, vs] = o[:seq]
            sout_ref[b, h] = s * d_cols[ks, 0:1] + _dot(kpp_t[ks, :], vb[:, vs])


def _gla_sample(q3, k3, la3, v3, state):
    nb, seq, _ = q3.shape
    assert seq <= SUBLANES
    ns = SAMPLE_GLA_SEQS
    col = pl.BlockSpec((ns, seq, QK_W), lambda b: (b, 0, 0))
    tok = pl.BlockSpec((ns, seq, V_W), lambda b: (b, 0, 0))
    st = pl.BlockSpec((ns, HEADS, DK, DV), lambda b: (b, 0, 0, 0))
    st_in = pl.BlockSpec((None, ns, HEADS, DK, DV), lambda b: (0, b, 0, 0, 0))
    return pl.pallas_call(
        functools.partial(_gla_sample_kernel, seq=seq),
        grid=(nb // ns,),
        in_specs=[col, col, col, tok, st_in],
        out_specs=[tok, st],
        out_shape=[jax.ShapeDtypeStruct((nb, seq, V_W), F32),
                   jax.ShapeDtypeStruct((nb, HEADS, DK, DV), F32)],
        compiler_params=pltpu.CompilerParams(dimension_semantics=("parallel",),
                                             vmem_limit_bytes=VMEM_LIMIT),
        name="gla_sample",
    )(q3, k3, la3, v3, state)


_L_E0, _L_E1, _L_R0, _L_R1 = 0, 1, 2, 3
_GROUP_LANE0 = N_EXPERTS
HALF = D_MODEL // 2


def _pack_rows(x):
    r = x.astype(BF16).astype(F32)
    top = lax.bitcast_convert_type(r[:, :HALF], jnp.uint32)
    bot = lax.bitcast_convert_type(r[:, HALF:], jnp.uint32)
    return top | jnp.right_shift(bot, jnp.uint32(16))


def _unpack_rows(packed):
    top = lax.bitcast_convert_type(packed & jnp.uint32(0xFFFF0000), F32)
    bot = lax.bitcast_convert_type(jnp.left_shift(packed, jnp.uint32(16)), F32)
    return top, bot


def _mix_route_kernel(x_ref, co_ref, o_ref, sr_ref, gc_ref, gg_ref, gn_ref, wgo_ref, wo_ref,
                      g_ref, wr_ref, br_ref, cin_ref,
                      h_ref, xp_ref, mi_ref, mf_ref, cnt_ref, carry_ref):
    i = pl.program_id(0)
    tm = x_ref.shape[0]

    @pl.when(i == 0)
    def _():
        carry_ref[...] = cin_ref[0:1, :]

    gla_out = None
    for h in range(HEADS):
        vs = slice(h * DV, (h + 1) * DV)
        oh = o_ref[:, vs]
        on = oh * lax.rsqrt(jnp.mean(oh * oh, axis=-1, keepdims=True) + EPS) * gn_ref[:, vs]
        part = _dot((on * sr_ref[:, vs]).astype(BF16), wgo_ref[vs, :])
        gla_out = part if gla_out is None else gla_out + part
    mix = (gc_ref[...] * co_ref[...] + gg_ref[...] * gla_out).astype(BF16)
    x = x_ref[...] + _dot(mix, wo_ref[...])
    h_ref[...] = x

    hn = x * lax.rsqrt(jnp.mean(x * x, axis=-1, keepdims=True) + EPS) * g_ref[...]
    hi = hn.astype(BF16)
    hi_f = hi.astype(F32)
    xp_ref[...] = _pack_rows(hn)
    lo = (hn - hi_f).astype(BF16)
    logits = _dot(hi, wr_ref[0]) + (_dot(lo, wr_ref[0]) + _dot(hi, wr_ref[1])) + br_ref[...]
    lane = lax.broadcasted_iota(jnp.int32, (tm, LANES), 1)
    lane_f = lane.astype(F32)
    group_f = jnp.right_shift(lane, EXPERTS_PER_GROUP.bit_length() - 1).astype(F32)
    big = jnp.float32(LANES)
    neg = jnp.float32(-jnp.inf)

    is_g = (lane >= _GROUP_LANE0) & (lane < _GROUP_LANE0 + N_GROUPS)
    lg = jnp.where(is_g, logits, neg)
    mg = jnp.max(lg, axis=1, keepdims=True)
    p_sel = 1.0 / jnp.sum(jnp.exp(lg - mg), axis=1, keepdims=True)
    g_idx = jnp.min(jnp.where(lg == mg, lane_f, big), axis=1, keepdims=True) - _GROUP_LANE0

    is_e = (lane < N_EXPERTS) & (group_f == g_idx)
    le = jnp.where(is_e, logits, neg)
    v0 = jnp.max(le, axis=1, keepdims=True)
    i0 = jnp.min(jnp.where(le == v0, lane_f, big), axis=1, keepdims=True)
    le1 = jnp.where(lane_f == i0, neg, le)
    v1 = jnp.max(le1, axis=1, keepdims=True)
    i1 = jnp.min(jnp.where(le1 == v1, lane_f, big), axis=1, keepdims=True)
    e1 = jnp.exp(v1 - v0)
    den = 1.0 + e1
    w0 = p_sel * (1.0 / den)
    w1 = p_sel * (e1 / den)

    hit0 = lane_f == i0
    hit1 = lane_f == i1
    cnt = jnp.where(hit0 | hit1, 1.0, 0.0)
    r_i = lax.broadcasted_iota(jnp.int32, (tm, tm), 0)
    c_i = lax.broadcasted_iota(jnp.int32, (tm, tm), 1)
    before = jnp.where(r_i > c_i, 1.0, 0.0).astype(BF16)
    seen = _dot(before, cnt.astype(BF16)) + carry_ref[...]
    rank0 = jnp.sum(jnp.where(hit0, seen, 0.0), axis=1, keepdims=True)
    rank1 = jnp.sum(jnp.where(hit1, seen, 0.0), axis=1, keepdims=True)
    carry_ref[...] = carry_ref[...] + jnp.sum(cnt, axis=0, keepdims=True)
    cnt_ref[...] = jnp.broadcast_to(carry_ref[...], cnt_ref.shape)

    rec = jnp.where(lane == _L_E0, i0, 0.0)
    rec = jnp.where(lane == _L_E1, i1, rec)
    rec = jnp.where(lane == _L_R0, rank0, rec)
    rec = jnp.where(lane == _L_R1, rank1, rec)
    mi_ref[...] = rec.T[:SUBLANES, :]
    mf_ref[...] = jnp.where(lane == 0, w0, jnp.where(lane == 1, w1, 0.0))


def _mix_route(x2d, conv_out, o, sr, gc, gg, gn, w_gla_out, w_out, norm_g, w_route, b_route, counts_in):
    n = x2d.shape[0]
    tm = MIX_TILE
    row = lambda w: pl.BlockSpec((tm, w), lambda i: (i, 0))
    full = lambda a: pl.BlockSpec(a.shape, lambda i: (0,) * a.ndim)
    consts = (gn, w_gla_out, w_out, norm_g, w_route, b_route, counts_in)
    return pl.pallas_call(
        _mix_route_kernel,
        grid=(n // tm,),
        in_specs=[row(D_MODEL)] * 6 + [full(a) for a in consts],
        out_specs=[row(D_MODEL), row(HALF), pl.BlockSpec((SUBLANES, tm), lambda i: (0, i)), row(LANES),
                   pl.BlockSpec((SUBLANES, LANES), lambda i: (0, 0))],
        out_shape=[jax.ShapeDtypeStruct((n, D_MODEL), F32),
                   jax.ShapeDtypeStruct((n, HALF), jnp.uint32),
                   jax.ShapeDtypeStruct((SUBLANES, n), F32),
                   jax.ShapeDtypeStruct((n, LANES), F32),
                   jax.ShapeDtypeStruct((SUBLANES, LANES), F32)],
        scratch_shapes=[pltpu.VMEM((1, LANES), F32)],
        compiler_params=pltpu.CompilerParams(dimension_semantics=("arbitrary",),
                                             vmem_limit_bytes=VMEM_LIMIT),
        name="mix_route",
    )(x2d, conv_out, o, sr, gc, gg, *consts)


def _row_copy(src_ref, src_row, dst_ref, dst_row, sem):
    return pltpu.make_async_copy(src_ref.at[pl.ds(src_row, 1), :], dst_ref.at[pl.ds(dst_row, 1), :], sem)


def _dispatch_kernel(p0_ref, p1_ref, xa_hbm, xb_hbm, xs_ref, buf_ref, load_sems, row_sems, *, tiles_a):
    tm = buf_ref.shape[1]
    i = pl.program_id(0)
    n_tiles = pl.num_programs(0)

    def load(tile, slot, wait):
        def run(src_hbm, first_row):
            c = pltpu.make_async_copy(src_hbm.at[pl.ds(first_row, tm), :], buf_ref.at[slot], load_sems.at[slot])
            c.wait() if wait else c.start()

        @pl.when(tile < tiles_a)
        def _():
            run(xa_hbm, tile * tm)

        @pl.when(tile >= tiles_a)
        def _():
            run(xb_hbm, (tile - tiles_a) * tm)

    def scatter(tile, slot, wait):
        base = tile * tm

        def body(t, carry):
            c0 = _row_copy(buf_ref.at[slot], t, xs_ref, p0_ref[base + t], row_sems.at[slot])
            c1 = _row_copy(buf_ref.at[slot], t, xs_ref, p1_ref[base + t], row_sems.at[slot])
            if wait:
                c0.wait()
                c1.wait()
            else:
                c0.start()
                c1.start()
            return carry
        if wait:
            lax.fori_loop(0, tm, body, 0, unroll=DMA_UNROLL)
        else:
            for t in range(tm):
                body(t, 0)

    slot = lax.rem(i, 3)

    @pl.when(i == 0)
    def _():
        load(i, slot, wait=False)

    load(i, slot, wait=True)

    @pl.when(i + 1 < n_tiles)
    def _():
        load(i + 1, lax.rem(i + 1, 3), wait=False)

    scatter(i, slot, wait=False)

    @pl.when(i >= 1)
    def _():
        scatter(i - 1, lax.rem(i + 2, 3), wait=True)

    @pl.when(i == n_tiles - 1)
    def _():
        scatter(i, slot, wait=True)


def _dispatch(pos0, pos1, x_a, x_b):
    tm = TOK_TILE
    tiles_a, tiles_b = x_a.shape[0] // tm, x_b.shape[0] // tm
    n = x_a.shape[0] + x_b.shape[0]
    width = x_a.shape[1]
    return pl.pallas_call(
        functools.partial(_dispatch_kernel, tiles_a=tiles_a),
        grid_spec=pltpu.PrefetchScalarGridSpec(
            num_scalar_prefetch=2,
            grid=(tiles_a + tiles_b,),
            in_specs=[pl.BlockSpec(memory_space=pl.ANY), pl.BlockSpec(memory_space=pl.ANY)],
            out_specs=pl.BlockSpec(memory_space=pl.ANY),
            scratch_shapes=[pltpu.VMEM((3, tm, width), x_a.dtype), pltpu.SemaphoreType.DMA((3,)),
                            pltpu.SemaphoreType.DMA((3,))],
        ),
        out_shape=jax.ShapeDtypeStruct((2 * n, width), x_a.dtype),
        compiler_params=pltpu.CompilerParams(dimension_semantics=("arbitrary",)),
        name="dispatch",
    )(pos0, pos1, x_a, x_b)


def _combine_kernel(p0_ref, p1_ref, h_ref, wf_ref, g_ref, ys_ref, out_ref, b0a, b1a, b0b, b1b, sems, *, tok_offset):
    tm = h_ref.shape[0]
    i = pl.program_id(0)
    bufs = ((b0a, b1a), (b0b, b1b))

    def copies(tile, slot, t):
        base = tok_offset + tile * tm
        return (_row_copy(ys_ref, p0_ref[base + t], bufs[slot][0], t, sems.at[slot]),
                _row_copy(ys_ref, p1_ref[base + t], bufs[slot][1], t, sems.at[slot]))

    def issue(tile, slot):
        for t in range(tm):
            for c in copies(tile, slot, t):
                c.start()

    def drain(tile, slot):
        def rows(t, carry):
            for c in copies(tile, slot, t):
                c.wait()
            return carry
        lax.fori_loop(0, tm, rows, 0, unroll=DMA_UNROLL)

    def finish(slot):
        top0, bot0 = _unpack_rows(bufs[slot][0][...])
        top1, bot1 = _unpack_rows(bufs[slot][1][...])
        w0, w1 = wf_ref[:, 0:1], wf_ref[:, 1:2]
        y = h_ref[...] + jnp.concatenate([top0 * w0 + top1 * w1, bot0 * w0 + bot1 * w1], axis=1)
        out_ref[...] = y * lax.rsqrt(jnp.mean(y * y, axis=-1, keepdims=True) + EPS) * g_ref[...]

    @pl.when(i == 0)
    def _():
        issue(i, 0)

    for slot in range(2):
        @pl.when(lax.rem(i, 2) == slot)
        def _(slot=slot):
            drain(i, slot)

            @pl.when(i + 1 < pl.num_programs(0))
            def _():
                issue(i + 1, 1 - slot)
                finish(slot)

            @pl.when(i + 1 >= pl.num_programs(0))
            def _():
                finish(slot)


def _combine(pos0, pos1, h, wf, norm_g, ys, tok_offset):
    n_rows = h.shape[0]
    tm = TOK_TILE
    return pl.pallas_call(
        functools.partial(_combine_kernel, tok_offset=tok_offset),
        grid_spec=pltpu.PrefetchScalarGridSpec(
            num_scalar_prefetch=2,
            grid=(n_rows // tm,),
            in_specs=[pl.BlockSpec((tm, D_MODEL), lambda i, p0, p1: (i, 0)),
                      pl.BlockSpec((tm, LANES), lambda i, p0, p1: (i, 0)),
                      pl.BlockSpec((1, D_MODEL), lambda i, p0, p1: (0, 0)),
                      pl.BlockSpec(memory_space=pl.ANY)],
            out_specs=pl.BlockSpec((tm, D_MODEL), lambda i, p0, p1: (i, 0)),
            scratch_shapes=[pltpu.VMEM((tm, HALF), jnp.uint32)] * 4 + [pltpu.SemaphoreType.DMA((2,))],
        ),
        out_shape=jax.ShapeDtypeStruct((n_rows, D_MODEL), F32),
        compiler_params=pltpu.CompilerParams(dimension_semantics=("arbitrary",),
                                             vmem_limit_bytes=VMEM_LIMIT),
        name="combine",
    )(pos0, pos1, h, wf, norm_g, ys)


def _expert_kernel(blk_ref, exp_ref, lo_ref, hi_ref, first_ref, slot_ref, next_ref,
                   xs_ref, wg_hbm, wu_hbm, wd_hbm, ys_ref,
                   wg32_ref, wu32_ref, wd32_ref, wgb_ref, wub_ref, wdb_ref, sems):
    w = pl.program_id(0)
    e = exp_ref[w]
    slot = slot_ref[w]
    changed = jnp.logical_or(w == 0, e != exp_ref[jnp.maximum(w - 1, 0)])

    def weight_copies(expert, s):
        return [pltpu.make_async_copy(src.at[expert], dst.at[s], sems.at[s, j])
                for j, (src, dst) in enumerate(((wg_hbm, wg32_ref), (wu_hbm, wu32_ref), (wd_hbm, wd32_ref)))]

    @pl.when(w == 0)
    def _():
        for c in weight_copies(e, slot):
            c.start()

    @pl.when(changed)
    def _():
        for c in weight_copies(e, slot):
            c.wait()
        wgb_ref[...] = wg32_ref[slot].astype(BF16)
        wub_ref[...] = wu32_ref[slot].astype(BF16)
        wdb_ref[...] = wd32_ref[slot].astype(BF16)

        @pl.when(next_ref[w] >= 0)
        def _():
            for c in weight_copies(next_ref[w], 1 - slot):
                c.start()

    lo = lo_ref[w]
    hi = hi_ref[w]

    @pl.when(hi > lo)
    def _():
        x = jnp.concatenate(_unpack_rows(xs_ref[...]), axis=1).astype(BF16)
        gate = _dot(x, wgb_ref[...])
        hid = (gate * _sigmoid(gate)) * _dot(x, wub_ref[...])
        y = _pack_rows(_dot(hid.astype(BF16), wdb_ref[...]))

        @pl.when(first_ref[w] == 1)
        def _():
            ys_ref[...] = y

        @pl.when(first_ref[w] == 0)
        def _():
            rows = blk_ref[w] * ROW_BLOCK + lax.broadcasted_iota(jnp.int32, (ROW_BLOCK, 1), 0)
            ys_ref[...] = jnp.where((rows >= lo) & (rows < hi), y, ys_ref[...])


def _experts(items, xs, w_gate, w_up, w_down):
    n_rows = xs.shape[0]
    n_items = items[0].shape[0]
    hbm = pl.BlockSpec(memory_space=pl.ANY)
    return pl.pallas_call(
        _expert_kernel,
        grid_spec=pltpu.PrefetchScalarGridSpec(
            num_scalar_prefetch=len(items),
            grid=(n_items,),
            in_specs=[pl.BlockSpec((ROW_BLOCK, HALF), lambda w, blk, *_: (blk[w], 0)), hbm, hbm, hbm],
            out_specs=pl.BlockSpec((ROW_BLOCK, HALF), lambda w, blk, *_: (blk[w], 0)),
            scratch_shapes=[pltpu.VMEM((2, D_MODEL, D_EXPERT), F32), pltpu.VMEM((2, D_MODEL, D_EXPERT), F32),
                            pltpu.VMEM((2, D_EXPERT, D_MODEL), F32),
                            pltpu.VMEM((D_MODEL, D_EXPERT), BF16), pltpu.VMEM((D_MODEL, D_EXPERT), BF16),
                            pltpu.VMEM((D_EXPERT, D_MODEL), BF16),
                            pltpu.SemaphoreType.DMA((2, 3))],
        ),
        out_shape=jax.ShapeDtypeStruct((n_rows, HALF), jnp.uint32),
        compiler_params=pltpu.CompilerParams(dimension_semantics=("arbitrary",),
                                             vmem_limit_bytes=VMEM_LIMIT),
        name="experts",
    )(*items, xs, w_gate, w_up, w_down)


def _work_items(counts, n_rows):
    n_blocks = n_rows // ROW_BLOCK
    n_items = n_blocks + N_EXPERTS - 1
    ends = jnp.cumsum(counts)
    starts = ends - counts
    blk_lo = starts // ROW_BLOCK
    blk_hi = jnp.maximum(ends - 1, 0) // ROW_BLOCK
    per_e = jnp.where(counts > 0, blk_hi - blk_lo + 1, 0)
    item_end = jnp.cumsum(per_e)
    item_start = item_end - per_e
    total = item_end[-1]
    w = jnp.arange(n_items, dtype=jnp.int32)
    live = w < total
    e = jnp.sum((jnp.minimum(w, total - 1)[:, None] >= item_end[None, :]).astype(jnp.int32), axis=1)
    e = jnp.minimum(e, N_EXPERTS - 1)
    sel = (e[:, None] == jnp.arange(N_EXPERTS, dtype=jnp.int32)[None, :]).astype(jnp.int32)
    pick = lambda table: jnp.sum(sel * table[None, :], axis=1)
    blk = jnp.where(live, pick(blk_lo) + (w - pick(item_start)), n_blocks - 1).astype(jnp.int32)
    lo = jnp.where(live, jnp.maximum(pick(starts), blk * ROW_BLOCK), 0).astype(jnp.int32)
    hi = jnp.where(live, jnp.minimum(pick(ends), (blk + 1) * ROW_BLOCK), 0).astype(jnp.int32)
    prev_blk = jnp.concatenate([jnp.full((1,), -1, jnp.int32), blk[:-1]])
    first = (live & (blk != prev_blk)).astype(jnp.int32)
    ids = jnp.arange(N_EXPERTS, dtype=jnp.int32)
    used = counts > 0
    rank_e = jnp.cumsum(used.astype(jnp.int32)) - 1
    later = used[None, :] & (ids[None, :] > ids[:, None])
    next_e = jnp.min(jnp.where(later, ids[None, :], N_EXPERTS), axis=1)
    next_e = jnp.where(next_e < N_EXPERTS, next_e, -1)
    slot = (pick(rank_e) % 2).astype(jnp.int32)
    nxt = pick(next_e).astype(jnp.int32)
    return (blk, e.astype(jnp.int32), lo, hi, first, slot, nxt), starts


def kernel(x_prompt, x_sample, state_conv, state_gla, norm_mix, w_in, b_gates, w_gla_gate_up, b_gla_gate_up, conv_dw, conv_db, conv_ln_g, conv_ln_b, w_conv_out, gla_norm_g, w_gla_out, w_out, norm_ffn, w_router_group, b_router_group, w_router_expert, b_router_expert, w_expert_gate, w_expert_up, w_expert_down, norm_final):
    depth = norm_mix.shape[0]
    assert depth == 1
    l = 0
    bp, seq_p, _ = x_prompt.shape
    bs, seq_s, _ = x_sample.shape
    n_p, n_s = bp * seq_p, bs * seq_s
    n_tok = n_p + n_s
    row2 = lambda a: a.reshape(1, -1)

    wi = w_in[l]
    w_main = wi.astype(BF16)
    w_alow = jnp.pad(w_main[:, _C_ALOW:_C_GATES], ((0, 0), (0, LANES - RANK)))
    w_gates = w_main[:, _C_GATES:]
    w_up_pad = jnp.pad(w_gla_gate_up[l], ((0, LANES - RANK), (0, 0))).astype(BF16)
    w_route = jnp.pad(jnp.concatenate([w_router_expert[l], w_router_group[l]], axis=1),
                      ((0, 0), (0, LANES - N_EXPERTS - N_GROUPS)))
    w_route_hi = w_route.astype(BF16)
    w_route = jnp.stack([w_route_hi, (w_route - w_route_hi.astype(F32)).astype(BF16)])
    b_route = jnp.pad(jnp.concatenate([b_router_expert[l], b_router_group[l]]),
                      (0, LANES - N_EXPERTS - N_GROUPS)).reshape(1, LANES)
    w_pw = w_conv_out[l].astype(BF16)
    w_go = w_gla_out[l].astype(BF16)
    w_o = w_out[l].astype(BF16)
    conv_args = (conv_dw[l], row2(conv_db[l]), row2(conv_ln_g[l]), row2(conv_ln_b[l]), w_pw)
    proj_args = (row2(norm_mix[l]), w_main, w_alow, w_gates, w_up_pad, row2(b_gla_gate_up[l]),
                 row2(b_gates[l]))
    mix_args = (row2(gla_norm_g[l]), w_go, w_o, row2(norm_ffn[l]), w_route, b_route)

    xp = x_prompt.reshape(n_p, D_MODEL)
    conv_out, q, k, v, la, sr, gc, gg, conv_p = _proj_conv(xp, bp, seq_p, proj_args, conv_args)
    o, gla_p = _gla_prompt(q, k, la, v, bp, seq_p)
    h_p, xr_p, rec_p, wf_p, cnt = _mix_route(xp, conv_out, o, sr, gc, gg, *mix_args,
                                             jnp.zeros((SUBLANES, LANES), F32))

    xs_ = x_sample.reshape(n_s, D_MODEL)
    u, q, k, v, la, sr, gc, gg = _proj(xs_, *proj_args)
    conv_out, conv_s = _conv(u.reshape(bs, seq_s, D_CONV), state_conv, *conv_args, nseq=SAMPLE_SEQS)
    seqs = lambda a: a.reshape(bs, seq_s, a.shape[-1])
    o, gla_s = _gla_sample(seqs(q), seqs(k), seqs(la), seqs(v), state_gla)
    h_s, xr_s, rec_s, wf_s, cnt = _mix_route(xs_, conv_out.reshape(n_s, D_MODEL), o.reshape(n_s, V_W), sr, gc, gg,
                                             *mix_args, cnt)

    counts = cnt[0, :N_EXPERTS].astype(jnp.int32)
    items, starts = _work_items(counts, 2 * n_tok)
    rec = jnp.concatenate([rec_p, rec_s], axis=1).astype(jnp.int32)
    e_ids = jnp.arange(N_EXPERTS, dtype=jnp.int32)[:, None]
    start_of = lambda e: jnp.sum(jnp.where(e[None, :] == e_ids, starts[:, None], 0), axis=0)
    pos0 = start_of(rec[_L_E0]) + rec[_L_R0]
    pos1 = start_of(rec[_L_E1]) + rec[_L_R1]
    xs_sorted = _dispatch(pos0, pos1, xr_p, xr_s)
    ys = _experts(items, xs_sorted, w_expert_gate[l], w_expert_up[l], w_expert_down[l])
    y_p = _combine(pos0, pos1, h_p, wf_p, row2(norm_final), ys, 0)
    y_s = _combine(pos0, pos1, h_s, wf_s, row2(norm_final), ys, n_p)

    return (y_p.reshape(bp, seq_p, D_MODEL), y_s.reshape(bs, seq_s, D_MODEL),
            conv_p[None], gla_p[None], conv_s[None], gla_s[None])
```
